```python
import math, functools
import jax, jax.numpy as jnp
from jax import lax
import numpy as np

D_MODEL = 1024
BATCH = 4
SEQ = 4096
DEPTH = 2
DEC_BATCH = 32
DEC_SEQ = 8
PAST_LEN = 8192
PAGE_SIZE = 128

N_META = 16
NORM_EPS = 1e-6
SSM_EXPAND = 2
D_INNER = SSM_EXPAND * D_MODEL
SSM_HEAD_DIM = 64
SSM_HEADS = D_INNER // SSM_HEAD_DIM
SSM_GROUPS = 4
SSM_HPG = SSM_HEADS // SSM_GROUPS
SSM_STATE = 128
CONV_W = 4
CONV_DIM = D_INNER + 2 * SSM_GROUPS * SSM_STATE
SSM_CHUNK = 128
ATT_HEADS = 16
ATT_KV_HEADS = 4
ATT_GROUP = ATT_HEADS // ATT_KV_HEADS
HEAD_DIM = 64
ATT_DIM = ATT_HEADS * HEAD_DIM
KV_DIM = ATT_KV_HEADS * HEAD_DIM
IDX_HEADS = 8
IDX_DIM = 64
TOPK_MAX = 256
Q_BLOCK = 128
D_FF = 4 * D_MODEL

IN_SPLITS = (D_INNER, CONV_DIM, SSM_HEADS,
             ATT_DIM, KV_DIM, KV_DIM,
             IDX_HEADS * IDX_DIM, IDX_DIM, IDX_HEADS,
             D_MODEL, D_MODEL)
IN_DIM = D_INNER + CONV_DIM + SSM_HEADS + ATT_DIM + 2 * KV_DIM + IDX_HEADS * IDX_DIM + IDX_DIM + IDX_HEADS + 2 * D_MODEL

kernel_name = "hybrid_ssd_dsa_gated_decoder_step"

F32 = jnp.float32


def rmsnorm(x, w):
    xf = x.astype(F32)
    y = xf * lax.rsqrt(jnp.mean(xf * xf, axis=-1, keepdims=True) + NORM_EPS)
    return (y * w.astype(F32)).astype(x.dtype)


def split_in(p):
    offs = np.cumsum(IN_SPLITS)[:-1].tolist()
    return jnp.split(p, offs, axis=-1)


def causal_conv(xbc, prefix, w, bias):
    L = xbc.shape[1]
    xp = jnp.concatenate([prefix.astype(xbc.dtype), xbc], axis=1)
    y = bias + xp[:, :L] * w[0]
    for k in range(1, CONV_W):
        y = y + xp[:, k:k + L] * w[k]
    return jax.nn.silu(y), xp[:, L:]


def ssd_scan(x, dt, A, Bm, Cm, h0, chunk):
    b, L = x.shape[:2]
    nc = L // chunk

    def to_chunks(t):
        return jnp.moveaxis(t.reshape((b, nc, chunk) + t.shape[2:]), 1, 0)

    xs = (to_chunks(x.astype(F32).reshape(b, L, SSM_GROUPS, SSM_HPG, SSM_HEAD_DIM)),
          to_chunks(dt.astype(F32).reshape(b, L, SSM_GROUPS, SSM_HPG)),
          to_chunks(Bm.astype(F32)), to_chunks(Cm.astype(F32)))
    a_h = A.astype(F32).reshape(SSM_GROUPS, SSM_HPG)
    causal = jnp.tril(jnp.ones((chunk, chunk), bool))

    def step(h, inp):
        xc, dc, bc, cc = inp
        acum = jnp.cumsum(dc * a_h, axis=1)
        seg = acum[:, :, None] - acum[:, None, :]
        decay = jnp.exp(jnp.where(causal[None, :, :, None, None], seg, -jnp.inf))
        cb = jnp.einsum('bign,bjgn->bijg', cc, bc)
        xdt = xc * dc[..., None]
        y = jnp.einsum('bijg,bijge,bjgep->bigep', cb, decay, xdt)
        y = y + jnp.einsum('bign,bgepn->bigep', cc, h) * jnp.exp(acum)[..., None]
        last = acum[:, -1]
        wj = jnp.exp(last[:, None] - acum)
        h = h * jnp.exp(last)[..., None, None] + jnp.einsum('bjge,bjgep,bjgn->bgepn', wj, xdt, bc)
        return h, y

    h0g = h0.astype(F32).reshape(b, SSM_GROUPS, SSM_HPG, SSM_HEAD_DIM, SSM_STATE)
    hT, ys = lax.scan(step, h0g, xs)
    y = jnp.moveaxis(ys, 0, 1).reshape(b, L, SSM_HEADS, SSM_HEAD_DIM)
    return y, hT.reshape(b, SSM_HEADS, SSM_HEAD_DIM, SSM_STATE)


def ssm_mixer(z, xbc_raw, dt_raw, conv_prefix, h0, segments,
              conv_w, conv_b, dt_bias, a_log, d_skip, ssm_norm_w):
    b, L = z.shape[:2]
    xbc, conv_state = causal_conv(xbc_raw, conv_prefix, conv_w, conv_b)
    xs, Bm, Cm = jnp.split(xbc, [D_INNER, D_INNER + SSM_GROUPS * SSM_STATE], axis=-1)
    xs = xs.reshape(b, L, SSM_HEADS, SSM_HEAD_DIM)
    Bm = Bm.reshape(b, L, SSM_GROUPS, SSM_STATE)
    Cm = Cm.reshape(b, L, SSM_GROUPS, SSM_STATE)
    dt = jax.nn.softplus(dt_raw.astype(F32) + dt_bias.astype(F32))
    A = -jnp.exp(a_log.astype(F32))
    ys, h, start = [], h0, 0
    for length, chunk in segments:
        sl = slice(start, start + length)
        y, h = ssd_scan(xs[:, sl], dt[:, sl], A, Bm[:, sl], Cm[:, sl], h, chunk)
        ys.append(y)
        start += length
    y = jnp.concatenate(ys, axis=1) if len(ys) > 1 else ys[0]
    y = y + xs.astype(F32) * d_skip.astype(F32)[:, None]
    y = y.reshape(b, L, D_INNER) * jax.nn.silu(z.astype(F32))
    yg = y.reshape(b, L, SSM_GROUPS, D_INNER // SSM_GROUPS)
    yg = yg * lax.rsqrt(jnp.mean(yg * yg, axis=-1, keepdims=True) + NORM_EPS)
    y = yg.reshape(b, L, D_INNER) * ssm_norm_w.astype(F32)
    return y.astype(z.dtype), h, conv_state


def indexer_scores(qi, wi, ki):
    s = jax.nn.relu(jnp.einsum('bqhd,bsd->bqhs', qi.astype(F32), ki.astype(F32)) * IDX_DIM ** -0.5)
    return jnp.einsum('bqhs,bqh->bqs', s, wi.astype(F32) * IDX_HEADS ** -0.5)


def select_keys(scores, q_pos, topk):
    s_pos = jnp.arange(scores.shape[-1])
    admissible = s_pos[None, None, :] <= q_pos[None, :, None]
    _, idx = lax.top_k(jnp.where(admissible, scores, -jnp.inf), topk)
    valid = idx <= q_pos[None, :, None]
    return idx, valid


def attend_selected(q, k_sel, v_sel, valid):
    b, nq = q.shape[:2]
    qh = q.reshape(b, nq, ATT_KV_HEADS, ATT_GROUP, HEAD_DIM).astype(F32)
    s = jnp.einsum('bqhgd,bqkhd->bqhgk', qh, k_sel.astype(F32)) * HEAD_DIM ** -0.5
    s = jnp.where(valid[:, :, None, None, :], s, -jnp.inf)
    p = jax.nn.softmax(s, axis=-1)
    o = jnp.einsum('bqhgk,bqkhd->bqhgd', p, v_sel.astype(F32))
    return o.reshape(b, nq, ATT_DIM).astype(q.dtype)


def take_rows(t, i):
    return jax.vmap(lambda tt, ii: tt[ii])(t, i)


def prompt_sparse_attention(q, k, v, qi, wi, ki, topk):
    b, L = q.shape[:2]
    nblk = -(-L // Q_BLOCK)
    pad = nblk * Q_BLOCK - L

    def blocks(t):
        t = jnp.pad(t, [(0, 0), (0, pad)] + [(0, 0)] * (t.ndim - 2))
        return jnp.moveaxis(t.reshape((b, nblk, Q_BLOCK) + t.shape[2:]), 1, 0)

    k_h = k.reshape(b, L, ATT_KV_HEADS, HEAD_DIM)
    v_h = v.reshape(b, L, ATT_KV_HEADS, HEAD_DIM)

    def body(inp):
        qb, qib, wib, start = inp
        q_pos = start + jnp.arange(Q_BLOCK)
        idx, valid = select_keys(indexer_scores(qib, wib, ki), q_pos, topk)
        return attend_selected(qb, take_rows(k_h, idx), take_rows(v_h, idx), valid)

    starts = jnp.arange(nblk) * Q_BLOCK
    out = lax.map(body, (blocks(q), blocks(qi), blocks(wi), starts))
    return jnp.moveaxis(out, 0, 1).reshape(b, nblk * Q_BLOCK, ATT_DIM)[:, :L]


def sample_sparse_attention(q, k, v, qi, wi, ki, pool_k, pool_v, pool_ki, page_table, topk):
    bd, ns = q.shape[:2]
    past = page_table.shape[1] * PAGE_SIZE
    ki_past = pool_ki[page_table].reshape(bd, past, IDX_DIM)
    ki_all = jnp.concatenate([ki_past, ki.astype(ki_past.dtype)], axis=1)
    q_pos = past + jnp.arange(ns)
    idx, valid = select_keys(indexer_scores(qi, wi, ki_all), q_pos, topk)
    in_past = (idx < past)[..., None, None]
    past_idx = jnp.minimum(idx, past - 1)
    phys = take_rows(page_table, past_idx // PAGE_SIZE)
    off = past_idx % PAGE_SIZE
    new_idx = jnp.clip(idx - past, 0, ns - 1)
    k_new = k.reshape(bd, ns, ATT_KV_HEADS, HEAD_DIM)
    v_new = v.reshape(bd, ns, ATT_KV_HEADS, HEAD_DIM)
    k_sel = jnp.where(in_past, pool_k[phys, off], take_rows(k_new, new_idx))
    v_sel = jnp.where(in_past, pool_v[phys, off], take_rows(v_new, new_idx))
    return attend_selected(q, k_sel, v_sel, valid)


def hybrid_layer(h, conv_prefix, h0, segments, attend,
                 norm1_w, w_in, conv_w, conv_b, dt_bias, a_log, d_skip, ssm_norm_w,
                 w_ssm_proj, w_attn_proj, w_out, norm2_w, w_up, w_down):
    b, L = h.shape[:2]
    u = rmsnorm(h, norm1_w)
    z, xbc, dt_raw, q, k, v, qi, ki, wi, g_ssm, g_attn = split_in(u @ w_in)
    y_ssm, ssm_new, conv_new = ssm_mixer(z, xbc, dt_raw, conv_prefix, h0, segments,
                                         conv_w, conv_b, dt_bias, a_log, d_skip, ssm_norm_w)
    qi = qi.reshape(b, L, IDX_HEADS, IDX_DIM)
    y_att = attend(q, k, v, qi, wi, ki)
    merged = jax.nn.sigmoid(g_ssm) * (y_ssm @ w_ssm_proj) + jax.nn.sigmoid(g_attn) * (y_att @ w_attn_proj)
    h = h + merged @ w_out
    u2 = rmsnorm(h, norm2_w)
    h = h + jnp.square(jax.nn.relu(u2 @ w_up)) @ w_down
    rows = (k.reshape(b, L, ATT_KV_HEADS, HEAD_DIM), v.reshape(b, L, ATT_KV_HEADS, HEAD_DIM), ki, ssm_new, conv_new)
    return h, rows


def setup_inputs(seed: int = 0) -> dict:
    key = jax.random.key(seed)
    ks = jax.random.split(key, 32)
    n_pages = PAST_LEN // PAGE_SIZE
    n_pool = (DEC_BATCH * n_pages * 5) // 4

    def nrm(k, shape, scale):
        return jax.random.normal(k, shape, F32) * scale

    dt0 = jnp.exp(jax.random.uniform(ks[13], (DEPTH, SSM_HEADS), F32, math.log(1e-3), math.log(1e-1)))
    page_table = jax.random.permutation(ks[7], n_pool)[:DEC_BATCH * n_pages].reshape(DEC_BATCH, n_pages).astype(jnp.int32)
    return {
        "x_prompt": nrm(ks[0], (BATCH, SEQ, D_MODEL), 1.0),
        "x_sample": nrm(ks[1], (DEC_BATCH, DEC_SEQ, D_MODEL), 1.0),
        "cache_k": nrm(ks[2], (DEPTH, n_pool, PAGE_SIZE, ATT_KV_HEADS, HEAD_DIM), 1.0),
        "cache_v": nrm(ks[3], (DEPTH, n_pool, PAGE_SIZE, ATT_KV_HEADS, HEAD_DIM), 1.0),
        "cache_kidx": nrm(ks[4], (DEPTH, n_pool, PAGE_SIZE, IDX_DIM), 1.0),
        "state_ssm": nrm(ks[5], (DEPTH, DEC_BATCH, SSM_HEADS, SSM_HEAD_DIM, SSM_STATE), 0.3),
        "state_conv": nrm(ks[6], (DEPTH, DEC_BATCH, CONV_W - 1, CONV_DIM), 1.0),
        "page_table": page_table,
        "meta_tokens": nrm(ks[8], (N_META, D_MODEL), 1.0),
        "norm1_w": 1.0 + nrm(ks[9], (DEPTH, D_MODEL), 0.02),
        "w_in": nrm(ks[10], (DEPTH, D_MODEL, IN_DIM), D_MODEL ** -0.5),
        "conv_w": nrm(ks[11], (DEPTH, CONV_W, CONV_DIM), CONV_W ** -0.5),
        "conv_b": nrm(ks[12], (DEPTH, CONV_DIM), 0.02),
        "dt_bias": dt0 + jnp.log(-jnp.expm1(-dt0)),
        "a_log": jnp.log(jax.random.uniform(ks[14], (DEPTH, SSM_HEADS), F32, 1.0, 16.0)),
        "d_skip": 1.0 + nrm(ks[15], (DEPTH, SSM_HEADS), 0.02),
        "ssm_norm_w": 1.0 + nrm(ks[16], (DEPTH, D_INNER), 0.02),
        "w_ssm_proj": nrm(ks[17], (DEPTH, D_INNER, D_MODEL), D_INNER ** -0.5),
        "w_attn_proj": nrm(ks[18], (DEPTH, ATT_DIM, D_MODEL), ATT_DIM ** -0.5),
        "w_out": nrm(ks[19], (DEPTH, D_MODEL, D_MODEL), D_MODEL ** -0.5),
        "norm2_w": 1.0 + nrm(ks[20], (DEPTH, D_MODEL), 0.02),
        "w_up": nrm(ks[21], (DEPTH, D_MODEL, D_FF), D_MODEL ** -0.5),
        "w_down": nrm(ks[22], (DEPTH, D_FF, D_MODEL), D_FF ** -0.5),
        "final_norm_w": 1.0 + nrm(ks[23], (D_MODEL,), 0.02),
    }


def reference(x_prompt, x_sample, cache_k, cache_v, cache_kidx, state_ssm, state_conv, page_table,
              meta_tokens, norm1_w, w_in, conv_w, conv_b, dt_bias, a_log, d_skip, ssm_norm_w,
              w_ssm_proj, w_attn_proj, w_out, norm2_w, w_up, w_down, final_norm_w):
    b, seq = x_prompt.shape[:2]
    bd, ns = x_sample.shape[:2]
    past = page_table.shape[1] * PAGE_SIZE
    hp = jnp.concatenate([jnp.broadcast_to(meta_tokens[None].astype(x_prompt.dtype), (b, N_META, D_MODEL)), x_prompt], axis=1)
    hs = x_sample
    topk_p = min(TOPK_MAX, seq // 4)
    topk_s = min(TOPK_MAX, (past + ns) // 4)
    prompt_segments = ((N_META, N_META), (seq, SSM_CHUNK))
    sample_segments = ((ns, ns),)
    zero_prefix = jnp.zeros((b, CONV_W - 1, CONV_DIM), x_prompt.dtype)
    zero_state = jnp.zeros((b, SSM_HEADS, SSM_HEAD_DIM, SSM_STATE), F32)
    attend_p = functools.partial(prompt_sparse_attention, topk=topk_p)
    p_rows = [[], [], [], [], []]
    s_rows = [[], [], [], [], []]
    for l in range(DEPTH):
        weights = (norm1_w[l], w_in[l], conv_w[l], conv_b[l], dt_bias[l], a_log[l], d_skip[l], ssm_norm_w[l],
                   w_ssm_proj[l], w_attn_proj[l], w_out[l], norm2_w[l], w_up[l], w_down[l])
        hp, rows_p = hybrid_layer(hp, zero_prefix, zero_state, prompt_segments, attend_p, *weights)
        attend_s = functools.partial(sample_sparse_attention, pool_k=cache_k[l], pool_v=cache_v[l],
                                     pool_ki=cache_kidx[l], page_table=page_table, topk=topk_s)
        hs, rows_s = hybrid_layer(hs, state_conv[l], state_ssm[l], sample_segments, attend_s, *weights)
        for acc, r in zip(p_rows, rows_p):
            acc.append(r)
        for acc, r in zip(s_rows, rows_s):
            acc.append(r)
    y_prompt = rmsnorm(hp, final_norm_w)[:, N_META:]
    y_sample = rmsnorm(hs, final_norm_w)
    pk, pv, pki, pssm, pconv = [jnp.stack(r) for r in p_rows]
    sk, sv, ski, sssm, sconv = [jnp.stack(r) for r in s_rows]
    return (y_prompt, y_sample, pk, pv, pki, pssm, pconv, sk, sv, ski, sssm, sconv)
```

```python
import functools

import jax
import jax.numpy as jnp
from jax import lax
from jax.experimental import pallas as pl
from jax.experimental.pallas import tpu as pltpu

F32 = jnp.float32
MXU_DTYPE = jnp.bfloat16

N_META = 16
NORM_EPS = 1e-6
SSM_HEAD_DIM = 64
SSM_GROUPS = 4
SSM_STATE = 128
CONV_W = 4
ATT_HEADS = 16
ATT_KV_HEADS = 4
ATT_GROUP = ATT_HEADS // ATT_KV_HEADS
HEAD_DIM = 64
IDX_HEADS = 8
IDX_DIM = 64
TOPK_MAX = 256

LANES = 128
Q_BLOCK = 128
SSD_CHUNK = 128
VMEM_LIMIT = 56 << 20
MASK_BIAS = -1e30
INT_MIN = -(2 ** 31)

D_MODEL = 1024
D_INNER = 2 * D_MODEL
BC_DIM = 2 * SSM_GROUPS * SSM_STATE
SSM_HEADS = D_INNER // SSM_HEAD_DIM
ATT_DIM = ATT_HEADS * HEAD_DIM
KV_DIM = ATT_KV_HEADS * HEAD_DIM
QI_DIM = IDX_HEADS * IDX_DIM
C_Z = 0
C_XS = C_Z + D_INNER
C_BC = C_XS + D_INNER
C_Q = C_BC + BC_DIM
C_K = C_Q + ATT_DIM
C_V = C_K + KV_DIM
C_QI = C_V + KV_DIM
C_GS = C_QI + QI_DIM
C_GA = C_GS + D_MODEL
C_SM = C_GA + D_MODEL
SM_KI = 0
SM_DT = IDX_DIM
SM_WI = IDX_DIM + SSM_HEADS
PROJ_TN = 512
PROJ_COLS = -(-(C_SM + LANES) // PROJ_TN) * PROJ_TN


def _divisor_tile(n, max_tile, align):
    best = None
    for t in range(align, min(n, max_tile) + 1, align):
        if n % t == 0:
            best = t
    assert best is not None, (n, max_tile, align)
    return best


def _params(*sem):
    return pltpu.CompilerParams(dimension_semantics=sem, vmem_limit_bytes=VMEM_LIMIT)


def _dot(a, b):
    return jnp.dot(a.astype(MXU_DTYPE), b.astype(MXU_DTYPE), preferred_element_type=F32)


def _dot_nt(a, b):
    return lax.dot_general(a.astype(MXU_DTYPE), b.astype(MXU_DTYPE), (((1,), (1,)), ((), ())),
                           preferred_element_type=F32)


def _dot_exact(a, b):
    return jnp.dot(a, b, preferred_element_type=F32, precision=lax.Precision.HIGHEST)


def _rmsnorm_kernel(x_ref, w_ref, o_ref):
    x = x_ref[...]
    ms = jnp.mean(x * x, axis=-1, keepdims=True)
    o_ref[...] = (x * lax.rsqrt(ms + NORM_EPS) * w_ref[...]).astype(o_ref.dtype)


def _rmsnorm(x, w, out_dtype):
    r, d = x.shape
    tm = _divisor_tile(r, 512, 16)
    return pl.pallas_call(
        _rmsnorm_kernel,
        grid=(r // tm,),
        in_specs=[pl.BlockSpec((tm, d), lambda i: (i, 0)), pl.BlockSpec((1, d), lambda i: (0, 0))],
        out_specs=pl.BlockSpec((tm, d), lambda i: (i, 0)),
        out_shape=jax.ShapeDtypeStruct((r, d), out_dtype),
        compiler_params=_params("parallel"),
        name="rmsnorm",
    )(x, w.reshape(1, d))


def _matmul_kernel(x_ref, w_ref, o_ref):
    o_ref[...] = jnp.dot(x_ref[...], w_ref[...], preferred_element_type=F32)


def _in_proj(u, w):
    r, k = u.shape
    n = w.shape[1]
    tm = _divisor_tile(r, 1100, 16)
    return pl.pallas_call(
        _matmul_kernel,
        grid=(n // PROJ_TN, r // tm),
        in_specs=[pl.BlockSpec((tm, k), lambda j, i: (i, 0)), pl.BlockSpec((k, PROJ_TN), lambda j, i: (0, j))],
        out_specs=pl.BlockSpec((tm, PROJ_TN), lambda j, i: (i, j)),
        out_shape=jax.ShapeDtypeStruct((r, n), F32),
        compiler_params=_params("parallel", "parallel"),
        name="in_proj",
    )(u, w)


def _merge_kernel(h_ref, ys_ref, ya_ref, gs_ref, ga_ref, wsp_ref, wap_ref, wo_ref, o_ref):
    a = jnp.dot(ys_ref[...], wsp_ref[...], preferred_element_type=F32)
    b = jnp.dot(ya_ref[...], wap_ref[...], preferred_element_type=F32)
    m = jax.nn.sigmoid(gs_ref[...]) * a + jax.nn.sigmoid(ga_ref[...]) * b
    o_ref[...] = h_ref[...] + _dot(m, wo_ref[...])


def _merge(h, y_ssm, y_att, proj, w_sp, w_ap, w_o):
    r, d = h.shape
    tm = _divisor_tile(r, 256, 16)
    const = lambda i: (0, 0)
    return pl.pallas_call(
        _merge_kernel,
        grid=(r // tm,),
        in_specs=[pl.BlockSpec((tm, d), lambda i: (i, 0)),
                  pl.BlockSpec((tm, D_INNER), lambda i: (i, 0)),
                  pl.BlockSpec((tm, ATT_DIM), lambda i: (i, 0)),
                  pl.BlockSpec((tm, d), lambda i: (i, C_GS // D_MODEL)),
                  pl.BlockSpec((tm, d), lambda i: (i, C_GA // D_MODEL)),
                  pl.BlockSpec(w_sp.shape, const), pl.BlockSpec(w_ap.shape, const), pl.BlockSpec(w_o.shape, const)],
        out_specs=pl.BlockSpec((tm, d), lambda i: (i, 0)),
        out_shape=jax.ShapeDtypeStruct((r, d), F32),
        compiler_params=_params("parallel"),
        name="merge",
    )(h, y_ssm, y_att, proj, proj, w_sp, w_ap, w_o)


def _mlp_kernel(h_ref, nw_ref, wu_ref, wd_ref, o_ref):
    x = h_ref[...]
    ms = jnp.mean(x * x, axis=-1, keepdims=True)
    u = x * lax.rsqrt(ms + NORM_EPS) * nw_ref[...]
    a = _dot(u, wu_ref[...])
    a = jnp.square(jnp.maximum(a, 0.0))
    o_ref[...] = x + _dot(a, wd_ref[...])


def _mlp(h, norm_w, w_up, w_down):
    r, d = h.shape
    tm = _divisor_tile(r, 256, 16)
    const = lambda i: (0, 0)
    return pl.pallas_call(
        _mlp_kernel,
        grid=(r // tm,),
        in_specs=[pl.BlockSpec((tm, d), lambda i: (i, 0)), pl.BlockSpec((1, d), const),
                  pl.BlockSpec(w_up.shape, const), pl.BlockSpec(w_down.shape, const)],
        out_specs=pl.BlockSpec((tm, d), lambda i: (i, 0)),
        out_shape=jax.ShapeDtypeStruct((r, d), F32),
        compiler_params=_params("parallel"),
        name="mlp",
    )(h, norm_w.reshape(1, d), w_up, w_down)


def _softplus(x):
    return jnp.maximum(x, 0.0) + jnp.log1p(jnp.exp(-jnp.abs(x)))


def _conv_silu(xp_sc, x, w_ref, b_ref, t):
    xp_sc[pl.ds(8, t), :] = x
    acc = b_ref[...] + x * w_ref[CONV_W - 1:CONV_W, :]
    for k in range(CONV_W - 1):
        acc = acc + xp_sc[pl.ds(5 + k, t), :] * w_ref[k:k + 1, :]
    xp_sc[pl.ds(5, CONV_W - 1), :] = xp_sc[pl.ds(t + 5, CONV_W - 1), :]
    return acc * jax.nn.sigmoid(acc)


def _ssd_kernel(z_ref, xs_ref, bc_ref, sm_ref, h0_ref, pxs_ref, pbc_ref,
                cwx_ref, cbx_ref, cwb_ref, cbb_ref, dtb_ref, aneg_ref, dskip_ref, normw_ref, expand_ref,
                y_ref, hout_ref, state_sc, xpx_sc, xpb_sc, *, t_in, valid_len):
    t = SSD_CHUNK
    c = pl.program_id(1)

    @pl.when(c == 0)
    def _():
        state_sc[...] = h0_ref[...]
        xpx_sc[pl.ds(5, CONV_W - 1), :] = pxs_ref[...]
        xpb_sc[pl.ds(5, CONV_W - 1), :] = pbc_ref[...]

    def rows(ref):
        x = ref[...]
        if t_in < t:
            x = jnp.concatenate([x, jnp.zeros((t - t_in, x.shape[1]), x.dtype)], axis=0)
        return x

    xs = _conv_silu(xpx_sc, rows(xs_ref), cwx_ref, cbx_ref, t)
    bcm = _conv_silu(xpb_sc, rows(bc_ref), cwb_ref, cbb_ref, t)
    z = rows(z_ref)

    row = lax.broadcasted_iota(jnp.int32, (t, LANES), 0)
    n_valid = jnp.minimum(valid_len - c * t, t_in)
    dt = jnp.where(row < n_valid, _softplus(rows(sm_ref) + dtb_ref[...]), 0.0)
    a = dt * aneg_ref[...]
    ri = lax.broadcasted_iota(jnp.int32, (t, t), 0)
    ci = lax.broadcasted_iota(jnp.int32, (t, t), 1)
    causal = ri >= ci
    acum = _dot_exact(jnp.where(causal, 1.0, 0.0), a)
    acum_t = acum.T
    dtx = _dot_exact(dt, expand_ref[...])
    acx = _dot_exact(acum, expand_ref[...])
    last = acx[t - 1:t, :]
    xdt = xs * dtx
    xw = xdt * jnp.exp(last - acx)
    eacx = jnp.exp(acx)
    elast = jnp.exp(last)

    gw = D_INNER // SSM_GROUPS
    hpg = SSM_HEADS // SSM_GROUPS
    ys = []
    for g in range(SSM_GROUPS):
        bg = bcm[:, g * SSM_STATE:(g + 1) * SSM_STATE]
        cg = bcm[:, (SSM_GROUPS + g) * SSM_STATE:(SSM_GROUPS + g + 1) * SSM_STATE]
        cb = _dot_nt(cg, bg)
        h_t = state_sc[g]
        yg = [_dot(cg, h_t) * eacx[:, g * gw:(g + 1) * gw]]
        intra = []
        for e in range(hpg):
            hd = g * hpg + e
            seg = acum[:, SM_DT + hd:SM_DT + hd + 1] - acum_t[SM_DT + hd:SM_DT + hd + 1, :]
            m = cb * jnp.exp(jnp.where(causal, seg, -jnp.inf))
            intra.append(_dot(m, xdt[:, hd * SSM_HEAD_DIM:(hd + 1) * SSM_HEAD_DIM]))
        ys.append(yg[0] + jnp.concatenate(intra, axis=1))
        state_sc[g] = h_t * elast[:, g * gw:(g + 1) * gw] + _dot(bg.T, xw[:, g * gw:(g + 1) * gw])
    y = jnp.concatenate(ys, axis=1) + xs * dskip_ref[...]
    y = y * (z * jax.nn.sigmoid(z))
    outs = []
    for g in range(SSM_GROUPS):
        yg = y[:, g * gw:(g + 1) * gw]
        outs.append(yg * lax.rsqrt(jnp.mean(yg * yg, axis=-1, keepdims=True) + NORM_EPS))
    y = jnp.concatenate(outs, axis=1) * normw_ref[...]
    y_ref[...] = y[:t_in].astype(y_ref.dtype)

    @pl.when(c == pl.num_programs(1) - 1)
    def _():
        hout_ref[...] = state_sc[...]


def _ssd(proj, row0, n_seq, t_in, n_chunks, valid_len, h0_t, prefix, lw):
    assert row0 % t_in == 0
    rb0 = row0 // t_in
    rowblk = lambda w, col: pl.BlockSpec((t_in, w), lambda s, c: (rb0 + s * n_chunks + c, col // w))
    per_seq = lambda shape: pl.BlockSpec((None,) + shape, lambda s, c: (s,) + (0,) * len(shape))
    const = lambda arr: pl.BlockSpec(arr.shape, lambda s, c: (0,) * arr.ndim)
    gw = D_INNER // SSM_GROUPS
    pxs, pbc = prefix[:, :, :D_INNER], prefix[:, :, D_INNER:]
    consts = (lw["cwx"], lw["cbx"], lw["cwb"], lw["cbb"], lw["dtb"], lw["aneg"], lw["dskip"], lw["normw"], lw["expand"])
    y, h_t = pl.pallas_call(
        functools.partial(_ssd_kernel, t_in=t_in, valid_len=valid_len),
        grid=(n_seq, n_chunks),
        in_specs=[rowblk(D_INNER, C_Z), rowblk(D_INNER, C_XS), rowblk(BC_DIM, C_BC), rowblk(LANES, C_SM),
                  per_seq((SSM_GROUPS, SSM_STATE, gw)), per_seq((CONV_W - 1, D_INNER)), per_seq((CONV_W - 1, BC_DIM))]
                 + [const(a) for a in consts],
        out_specs=[pl.BlockSpec((t_in, D_INNER), lambda s, c: (s * n_chunks + c, 0)),
                   per_seq((SSM_GROUPS, SSM_STATE, gw))],
        out_shape=[jax.ShapeDtypeStruct((n_seq * n_chunks * t_in, D_INNER), F32),
                   jax.ShapeDtypeStruct((n_seq, SSM_GROUPS, SSM_STATE, gw), F32)],
        scratch_shapes=[pltpu.VMEM((SSM_GROUPS, SSM_STATE, gw), F32),
                        pltpu.VMEM((SSD_CHUNK + 8, D_INNER), F32),
                        pltpu.VMEM((SSD_CHUNK + 8, BC_DIM), F32)],
        compiler_params=_params("parallel", "arbitrary"),
        name="ssd",
    )(proj, proj, proj, proj, h0_t, pxs, pbc, *consts)
    return y, h_t


def _state_to_t(h):
    n = h.shape[0]
    hpg = SSM_HEADS // SSM_GROUPS
    return h.reshape(n, SSM_GROUPS, hpg, SSM_HEAD_DIM, SSM_STATE).transpose(0, 1, 4, 2, 3).reshape(
        n, SSM_GROUPS, SSM_STATE, hpg * SSM_HEAD_DIM)


def _state_from_t(h_t):
    n = h_t.shape[0]
    hpg = SSM_HEADS // SSM_GROUPS
    return h_t.reshape(n, SSM_GROUPS, SSM_STATE, hpg, SSM_HEAD_DIM).transpose(0, 1, 3, 4, 2).reshape(
        n, SSM_HEADS, SSM_HEAD_DIM, SSM_STATE)


def _sort_key(x):
    bits = lax.bitcast_convert_type(x, jnp.int32)
    return bits ^ ((bits >> 31) & 0x7FFFFFFF)


def _kth_largest_key(keys_sc, n_chunks, k, rows):
    width = keys_sc.shape[2]

    def bit_body(i, res):
        cand = res ^ lax.shift_left(jnp.int32(1), 31 - i)

        def cnt_body(c, cnt):
            hit = jnp.where(keys_sc[c] >= cand, 1.0, 0.0)
            for w in range(width // LANES):
                cnt = cnt + hit[:, w * LANES:(w + 1) * LANES]
            return cnt

        cnt = lax.fori_loop(0, n_chunks, cnt_body, jnp.zeros((rows, LANES), F32))
        return jnp.where(jnp.sum(cnt, axis=1, keepdims=True) >= k, cand, res)

    return lax.fori_loop(0, 32, bit_body, jnp.full((rows, 1), INT_MIN, jnp.int32))


def _count_greater(keys_sc, n_chunks, thr, rows):
    width = keys_sc.shape[2]

    def body(c, cnt):
        hit = jnp.where(keys_sc[c] > thr, 1.0, 0.0)
        for w in range(width // LANES):
            cnt = cnt + hit[:, w * LANES:(w + 1) * LANES]
        return cnt

    cnt = lax.fori_loop(0, n_chunks, body, jnp.zeros((rows, LANES), F32))
    return jnp.sum(cnt, axis=1, keepdims=True)


def _write_bias(keys_sc, store, n_chunks, thr, need, admissible):
    rows, width = keys_sc.shape[1], keys_sc.shape[2]
    ri = lax.broadcasted_iota(jnp.int32, (width, width), 0)
    ci = lax.broadcasted_iota(jnp.int32, (width, width), 1)
    upper = jnp.where(ri <= ci, 1.0, 0.0).astype(jnp.bfloat16)

    def body(c, run):
        key = keys_sc[c]
        eq = key == thr
        eqf = jnp.where(eq, 1.0, 0.0)
        rank = jnp.dot(eqf.astype(jnp.bfloat16), upper, preferred_element_type=F32) + run
        sel = admissible(c) & ((key > thr) | (eq & (rank <= need)))
        store(c, jnp.where(sel, 0.0, MASK_BIAS))
        return run + jnp.sum(eqf, axis=1, keepdims=True)

    lax.fori_loop(0, n_chunks, body, jnp.zeros((rows, 1), F32))


def _select(keys_sc, store, n_chunks, k, admissible):
    rows = keys_sc.shape[1]
    thr = _kth_largest_key(keys_sc, n_chunks, k, rows)
    need = k - _count_greater(keys_sc, n_chunks, thr, rows)
    _write_bias(keys_sc, store, n_chunks, thr, need, admissible)


def _softmax_update(m_sc, l_sc, acc_sc, idx, s, v):
    m_old = m_sc[idx]
    m_new = jnp.maximum(m_old, jnp.max(s, axis=1, keepdims=True))
    alpha = jnp.exp(m_old - m_new)
    p = jnp.exp(s - m_new)
    l_sc[idx] = alpha * l_sc[idx] + jnp.sum(p, axis=1, keepdims=True)
    acc_sc[idx] = alpha * acc_sc[idx] + _dot(p, v)
    m_sc[idx] = m_new


def _pattn_kernel(qi_ref, smq_ref, q_ref, smk_ref, k_ref, v_ref, o_ref,
                  keys_sc, bias_sc, m_sc, l_sc, acc_sc, *, kc, topk):
    j = pl.program_id(1)
    nq = Q_BLOCK
    n_chunks = ((j + 1) * nq - 1) // kc + 1
    qrow = j * nq + lax.broadcasted_iota(jnp.int32, (nq, kc), 0)
    lane = lax.broadcasted_iota(jnp.int32, (nq, kc), 1)

    def admissible(c):
        return (c * kc + lane) <= qrow

    qi = (qi_ref[...] * IDX_DIM ** -0.5).astype(MXU_DTYPE)
    qi_h = [qi[:, h * IDX_DIM:(h + 1) * IDX_DIM] for h in range(IDX_HEADS)]
    wi = smq_ref[...][:, SM_WI:SM_WI + IDX_HEADS] * IDX_HEADS ** -0.5

    def score_body(c, carry):
        kic = smk_ref[pl.ds(pl.multiple_of(c * kc, LANES), kc), :][:, SM_KI:SM_KI + IDX_DIM]
        acc = jnp.zeros((nq, kc), F32)
        for h in range(IDX_HEADS):
            acc = acc + jnp.maximum(_dot_nt(qi_h[h], kic), 0.0) * wi[:, h:h + 1]
        keys_sc[c] = _sort_key(jnp.where(admissible(c), acc, -jnp.inf))
        return carry

    lax.fori_loop(0, n_chunks, score_body, 0)

    def store(c, b):
        bias_sc[c] = b

    _select(keys_sc, store, n_chunks, topk, admissible)

    q = (q_ref[...] * HEAD_DIM ** -0.5).astype(MXU_DTYPE)
    q_g = [jnp.concatenate([q[:, (hk * ATT_GROUP + g) * HEAD_DIM:(hk * ATT_GROUP + g + 1) * HEAD_DIM]
                            for g in range(ATT_GROUP)], axis=0) for hk in range(ATT_KV_HEADS)]
    m_sc[...] = jnp.full(m_sc.shape, MASK_BIAS, F32)
    l_sc[...] = jnp.zeros(l_sc.shape, F32)
    acc_sc[...] = jnp.zeros(acc_sc.shape, F32)

    def att_body(c, carry):
        r0 = pl.multiple_of(c * kc, LANES)
        kch = k_ref[pl.ds(r0, kc), :].astype(MXU_DTYPE)
        vch = v_ref[pl.ds(r0, kc), :].astype(MXU_DTYPE)
        b = bias_sc[c]
        b4 = jnp.concatenate([b] * ATT_GROUP, axis=0)
        for hk in range(ATT_KV_HEADS):
            s = _dot_nt(q_g[hk], kch[:, hk * HEAD_DIM:(hk + 1) * HEAD_DIM]) + b4
            _softmax_update(m_sc, l_sc, acc_sc, hk, s, vch[:, hk * HEAD_DIM:(hk + 1) * HEAD_DIM])
        return carry

    lax.fori_loop(0, n_chunks, att_body, 0)

    outs = []
    for hk in range(ATT_KV_HEADS):
        o = acc_sc[hk] / l_sc[hk]
        outs += [o[g * nq:(g + 1) * nq] for g in range(ATT_GROUP)]
    o_ref[...] = jnp.concatenate(outs, axis=1).astype(o_ref.dtype)


def _prompt_attention(proj, n_seq, lp, topk):
    nqb = lp // Q_BLOCK
    kc = LANES * max(d for d in (1, 2, 3, 4) if nqb % d == 0)
    nch = lp // kc
    qblk = lambda w, col: pl.BlockSpec((Q_BLOCK, w), lambda b, j: (b * nqb + j, col // w))
    seqblk = lambda w, col: pl.BlockSpec((lp, w), lambda b, j: (b, col // w))
    rows = ATT_GROUP * Q_BLOCK
    return pl.pallas_call(
        functools.partial(_pattn_kernel, kc=kc, topk=topk),
        grid=(n_seq, nqb),
        in_specs=[qblk(QI_DIM, C_QI), qblk(LANES, C_SM), qblk(ATT_DIM, C_Q),
                  seqblk(LANES, C_SM), seqblk(KV_DIM, C_K), seqblk(KV_DIM, C_V)],
        out_specs=pl.BlockSpec((Q_BLOCK, ATT_DIM), lambda b, j: (b * nqb + j, 0)),
        out_shape=jax.ShapeDtypeStruct((n_seq * lp, ATT_DIM), F32),
        scratch_shapes=[pltpu.VMEM((nch, Q_BLOCK, kc), jnp.int32), pltpu.VMEM((nch, Q_BLOCK, kc), F32),
                        pltpu.VMEM((ATT_KV_HEADS, rows, 1), F32), pltpu.VMEM((ATT_KV_HEADS, rows, 1), F32),
                        pltpu.VMEM((ATT_KV_HEADS, rows, HEAD_DIM), F32)],
        compiler_params=_params("parallel", "arbitrary"),
        name="prompt_attention",
    )(proj, proj, proj, proj, proj, proj)


def _pad_rows(x, n):
    return jnp.concatenate([x, jnp.zeros((n - x.shape[0], x.shape[1]), x.dtype)], axis=0)


def _sselect_kernel(pt_ref, qi_ref, sm_ref, *rest, pages_per_step, n_pages, page, ns, topk):
    page_refs = rest[:pages_per_step]
    bias_ref, keys_sc = rest[pages_per_step:]
    g = pl.program_id(1)
    qi = (qi_ref[...] * IDX_DIM ** -0.5).astype(MXU_DTYPE)
    qs = jnp.concatenate([qi[:, h * IDX_DIM:(h + 1) * IDX_DIM] for h in range(IDX_HEADS)], axis=0)
    sm = sm_ref[...]
    wi = sm[:, SM_WI:SM_WI + IDX_HEADS] * IDX_HEADS ** -0.5

    def scores(keys):
        s = jnp.maximum(_dot_nt(qs, keys), 0.0)
        acc = jnp.zeros((ns, page), F32)
        for h in range(IDX_HEADS):
            acc = acc + s[h * ns:(h + 1) * ns] * wi[:, h:h + 1]
        return acc

    for i in range(pages_per_step):
        keys_sc[g * pages_per_step + i] = _sort_key(scores(page_refs[i][...]))

    @pl.when(g == pl.num_programs(1) - 1)
    def _():
        qrow = lax.broadcasted_iota(jnp.int32, (ns, page), 0)
        lane = lax.broadcasted_iota(jnp.int32, (ns, page), 1)
        new_ok = lane <= qrow
        s_new = scores(_pad_rows(sm[:, SM_KI:SM_KI + IDX_DIM], page))
        keys_sc[n_pages] = _sort_key(jnp.where(new_ok, s_new, -jnp.inf))

        def admissible(c):
            return (lane + (c - n_pages) * page) <= qrow

        def store(c, b):
            bias_ref[c] = b

        _select(keys_sc, store, n_pages + 1, topk, admissible)


def _sattn_kernel(pt_ref, q_ref, kn_ref, vn_ref, bias_ref, *rest, pages_per_step, n_pages, page, ns):
    k_refs = rest[:pages_per_step]
    v_refs = rest[pages_per_step:2 * pages_per_step]
    o_ref, m_sc, l_sc, acc_sc = rest[2 * pages_per_step:]
    g = pl.program_id(1)

    @pl.when(g == 0)
    def _():
        m_sc[...] = jnp.full(m_sc.shape, MASK_BIAS, F32)
        l_sc[...] = jnp.zeros(l_sc.shape, F32)
        acc_sc[...] = jnp.zeros(acc_sc.shape, F32)

    q = (q_ref[...] * HEAD_DIM ** -0.5).astype(MXU_DTYPE)
    q_g = [jnp.concatenate([q[:, (hk * ATT_GROUP + a) * HEAD_DIM:(hk * ATT_GROUP + a + 1) * HEAD_DIM]
                            for a in range(ATT_GROUP)], axis=0) for hk in range(ATT_KV_HEADS)]

    def attend(kch, vch, b):
        b4 = jnp.concatenate([b] * ATT_GROUP, axis=0)
        kch = kch.astype(MXU_DTYPE)
        vch = vch.astype(MXU_DTYPE)
        for hk in range(ATT_KV_HEADS):
            s = _dot_nt(q_g[hk], kch[:, hk * HEAD_DIM:(hk + 1) * HEAD_DIM]) + b4
            _softmax_update(m_sc, l_sc, acc_sc, hk, s, vch[:, hk * HEAD_DIM:(hk + 1) * HEAD_DIM])

    kcat = jnp.concatenate([r[...] for r in k_refs], axis=0)
    vcat = jnp.concatenate([r[...] for r in v_refs], axis=0)
    bcat = jnp.concatenate([bias_ref[g * pages_per_step + i] for i in range(pages_per_step)], axis=1)
    attend(kcat, vcat, bcat)

    @pl.when(g == pl.num_programs(1) - 1)
    def _():
        attend(_pad_rows(kn_ref[...], page), _pad_rows(vn_ref[...], page), bias_ref[n_pages])
        outs = []
        for hk in range(ATT_KV_HEADS):
            o = acc_sc[hk] / l_sc[hk]
            outs += [o[a * ns:(a + 1) * ns] for a in range(ATT_GROUP)]
        o_ref[...] = jnp.concatenate(outs, axis=1)


def _sample_attention(proj, row0, page_table, cache_k, cache_v, cache_kidx, layer, ns, topk):
    bd, n_pages = page_table.shape
    page = cache_kidx.shape[2]
    assert page == LANES and row0 % ns == 0
    pps = max(d for d in range(1, 9) if n_pages % d == 0)
    ng = n_pages // pps
    rb0 = row0 // ns
    rowblk = lambda w, col: pl.BlockSpec((ns, w), lambda s, g, pt: (rb0 + s, col // w))
    pageblk = lambda w, i: pl.BlockSpec((None, None, page, w), lambda s, g, pt: (layer, pt[s, g * pps + i], 0, 0))
    biasblk = pl.BlockSpec((None, n_pages + 1, ns, page), lambda s, g, pt: (s, 0, 0, 0))
    static = dict(pages_per_step=pps, n_pages=n_pages, page=page, ns=ns)

    bias = pl.pallas_call(
        functools.partial(_sselect_kernel, topk=topk, **static),
        grid_spec=pltpu.PrefetchScalarGridSpec(
            num_scalar_prefetch=1, grid=(bd, ng),
            in_specs=[rowblk(QI_DIM, C_QI), rowblk(LANES, C_SM)] + [pageblk(IDX_DIM, i) for i in range(pps)],
            out_specs=biasblk,
            scratch_shapes=[pltpu.VMEM((n_pages + 1, ns, page), jnp.int32)]),
        out_shape=jax.ShapeDtypeStruct((bd, n_pages + 1, ns, page), F32),
        compiler_params=_params("parallel", "arbitrary"),
        name="sample_select",
    )(page_table, proj, proj, *([cache_kidx] * pps))

    rows = ATT_GROUP * ns
    ck = cache_k.reshape(cache_k.shape[:3] + (KV_DIM,))
    cv = cache_v.reshape(cache_v.shape[:3] + (KV_DIM,))
    return pl.pallas_call(
        functools.partial(_sattn_kernel, **static),
        grid_spec=pltpu.PrefetchScalarGridSpec(
            num_scalar_prefetch=1, grid=(bd, ng),
            in_specs=[rowblk(ATT_DIM, C_Q), rowblk(KV_DIM, C_K), rowblk(KV_DIM, C_V), biasblk]
                     + [pageblk(KV_DIM, i) for i in range(pps)] * 2,
            out_specs=pl.BlockSpec((ns, ATT_DIM), lambda s, g, pt: (s, 0)),
            scratch_shapes=[pltpu.VMEM((ATT_KV_HEADS, rows, 1), F32), pltpu.VMEM((ATT_KV_HEADS, rows, 1), F32),
                            pltpu.VMEM((ATT_KV_HEADS, rows, HEAD_DIM), F32)]),
        out_shape=jax.ShapeDtypeStruct((bd * ns, ATT_DIM), F32),
        compiler_params=_params("parallel", "arbitrary"),
        name="sample_attention",
    )(page_table, proj, proj, proj, bias, *([ck] * pps), *([cv] * pps))


def _regroup_w_in(w):
    o_dt = D_INNER + D_INNER + BC_DIM
    o_q = o_dt + SSM_HEADS
    o_ki = o_q + ATT_DIM + 2 * KV_DIM + QI_DIM
    o_wi = o_ki + IDX_DIM
    o_g = o_wi + IDX_HEADS
    parts = [w[:, :o_dt], w[:, o_q:o_ki], w[:, o_g:o_g + 2 * D_MODEL], w[:, o_ki:o_wi], w[:, o_dt:o_q], w[:, o_wi:o_g]]
    used = sum(p.shape[1] for p in parts)
    parts.append(jnp.zeros((w.shape[0], PROJ_COLS - used), w.dtype))
    return jnp.concatenate(parts, axis=1).astype(MXU_DTYPE)


def _small_lanes(v):
    return jnp.zeros((1, LANES), F32).at[0, SM_DT:SM_DT + SSM_HEADS].set(v.astype(F32))


def _layer_consts(conv_w, conv_b, dt_bias, a_log, d_skip, ssm_norm_w):
    head_of_lane = jnp.arange(D_INNER) // SSM_HEAD_DIM
    expand = (jnp.arange(LANES)[:, None] == (SM_DT + head_of_lane)[None, :]).astype(F32)
    return dict(
        cwx=conv_w[:, :D_INNER], cbx=conv_b[None, :D_INNER], cwb=conv_w[:, D_INNER:], cbb=conv_b[None, D_INNER:],
        dtb=_small_lanes(dt_bias), aneg=_small_lanes(-jnp.exp(a_log.astype(F32))),
        dskip=jnp.repeat(d_skip.astype(F32), SSM_HEAD_DIM)[None, :], normw=ssm_norm_w.astype(F32)[None, :],
        expand=expand)


def kernel(x_prompt, x_sample, cache_k, cache_v, cache_kidx, state_ssm, state_conv, page_table, meta_tokens, norm1_w, w_in, conv_w, conv_b, dt_bias, a_log, d_skip, ssm_norm_w, w_ssm_proj, w_attn_proj, w_out, norm2_w, w_up, w_down, final_norm_w):
    b, seq, d = x_prompt.shape
    bd, ns = x_sample.shape[:2]
    depth = w_in.shape[0]
    assert d == D_MODEL and ns % 8 == 0
    past = page_table.shape[1] * cache_kidx.shape[2]
    lv = N_META + seq
    lp = -(-lv // Q_BLOCK) * Q_BLOCK
    rp = b * lp
    rs = bd * ns
    r = -(-(rp + rs) // LANES) * LANES
    topk_p = min(TOPK_MAX, seq // 4)
    topk_s = min(TOPK_MAX, (past + ns) // 4)

    hp = jnp.concatenate([jnp.broadcast_to(meta_tokens[None].astype(F32), (b, N_META, d)), x_prompt,
                          jnp.zeros((b, lp - lv, d), F32)], axis=1).reshape(rp, d)
    h = jnp.concatenate([hp, x_sample.reshape(rs, d), jnp.zeros((r - rp - rs, d), F32)], axis=0)

    zero_state = jnp.zeros((b, SSM_GROUPS, SSM_STATE, D_INNER // SSM_GROUPS), F32)
    zero_prefix = jnp.zeros((b, CONV_W - 1, D_INNER + BC_DIM), F32)
    outs = [[] for _ in range(10)]
    for l in range(depth):
        lw = _layer_consts(conv_w[l], conv_b[l], dt_bias[l], a_log[l], d_skip[l], ssm_norm_w[l])
        proj = _in_proj(_rmsnorm(h, norm1_w[l], MXU_DTYPE), _regroup_w_in(w_in[l]))

        yp, hp_t = _ssd(proj, 0, b, SSD_CHUNK, lp // SSD_CHUNK, lv, zero_state, zero_prefix, lw)
        ysm, hs_t = _ssd(proj, rp, bd, ns, 1, ns, _state_to_t(state_ssm[l]), state_conv[l], lw)
        ap = _prompt_attention(proj, b, lp, topk_p)
        asm = _sample_attention(proj, rp, page_table, cache_k, cache_v, cache_kidx, l, ns, topk_s)
        tail = r - rp - rs
        y_ssm = jnp.concatenate([yp, ysm, jnp.zeros((tail, D_INNER), F32)], axis=0).astype(MXU_DTYPE)
        y_att = jnp.concatenate([ap, asm, jnp.zeros((tail, ATT_DIM), F32)], axis=0).astype(MXU_DTYPE)

        h = _merge(h, y_ssm, y_att, proj, w_ssm_proj[l].astype(MXU_DTYPE), w_attn_proj[l].astype(MXU_DTYPE),
                   w_out[l].astype(MXU_DTYPE))
        h = _mlp(h, norm2_w[l], w_up[l].astype(MXU_DTYPE), w_down[l].astype(MXU_DTYPE))

        pp = proj[:rp].reshape(b, lp, PROJ_COLS)[:, :lv]
        ps = proj[rp:rp + rs].reshape(bd, ns, PROJ_COLS)
        for acc, t in zip(outs, (
                pp[..., C_K:C_K + KV_DIM].reshape(b, lv, ATT_KV_HEADS, HEAD_DIM),
                pp[..., C_V:C_V + KV_DIM].reshape(b, lv, ATT_KV_HEADS, HEAD_DIM),
                pp[..., C_SM + SM_KI:C_SM + SM_KI + IDX_DIM],
                _state_from_t(hp_t),
                pp[:, lv - (CONV_W - 1):, C_XS:C_XS + D_INNER + BC_DIM],
                ps[..., C_K:C_K + KV_DIM].reshape(bd, ns, ATT_KV_HEADS, HEAD_DIM),
                ps[..., C_V:C_V + KV_DIM].reshape(bd, ns, ATT_KV_HEADS, HEAD_DIM),
                ps[..., C_SM + SM_KI:C_SM + SM_KI + IDX_DIM],
                _state_from_t(hs_t),
                jnp.concatenate([state_conv[l], ps[..., C_XS:C_XS + D_INNER + BC_DIM]], axis=1)[:, -(CONV_W - 1):])):
            acc.append(t)

    y = _rmsnorm(h, final_norm_w, F32)
    y_prompt = y[:rp].reshape(b, lp, d)[:, N_META:lv]
    y_sample = y[rp:rp + rs].reshape(bd, ns, d)
    return (y_prompt, y_sample) + tuple(jnp.stack(o) for o in outs)
```

```python
import functools

import jax
import jax.numpy as jnp
from jax import lax
from jax.experimental import pallas as pl
from jax.experimental.pallas import tpu as pltpu

F32 = jnp.float32
MXU_DTYPE = jnp.bfloat16

N_META = 16
NORM_EPS = 1e-6
SSM_HEAD_DIM = 64
SSM_GROUPS = 4
SSM_STATE = 128
CONV_W = 4
ATT_HEADS = 16
ATT_KV_HEADS = 4
ATT_GROUP = ATT_HEADS // ATT_KV_HEADS
HEAD_DIM = 64
IDX_HEADS = 8
IDX_DIM = 64
TOPK_MAX = 256

LANES = 128
Q_BLOCK = 128
SSD_CHUNK = 128
VMEM_LIMIT = 56 << 20
MASK_BIAS = -1e30
INT_MIN = -(2 ** 31)

D_MODEL = 1024
D_INNER = 2 * D_MODEL
BC_DIM = 2 * SSM_GROUPS * SSM_STATE
SSM_HEADS = D_INNER // SSM_HEAD_DIM
ATT_DIM = ATT_HEADS * HEAD_DIM
KV_DIM = ATT_KV_HEADS * HEAD_DIM
QI_DIM = IDX_HEADS * IDX_DIM
C_Z = 0
C_XS = C_Z + D_INNER
C_BC = C_XS + D_INNER
C_Q = C_BC + BC_DIM
C_K = C_Q + ATT_DIM
C_V = C_K + KV_DIM
C_QI = C_V + KV_DIM
C_GS = C_QI + QI_DIM
C_GA = C_GS + D_MODEL
C_SM = C_GA + D_MODEL
SM_KI = 0
SM_DT = IDX_DIM
SM_WI = IDX_DIM + SSM_HEADS
PROJ_TN = 512
PROJ_COLS = -(-(C_SM + LANES) // PROJ_TN) * PROJ_TN


def _divisor_tile(n, max_tile, align):
    best = None
    for t in range(align, min(n, max_tile) + 1, align):
        if n % t == 0:
            best = t
    assert best is not None, (n, max_tile, align)
    return best


def _params(*sem):
    return pltpu.CompilerParams(dimension_semantics=sem, vmem_limit_bytes=VMEM_LIMIT)


def _dot(a, b):
    return jnp.dot(a.astype(MXU_DTYPE), b.astype(MXU_DTYPE), preferred_element_type=F32)


def _dot_nt(a, b):
    return lax.dot_general(a.astype(MXU_DTYPE), b.astype(MXU_DTYPE), (((1,), (1,)), ((), ())),
                           preferred_element_type=F32)


def _dot_exact(a, b):
    return jnp.dot(a, b, preferred_element_type=F32, precision=lax.Precision.HIGHEST)


def _rmsnorm_kernel(x_ref, w_ref, o_ref):
    x = x_ref[...]
    ms = jnp.mean(x * x, axis=-1, keepdims=True)
    o_ref[...] = (x * lax.rsqrt(ms + NORM_EPS) * w_ref[...]).astype(o_ref.dtype)


def _rmsnorm(x, w, out_dtype):
    r, d = x.shape
    tm = _divisor_tile(r, 512, 16)
    return pl.pallas_call(
        _rmsnorm_kernel,
        grid=(r // tm,),
        in_specs=[pl.BlockSpec((tm, d), lambda i: (i, 0)), pl.BlockSpec((1, d), lambda i: (0, 0))],
        out_specs=pl.BlockSpec((tm, d), lambda i: (i, 0)),
        out_shape=jax.ShapeDtypeStruct((r, d), out_dtype),
        compiler_params=_params("parallel"),
        name="rmsnorm",
    )(x, w.reshape(1, d))


def _matmul_kernel(x_ref, w_ref, o_ref):
    o_ref[...] = jnp.dot(x_ref[...], w_ref[...], preferred_element_type=F32)


def _in_proj(u, w):
    r, k = u.shape
    n = w.shape[1]
    tm = _divisor_tile(r, 1100, 16)
    return pl.pallas_call(
        _matmul_kernel,
        grid=(n // PROJ_TN, r // tm),
        in_specs=[pl.BlockSpec((tm, k), lambda j, i: (i, 0)), pl.BlockSpec((k, PROJ_TN), lambda j, i: (0, j))],
        out_specs=pl.BlockSpec((tm, PROJ_TN), lambda j, i: (i, j)),
        out_shape=jax.ShapeDtypeStruct((r, n), F32),
        compiler_params=_params("parallel", "parallel"),
        name="in_proj",
    )(u, w)


def _merge_kernel(h_ref, ys_ref, ya_ref, gs_ref, ga_ref, wsp_ref, wap_ref, wo_ref, o_ref):
    a = jnp.dot(ys_ref[...], wsp_ref[...], preferred_element_type=F32)
    b = jnp.dot(ya_ref[...], wap_ref[...], preferred_element_type=F32)
    m = jax.nn.sigmoid(gs_ref[...]) * a + jax.nn.sigmoid(ga_ref[...]) * b
    o_ref[...] = h_ref[...] + _dot(m, wo_ref[...])


def _merge(h, y_ssm, y_att, proj, w_sp, w_ap, w_o):
    r, d = h.shape
    tm = _divisor_tile(r, 256, 16)
    const = lambda i: (0, 0)
    return pl.pallas_call(
        _merge_kernel,
        grid=(r // tm,),
        in_specs=[pl.BlockSpec((tm, d), lambda i: (i, 0)),
                  pl.BlockSpec((tm, D_INNER), lambda i: (i, 0)),
                  pl.BlockSpec((tm, ATT_DIM), lambda i: (i, 0)),
                  pl.BlockSpec((tm, d), lambda i: (i, C_GS // D_MODEL)),
                  pl.BlockSpec((tm, d), lambda i: (i, C_GA // D_MODEL)),
                  pl.BlockSpec(w_sp.shape, const), pl.BlockSpec(w_ap.shape, const), pl.BlockSpec(w_o.shape, const)],
        out_specs=pl.BlockSpec((tm, d), lambda i: (i, 0)),
        out_shape=jax.ShapeDtypeStruct((r, d), F32),
        compiler_params=_params("parallel"),
        name="merge",
    )(h, y_ssm, y_att, proj, proj, w_sp, w_ap, w_o)


def _mlp_kernel(h_ref, nw_ref, wu_ref, wd_ref, o_ref):
    x = h_ref[...]
    ms = jnp.mean(x * x, axis=-1, keepdims=True)
    u = x * lax.rsqrt(ms + NORM_EPS) * nw_ref[...]
    a = _dot(u, wu_ref[...])
    a = jnp.square(jnp.maximum(a, 0.0))
    o_ref[...] = x + _dot(a, wd_ref[...])


def _mlp(h, norm_w, w_up, w_down):
    r, d = h.shape
    tm = _divisor_tile(r, 256, 16)
    const = lambda i: (0, 0)
    return pl.pallas_call(
        _mlp_kernel,
        grid=(r // tm,),
        in_specs=[pl.BlockSpec((tm, d), lambda i: (i, 0)), pl.BlockSpec((1, d), const),
                  pl.BlockSpec(w_up.shape, const), pl.BlockSpec(w_down.shape, const)],
        out_specs=pl.BlockSpec((tm, d), lambda i: (i, 0)),
        out_shape=jax.ShapeDtypeStruct((r, d), F32),
        compiler_params=_params("parallel"),
        name="mlp",
    )(h, norm_w.reshape(1, d), w_up, w_down)


def _softplus(x):
    return jnp.maximum(x, 0.0) + jnp.log1p(jnp.exp(-jnp.abs(x)))


def _conv_silu(xp_sc, x, w_ref, b_ref, t):
    xp_sc[pl.ds(8, t), :] = x
    acc = b_ref[...] + x * w_ref[CONV_W - 1:CONV_W, :]
    for k in range(CONV_W - 1):
        acc = acc + xp_sc[pl.ds(5 + k, t), :] * w_ref[k:k + 1, :]
    xp_sc[pl.ds(5, CONV_W - 1), :] = xp_sc[pl.ds(t + 5, CONV_W - 1), :]
    return acc * jax.nn.sigmoid(acc)


def _ssd_kernel(z_ref, xs_ref, bc_ref, sm_ref, h0_ref, pxs_ref, pbc_ref,
                cwx_ref, cbx_ref, cwb_ref, cbb_ref, dtb_ref, aneg_ref, dskip_ref, normw_ref, expand_ref,
                y_ref, hout_ref, state_sc, xpx_sc, xpb_sc, *, t_in, valid_len):
    t = SSD_CHUNK
    c = pl.program_id(1)

    @pl.when(c == 0)
    def _():
        state_sc[...] = h0_ref[...]
        xpx_sc[pl.ds(5, CONV_W - 1), :] = pxs_ref[...]
        xpb_sc[pl.ds(5, CONV_W - 1), :] = pbc_ref[...]

    def rows(ref):
        x = ref[...]
        if t_in < t:
            x = jnp.concatenate([x, jnp.zeros((t - t_in, x.shape[1]), x.dtype)], axis=0)
        return x

    xs = _conv_silu(xpx_sc, rows(xs_ref), cwx_ref, cbx_ref, t)
    bcm = _conv_silu(xpb_sc, rows(bc_ref), cwb_ref, cbb_ref, t)
    z = rows(z_ref)

    row = lax.broadcasted_iota(jnp.int32, (t, LANES), 0)
    n_valid = jnp.minimum(valid_len - c * t, t_in)
    dt = jnp.where(row < n_valid, _softplus(rows(sm_ref) + dtb_ref[...]), 0.0)
    a = dt * aneg_ref[...]
    ri = lax.broadcasted_iota(jnp.int32, (t, t), 0)
    ci = lax.broadcasted_iota(jnp.int32, (t, t), 1)
    causal = ri >= ci
    acum = _dot_exact(jnp.where(causal, 1.0, 0.0), a)
    acum_t = acum.T
    dtx = _dot_exact(dt, expand_ref[...])
    acx = _dot_exact(acum, expand_ref[...])
    last = acx[t - 1:t, :]
    xdt = xs * dtx
    xw = xdt * jnp.exp(last - acx)
    eacx = jnp.exp(acx)
    elast = jnp.exp(last)

    gw = D_INNER // SSM_GROUPS
    hpg = SSM_HEADS // SSM_GROUPS
    ys = []
    for g in range(SSM_GROUPS):
        bg = bcm[:, g * SSM_STATE:(g + 1) * SSM_STATE]
        cg = bcm[:, (SSM_GROUPS + g) * SSM_STATE:(SSM_GROUPS + g + 1) * SSM_STATE]
        cb = _dot_nt(cg, bg)
        h_t = state_sc[g]
        yg = [_dot(cg, h_t) * eacx[:, g * gw:(g + 1) * gw]]
        intra = []
        for e in range(hpg):
            hd = g * hpg + e
            seg = acum[:, SM_DT + hd:SM_DT + hd + 1] - acum_t[SM_DT + hd:SM_DT + hd + 1, :]
            m = cb * jnp.exp(jnp.where(causal, seg, -jnp.inf))
            intra.append(_dot(m, xdt[:, hd * SSM_HEAD_DIM:(hd + 1) * SSM_HEAD_DIM]))
        ys.append(yg[0] + jnp.concatenate(intra, axis=1))
        state_sc[g] = h_t * elast[:, g * gw:(g + 1) * gw] + _dot(bg.T, xw[:, g * gw:(g + 1) * gw])
    y = jnp.concatenate(ys, axis=1) + xs * dskip_ref[...]
    y = y * (z * jax.nn.sigmoid(z))
    outs = []
    for g in range(SSM_GROUPS):
        yg = y[:, g * gw:(g + 1) * gw]
        outs.append(yg * lax.rsqrt(jnp.mean(yg * yg, axis=-1, keepdims=True) + NORM_EPS))
    y = jnp.concatenate(outs, axis=1) * normw_ref[...]
    y_ref[...] = y[:t_in].astype(y_ref.dtype)

    @pl.when(c == pl.num_programs(1) - 1)
    def _():
        hout_ref[...] = state_sc[...]


def _ssd(proj, row0, n_seq, t_in, n_chunks, valid_len, h0_t, prefix, lw, out_dtype):
    assert row0 % t_in == 0
    rb0 = row0 // t_in
    rowblk = lambda w, col: pl.BlockSpec((t_in, w), lambda s, c: (rb0 + s * n_chunks + c, col // w))
    per_seq = lambda shape: pl.BlockSpec((None,) + shape, lambda s, c: (s,) + (0,) * len(shape))
    const = lambda arr: pl.BlockSpec(arr.shape, lambda s, c: (0,) * arr.ndim)
    gw = D_INNER // SSM_GROUPS
    pxs, pbc = prefix[:, :, :D_INNER], prefix[:, :, D_INNER:]
    consts = (lw["cwx"], lw["cbx"], lw["cwb"], lw["cbb"], lw["dtb"], lw["aneg"], lw["dskip"], lw["normw"], lw["expand"])
    y, h_t = pl.pallas_call(
        functools.partial(_ssd_kernel, t_in=t_in, valid_len=valid_len),
        grid=(n_seq, n_chunks),
        in_specs=[rowblk(D_INNER, C_Z), rowblk(D_INNER, C_XS), rowblk(BC_DIM, C_BC), rowblk(LANES, C_SM),
                  per_seq((SSM_GROUPS, SSM_STATE, gw)), per_seq((CONV_W - 1, D_INNER)), per_seq((CONV_W - 1, BC_DIM))]
                 + [const(a) for a in consts],
        out_specs=[pl.BlockSpec((t_in, D_INNER), lambda s, c: (s * n_chunks + c, 0)),
                   per_seq((SSM_GROUPS, SSM_STATE, gw))],
        out_shape=[jax.ShapeDtypeStruct((n_seq * n_chunks * t_in, D_INNER), out_dtype),
                   jax.ShapeDtypeStruct((n_seq, SSM_GROUPS, SSM_STATE, gw), F32)],
        scratch_shapes=[pltpu.VMEM((SSM_GROUPS, SSM_STATE, gw), F32),
                        pltpu.VMEM((SSD_CHUNK + 8, D_INNER), F32),
                        pltpu.VMEM((SSD_CHUNK + 8, BC_DIM), F32)],
        compiler_params=_params("parallel", "arbitrary"),
        name="ssd",
    )(proj, proj, proj, proj, h0_t, pxs, pbc, *consts)
    return y, h_t


def _state_to_t(h):
    n = h.shape[0]
    hpg = SSM_HEADS // SSM_GROUPS
    return h.reshape(n, SSM_GROUPS, hpg, SSM_HEAD_DIM, SSM_STATE).transpose(0, 1, 4, 2, 3).reshape(
        n, SSM_GROUPS, SSM_STATE, hpg * SSM_HEAD_DIM)


def _state_from_t(h_t):
    n = h_t.shape[0]
    hpg = SSM_HEADS // SSM_GROUPS
    return h_t.reshape(n, SSM_GROUPS, SSM_STATE, hpg, SSM_HEAD_DIM).transpose(0, 1, 3, 4, 2).reshape(
        n, SSM_HEADS, SSM_HEAD_DIM, SSM_STATE)


def _sort_key(x):
    bits = lax.bitcast_convert_type(x, jnp.int32)
    bits = jnp.where(bits == INT_MIN, 0, bits)
    return bits ^ ((bits >> 31) & 0x7FFFFFFF)


def _fold_rows(x):
    x = x.reshape(x.shape[0] // 8, 8, x.shape[1])
    while x.shape[0] > 1:
        half, odd = divmod(x.shape[0], 2)
        folded = x[:half] + x[half:2 * half]
        if odd:
            folded = jnp.concatenate([folded[:1] + x[2 * half:], folded[1:]], axis=0) if half > 1 else folded + x[2 * half:]
        x = folded
    return x[0]


def _count_t(keys_sc, n_chunks, pred):
    def body(c, cnt):
        return cnt + _fold_rows(jnp.where(pred(keys_sc[c]), 1.0, 0.0))

    cnt = lax.fori_loop(0, n_chunks, body, jnp.zeros((8, LANES), F32))
    return jnp.sum(cnt, axis=0, keepdims=True)


def _kth_key_t(keys_sc, n_chunks, k):
    def bit_body(i, res):
        cand = res ^ lax.shift_left(jnp.int32(1), 31 - i)
        return jnp.where(_count_t(keys_sc, n_chunks, lambda key: key >= cand) >= k, cand, res)

    return lax.fori_loop(0, 32, bit_body, jnp.full((1, LANES), INT_MIN, jnp.int32))


def _select_t(keys_sc, bias_sc, n_chunks, k):
    kc = keys_sc.shape[1]
    thr = _kth_key_t(keys_sc, n_chunks, k)
    n_ge = _count_t(keys_sc, n_chunks, lambda key: key >= thr)
    surplus = jnp.max(jnp.where((n_ge > k) & (thr != INT_MIN), 1.0, 0.0))

    @pl.when(surplus == 0.0)
    def _():
        floor = jnp.maximum(thr, INT_MIN + 1)

        def body(c, carry):
            bias_sc[c] = jnp.where(keys_sc[c] >= floor, 0.0, MASK_BIAS)
            return carry

        lax.fori_loop(0, n_chunks, body, 0)

    @pl.when(surplus != 0.0)
    def _():
        need = k - _count_t(keys_sc, n_chunks, lambda key: key > thr)
        ri = lax.broadcasted_iota(jnp.int32, (kc, kc), 0)
        ci = lax.broadcasted_iota(jnp.int32, (kc, kc), 1)
        lower = jnp.where(ci <= ri, 1.0, 0.0).astype(jnp.bfloat16)

        def body(c, run):
            key = keys_sc[c]
            eqf = jnp.where((key == thr) & (key != INT_MIN), 1.0, 0.0)
            rank = jnp.dot(lower, eqf.astype(jnp.bfloat16), preferred_element_type=F32) + run
            sel = (key > thr) | ((eqf > 0.0) & (rank <= need))
            bias_sc[c] = jnp.where(sel, 0.0, MASK_BIAS)
            return run + jnp.sum(_fold_rows(eqf), axis=0, keepdims=True)

        lax.fori_loop(0, n_chunks, body, jnp.zeros((1, LANES), F32))


def _select_s(keys_sc, bias_ref, k):
    n_chunks, rows, width = keys_sc.shape

    def count(pred):
        return jnp.sum(jnp.sum(jnp.where(pred(keys_sc[...]), 1.0, 0.0), axis=0), axis=1, keepdims=True)

    def bit_body(i, res):
        cand = res ^ lax.shift_left(jnp.int32(1), 31 - i)
        return jnp.where(count(lambda key: key >= cand[None]) >= k, cand, res)

    thr = lax.fori_loop(0, 32, bit_body, jnp.full((rows, 1), INT_MIN, jnp.int32))
    n_ge = count(lambda key: key >= thr[None])
    surplus = jnp.max(jnp.where((n_ge > k) & (thr != INT_MIN), 1.0, 0.0))

    @pl.when(surplus == 0.0)
    def _():
        floor = jnp.maximum(thr, INT_MIN + 1)
        bias_ref[...] = jnp.where(keys_sc[...] >= floor[None], 0.0, MASK_BIAS)

    @pl.when(surplus != 0.0)
    def _():
        need = k - count(lambda key: key > thr[None])
        ri = lax.broadcasted_iota(jnp.int32, (width, width), 0)
        ci = lax.broadcasted_iota(jnp.int32, (width, width), 1)
        upper = jnp.where(ri <= ci, 1.0, 0.0).astype(jnp.bfloat16)

        def body(c, run):
            key = keys_sc[c]
            eqf = jnp.where((key == thr) & (key != INT_MIN), 1.0, 0.0)
            rank = jnp.dot(eqf.astype(jnp.bfloat16), upper, preferred_element_type=F32) + run
            sel = (key > thr) | ((eqf > 0.0) & (rank <= need))
            bias_ref[c] = jnp.where(sel, 0.0, MASK_BIAS)
            return run + jnp.sum(eqf, axis=1, keepdims=True)

        lax.fori_loop(0, n_chunks, body, jnp.zeros((rows, 1), F32))


def _pattn_kernel(qi_ref, smq_ref, q_ref, smk_ref, k_ref, v_ref, o_ref,
                  kb_sc, kib_sc, vt_sc, keys_sc, bias_sc, qz_sc, m_sc, l_sc, acc_sc, *, kc, lp, topk):
    j = pl.program_id(1)
    nq = Q_BLOCK
    nkc = kb_sc.shape[0]

    @pl.when(j == 0)
    def _():
        for c in range(nkc):
            r0 = c * kc
            n = min(kc, lp - r0)

            def chunk(ref):
                x = ref[r0:r0 + n, :]
                return x if n == kc else jnp.concatenate([x, jnp.zeros((kc - n, x.shape[1]), x.dtype)], axis=0)

            kb_sc[c] = chunk(k_ref).astype(MXU_DTYPE)
            kib_sc[c] = chunk(smk_ref).astype(MXU_DTYPE)
            vt_sc[c] = chunk(v_ref).T.astype(MXU_DTYPE)

    n_chunks = ((j + 1) * nq - 1) // kc + 1

    qi_t = (qi_ref[...] * IDX_DIM ** -0.5).T
    qiz = jnp.concatenate([qi_t[h * IDX_DIM:(h + 1) * IDX_DIM] for h in range(IDX_HEADS)], axis=1)
    qiz = jnp.concatenate([qiz, jnp.zeros((LANES - IDX_DIM, IDX_HEADS * nq), F32)], axis=0).astype(MXU_DTYPE)
    smq_t = smq_ref[...].T
    wi_row = jnp.concatenate([smq_t[SM_WI + h:SM_WI + h + 1] for h in range(IDX_HEADS)], axis=1) * IDX_HEADS ** -0.5
    kpos = lax.broadcasted_iota(jnp.int32, (kc, nq), 0)
    qpos = j * nq + lax.broadcasted_iota(jnp.int32, (kc, nq), 1)

    def score_body(c, carry):
        s = jnp.maximum(jnp.dot(kib_sc[c], qiz, preferred_element_type=F32), 0.0) * wi_row
        acc = s[:, :nq]
        for h in range(1, IDX_HEADS):
            acc = acc + s[:, h * nq:(h + 1) * nq]
        keys_sc[c] = jnp.where(c * kc + kpos <= qpos, _sort_key(acc), INT_MIN)
        return carry

    lax.fori_loop(0, n_chunks, score_body, 0)
    _select_t(keys_sc, bias_sc, n_chunks, topk)

    q_t = (q_ref[...] * HEAD_DIM ** -0.5).T
    gw = ATT_GROUP * nq
    zero = jnp.zeros((HEAD_DIM, gw), F32)
    for hk in range(ATT_KV_HEADS):
        own = jnp.concatenate([q_t[(hk * ATT_GROUP + g) * HEAD_DIM:(hk * ATT_GROUP + g + 1) * HEAD_DIM]
                               for g in range(ATT_GROUP)], axis=1)
        qz_sc[hk] = jnp.concatenate([own if i == hk else zero for i in range(ATT_KV_HEADS)],
                                    axis=0).astype(MXU_DTYPE)
    m_sc[...] = jnp.full(m_sc.shape, MASK_BIAS, F32)
    l_sc[...] = jnp.zeros(l_sc.shape, F32)
    acc_sc[...] = jnp.zeros(acc_sc.shape, F32)

    def att_body(c, carry):
        kch = kb_sc[c]
        vt = vt_sc[c]
        b = bias_sc[c]
        b4 = jnp.concatenate([b] * ATT_GROUP, axis=1)
        for hk in range(ATT_KV_HEADS):
            s = jnp.dot(kch, qz_sc[hk], preferred_element_type=F32) + b4
            m_old = m_sc[hk]
            m_new = jnp.maximum(m_old, jnp.max(s, axis=0, keepdims=True))
            alpha = jnp.exp(m_old - m_new)
            p = jnp.exp(s - m_new)
            l_sc[hk] = alpha * l_sc[hk] + jnp.sum(p, axis=0, keepdims=True)
            acc_sc[hk] = alpha * acc_sc[hk] + jnp.dot(vt[hk * HEAD_DIM:(hk + 1) * HEAD_DIM], p.astype(MXU_DTYPE),
                                                      preferred_element_type=F32)
            m_sc[hk] = m_new
        return carry

    lax.fori_loop(0, n_chunks, att_body, 0)

    heads = []
    for hk in range(ATT_KV_HEADS):
        o = acc_sc[hk] / l_sc[hk]
        heads += [o[:, g * nq:(g + 1) * nq] for g in range(ATT_GROUP)]
    o_ref[...] = jnp.concatenate(heads, axis=0).T.astype(o_ref.dtype)


def _prompt_attention(proj, n_seq, lp, topk):
    nqb = lp // Q_BLOCK
    kc = 512
    nkc = -(-lp // kc)
    qblk = lambda w, col: pl.BlockSpec((Q_BLOCK, w), lambda b, j: (b * nqb + j, col // w))
    seqblk = lambda w, col: pl.BlockSpec((lp, w), lambda b, j: (b, col // w))
    gw = ATT_GROUP * Q_BLOCK
    return pl.pallas_call(
        functools.partial(_pattn_kernel, kc=kc, lp=lp, topk=topk),
        grid=(n_seq, nqb),
        in_specs=[qblk(QI_DIM, C_QI), qblk(LANES, C_SM), qblk(ATT_DIM, C_Q),
                  seqblk(LANES, C_SM), seqblk(KV_DIM, C_K), seqblk(KV_DIM, C_V)],
        out_specs=pl.BlockSpec((Q_BLOCK, ATT_DIM), lambda b, j: (b * nqb + j, 0)),
        out_shape=jax.ShapeDtypeStruct((n_seq * lp, ATT_DIM), MXU_DTYPE),
        scratch_shapes=[pltpu.VMEM((nkc, kc, KV_DIM), MXU_DTYPE), pltpu.VMEM((nkc, kc, LANES), MXU_DTYPE),
                        pltpu.VMEM((nkc, KV_DIM, kc), MXU_DTYPE),
                        pltpu.VMEM((nkc, kc, Q_BLOCK), jnp.int32), pltpu.VMEM((nkc, kc, Q_BLOCK), F32),
                        pltpu.VMEM((ATT_KV_HEADS, KV_DIM, gw), MXU_DTYPE),
                        pltpu.VMEM((ATT_KV_HEADS, 1, gw), F32), pltpu.VMEM((ATT_KV_HEADS, 1, gw), F32),
                        pltpu.VMEM((ATT_KV_HEADS, HEAD_DIM, gw), F32)],
        compiler_params=_params("parallel", "arbitrary"),
        name="prompt_attention",
    )(proj, proj, proj, proj, proj, proj)


def _pad_rows(x, n):
    return jnp.concatenate([x, jnp.zeros((n - x.shape[0], x.shape[1]), x.dtype)], axis=0)


def _sselect_kernel(pt_ref, qi_ref, sm_ref, *rest, pages_per_step, n_pages, page, ns, topk):
    page_refs = rest[:pages_per_step]
    bias_ref, keys_sc = rest[pages_per_step:]
    g = pl.program_id(1)
    qi = (qi_ref[...] * IDX_DIM ** -0.5).astype(MXU_DTYPE)
    qs = jnp.concatenate([qi[:, h * IDX_DIM:(h + 1) * IDX_DIM] for h in range(IDX_HEADS)], axis=0)
    sm = sm_ref[...]
    wi = sm[:, SM_WI:SM_WI + IDX_HEADS] * IDX_HEADS ** -0.5

    def scores(keys):
        s = jnp.maximum(_dot_nt(qs, keys), 0.0)
        acc = s[:ns] * wi[:, 0:1]
        for h in range(1, IDX_HEADS):
            acc = acc + s[h * ns:(h + 1) * ns] * wi[:, h:h + 1]
        return acc

    for i in range(pages_per_step):
        keys_sc[g * pages_per_step + i] = _sort_key(scores(page_refs[i][...]))

    @pl.when(g == pl.num_programs(1) - 1)
    def _():
        qrow = lax.broadcasted_iota(jnp.int32, (ns, page), 0)
        lane = lax.broadcasted_iota(jnp.int32, (ns, page), 1)
        s_new = scores(_pad_rows(sm[:, SM_KI:SM_KI + IDX_DIM], page))
        keys_sc[n_pages] = jnp.where(lane <= qrow, _sort_key(s_new), INT_MIN)
        _select_s(keys_sc, bias_ref, topk)


def _sattn_kernel(pt_ref, q_ref, kn_ref, vn_ref, bias_ref, *rest, pages_per_step, n_pages, page, ns):
    k_refs = rest[:pages_per_step]
    v_refs = rest[pages_per_step:2 * pages_per_step]
    o_ref, qbd_sc, m_sc, l_sc, acc_sc = rest[2 * pages_per_step:]
    g = pl.program_id(1)

    @pl.when(g == 0)
    def _():
        q = q_ref[...] * HEAD_DIM ** -0.5
        rows = []
        for h in range(ATT_HEADS):
            hk = h // ATT_GROUP
            parts = [jnp.zeros((ns, HEAD_DIM), F32)] * ATT_KV_HEADS
            parts[hk] = q[:, h * HEAD_DIM:(h + 1) * HEAD_DIM]
            rows.append(jnp.concatenate(parts, axis=1))
        qbd_sc[...] = jnp.concatenate(rows, axis=0).astype(MXU_DTYPE)
        m_sc[...] = jnp.full(m_sc.shape, MASK_BIAS, F32)
        l_sc[...] = jnp.zeros(l_sc.shape, F32)
        acc_sc[...] = jnp.zeros(acc_sc.shape, F32)

    def attend(kch, vch, b):
        s = _dot_nt(qbd_sc[...], kch) + jnp.concatenate([b] * ATT_HEADS, axis=0)
        m_old = m_sc[...]
        m_new = jnp.maximum(m_old, jnp.max(s, axis=1, keepdims=True))
        alpha = jnp.exp(m_old - m_new)
        p = jnp.exp(s - m_new)
        l_sc[...] = alpha * l_sc[...] + jnp.sum(p, axis=1, keepdims=True)
        acc_sc[...] = alpha * acc_sc[...] + _dot(p, vch)
        m_sc[...] = m_new

    kcat = jnp.concatenate([r[...] for r in k_refs], axis=0)
    vcat = jnp.concatenate([r[...] for r in v_refs], axis=0)
    bcat = jnp.concatenate([bias_ref[g * pages_per_step + i] for i in range(pages_per_step)], axis=1)
    attend(kcat, vcat, bcat)

    @pl.when(g == pl.num_programs(1) - 1)
    def _():
        attend(_pad_rows(kn_ref[...], page), _pad_rows(vn_ref[...], page), bias_ref[n_pages])
        o = acc_sc[...] / l_sc[...]
        o_ref[...] = jnp.concatenate(
            [o[h * ns:(h + 1) * ns, (h // ATT_GROUP) * HEAD_DIM:(h // ATT_GROUP + 1) * HEAD_DIM]
             for h in range(ATT_HEADS)], axis=1)


def _sample_attention(proj, row0, page_table, cache_k, cache_v, cache_kidx, layer, ns, topk):
    bd, n_pages = page_table.shape
    page = cache_kidx.shape[2]
    assert page == LANES and row0 % ns == 0
    pps = max(d for d in range(1, 9) if n_pages % d == 0)
    ng = n_pages // pps
    rb0 = row0 // ns
    rowblk = lambda w, col: pl.BlockSpec((ns, w), lambda s, g, pt: (rb0 + s, col // w))
    pageblk = lambda w, i: pl.BlockSpec((None, None, page, w), lambda s, g, pt: (layer, pt[s, g * pps + i], 0, 0))
    biasblk = pl.BlockSpec((None, n_pages + 1, ns, page), lambda s, g, pt: (s, 0, 0, 0))
    static = dict(pages_per_step=pps, n_pages=n_pages, page=page, ns=ns)

    bias = pl.pallas_call(
        functools.partial(_sselect_kernel, topk=topk, **static),
        grid_spec=pltpu.PrefetchScalarGridSpec(
            num_scalar_prefetch=1, grid=(bd, ng),
            in_specs=[rowblk(QI_DIM, C_QI), rowblk(LANES, C_SM)] + [pageblk(IDX_DIM, i) for i in range(pps)],
            out_specs=biasblk,
            scratch_shapes=[pltpu.VMEM((n_pages + 1, ns, page), jnp.int32)]),
        out_shape=jax.ShapeDtypeStruct((bd, n_pages + 1, ns, page), F32),
        compiler_params=_params("parallel", "arbitrary"),
        name="sample_select",
    )(page_table, proj, proj, *([cache_kidx] * pps))

    rows = ATT_HEADS * ns
    ck = cache_k.reshape(cache_k.shape[:3] + (KV_DIM,))
    cv = cache_v.reshape(cache_v.shape[:3] + (KV_DIM,))
    return pl.pallas_call(
        functools.partial(_sattn_kernel, **static),
        grid_spec=pltpu.PrefetchScalarGridSpec(
            num_scalar_prefetch=1, grid=(bd, ng),
            in_specs=[rowblk(ATT_DIM, C_Q), rowblk(KV_DIM, C_K), rowblk(KV_DIM, C_V), biasblk]
                     + [pageblk(KV_DIM, i) for i in range(pps)] * 2,
            out_specs=pl.BlockSpec((ns, ATT_DIM), lambda s, g, pt: (s, 0)),
            scratch_shapes=[pltpu.VMEM((rows, KV_DIM), MXU_DTYPE), pltpu.VMEM((rows, 1), F32),
                            pltpu.VMEM((rows, 1), F32), pltpu.VMEM((rows, KV_DIM), F32)]),
        out_shape=jax.ShapeDtypeStruct((bd * ns, ATT_DIM), F32),
        compiler_params=_params("parallel", "arbitrary"),
        name="sample_attention",
    )(page_table, proj, proj, proj, bias, *([ck] * pps), *([cv] * pps))


def _regroup_w_in(w):
    o_dt = D_INNER + D_INNER + BC_DIM
    o_q = o_dt + SSM_HEADS
    o_ki = o_q + ATT_DIM + 2 * KV_DIM + QI_DIM
    o_wi = o_ki + IDX_DIM
    o_g = o_wi + IDX_HEADS
    parts = [w[:, :o_dt], w[:, o_q:o_ki], w[:, o_g:o_g + 2 * D_MODEL], w[:, o_ki:o_wi], w[:, o_dt:o_q], w[:, o_wi:o_g]]
    used = sum(p.shape[1] for p in parts)
    parts.append(jnp.zeros((w.shape[0], PROJ_COLS - used), w.dtype))
    return jnp.concatenate(parts, axis=1).astype(MXU_DTYPE)


def _small_lanes(v):
    return jnp.zeros((1, LANES), F32).at[0, SM_DT:SM_DT + SSM_HEADS].set(v.astype(F32))


def _layer_consts(conv_w, conv_b, dt_bias, a_log, d_skip, ssm_norm_w):
    head_of_lane = jnp.arange(D_INNER) // SSM_HEAD_DIM
    expand = (jnp.arange(LANES)[:, None] == (SM_DT + head_of_lane)[None, :]).astype(F32)
    return dict(
        cwx=conv_w[:, :D_INNER], cbx=conv_b[None, :D_INNER], cwb=conv_w[:, D_INNER:], cbb=conv_b[None, D_INNER:],
        dtb=_small_lanes(dt_bias), aneg=_small_lanes(-jnp.exp(a_log.astype(F32))),
        dskip=jnp.repeat(d_skip.astype(F32), SSM_HEAD_DIM)[None, :], normw=ssm_norm_w.astype(F32)[None, :],
        expand=expand)


def kernel(x_prompt, x_sample, cache_k, cache_v, cache_kidx, state_ssm, state_conv, page_table, meta_tokens, norm1_w, w_in, conv_w, conv_b, dt_bias, a_log, d_skip, ssm_norm_w, w_ssm_proj, w_attn_proj, w_out, norm2_w, w_up, w_down, final_norm_w):
    b, seq, d = x_prompt.shape
    bd, ns = x_sample.shape[:2]
    depth = w_in.shape[0]
    assert d == D_MODEL and ns % 8 == 0
    past = page_table.shape[1] * cache_kidx.shape[2]
    lv = N_META + seq
    lp = -(-lv // Q_BLOCK) * Q_BLOCK
    rp = b * lp
    rs = bd * ns
    r = -(-(rp + rs) // LANES) * LANES
    topk_p = min(TOPK_MAX, seq // 4)
    topk_s = min(TOPK_MAX, (past + ns) // 4)

    hp = jnp.concatenate([jnp.broadcast_to(meta_tokens[None].astype(F32), (b, N_META, d)), x_prompt,
                          jnp.zeros((b, lp - lv, d), F32)], axis=1).reshape(rp, d)
    h = jnp.concatenate([hp, x_sample.reshape(rs, d), jnp.zeros((r - rp - rs, d), F32)], axis=0)

    zero_state = jnp.zeros((b, SSM_GROUPS, SSM_STATE, D_INNER // SSM_GROUPS), F32)
    zero_prefix = jnp.zeros((b, CONV_W - 1, D_INNER + BC_DIM), F32)
    outs = [[] for _ in range(10)]
    for l in range(depth):
        lw = _layer_consts(conv_w[l], conv_b[l], dt_bias[l], a_log[l], d_skip[l], ssm_norm_w[l])
        proj = _in_proj(_rmsnorm(h, norm1_w[l], MXU_DTYPE), _regroup_w_in(w_in[l]))

        yp, hp_t = _ssd(proj, 0, b, SSD_CHUNK, lp // SSD_CHUNK, lv, zero_state, zero_prefix, lw, MXU_DTYPE)
        ysm, hs_t = _ssd(proj, rp, bd, ns, 1, ns, _state_to_t(state_ssm[l]), state_conv[l], lw, F32)
        ap = _prompt_attention(proj, b, lp, topk_p)
        asm = _sample_attention(proj, rp, page_table, cache_k, cache_v, cache_kidx, l, ns, topk_s)
        tail = r - rp - rs
        y_ssm = jnp.concatenate([yp, ysm.astype(MXU_DTYPE), jnp.zeros((tail, D_INNER), MXU_DTYPE)], axis=0)
        y_att = jnp.concatenate([ap, asm.astype(MXU_DTYPE), jnp.zeros((tail, ATT_DIM), MXU_DTYPE)], axis=0)

        h = _merge(h, y_ssm, y_att, proj, w_ssm_proj[l].astype(MXU_DTYPE), w_attn_proj[l].astype(MXU_DTYPE),
                   w_out[l].astype(MXU_DTYPE))
        h = _mlp(h, norm2_w[l], w_up[l].astype(MXU_DTYPE), w_down[l].astype(MXU_DTYPE))

        pp = proj[:rp].reshape(b, lp, PROJ_COLS)[:, :lv]
        ps = proj[rp:rp + rs].reshape(bd, ns, PROJ_COLS)
        for acc, t in zip(outs, (
                pp[..., C_K:C_K + KV_DIM].reshape(b, lv, ATT_KV_HEADS, HEAD_DIM),
                pp[..., C_V:C_V + KV_DIM].reshape(b, lv, ATT_KV_HEADS, HEAD_DIM),
                pp[..., C_SM + SM_KI:C_SM + SM_KI + IDX_DIM],
                _state_from_t(hp_t),
                pp[:, lv - (CONV_W - 1):, C_XS:C_XS + D_INNER + BC_DIM],
                ps[..., C_K:C_K + KV_DIM].reshape(bd, ns, ATT_KV_HEADS, HEAD_DIM),
                ps[..., C_V:C_V + KV_DIM].reshape(bd, ns, ATT_KV_HEADS, HEAD_DIM),
                ps[..., C_SM + SM_KI:C_SM + SM_KI + IDX_DIM],
                _state_from_t(hs_t),
                jnp.concatenate([state_conv[l], ps[..., C_XS:C_XS + D_INNER + BC_DIM]], axis=1)[:, -(CONV_W - 1):])):
            acc.append(t)

    y = _rmsnorm(h, final_norm_w, F32)
    y_prompt = y[:rp].reshape(b, lp, d)[:, N_META:lv]
    y_sample = y[rp:rp + rs].reshape(bd, ns, d)
    return (y_prompt, y_sample) + tuple(jnp.stack(o) for o in outs)
```

```python
import functools

import jax
import jax.numpy as jnp
from jax import lax
from jax.experimental import pallas as pl
from jax.experimental.pallas import tpu as pltpu

F32 = jnp.float32
MXU_DTYPE = jnp.bfloat16

N_META = 16
NORM_EPS = 1e-6
SSM_HEAD_DIM = 64
SSM_GROUPS = 4
SSM_STATE = 128
CONV_W = 4
ATT_HEADS = 16
ATT_KV_HEADS = 4
ATT_GROUP = ATT_HEADS // ATT_KV_HEADS
HEAD_DIM = 64
IDX_HEADS = 8
IDX_DIM = 64
TOPK_MAX = 256

LANES = 128
Q_BLOCK = 128
SSD_CHUNK = 128
VMEM_LIMIT = 56 << 20
MASK_BIAS = -1e30
INT_MIN = -(2 ** 31)

D_MODEL = 1024
D_INNER = 2 * D_MODEL
BC_DIM = 2 * SSM_GROUPS * SSM_STATE
SSM_HEADS = D_INNER // SSM_HEAD_DIM
ATT_DIM = ATT_HEADS * HEAD_DIM
KV_DIM = ATT_KV_HEADS * HEAD_DIM
QI_DIM = IDX_HEADS * IDX_DIM
C_Z = 0
C_XS = C_Z + D_INNER
C_BC = C_XS + D_INNER
C_Q = C_BC + BC_DIM
C_K = C_Q + ATT_DIM
C_V = C_K + KV_DIM
C_QI = C_V + KV_DIM
C_GS = C_QI + QI_DIM
C_GA = C_GS + D_MODEL
C_SM = C_GA + D_MODEL
SM_KI = 0
SM_DT = IDX_DIM
SM_WI = IDX_DIM + SSM_HEADS
PROJ_TN = 512
PROJ_COLS = -(-(C_SM + LANES) // PROJ_TN) * PROJ_TN


def _divisor_tile(n, max_tile, align):
    best = None
    for t in range(align, min(n, max_tile) + 1, align):
        if n % t == 0:
            best = t
    assert best is not None, (n, max_tile, align)
    return best


def _params(*sem):
    return pltpu.CompilerParams(dimension_semantics=sem, vmem_limit_bytes=VMEM_LIMIT)


def _dot(a, b):
    return jnp.dot(a.astype(MXU_DTYPE), b.astype(MXU_DTYPE), preferred_element_type=F32)


def _dot_nt(a, b):
    return lax.dot_general(a.astype(MXU_DTYPE), b.astype(MXU_DTYPE), (((1,), (1,)), ((), ())),
                           preferred_element_type=F32)


def _dot_exact(a, b):
    return jnp.dot(a, b, preferred_element_type=F32, precision=lax.Precision.HIGHEST)


def _rmsnorm_kernel(x_ref, w_ref, o_ref):
    x = x_ref[...]
    ms = jnp.mean(x * x, axis=-1, keepdims=True)
    o_ref[...] = (x * lax.rsqrt(ms + NORM_EPS) * w_ref[...]).astype(o_ref.dtype)


def _rmsnorm(x, w, out_dtype):
    r, d = x.shape
    tm = _divisor_tile(r, 512, 16)
    return pl.pallas_call(
        _rmsnorm_kernel,
        grid=(r // tm,),
        in_specs=[pl.BlockSpec((tm, d), lambda i: (i, 0)), pl.BlockSpec((1, d), lambda i: (0, 0))],
        out_specs=pl.BlockSpec((tm, d), lambda i: (i, 0)),
        out_shape=jax.ShapeDtypeStruct((r, d), out_dtype),
        compiler_params=_params("parallel"),
        name="rmsnorm",
    )(x, w.reshape(1, d))


def _in_proj_kernel(x_ref, w_ref, o_ref):
    o_ref[...] = jnp.dot(x_ref[...], w_ref[pl.program_id(1)], preferred_element_type=F32)


def _in_proj(u, w):
    r, k = u.shape
    nt = w.shape[0]
    tm = _divisor_tile(r, 1100, 16)
    return pl.pallas_call(
        _in_proj_kernel,
        grid=(r // tm, nt),
        in_specs=[pl.BlockSpec((tm, k), lambda i, j: (i, 0)), pl.BlockSpec(w.shape, lambda i, j: (0, 0, 0))],
        out_specs=pl.BlockSpec((tm, PROJ_TN), lambda i, j: (i, j)),
        out_shape=jax.ShapeDtypeStruct((r, nt * PROJ_TN), F32),
        compiler_params=_params("parallel", "arbitrary"),
        name="in_proj",
    )(u, w)


def _merge_kernel(h_ref, ys_ref, ya_ref, gs_ref, ga_ref, wsp_ref, wap_ref, wo_ref, o_ref):
    a = jnp.dot(ys_ref[...], wsp_ref[...], preferred_element_type=F32)
    b = jnp.dot(ya_ref[...], wap_ref[...], preferred_element_type=F32)
    m = jax.nn.sigmoid(gs_ref[...]) * a + jax.nn.sigmoid(ga_ref[...]) * b
    o_ref[...] = h_ref[...] + _dot(m, wo_ref[...])


def _merge(h, y_ssm, y_att, proj, w_sp, w_ap, w_o):
    r, d = h.shape
    tm = _divisor_tile(r, 256, 16)
    const = lambda i: (0, 0)
    return pl.pallas_call(
        _merge_kernel,
        grid=(r // tm,),
        in_specs=[pl.BlockSpec((tm, d), lambda i: (i, 0)),
                  pl.BlockSpec((tm, D_INNER), lambda i: (i, 0)),
                  pl.BlockSpec((tm, ATT_DIM), lambda i: (i, 0)),
                  pl.BlockSpec((tm, d), lambda i: (i, C_GS // D_MODEL)),
                  pl.BlockSpec((tm, d), lambda i: (i, C_GA // D_MODEL)),
                  pl.BlockSpec(w_sp.shape, const), pl.BlockSpec(w_ap.shape, const), pl.BlockSpec(w_o.shape, const)],
        out_specs=pl.BlockSpec((tm, d), lambda i: (i, 0)),
        out_shape=jax.ShapeDtypeStruct((r, d), F32),
        compiler_params=_params("parallel"),
        name="merge",
    )(h, y_ssm, y_att, proj, proj, w_sp, w_ap, w_o)


def _mlp_kernel(h_ref, nw_ref, wu_ref, wd_ref, o_ref):
    x = h_ref[...]
    ms = jnp.mean(x * x, axis=-1, keepdims=True)
    u = x * lax.rsqrt(ms + NORM_EPS) * nw_ref[...]
    a = _dot(u, wu_ref[...])
    a = jnp.square(jnp.maximum(a, 0.0))
    o_ref[...] = x + _dot(a, wd_ref[...])


def _mlp(h, norm_w, w_up, w_down):
    r, d = h.shape
    tm = _divisor_tile(r, 256, 16)
    const = lambda i: (0, 0)
    return pl.pallas_call(
        _mlp_kernel,
        grid=(r // tm,),
        in_specs=[pl.BlockSpec((tm, d), lambda i: (i, 0)), pl.BlockSpec((1, d), const),
                  pl.BlockSpec(w_up.shape, const), pl.BlockSpec(w_down.shape, const)],
        out_specs=pl.BlockSpec((tm, d), lambda i: (i, 0)),
        out_shape=jax.ShapeDtypeStruct((r, d), F32),
        compiler_params=_params("parallel"),
        name="mlp",
    )(h, norm_w.reshape(1, d), w_up, w_down)


def _softplus(x):
    return jnp.maximum(x, 0.0) + jnp.log1p(jnp.exp(-jnp.abs(x)))


def _conv_silu(xp_sc, x, w_ref, b_ref, t):
    xp_sc[pl.ds(8, t), :] = x
    acc = b_ref[...] + x * w_ref[CONV_W - 1:CONV_W, :]
    for k in range(CONV_W - 1):
        acc = acc + xp_sc[pl.ds(5 + k, t), :] * w_ref[k:k + 1, :]
    xp_sc[pl.ds(5, CONV_W - 1), :] = xp_sc[pl.ds(t + 5, CONV_W - 1), :]
    return acc * jax.nn.sigmoid(acc)


def _ssd_kernel(z_ref, xs_ref, bc_ref, sm_ref, h0_ref, pxs_ref, pbc_ref,
                cwx_ref, cbx_ref, cwb_ref, cbb_ref, dtb_ref, aneg_ref, dskip_ref, normw_ref, expand_ref,
                y_ref, hout_ref, state_sc, xpx_sc, xpb_sc, *, t_in, valid_len):
    t = SSD_CHUNK
    c = pl.program_id(1)

    @pl.when(c == 0)
    def _():
        state_sc[...] = h0_ref[...]
        xpx_sc[pl.ds(5, CONV_W - 1), :] = pxs_ref[...]
        xpb_sc[pl.ds(5, CONV_W - 1), :] = pbc_ref[...]

    def rows(ref):
        x = ref[...]
        if t_in < t:
            x = jnp.concatenate([x, jnp.zeros((t - t_in, x.shape[1]), x.dtype)], axis=0)
        return x

    xs = _conv_silu(xpx_sc, rows(xs_ref), cwx_ref, cbx_ref, t)
    bcm = _conv_silu(xpb_sc, rows(bc_ref), cwb_ref, cbb_ref, t)
    z = rows(z_ref)

    row = lax.broadcasted_iota(jnp.int32, (t, LANES), 0)
    n_valid = jnp.minimum(valid_len - c * t, t_in)
    dt = jnp.where(row < n_valid, _softplus(rows(sm_ref) + dtb_ref[...]), 0.0)
    a = dt * aneg_ref[...]
    ri = lax.broadcasted_iota(jnp.int32, (t, t), 0)
    ci = lax.broadcasted_iota(jnp.int32, (t, t), 1)
    causal = ri >= ci
    acum = _dot_exact(jnp.where(causal, 1.0, 0.0), a)
    acum_t = acum.T
    dtx = _dot_exact(dt, expand_ref[...])
    acx = _dot_exact(acum, expand_ref[...])
    last = acx[t - 1:t, :]
    xdt = xs * dtx
    xw = xdt * jnp.exp(last - acx)
    eacx = jnp.exp(acx)
    elast = jnp.exp(last)

    gw = D_INNER // SSM_GROUPS
    hpg = SSM_HEADS // SSM_GROUPS
    ys = []
    for g in range(SSM_GROUPS):
        bg = bcm[:, g * SSM_STATE:(g + 1) * SSM_STATE]
        cg = bcm[:, (SSM_GROUPS + g) * SSM_STATE:(SSM_GROUPS + g + 1) * SSM_STATE]
        cb = _dot_nt(cg, bg)
        h_t = state_sc[g]
        yg = [_dot(cg, h_t) * eacx[:, g * gw:(g + 1) * gw]]
        intra = []
        for e in range(hpg):
            hd = g * hpg + e
            seg = acum[:, SM_DT + hd:SM_DT + hd + 1] - acum_t[SM_DT + hd:SM_DT + hd + 1, :]
            m = cb * jnp.exp(jnp.where(causal, seg, -jnp.inf))
            intra.append(_dot(m, xdt[:, hd * SSM_HEAD_DIM:(hd + 1) * SSM_HEAD_DIM]))
        ys.append(yg[0] + jnp.concatenate(intra, axis=1))
        state_sc[g] = h_t * elast[:, g * gw:(g + 1) * gw] + _dot(bg.T, xw[:, g * gw:(g + 1) * gw])
    y = jnp.concatenate(ys, axis=1) + xs * dskip_ref[...]
    y = y * (z * jax.nn.sigmoid(z))
    outs = []
    for g in range(SSM_GROUPS):
        yg = y[:, g * gw:(g + 1) * gw]
        outs.append(yg * lax.rsqrt(jnp.mean(yg * yg, axis=-1, keepdims=True) + NORM_EPS))
    y = jnp.concatenate(outs, axis=1) * normw_ref[...]
    y_ref[...] = y[:t_in].astype(y_ref.dtype)

    @pl.when(c == pl.num_programs(1) - 1)
    def _():
        hout_ref[...] = state_sc[...]


def _ssd(proj, row0, n_seq, t_in, n_chunks, valid_len, h0_t, prefix, lw, out_dtype):
    assert row0 % t_in == 0
    rb0 = row0 // t_in
    rowblk = lambda w, col: pl.BlockSpec((t_in, w), lambda s, c: (rb0 + s * n_chunks + c, col // w))
    per_seq = lambda shape: pl.BlockSpec((None,) + shape, lambda s, c: (s,) + (0,) * len(shape))
    const = lambda arr: pl.BlockSpec(arr.shape, lambda s, c: (0,) * arr.ndim)
    gw = D_INNER // SSM_GROUPS
    pxs, pbc = prefix[:, :, :D_INNER], prefix[:, :, D_INNER:]
    consts = (lw["cwx"], lw["cbx"], lw["cwb"], lw["cbb"], lw["dtb"], lw["aneg"], lw["dskip"], lw["normw"], lw["expand"])
    y, h_t = pl.pallas_call(
        functools.partial(_ssd_kernel, t_in=t_in, valid_len=valid_len),
        grid=(n_seq, n_chunks),
        in_specs=[rowblk(D_INNER, C_Z), rowblk(D_INNER, C_XS), rowblk(BC_DIM, C_BC), rowblk(LANES, C_SM),
                  per_seq((SSM_GROUPS, SSM_STATE, gw)), per_seq((CONV_W - 1, D_INNER)), per_seq((CONV_W - 1, BC_DIM))]
                 + [const(a) for a in consts],
        out_specs=[pl.BlockSpec((t_in, D_INNER), lambda s, c: (s * n_chunks + c, 0)),
                   per_seq((SSM_GROUPS, SSM_STATE, gw))],
        out_shape=[jax.ShapeDtypeStruct((n_seq * n_chunks * t_in, D_INNER), out_dtype),
                   jax.ShapeDtypeStruct((n_seq, SSM_GROUPS, SSM_STATE, gw), F32)],
        scratch_shapes=[pltpu.VMEM((SSM_GROUPS, SSM_STATE, gw), F32),
                        pltpu.VMEM((SSD_CHUNK + 8, D_INNER), F32),
                        pltpu.VMEM((SSD_CHUNK + 8, BC_DIM), F32)],
        compiler_params=_params("parallel", "arbitrary"),
        name="ssd",
    )(proj, proj, proj, proj, h0_t, pxs, pbc, *consts)
    return y, h_t


def _state_to_t(h):
    n = h.shape[0]
    hpg = SSM_HEADS // SSM_GROUPS
    return h.reshape(n, SSM_GROUPS, hpg, SSM_HEAD_DIM, SSM_STATE).transpose(0, 1, 4, 2, 3).reshape(
        n, SSM_GROUPS, SSM_STATE, hpg * SSM_HEAD_DIM)


def _state_from_t(h_t):
    n = h_t.shape[0]
    hpg = SSM_HEADS // SSM_GROUPS
    return h_t.reshape(n, SSM_GROUPS, SSM_STATE, hpg, SSM_HEAD_DIM).transpose(0, 1, 3, 4, 2).reshape(
        n, SSM_HEADS, SSM_HEAD_DIM, SSM_STATE)


def _sort_key(x):
    bits = lax.bitcast_convert_type(x, jnp.int32)
    bits = jnp.where(bits == INT_MIN, 0, bits)
    return bits ^ ((bits >> 31) & 0x7FFFFFFF)


def _fold_rows(x):
    x = x.reshape(x.shape[0] // 8, 8, x.shape[1])
    while x.shape[0] > 1:
        half, odd = divmod(x.shape[0], 2)
        folded = x[:half] + x[half:2 * half]
        if odd:
            folded = jnp.concatenate([folded[:1] + x[2 * half:], folded[1:]], axis=0) if half > 1 else folded + x[2 * half:]
        x = folded
    return x[0]


def _count_t(keys_sc, n_chunks, pred):
    def body(c, cnt):
        return cnt + _fold_rows(jnp.where(pred(keys_sc[c]), 1.0, 0.0))

    cnt = lax.fori_loop(0, n_chunks, body, jnp.zeros((8, LANES), F32))
    return jnp.sum(cnt, axis=0, keepdims=True)


def _kth_key_t(keys_sc, n_chunks, k):
    def bit_body(i, res):
        cand = res ^ lax.shift_left(jnp.int32(1), 31 - i)
        return jnp.where(_count_t(keys_sc, n_chunks, lambda key: key >= cand) >= k, cand, res)

    return lax.fori_loop(0, 32, bit_body, jnp.full((1, LANES), INT_MIN, jnp.int32))


def _select_t(keys_sc, bias_sc, n_chunks, k):
    kc = keys_sc.shape[1]
    thr = _kth_key_t(keys_sc, n_chunks, k)
    n_ge = _count_t(keys_sc, n_chunks, lambda key: key >= thr)
    surplus = jnp.max(jnp.where((n_ge > k) & (thr != INT_MIN), 1.0, 0.0))

    @pl.when(surplus == 0.0)
    def _():
        floor = jnp.maximum(thr, INT_MIN + 1)

        def body(c, carry):
            bias_sc[c] = jnp.where(keys_sc[c] >= floor, 0.0, MASK_BIAS)
            return carry

        lax.fori_loop(0, n_chunks, body, 0)

    @pl.when(surplus != 0.0)
    def _():
        need = k - _count_t(keys_sc, n_chunks, lambda key: key > thr)
        ri = lax.broadcasted_iota(jnp.int32, (kc, kc), 0)
        ci = lax.broadcasted_iota(jnp.int32, (kc, kc), 1)
        lower = jnp.where(ci <= ri, 1.0, 0.0).astype(jnp.bfloat16)

        def body(c, run):
            key = keys_sc[c]
            eqf = jnp.where((key == thr) & (key != INT_MIN), 1.0, 0.0)
            rank = jnp.dot(lower, eqf.astype(jnp.bfloat16), preferred_element_type=F32) + run
            sel = (key > thr) | ((eqf > 0.0) & (rank <= need))
            bias_sc[c] = jnp.where(sel, 0.0, MASK_BIAS)
            return run + jnp.sum(_fold_rows(eqf), axis=0, keepdims=True)

        lax.fori_loop(0, n_chunks, body, jnp.zeros((1, LANES), F32))


def _select_s(keys_sc, bias_ref, k):
    n_chunks, rows, width = keys_sc.shape

    def count(pred):
        return jnp.sum(jnp.sum(jnp.where(pred(keys_sc[...]), 1.0, 0.0), axis=0), axis=1, keepdims=True)

    def bit_body(i, res):
        cand = res ^ lax.shift_left(jnp.int32(1), 31 - i)
        return jnp.where(count(lambda key: key >= cand[None]) >= k, cand, res)

    thr = lax.fori_loop(0, 32, bit_body, jnp.full((rows, 1), INT_MIN, jnp.int32))
    n_ge = count(lambda key: key >= thr[None])
    surplus = jnp.max(jnp.where((n_ge > k) & (thr != INT_MIN), 1.0, 0.0))

    @pl.when(surplus == 0.0)
    def _():
        floor = jnp.maximum(thr, INT_MIN + 1)
        bias_ref[...] = jnp.where(keys_sc[...] >= floor[None], 0.0, MASK_BIAS)

    @pl.when(surplus != 0.0)
    def _():
        need = k - count(lambda key: key > thr[None])
        ri = lax.broadcasted_iota(jnp.int32, (width, width), 0)
        ci = lax.broadcasted_iota(jnp.int32, (width, width), 1)
        upper = jnp.where(ri <= ci, 1.0, 0.0).astype(jnp.bfloat16)

        def body(c, run):
            key = keys_sc[c]
            eqf = jnp.where((key == thr) & (key != INT_MIN), 1.0, 0.0)
            rank = jnp.dot(eqf.astype(jnp.bfloat16), upper, preferred_element_type=F32) + run
            sel = (key > thr) | ((eqf > 0.0) & (rank <= need))
            bias_ref[c] = jnp.where(sel, 0.0, MASK_BIAS)
            return run + jnp.sum(eqf, axis=1, keepdims=True)

        lax.fori_loop(0, n_chunks, body, jnp.zeros((rows, 1), F32))


def _pattn_kernel(qi_ref, smq_ref, q_ref, smk_ref, k_ref, v_ref, o_ref,
                  kb_sc, kib_sc, vt_sc, keys_sc, bias_sc, qz_sc, m_sc, l_sc, acc_sc, *, kc, lp, topk):
    j = pl.program_id(1)
    nq = Q_BLOCK
    nkc = kb_sc.shape[0]

    @pl.when(j == 0)
    def _():
        for c in range(nkc):
            r0 = c * kc
            n = min(kc, lp - r0)

            def chunk(ref):
                x = ref[r0:r0 + n, :]
                return x if n == kc else jnp.concatenate([x, jnp.zeros((kc - n, x.shape[1]), x.dtype)], axis=0)

            kb_sc[c] = chunk(k_ref).astype(MXU_DTYPE)
            kib_sc[c] = chunk(smk_ref).astype(MXU_DTYPE)
            vt_sc[c] = chunk(v_ref).T.astype(MXU_DTYPE)

    n_chunks = ((j + 1) * nq - 1) // kc + 1

    qi_t = (qi_ref[...] * IDX_DIM ** -0.5).T
    qiz = jnp.concatenate([qi_t[h * IDX_DIM:(h + 1) * IDX_DIM] for h in range(IDX_HEADS)], axis=1)
    qiz = jnp.concatenate([qiz, jnp.zeros((LANES - IDX_DIM, IDX_HEADS * nq), F32)], axis=0).astype(MXU_DTYPE)
    smq_t = smq_ref[...].T
    wi_row = jnp.concatenate([smq_t[SM_WI + h:SM_WI + h + 1] for h in range(IDX_HEADS)], axis=1) * IDX_HEADS ** -0.5
    kpos = lax.broadcasted_iota(jnp.int32, (kc, nq), 0)
    qpos = j * nq + lax.broadcasted_iota(jnp.int32, (kc, nq), 1)

    def score_body(c, carry):
        s = jnp.maximum(jnp.dot(kib_sc[c], qiz, preferred_element_type=F32), 0.0) * wi_row
        acc = s[:, :nq]
        for h in range(1, IDX_HEADS):
            acc = acc + s[:, h * nq:(h + 1) * nq]
        keys_sc[c] = jnp.where(c * kc + kpos <= qpos, _sort_key(acc), INT_MIN)
        return carry

    lax.fori_loop(0, n_chunks, score_body, 0)
    _select_t(keys_sc, bias_sc, n_chunks, topk)

    q_t = (q_ref[...] * HEAD_DIM ** -0.5).T
    gw = ATT_GROUP * nq
    zero = jnp.zeros((HEAD_DIM, gw), F32)
    for hk in range(ATT_KV_HEADS):
        own = jnp.concatenate([q_t[(hk * ATT_GROUP + g) * HEAD_DIM:(hk * ATT_GROUP + g + 1) * HEAD_DIM]
                               for g in range(ATT_GROUP)], axis=1)
        qz_sc[hk] = jnp.concatenate([own if i == hk else zero for i in range(ATT_KV_HEADS)],
                                    axis=0).astype(MXU_DTYPE)
    m_sc[...] = jnp.full(m_sc.shape, MASK_BIAS, F32)
    l_sc[...] = jnp.zeros(l_sc.shape, F32)
    acc_sc[...] = jnp.zeros(acc_sc.shape, F32)

    def att_body(c, carry):
        kch = kb_sc[c]
        vt = vt_sc[c]
        b = bias_sc[c]
        b4 = jnp.concatenate([b] * ATT_GROUP, axis=1)
        for hk in range(ATT_KV_HEADS):
            s = jnp.dot(kch, qz_sc[hk], preferred_element_type=F32) + b4
            m_old = m_sc[hk]
            m_new = jnp.maximum(m_old, jnp.max(s, axis=0, keepdims=True))
            alpha = jnp.exp(m_old - m_new)
            p = jnp.exp(s - m_new)
            l_sc[hk] = alpha * l_sc[hk] + jnp.sum(p, axis=0, keepdims=True)
            acc_sc[hk] = alpha * acc_sc[hk] + jnp.dot(vt[hk * HEAD_DIM:(hk + 1) * HEAD_DIM], p.astype(MXU_DTYPE),
                                                      preferred_element_type=F32)
            m_sc[hk] = m_new
        return carry

    lax.fori_loop(0, n_chunks, att_body, 0)

    heads = []
    for hk in range(ATT_KV_HEADS):
        o = acc_sc[hk] / l_sc[hk]
        heads += [o[:, g * nq:(g + 1) * nq] for g in range(ATT_GROUP)]
    o_ref[...] = jnp.concatenate(heads, axis=0).T.astype(o_ref.dtype)


def _prompt_attention(proj, n_seq, lp, topk):
    nqb = lp // Q_BLOCK
    kc = 512
    nkc = -(-lp // kc)
    qblk = lambda w, col: pl.BlockSpec((Q_BLOCK, w), lambda b, j: (b * nqb + j, col // w))
    seqblk = lambda w, col: pl.BlockSpec((lp, w), lambda b, j: (b, col // w))
    gw = ATT_GROUP * Q_BLOCK
    return pl.pallas_call(
        functools.partial(_pattn_kernel, kc=kc, lp=lp, topk=topk),
        grid=(n_seq, nqb),
        in_specs=[qblk(QI_DIM, C_QI), qblk(LANES, C_SM), qblk(ATT_DIM, C_Q),
                  seqblk(LANES, C_SM), seqblk(KV_DIM, C_K), seqblk(KV_DIM, C_V)],
        out_specs=pl.BlockSpec((Q_BLOCK, ATT_DIM), lambda b, j: (b * nqb + j, 0)),
        out_shape=jax.ShapeDtypeStruct((n_seq * lp, ATT_DIM), MXU_DTYPE),
        scratch_shapes=[pltpu.VMEM((nkc, kc, KV_DIM), MXU_DTYPE), pltpu.VMEM((nkc, kc, LANES), MXU_DTYPE),
                        pltpu.VMEM((nkc, KV_DIM, kc), MXU_DTYPE),
                        pltpu.VMEM((nkc, kc, Q_BLOCK), jnp.int32), pltpu.VMEM((nkc, kc, Q_BLOCK), F32),
                        pltpu.VMEM((ATT_KV_HEADS, KV_DIM, gw), MXU_DTYPE),
                        pltpu.VMEM((ATT_KV_HEADS, 1, gw), F32), pltpu.VMEM((ATT_KV_HEADS, 1, gw), F32),
                        pltpu.VMEM((ATT_KV_HEADS, HEAD_DIM, gw), F32)],
        compiler_params=_params("parallel", "arbitrary"),
        name="prompt_attention",
    )(proj, proj, proj, proj, proj, proj)


def _pad_rows(x, n):
    return jnp.concatenate([x, jnp.zeros((n - x.shape[0], x.shape[1]), x.dtype)], axis=0)


def _sselect_kernel(pt_ref, qi_ref, sm_ref, *rest, pages_per_step, n_pages, page, ns, topk):
    page_refs = rest[:pages_per_step]
    bias_ref, keys_sc = rest[pages_per_step:]
    g = pl.program_id(1)
    qi = (qi_ref[...] * IDX_DIM ** -0.5).astype(MXU_DTYPE)
    qs = jnp.concatenate([qi[:, h * IDX_DIM:(h + 1) * IDX_DIM] for h in range(IDX_HEADS)], axis=0)
    sm = sm_ref[...]
    wi = sm[:, SM_WI:SM_WI + IDX_HEADS] * IDX_HEADS ** -0.5

    def weigh(s):
        s = jnp.maximum(s, 0.0)
        acc = s[:ns] * wi[:, 0:1]
        for h in range(1, IDX_HEADS):
            acc = acc + s[h * ns:(h + 1) * ns] * wi[:, h:h + 1]
        return _sort_key(acc)

    keys = weigh(_dot(qs, jnp.concatenate([r[...] for r in page_refs], axis=1)))
    for i in range(pages_per_step):
        keys_sc[g * pages_per_step + i] = keys[:, i * page:(i + 1) * page]

    @pl.when(g == pl.num_programs(1) - 1)
    def _():
        qrow = lax.broadcasted_iota(jnp.int32, (ns, page), 0)
        lane = lax.broadcasted_iota(jnp.int32, (ns, page), 1)
        new = weigh(_dot_nt(qs, _pad_rows(sm[:, SM_KI:SM_KI + IDX_DIM], page)))
        keys_sc[n_pages] = jnp.where(lane <= qrow, new, INT_MIN)
        _select_s(keys_sc, bias_ref, topk)


def _sattn_kernel(pt_ref, q_ref, kn_ref, vn_ref, bias_ref, *rest, pages_per_step, n_pages, page, ns):
    k_refs = rest[:pages_per_step]
    v_refs = rest[pages_per_step:2 * pages_per_step]
    o_ref, qbd_sc, m_sc, l_sc, acc_sc = rest[2 * pages_per_step:]
    g = pl.program_id(1)

    @pl.when(g == 0)
    def _():
        q = q_ref[...] * HEAD_DIM ** -0.5
        rows = []
        for h in range(ATT_HEADS):
            hk = h // ATT_GROUP
            parts = [jnp.zeros((ns, HEAD_DIM), F32)] * ATT_KV_HEADS
            parts[hk] = q[:, h * HEAD_DIM:(h + 1) * HEAD_DIM]
            rows.append(jnp.concatenate(parts, axis=1))
        qbd_sc[...] = jnp.concatenate(rows, axis=0).astype(MXU_DTYPE)
        m_sc[...] = jnp.full(m_sc.shape, MASK_BIAS, F32)
        l_sc[...] = jnp.zeros(l_sc.shape, F32)
        acc_sc[...] = jnp.zeros(acc_sc.shape, F32)

    def attend(s, b, pv):
        s = s + jnp.concatenate([b] * ATT_HEADS, axis=0)
        m_old = m_sc[...]
        m_new = jnp.maximum(m_old, jnp.max(s, axis=1, keepdims=True))
        alpha = jnp.exp(m_old - m_new)
        p = jnp.exp(s - m_new)
        l_sc[...] = alpha * l_sc[...] + jnp.sum(p, axis=1, keepdims=True)
        acc_sc[...] = alpha * acc_sc[...] + pv(p)
        m_sc[...] = m_new

    kcat = jnp.concatenate([r[...] for r in k_refs], axis=1)
    vcat = jnp.concatenate([r[...] for r in v_refs], axis=1)
    bcat = jnp.concatenate([bias_ref[g * pages_per_step + i] for i in range(pages_per_step)], axis=1)
    attend(_dot(qbd_sc[...], kcat), bcat, lambda p: _dot_nt(p, vcat))

    @pl.when(g == pl.num_programs(1) - 1)
    def _():
        vn = _pad_rows(vn_ref[...], page)
        attend(_dot_nt(qbd_sc[...], _pad_rows(kn_ref[...], page)), bias_ref[n_pages], lambda p: _dot(p, vn))
        o = acc_sc[...] / l_sc[...]
        o_ref[...] = jnp.concatenate(
            [o[h * ns:(h + 1) * ns, (h // ATT_GROUP) * HEAD_DIM:(h // ATT_GROUP + 1) * HEAD_DIM]
             for h in range(ATT_HEADS)], axis=1)


def _key_minor(cache):
    nd = cache.ndim
    t = jnp.transpose(cache, (0, 1) + tuple(range(3, nd)) + (2,))
    return t.reshape(t.shape[:2] + (-1, t.shape[-1]))


def _sample_attention(proj, row0, page_table, ck_t, cv_t, cki_t, layer, ns, topk):
    bd, n_pages = page_table.shape
    page = cki_t.shape[3]
    assert page == LANES and row0 % ns == 0
    pps = max(d for d in range(1, 9) if n_pages % d == 0)
    ng = n_pages // pps
    rb0 = row0 // ns
    rowblk = lambda w, col: pl.BlockSpec((ns, w), lambda s, g, pt: (rb0 + s, col // w))
    pageblk = lambda w, i: pl.BlockSpec((None, None, w, page), lambda s, g, pt: (layer, pt[s, g * pps + i], 0, 0))
    biasblk = pl.BlockSpec((None, n_pages + 1, ns, page), lambda s, g, pt: (s, 0, 0, 0))
    static = dict(pages_per_step=pps, n_pages=n_pages, page=page, ns=ns)

    bias = pl.pallas_call(
        functools.partial(_sselect_kernel, topk=topk, **static),
        grid_spec=pltpu.PrefetchScalarGridSpec(
            num_scalar_prefetch=1, grid=(bd, ng),
            in_specs=[rowblk(QI_DIM, C_QI), rowblk(LANES, C_SM)] + [pageblk(IDX_DIM, i) for i in range(pps)],
            out_specs=biasblk,
            scratch_shapes=[pltpu.VMEM((n_pages + 1, ns, page), jnp.int32)]),
        out_shape=jax.ShapeDtypeStruct((bd, n_pages + 1, ns, page), F32),
        compiler_params=_params("parallel", "arbitrary"),
        name="sample_select",
    )(page_table, proj, proj, *([cki_t] * pps))

    rows = ATT_HEADS * ns
    return pl.pallas_call(
        functools.partial(_sattn_kernel, **static),
        grid_spec=pltpu.PrefetchScalarGridSpec(
            num_scalar_prefetch=1, grid=(bd, ng),
            in_specs=[rowblk(ATT_DIM, C_Q), rowblk(KV_DIM, C_K), rowblk(KV_DIM, C_V), biasblk]
                     + [pageblk(KV_DIM, i) for i in range(pps)] * 2,
            out_specs=pl.BlockSpec((ns, ATT_DIM), lambda s, g, pt: (s, 0)),
            scratch_shapes=[pltpu.VMEM((rows, KV_DIM), MXU_DTYPE), pltpu.VMEM((rows, 1), F32),
                            pltpu.VMEM((rows, 1), F32), pltpu.VMEM((rows, KV_DIM), F32)]),
        out_shape=jax.ShapeDtypeStruct((bd * ns, ATT_DIM), F32),
        compiler_params=_params("parallel", "arbitrary"),
        name="sample_attention",
    )(page_table, proj, proj, proj, bias, *([ck_t] * pps), *([cv_t] * pps))


def _regroup_w_in(w):
    o_dt = D_INNER + D_INNER + BC_DIM
    o_q = o_dt + SSM_HEADS
    o_ki = o_q + ATT_DIM + 2 * KV_DIM + QI_DIM
    o_wi = o_ki + IDX_DIM
    o_g = o_wi + IDX_HEADS
    parts = [w[:, :o_dt], w[:, o_q:o_ki], w[:, o_g:o_g + 2 * D_MODEL], w[:, o_ki:o_wi], w[:, o_dt:o_q], w[:, o_wi:o_g]]
    used = sum(p.shape[1] for p in parts)
    parts.append(jnp.zeros((w.shape[0], PROJ_COLS - used), w.dtype))
    w = jnp.concatenate(parts, axis=1).astype(MXU_DTYPE)
    return w.reshape(w.shape[0], PROJ_COLS // PROJ_TN, PROJ_TN).transpose(1, 0, 2)


def _small_lanes(v):
    return jnp.zeros((1, LANES), F32).at[0, SM_DT:SM_DT + SSM_HEADS].set(v.astype(F32))


def _layer_consts(conv_w, conv_b, dt_bias, a_log, d_skip, ssm_norm_w):
    head_of_lane = jnp.arange(D_INNER) // SSM_HEAD_DIM
    expand = (jnp.arange(LANES)[:, None] == (SM_DT + head_of_lane)[None, :]).astype(F32)
    return dict(
        cwx=conv_w[:, :D_INNER], cbx=conv_b[None, :D_INNER], cwb=conv_w[:, D_INNER:], cbb=conv_b[None, D_INNER:],
        dtb=_small_lanes(dt_bias), aneg=_small_lanes(-jnp.exp(a_log.astype(F32))),
        dskip=jnp.repeat(d_skip.astype(F32), SSM_HEAD_DIM)[None, :], normw=ssm_norm_w.astype(F32)[None, :],
        expand=expand)


def kernel(x_prompt, x_sample, cache_k, cache_v, cache_kidx, state_ssm, state_conv, page_table, meta_tokens, norm1_w, w_in, conv_w, conv_b, dt_bias, a_log, d_skip, ssm_norm_w, w_ssm_proj, w_attn_proj, w_out, norm2_w, w_up, w_down, final_norm_w):
    b, seq, d = x_prompt.shape
    bd, ns = x_sample.shape[:2]
    depth = w_in.shape[0]
    assert d == D_MODEL and ns % 8 == 0
    past = page_table.shape[1] * cache_kidx.shape[2]
    lv = N_META + seq
    lp = -(-lv // Q_BLOCK) * Q_BLOCK
    rp = b * lp
    rs = bd * ns
    r = -(-(rp + rs) // LANES) * LANES
    topk_p = min(TOPK_MAX, seq // 4)
    topk_s = min(TOPK_MAX, (past + ns) // 4)

    hp = jnp.concatenate([jnp.broadcast_to(meta_tokens[None].astype(F32), (b, N_META, d)), x_prompt,
                          jnp.zeros((b, lp - lv, d), F32)], axis=1).reshape(rp, d)
    h = jnp.concatenate([hp, x_sample.reshape(rs, d), jnp.zeros((r - rp - rs, d), F32)], axis=0)

    ck_t, cv_t, cki_t = _key_minor(cache_k), _key_minor(cache_v), _key_minor(cache_kidx)
    zero_state = jnp.zeros((b, SSM_GROUPS, SSM_STATE, D_INNER // SSM_GROUPS), F32)
    zero_prefix = jnp.zeros((b, CONV_W - 1, D_INNER + BC_DIM), F32)
    outs = [[] for _ in range(10)]
    for l in range(depth):
        lw = _layer_consts(conv_w[l], conv_b[l], dt_bias[l], a_log[l], d_skip[l], ssm_norm_w[l])
        proj = _in_proj(_rmsnorm(h, norm1_w[l], MXU_DTYPE), _regroup_w_in(w_in[l]))

        yp, hp_t = _ssd(proj, 0, b, SSD_CHUNK, lp // SSD_CHUNK, lv, zero_state, zero_prefix, lw, MXU_DTYPE)
        ysm, hs_t = _ssd(proj, rp, bd, ns, 1, ns, _state_to_t(state_ssm[l]), state_conv[l], lw, F32)
        ap = _prompt_attention(proj, b, lp, topk_p)
        asm = _sample_attention(proj, rp, page_table, ck_t, cv_t, cki_t, l, ns, topk_s)
        tail = r - rp - rs
        y_ssm = jnp.concatenate([yp, ysm.astype(MXU_DTYPE), jnp.zeros((tail, D_INNER), MXU_DTYPE)], axis=0)
        y_att = jnp.concatenate([ap, asm.astype(MXU_DTYPE), jnp.zeros((tail, ATT_DIM), MXU_DTYPE)], axis=0)

        h = _merge(h, y_ssm, y_att, proj, w_ssm_proj[l].astype(MXU_DTYPE), w_attn_proj[l].astype(MXU_DTYPE),
                   w_out[l].astype(MXU_DTYPE))
        h = _mlp(h, norm2_w[l], w_up[l].astype(MXU_DTYPE), w_down[l].astype(MXU_DTYPE))

        def pcols(c0, w):
            return proj[:rp, c0:c0 + w].reshape(b, lp, w)[:, :lv]

        def scols(c0, w):
            return proj[rp:rp + rs, c0:c0 + w].reshape(bd, ns, w)

        xbc_w = D_INNER + BC_DIM
        p_conv = jnp.stack([proj[s * lp + lv - (CONV_W - 1):s * lp + lv, C_XS:C_XS + xbc_w] for s in range(b)])
        s_conv = jnp.concatenate([state_conv[l], scols(C_XS, xbc_w)], axis=1)[:, -(CONV_W - 1):]
        for acc, t in zip(outs, (
                pcols(C_K, KV_DIM).reshape(b, lv, ATT_KV_HEADS, HEAD_DIM),
                pcols(C_V, KV_DIM).reshape(b, lv, ATT_KV_HEADS, HEAD_DIM),
                pcols(C_SM + SM_KI, IDX_DIM),
                _state_from_t(hp_t),
                p_conv,
                scols(C_K, KV_DIM).reshape(bd, ns, ATT_KV_HEADS, HEAD_DIM),
                scols(C_V, KV_DIM).reshape(bd, ns, ATT_KV_HEADS, HEAD_DIM),
                scols(C_SM + SM_KI, IDX_DIM),
                _state_from_t(hs_t),
                s_conv)):
            acc.append(t)

    y = _rmsnorm(h, final_norm_w, F32)
    y_prompt = y[:rp].reshape(b, lp, d)[:, N_META:lv]
    y_sample = y[rp:rp + rs].reshape(bd, ns, d)
    return (y_prompt, y_sample) + tuple(jnp.stack(o) for o in outs)
```

```python
import functools

import jax
import jax.numpy as jnp
from jax import lax
from jax.experimental import pallas as pl
from jax.experimental.pallas import tpu as pltpu

F32 = jnp.float32
MXU_DTYPE = jnp.bfloat16

N_META = 16
NORM_EPS = 1e-6
SSM_HEAD_DIM = 64
SSM_GROUPS = 4
SSM_STATE = 128
CONV_W = 4
ATT_HEADS = 16
ATT_KV_HEADS = 4
ATT_GROUP = ATT_HEADS // ATT_KV_HEADS
HEAD_DIM = 64
IDX_HEADS = 8
IDX_DIM = 64
TOPK_MAX = 256

LANES = 128
Q_BLOCK = 128
SSD_CHUNK = 128
VMEM_LIMIT = 56 << 20
MASK_BIAS = -1e30
INT_MIN = -(2 ** 31)
LOG2_E = 1.4426950408889634
V_ROWS = HEAD_DIM + 16

D_MODEL = 1024
D_INNER = 2 * D_MODEL
BC_DIM = 2 * SSM_GROUPS * SSM_STATE
SSM_HEADS = D_INNER // SSM_HEAD_DIM
ATT_DIM = ATT_HEADS * HEAD_DIM
KV_DIM = ATT_KV_HEADS * HEAD_DIM
QI_DIM = IDX_HEADS * IDX_DIM
C_Z = 0
C_XS = C_Z + D_INNER
C_BC = C_XS + D_INNER
C_Q = C_BC + BC_DIM
C_K = C_Q + ATT_DIM
C_V = C_K + KV_DIM
C_QI = C_V + KV_DIM
C_GS = C_QI + QI_DIM
C_GA = C_GS + D_MODEL
C_SM = C_GA + D_MODEL
SM_KI = 0
SM_DT = IDX_DIM
SM_WI = IDX_DIM + SSM_HEADS
PROJ_TN = 512
PROJ_COLS = -(-(C_SM + LANES) // PROJ_TN) * PROJ_TN


def _divisor_tile(n, max_tile, align):
    best = None
    for t in range(align, min(n, max_tile) + 1, align):
        if n % t == 0:
            best = t
    assert best is not None, (n, max_tile, align)
    return best


def _params(*sem):
    return pltpu.CompilerParams(dimension_semantics=sem, vmem_limit_bytes=VMEM_LIMIT)


def _dot(a, b):
    return jnp.dot(a.astype(MXU_DTYPE), b.astype(MXU_DTYPE), preferred_element_type=F32)


def _dot_nt(a, b):
    return lax.dot_general(a.astype(MXU_DTYPE), b.astype(MXU_DTYPE), (((1,), (1,)), ((), ())),
                           preferred_element_type=F32)


def _bf16_pieces(x):
    hi = x.astype(jnp.bfloat16)
    r = x - hi.astype(F32)
    mid = r.astype(jnp.bfloat16)
    return [hi, mid, (r - mid.astype(F32)).astype(jnp.bfloat16)]


def _dot_exact(a, b):
    return jnp.dot(a, b, preferred_element_type=F32, precision=lax.Precision.HIGHEST)


def _rmsnorm_kernel(x_ref, w_ref, o_ref):
    x = x_ref[...]
    ms = jnp.mean(x * x, axis=-1, keepdims=True)
    o_ref[...] = (x * lax.rsqrt(ms + NORM_EPS) * w_ref[...]).astype(o_ref.dtype)


def _rmsnorm(x, w, out_dtype):
    r, d = x.shape
    tm = _divisor_tile(r, 512, 16)
    return pl.pallas_call(
        _rmsnorm_kernel,
        grid=(r // tm,),
        in_specs=[pl.BlockSpec((tm, d), lambda i: (i, 0)), pl.BlockSpec((1, d), lambda i: (0, 0))],
        out_specs=pl.BlockSpec((tm, d), lambda i: (i, 0)),
        out_shape=jax.ShapeDtypeStruct((r, d), out_dtype),
        compiler_params=_params("parallel"),
        name="rmsnorm",
    )(x, w.reshape(1, d))


def _in_proj_kernel(x_ref, w_ref, o_ref):
    o_ref[...] = jnp.dot(x_ref[...], w_ref[pl.program_id(1)], preferred_element_type=F32)


def _in_proj(u, w):
    r, k = u.shape
    nt = w.shape[0]
    tm = _divisor_tile(r, 1100, 16)
    return pl.pallas_call(
        _in_proj_kernel,
        grid=(r // tm, nt),
        in_specs=[pl.BlockSpec((tm, k), lambda i, j: (i, 0)), pl.BlockSpec(w.shape, lambda i, j: (0, 0, 0))],
        out_specs=pl.BlockSpec((tm, PROJ_TN), lambda i, j: (i, j)),
        out_shape=jax.ShapeDtypeStruct((r, nt * PROJ_TN), F32),
        compiler_params=_params("parallel", "arbitrary"),
        name="in_proj",
    )(u, w)


def _merge_kernel(h_ref, ys_ref, ya_ref, gs_ref, ga_ref, wsp_ref, wap_ref, wo_ref, o_ref):
    a = jnp.dot(ys_ref[...], wsp_ref[...], preferred_element_type=F32)
    b = jnp.dot(ya_ref[...], wap_ref[...], preferred_element_type=F32)
    m = jax.nn.sigmoid(gs_ref[...]) * a + jax.nn.sigmoid(ga_ref[...]) * b
    o_ref[...] = h_ref[...] + _dot(m, wo_ref[...])


def _merge(h, y_ssm, y_att, proj, w_sp, w_ap, w_o):
    r, d = h.shape
    tm = _divisor_tile(r, 256, 16)
    const = lambda i: (0, 0)
    return pl.pallas_call(
        _merge_kernel,
        grid=(r // tm,),
        in_specs=[pl.BlockSpec((tm, d), lambda i: (i, 0)),
                  pl.BlockSpec((tm, D_INNER), lambda i: (i, 0)),
                  pl.BlockSpec((tm, ATT_DIM), lambda i: (i, 0)),
                  pl.BlockSpec((tm, d), lambda i: (i, C_GS // D_MODEL)),
                  pl.BlockSpec((tm, d), lambda i: (i, C_GA // D_MODEL)),
                  pl.BlockSpec(w_sp.shape, const), pl.BlockSpec(w_ap.shape, const), pl.BlockSpec(w_o.shape, const)],
        out_specs=pl.BlockSpec((tm, d), lambda i: (i, 0)),
        out_shape=jax.ShapeDtypeStruct((r, d), F32),
        compiler_params=_params("parallel"),
        name="merge",
    )(h, y_ssm, y_att, proj, proj, w_sp, w_ap, w_o)


def _mlp_kernel(h_ref, nw_ref, wu_ref, wd_ref, o_ref):
    x = h_ref[...]
    ms = jnp.mean(x * x, axis=-1, keepdims=True)
    u = x * lax.rsqrt(ms + NORM_EPS) * nw_ref[...]
    a = _dot(u, wu_ref[...])
    a = jnp.square(jnp.maximum(a, 0.0))
    o_ref[...] = x + _dot(a, wd_ref[...])


def _mlp(h, norm_w, w_up, w_down):
    r, d = h.shape
    tm = _divisor_tile(r, 256, 16)
    const = lambda i: (0, 0)
    return pl.pallas_call(
        _mlp_kernel,
        grid=(r // tm,),
        in_specs=[pl.BlockSpec((tm, d), lambda i: (i, 0)), pl.BlockSpec((1, d), const),
                  pl.BlockSpec(w_up.shape, const), pl.BlockSpec(w_down.shape, const)],
        out_specs=pl.BlockSpec((tm, d), lambda i: (i, 0)),
        out_shape=jax.ShapeDtypeStruct((r, d), F32),
        compiler_params=_params("parallel"),
        name="mlp",
    )(h, norm_w.reshape(1, d), w_up, w_down)


def _softplus(x):
    return jnp.maximum(x, 0.0) + jnp.log1p(jnp.exp(-jnp.abs(x)))


def _conv_silu(xp_sc, x, w_ref, b_ref, t):
    xp_sc[pl.ds(8, t), :] = x
    acc = b_ref[...] + x * w_ref[CONV_W - 1:CONV_W, :]
    for k in range(CONV_W - 1):
        acc = acc + xp_sc[pl.ds(5 + k, t), :] * w_ref[k:k + 1, :]
    xp_sc[pl.ds(5, CONV_W - 1), :] = xp_sc[pl.ds(t + 5, CONV_W - 1), :]
    return acc * jax.nn.sigmoid(acc)


def _ssd_kernel(z_ref, xs_ref, bc_ref, sm_ref, h0_ref, pxs_ref, pbc_ref,
                cwx_ref, cbx_ref, cwb_ref, cbb_ref, dtb_ref, aneg_ref, dskip_ref, normw_ref, expand_ref,
                y_ref, hout_ref, state_sc, xpx_sc, xpb_sc, *, t_in, valid_len):
    t = SSD_CHUNK
    c = pl.program_id(1)

    @pl.when(c == 0)
    def _():
        state_sc[...] = h0_ref[...]
        xpx_sc[pl.ds(5, CONV_W - 1), :] = pxs_ref[...]
        xpb_sc[pl.ds(5, CONV_W - 1), :] = pbc_ref[...]

    def rows(ref):
        x = ref[...]
        if t_in < t:
            x = jnp.concatenate([x, jnp.zeros((t - t_in, x.shape[1]), x.dtype)], axis=0)
        return x

    xs = _conv_silu(xpx_sc, rows(xs_ref), cwx_ref, cbx_ref, t)
    bcm = _conv_silu(xpb_sc, rows(bc_ref), cwb_ref, cbb_ref, t)
    z = rows(z_ref)

    row = lax.broadcasted_iota(jnp.int32, (t, LANES), 0)
    n_valid = jnp.minimum(valid_len - c * t, t_in)
    dt = jnp.where(row < n_valid, _softplus(rows(sm_ref) + dtb_ref[...]), 0.0)
    a = dt * aneg_ref[...]
    ri = lax.broadcasted_iota(jnp.int32, (t, t), 0)
    ci = lax.broadcasted_iota(jnp.int32, (t, t), 1)
    causal = ri >= ci
    acum = _dot_exact(jnp.where(causal, 1.0, 0.0), a)
    acum_t = acum.T
    ex = jnp.dot(jnp.concatenate(_bf16_pieces(dt) + _bf16_pieces(acum), axis=0), expand_ref[...],
                 preferred_element_type=F32)
    dtx = ex[:t] + ex[t:2 * t] + ex[2 * t:3 * t]
    acx = ex[3 * t:4 * t] + ex[4 * t:5 * t] + ex[5 * t:]
    last = acx[t - 1:t, :]
    xdt = xs * dtx
    xw = xdt * jnp.exp(last - acx)
    eacx = jnp.exp(acx)
    elast = jnp.exp(last)

    gw = D_INNER // SSM_GROUPS
    hpg = SSM_HEADS // SSM_GROUPS
    ys = []
    for g in range(SSM_GROUPS):
        bg = bcm[:, g * SSM_STATE:(g + 1) * SSM_STATE]
        cg = bcm[:, (SSM_GROUPS + g) * SSM_STATE:(SSM_GROUPS + g + 1) * SSM_STATE]
        cb = _dot_nt(cg, bg)
        h_t = state_sc[g]
        yg = [_dot(cg, h_t) * eacx[:, g * gw:(g + 1) * gw]]
        intra = []
        for e in range(hpg):
            hd = g * hpg + e
            seg = acum[:, SM_DT + hd:SM_DT + hd + 1] - acum_t[SM_DT + hd:SM_DT + hd + 1, :]
            m = cb * jnp.exp(jnp.where(causal, seg, -jnp.inf))
            intra.append(_dot(m, xdt[:, hd * SSM_HEAD_DIM:(hd + 1) * SSM_HEAD_DIM]))
        ys.append(yg[0] + jnp.concatenate(intra, axis=1))
        state_sc[g] = h_t * elast[:, g * gw:(g + 1) * gw] + _dot(bg.T, xw[:, g * gw:(g + 1) * gw])
    y = jnp.concatenate(ys, axis=1) + xs * dskip_ref[...]
    y = y * (z * jax.nn.sigmoid(z))
    outs = []
    for g in range(SSM_GROUPS):
        yg = y[:, g * gw:(g + 1) * gw]
        outs.append(yg * lax.rsqrt(jnp.mean(yg * yg, axis=-1, keepdims=True) + NORM_EPS))
    y = jnp.concatenate(outs, axis=1) * normw_ref[...]
    y_ref[...] = y[:t_in].astype(y_ref.dtype)

    @pl.when(c == pl.num_programs(1) - 1)
    def _():
        hout_ref[...] = state_sc[...]


def _ssd(proj, row0, n_seq, t_in, n_chunks, valid_len, h0_t, prefix, lw, out_dtype):
    assert row0 % t_in == 0
    rb0 = row0 // t_in
    rowblk = lambda w, col: pl.BlockSpec((t_in, w), lambda s, c: (rb0 + s * n_chunks + c, col // w))
    per_seq = lambda shape: pl.BlockSpec((None,) + shape, lambda s, c: (s,) + (0,) * len(shape))
    const = lambda arr: pl.BlockSpec(arr.shape, lambda s, c: (0,) * arr.ndim)
    gw = D_INNER // SSM_GROUPS
    pxs, pbc = prefix[:, :, :D_INNER], prefix[:, :, D_INNER:]
    consts = (lw["cwx"], lw["cbx"], lw["cwb"], lw["cbb"], lw["dtb"], lw["aneg"], lw["dskip"], lw["normw"], lw["expand"])
    y, h_t = pl.pallas_call(
        functools.partial(_ssd_kernel, t_in=t_in, valid_len=valid_len),
        grid=(n_seq, n_chunks),
        in_specs=[rowblk(D_INNER, C_Z), rowblk(D_INNER, C_XS), rowblk(BC_DIM, C_BC), rowblk(LANES, C_SM),
                  per_seq((SSM_GROUPS, SSM_STATE, gw)), per_seq((CONV_W - 1, D_INNER)), per_seq((CONV_W - 1, BC_DIM))]
                 + [const(a) for a in consts],
        out_specs=[pl.BlockSpec((t_in, D_INNER), lambda s, c: (s * n_chunks + c, 0)),
                   per_seq((SSM_GROUPS, SSM_STATE, gw))],
        out_shape=[jax.ShapeDtypeStruct((n_seq * n_chunks * t_in, D_INNER), out_dtype),
                   jax.ShapeDtypeStruct((n_seq, SSM_GROUPS, SSM_STATE, gw), F32)],
        scratch_shapes=[pltpu.VMEM((SSM_GROUPS, SSM_STATE, gw), F32),
                        pltpu.VMEM((SSD_CHUNK + 8, D_INNER), F32),
                        pltpu.VMEM((SSD_CHUNK + 8, BC_DIM), F32)],
        compiler_params=_params("parallel", "arbitrary"),
        name="ssd",
    )(proj, proj, proj, proj, h0_t, pxs, pbc, *consts)
    return y, h_t


def _state_to_t(h):
    n = h.shape[0]
    hpg = SSM_HEADS // SSM_GROUPS
    return h.reshape(n, SSM_GROUPS, hpg, SSM_HEAD_DIM, SSM_STATE).transpose(0, 1, 4, 2, 3).reshape(
        n, SSM_GROUPS, SSM_STATE, hpg * SSM_HEAD_DIM)


def _state_from_t(h_t):
    n = h_t.shape[0]
    hpg = SSM_HEADS // SSM_GROUPS
    return h_t.reshape(n, SSM_GROUPS, SSM_STATE, hpg, SSM_HEAD_DIM).transpose(0, 1, 3, 4, 2).reshape(
        n, SSM_HEADS, SSM_HEAD_DIM, SSM_STATE)


def _sort_key(x):
    bits = lax.bitcast_convert_type(x, jnp.int32)
    bits = jnp.where(bits == INT_MIN, 0, bits)
    return bits ^ ((bits >> 31) & 0x7FFFFFFF)


def _fold_rows(x, group=8):
    x = x.reshape(x.shape[0] // group, group, x.shape[1])
    while x.shape[0] > 1:
        half, odd = divmod(x.shape[0], 2)
        folded = x[:half] + x[half:2 * half]
        if odd:
            folded = jnp.concatenate([folded[:1] + x[2 * half:], folded[1:]], axis=0) if half > 1 else folded + x[2 * half:]
        x = folded
    return x[0]


def _count_t(keys_sc, n_chunks, pred):
    def body(c, cnt):
        return cnt + _fold_rows(jnp.where(pred(keys_sc[c]), 1.0, 0.0))

    cnt = lax.fori_loop(0, n_chunks, body, jnp.zeros((8, LANES), F32))
    return jnp.sum(cnt, axis=0, keepdims=True)


I16_MIN, I16_MAX = -(2 ** 15), 2 ** 15 - 1


def _kth_half_t(half_sc, n_chunks, k):
    def count(cand):
        c16 = cand.astype(jnp.int16)

        def body(c, cnt):
            return cnt + _fold_rows(jnp.where(half_sc[c] >= c16, jnp.int16(1), jnp.int16(0)), 16)

        cnt = lax.fori_loop(0, n_chunks, body, jnp.zeros((16, LANES), jnp.int16))
        return jnp.sum(cnt.astype(jnp.int32).astype(F32), axis=0, keepdims=True)

    def bit_body(i, u):
        cand = u | lax.shift_left(jnp.int32(1), 15 - i)
        return jnp.where(count(cand + I16_MIN) >= k, cand, u)

    return lax.fori_loop(0, 16, bit_body, jnp.zeros((1, LANES), jnp.int32)) + I16_MIN


def _kth_key_t(keys_sc, hi_sc, lo_sc, n_chunks, k):
    t_hi = _kth_half_t(hi_sc, n_chunks, k)
    t_hi16 = t_hi.astype(jnp.int16)

    def lo_body(c, carry):
        hi = hi_sc[c]
        lo = ((keys_sc[c] & 0xFFFF) + I16_MIN).astype(jnp.int16)
        lo_sc[c] = jnp.where(hi == t_hi16, lo, jnp.where(hi > t_hi16, jnp.int16(I16_MAX), jnp.int16(I16_MIN)))
        return carry

    lax.fori_loop(0, n_chunks, lo_body, 0)
    t_lo = _kth_half_t(lo_sc, n_chunks, k)
    return t_hi * 65536 + (t_lo - I16_MIN)


def _select_t(keys_sc, hi_sc, lo_sc, bias_sc, n_chunks, k):
    kc = keys_sc.shape[1]
    thr = _kth_key_t(keys_sc, hi_sc, lo_sc, n_chunks, k)
    n_ge = _count_t(keys_sc, n_chunks, lambda key: key >= thr)
    surplus = jnp.max(jnp.where((n_ge > k) & (thr != INT_MIN), 1.0, 0.0))

    @pl.when(surplus == 0.0)
    def _():
        floor = jnp.maximum(thr, INT_MIN + 1)

        def body(c, carry):
            bias_sc[c] = jnp.where(keys_sc[c] >= floor, 0.0, MASK_BIAS)
            return carry

        lax.fori_loop(0, n_chunks, body, 0)

    @pl.when(surplus != 0.0)
    def _():
        need = k - _count_t(keys_sc, n_chunks, lambda key: key > thr)
        ri = lax.broadcasted_iota(jnp.int32, (kc, kc), 0)
        ci = lax.broadcasted_iota(jnp.int32, (kc, kc), 1)
        lower = jnp.where(ci <= ri, 1.0, 0.0).astype(jnp.bfloat16)

        def body(c, run):
            key = keys_sc[c]
            eqf = jnp.where((key == thr) & (key != INT_MIN), 1.0, 0.0)
            rank = jnp.dot(lower, eqf.astype(jnp.bfloat16), preferred_element_type=F32) + run
            sel = (key > thr) | ((eqf > 0.0) & (rank <= need))
            bias_sc[c] = jnp.where(sel, 0.0, MASK_BIAS)
            return run + jnp.sum(_fold_rows(eqf), axis=0, keepdims=True)

        lax.fori_loop(0, n_chunks, body, jnp.zeros((1, LANES), F32))


def _select_s(keys_sc, bias_ref, k):
    n_chunks, rows, width = keys_sc.shape

    def count(pred):
        return jnp.sum(jnp.sum(jnp.where(pred(keys_sc[...]), 1.0, 0.0), axis=0), axis=1, keepdims=True)

    def bit_body(i, res):
        cand = res ^ lax.shift_left(jnp.int32(1), 31 - i)
        return jnp.where(count(lambda key: key >= cand[None]) >= k, cand, res)

    thr = lax.fori_loop(0, 32, bit_body, jnp.full((rows, 1), INT_MIN, jnp.int32))
    n_ge = count(lambda key: key >= thr[None])
    surplus = jnp.max(jnp.where((n_ge > k) & (thr != INT_MIN), 1.0, 0.0))

    @pl.when(surplus == 0.0)
    def _():
        floor = jnp.maximum(thr, INT_MIN + 1)
        bias_ref[...] = jnp.where(keys_sc[...] >= floor[None], 0.0, MASK_BIAS)

    @pl.when(surplus != 0.0)
    def _():
        need = k - count(lambda key: key > thr[None])
        ri = lax.broadcasted_iota(jnp.int32, (width, width), 0)
        ci = lax.broadcasted_iota(jnp.int32, (width, width), 1)
        upper = jnp.where(ri <= ci, 1.0, 0.0).astype(jnp.bfloat16)

        def body(c, run):
            key = keys_sc[c]
            eqf = jnp.where((key == thr) & (key != INT_MIN), 1.0, 0.0)
            rank = jnp.dot(eqf.astype(jnp.bfloat16), upper, preferred_element_type=F32) + run
            sel = (key > thr) | ((eqf > 0.0) & (rank <= need))
            bias_ref[c] = jnp.where(sel, 0.0, MASK_BIAS)
            return run + jnp.sum(eqf, axis=1, keepdims=True)

        lax.fori_loop(0, n_chunks, body, jnp.zeros((rows, 1), F32))


def _pattn_kernel(qi_ref, smq_ref, q_ref, smk_ref, k_ref, v_ref, o_ref,
                  kb_sc, kib_sc, vt_sc, keys_sc, hi_sc, lo_sc, bias_sc, qz_sc, m_sc, acc_sc, *, kc, lp, topk):
    j = pl.program_id(1)
    nq = Q_BLOCK
    nkc = kb_sc.shape[0]

    @pl.when(j == 0)
    def _():
        for c in range(nkc):
            r0 = c * kc
            n = min(kc, lp - r0)

            def chunk(ref):
                x = ref[r0:r0 + n, :]
                return x if n == kc else jnp.concatenate([x, jnp.zeros((kc - n, x.shape[1]), x.dtype)], axis=0)

            kb_sc[c] = chunk(k_ref).astype(MXU_DTYPE)
            kib_sc[c] = chunk(smk_ref).astype(MXU_DTYPE)
            v_t = chunk(v_ref).T
            pad = jnp.concatenate([jnp.ones((1, kc), F32), jnp.zeros((V_ROWS - HEAD_DIM - 1, kc), F32)], axis=0)
            vt_sc[c] = jnp.concatenate([x for hk in range(ATT_KV_HEADS)
                                        for x in (v_t[hk * HEAD_DIM:(hk + 1) * HEAD_DIM], pad)], axis=0).astype(MXU_DTYPE)

    n_chunks = ((j + 1) * nq - 1) // kc + 1

    qi_t = (qi_ref[...] * IDX_DIM ** -0.5).T
    qiz = jnp.concatenate([qi_t[h * IDX_DIM:(h + 1) * IDX_DIM] for h in range(IDX_HEADS)], axis=1)
    qiz = jnp.concatenate([qiz, jnp.zeros((LANES - IDX_DIM, IDX_HEADS * nq), F32)], axis=0).astype(MXU_DTYPE)
    smq_t = smq_ref[...].T
    wi_row = jnp.concatenate([smq_t[SM_WI + h:SM_WI + h + 1] for h in range(IDX_HEADS)], axis=1) * IDX_HEADS ** -0.5
    kpos = lax.broadcasted_iota(jnp.int32, (kc, nq), 0)
    qpos = j * nq + lax.broadcasted_iota(jnp.int32, (kc, nq), 1)

    def score_body(c, carry):
        s = jnp.maximum(jnp.dot(kib_sc[c], qiz, preferred_element_type=F32), 0.0) * wi_row
        acc = s[:, :nq]
        for h in range(1, IDX_HEADS):
            acc = acc + s[:, h * nq:(h + 1) * nq]
        key = jnp.where(c * kc + kpos <= qpos, _sort_key(acc), INT_MIN)
        keys_sc[c] = key
        hi_sc[c] = (key >> 16).astype(jnp.int16)
        return carry

    lax.fori_loop(0, n_chunks, score_body, 0)
    _select_t(keys_sc, hi_sc, lo_sc, bias_sc, n_chunks, topk)

    q_t = (q_ref[...] * (HEAD_DIM ** -0.5 * LOG2_E)).T
    gw = ATT_GROUP * nq
    zero = jnp.zeros((HEAD_DIM, gw), F32)
    for hk in range(ATT_KV_HEADS):
        own = jnp.concatenate([q_t[(hk * ATT_GROUP + g) * HEAD_DIM:(hk * ATT_GROUP + g + 1) * HEAD_DIM]
                               for g in range(ATT_GROUP)], axis=1)
        qz_sc[hk] = jnp.concatenate([own if i == hk else zero for i in range(ATT_KV_HEADS)],
                                    axis=0).astype(MXU_DTYPE)
    m_sc[...] = jnp.full(m_sc.shape, MASK_BIAS, F32)
    acc_sc[...] = jnp.zeros(acc_sc.shape, F32)

    def att_body(c, carry):
        kch = kb_sc[c]
        vt = vt_sc[c]
        b = bias_sc[c]
        b4 = jnp.concatenate([b] * ATT_GROUP, axis=1)

        def qk(hk):
            return jnp.dot(kch, qz_sc[hk], preferred_element_type=F32) + b4

        def soft(hk, s):
            m_old = m_sc[hk]
            m_new = jnp.maximum(m_old, jnp.max(s, axis=0, keepdims=True))
            m_sc[hk] = m_new
            return jnp.exp2(m_old - m_new), jnp.exp2(s - m_new).astype(MXU_DTYPE)

        def pv(hk, alpha, p):
            acc_sc[hk] = alpha * acc_sc[hk] + jnp.dot(vt[hk * V_ROWS:(hk + 1) * V_ROWS], p,
                                                      preferred_element_type=F32)

        nh = ATT_KV_HEADS
        s = {0: qk(0), 1: qk(1)}
        ap = {}
        for hk in range(nh):
            ap[hk] = soft(hk, s.pop(hk))
            if hk + 2 < nh:
                s[hk + 2] = qk(hk + 2)
            if hk >= 1:
                pv(hk - 1, *ap.pop(hk - 1))
        pv(nh - 1, *ap.pop(nh - 1))
        return carry

    lax.fori_loop(0, n_chunks, att_body, 0)

    heads = []
    for hk in range(ATT_KV_HEADS):
        acc = acc_sc[hk]
        o = acc[:HEAD_DIM] / acc[HEAD_DIM:HEAD_DIM + 1]
        heads += [o[:, g * nq:(g + 1) * nq] for g in range(ATT_GROUP)]
    o_ref[...] = jnp.concatenate(heads, axis=0).T.astype(o_ref.dtype)


def _prompt_attention(proj, n_seq, lp, topk):
    nqb = lp // Q_BLOCK
    kc = 512
    nkc = -(-lp // kc)
    qblk = lambda w, col: pl.BlockSpec((Q_BLOCK, w), lambda b, j: (b * nqb + j, col // w))
    seqblk = lambda w, col: pl.BlockSpec((lp, w), lambda b, j: (b, col // w))
    gw = ATT_GROUP * Q_BLOCK
    return pl.pallas_call(
        functools.partial(_pattn_kernel, kc=kc, lp=lp, topk=topk),
        grid=(n_seq, nqb),
        in_specs=[qblk(QI_DIM, C_QI), qblk(LANES, C_SM), qblk(ATT_DIM, C_Q),
                  seqblk(LANES, C_SM), seqblk(KV_DIM, C_K), seqblk(KV_DIM, C_V)],
        out_specs=pl.BlockSpec((Q_BLOCK, ATT_DIM), lambda b, j: (b * nqb + j, 0)),
        out_shape=jax.ShapeDtypeStruct((n_seq * lp, ATT_DIM), MXU_DTYPE),
        scratch_shapes=[pltpu.VMEM((nkc, kc, KV_DIM), MXU_DTYPE), pltpu.VMEM((nkc, kc, LANES), MXU_DTYPE),
                        pltpu.VMEM((nkc, ATT_KV_HEADS * V_ROWS, kc), MXU_DTYPE),
                        pltpu.VMEM((nkc, kc, Q_BLOCK), jnp.int32), pltpu.VMEM((nkc, kc, Q_BLOCK), jnp.int16),
                        pltpu.VMEM((nkc, kc, Q_BLOCK), jnp.int16), pltpu.VMEM((nkc, kc, Q_BLOCK), F32),
                        pltpu.VMEM((ATT_KV_HEADS, KV_DIM, gw), MXU_DTYPE),
                        pltpu.VMEM((ATT_KV_HEADS, 1, gw), F32),
                        pltpu.VMEM((ATT_KV_HEADS, V_ROWS, gw), F32)],
        compiler_params=_params("parallel", "arbitrary"),
        name="prompt_attention",
    )(proj, proj, proj, proj, proj, proj)


def _pad_rows(x, n):
    return jnp.concatenate([x, jnp.zeros((n - x.shape[0], x.shape[1]), x.dtype)], axis=0)


def _sselect_kernel(pt_ref, qi_ref, sm_ref, *rest, pages_per_step, n_pages, page, ns, topk):
    page_refs = rest[:pages_per_step]
    bias_ref, keys_sc = rest[pages_per_step:]
    g = pl.program_id(1)
    qi = (qi_ref[...] * IDX_DIM ** -0.5).astype(MXU_DTYPE)
    qs = jnp.concatenate([qi[:, h * IDX_DIM:(h + 1) * IDX_DIM] for h in range(IDX_HEADS)], axis=0)
    sm = sm_ref[...]
    wi = sm[:, SM_WI:SM_WI + IDX_HEADS] * IDX_HEADS ** -0.5

    def weigh(s):
        s = jnp.maximum(s, 0.0)
        acc = s[:ns] * wi[:, 0:1]
        for h in range(1, IDX_HEADS):
            acc = acc + s[h * ns:(h + 1) * ns] * wi[:, h:h + 1]
        return _sort_key(acc)

    keys = weigh(_dot(qs, jnp.concatenate([r[...] for r in page_refs], axis=1)))
    for i in range(pages_per_step):
        keys_sc[g * pages_per_step + i] = keys[:, i * page:(i + 1) * page]

    @pl.when(g == pl.num_programs(1) - 1)
    def _():
        qrow = lax.broadcasted_iota(jnp.int32, (ns, page), 0)
        lane = lax.broadcasted_iota(jnp.int32, (ns, page), 1)
        new = weigh(_dot_nt(qs, _pad_rows(sm[:, SM_KI:SM_KI + IDX_DIM], page)))
        keys_sc[n_pages] = jnp.where(lane <= qrow, new, INT_MIN)
        _select_s(keys_sc, bias_ref, topk)


def _sattn_kernel(pt_ref, q_ref, kn_ref, vn_ref, bias_ref, *rest, pages_per_step, n_pages, page, ns):
    k_refs = rest[:pages_per_step]
    v_refs = rest[pages_per_step:2 * pages_per_step]
    o_ref, qbd_sc, m_sc, l_sc, acc_sc = rest[2 * pages_per_step:]
    g = pl.program_id(1)

    @pl.when(g == 0)
    def _():
        q = q_ref[...] * HEAD_DIM ** -0.5
        rows = []
        for h in range(ATT_HEADS):
            hk = h // ATT_GROUP
            parts = [jnp.zeros((ns, HEAD_DIM), F32)] * ATT_KV_HEADS
            parts[hk] = q[:, h * HEAD_DIM:(h + 1) * HEAD_DIM]
            rows.append(jnp.concatenate(parts, axis=1))
        qbd_sc[...] = jnp.concatenate(rows, axis=0).astype(MXU_DTYPE)
        m_sc[...] = jnp.full(m_sc.shape, MASK_BIAS, F32)
        l_sc[...] = jnp.zeros(l_sc.shape, F32)
        acc_sc[...] = jnp.zeros(acc_sc.shape, F32)

    def attend(s, b, pv):
        s = s + jnp.concatenate([b] * ATT_HEADS, axis=0)
        m_old = m_sc[...]
        m_new = jnp.maximum(m_old, jnp.max(s, axis=1, keepdims=True))
        alpha = jnp.exp(m_old - m_new)
        p = jnp.exp(s - m_new)
        l_sc[...] = alpha * l_sc[...] + jnp.sum(p, axis=1, keepdims=True)
        acc_sc[...] = alpha * acc_sc[...] + pv(p)
        m_sc[...] = m_new

    kcat = jnp.concatenate([r[...] for r in k_refs], axis=1)
    vcat = jnp.concatenate([r[...] for r in v_refs], axis=1)
    bcat = jnp.concatenate([bias_ref[g * pages_per_step + i] for i in range(pages_per_step)], axis=1)
    attend(_dot(qbd_sc[...], kcat), bcat, lambda p: _dot_nt(p, vcat))

    @pl.when(g == pl.num_programs(1) - 1)
    def _():
        vn = _pad_rows(vn_ref[...], page)
        attend(_dot_nt(qbd_sc[...], _pad_rows(kn_ref[...], page)), bias_ref[n_pages], lambda p: _dot(p, vn))
        o = acc_sc[...] / l_sc[...]
        o_ref[...] = jnp.concatenate(
            [o[h * ns:(h + 1) * ns, (h // ATT_GROUP) * HEAD_DIM:(h // ATT_GROUP + 1) * HEAD_DIM]
             for h in range(ATT_HEADS)], axis=1)


def _key_minor(cache):
    nd = cache.ndim
    t = jnp.transpose(cache, (0, 1) + tuple(range(3, nd)) + (2,))
    return t.reshape(t.shape[:2] + (-1, t.shape[-1]))


def _sample_attention(proj, row0, page_table, ck_t, cv_t, cki_t, layer, ns, topk):
    bd, n_pages = page_table.shape
    page = cki_t.shape[3]
    assert page == LANES and row0 % ns == 0
    pps = max(d for d in range(1, 17) if n_pages % d == 0)
    ng = n_pages // pps
    rb0 = row0 // ns
    rowblk = lambda w, col: pl.BlockSpec((ns, w), lambda s, g, pt: (rb0 + s, col // w))
    pageblk = lambda w, i: pl.BlockSpec((None, None, w, page), lambda s, g, pt: (layer, pt[s, g * pps + i], 0, 0))
    biasblk = pl.BlockSpec((None, n_pages + 1, ns, page), lambda s, g, pt: (s, 0, 0, 0))
    static = dict(pages_per_step=pps, n_pages=n_pages, page=page, ns=ns)

    bias = pl.pallas_call(
        functools.partial(_sselect_kernel, topk=topk, **static),
        grid_spec=pltpu.PrefetchScalarGridSpec(
            num_scalar_prefetch=1, grid=(bd, ng),
            in_specs=[rowblk(QI_DIM, C_QI), rowblk(LANES, C_SM)] + [pageblk(IDX_DIM, i) for i in range(pps)],
            out_specs=biasblk,
            scratch_shapes=[pltpu.VMEM((n_pages + 1, ns, page), jnp.int32)]),
        out_shape=jax.ShapeDtypeStruct((bd, n_pages + 1, ns, page), F32),
        compiler_params=_params("parallel", "arbitrary"),
        name="sample_select",
    )(page_table, proj, proj, *([cki_t] * pps))

    rows = ATT_HEADS * ns
    return pl.pallas_call(
        functools.partial(_sattn_kernel, **static),
        grid_spec=pltpu.PrefetchScalarGridSpec(
            num_scalar_prefetch=1, grid=(bd, ng),
            in_specs=[rowblk(ATT_DIM, C_Q), rowblk(KV_DIM, C_K), rowblk(KV_DIM, C_V), biasblk]
                     + [pageblk(KV_DIM, i) for i in range(pps)] * 2,
            out_specs=pl.BlockSpec((ns, ATT_DIM), lambda s, g, pt: (s, 0)),
            scratch_shapes=[pltpu.VMEM((rows, KV_DIM), MXU_DTYPE), pltpu.VMEM((rows, 1), F32),
                            pltpu.VMEM((rows, 1), F32), pltpu.VMEM((rows, KV_DIM), F32)]),
        out_shape=jax.ShapeDtypeStruct((bd * ns, ATT_DIM), F32),
        compiler_params=_params("parallel", "arbitrary"),
        name="sample_attention",
    )(page_table, proj, proj, proj, bias, *([ck_t] * pps), *([cv_t] * pps))


def _regroup_w_in(w):
    o_dt = D_INNER + D_INNER + BC_DIM
    o_q = o_dt + SSM_HEADS
    o_ki = o_q + ATT_DIM + 2 * KV_DIM + QI_DIM
    o_wi = o_ki + IDX_DIM
    o_g = o_wi + IDX_HEADS
    parts = [w[:, :o_dt], w[:, o_q:o_ki], w[:, o_g:o_g + 2 * D_MODEL], w[:, o_ki:o_wi], w[:, o_dt:o_q], w[:, o_wi:o_g]]
    used = sum(p.shape[1] for p in parts)
    parts.append(jnp.zeros((w.shape[0], PROJ_COLS - used), w.dtype))
    w = jnp.concatenate(parts, axis=1).astype(MXU_DTYPE)
    return w.reshape(w.shape[0], PROJ_COLS // PROJ_TN, PROJ_TN).transpose(1, 0, 2)


def _small_lanes(v):
    return jnp.zeros((1, LANES), F32).at[0, SM_DT:SM_DT + SSM_HEADS].set(v.astype(F32))


def _layer_consts(conv_w, conv_b, dt_bias, a_log, d_skip, ssm_norm_w):
    head_of_lane = jnp.arange(D_INNER) // SSM_HEAD_DIM
    expand = (jnp.arange(LANES)[:, None] == (SM_DT + head_of_lane)[None, :]).astype(jnp.bfloat16)
    return dict(
        cwx=conv_w[:, :D_INNER], cbx=conv_b[None, :D_INNER], cwb=conv_w[:, D_INNER:], cbb=conv_b[None, D_INNER:],
        dtb=_small_lanes(dt_bias), aneg=_small_lanes(-jnp.exp(a_log.astype(F32))),
        dskip=jnp.repeat(d_skip.astype(F32), SSM_HEAD_DIM)[None, :], normw=ssm_norm_w.astype(F32)[None, :],
        expand=expand)


def kernel(x_prompt, x_sample, cache_k, cache_v, cache_kidx, state_ssm, state_conv, page_table, meta_tokens, norm1_w, w_in, conv_w, conv_b, dt_bias, a_log, d_skip, ssm_norm_w, w_ssm_proj, w_attn_proj, w_out, norm2_w, w_up, w_down, final_norm_w):
    b, seq, d = x_prompt.shape
    bd, ns = x_sample.shape[:2]
    depth = w_in.shape[0]
    assert d == D_MODEL and ns % 8 == 0
    past = page_table.shape[1] * cache_kidx.shape[2]
    lv = N_META + seq
    lp = -(-lv // Q_BLOCK) * Q_BLOCK
    rp = b * lp
    rs = bd * ns
    r = -(-(rp + rs) // LANES) * LANES
    topk_p = min(TOPK_MAX, seq // 4)
    topk_s = min(TOPK_MAX, (past + ns) // 4)

    hp = jnp.concatenate([jnp.broadcast_to(meta_tokens[None].astype(F32), (b, N_META, d)), x_prompt,
                          jnp.zeros((b, lp - lv, d), F32)], axis=1).reshape(rp, d)
    h = jnp.concatenate([hp, x_sample.reshape(rs, d), jnp.zeros((r - rp - rs, d), F32)], axis=0)

    ck_t, cv_t, cki_t = _key_minor(cache_k), _key_minor(cache_v), _key_minor(cache_kidx)
    zero_state = jnp.zeros((b, SSM_GROUPS, SSM_STATE, D_INNER // SSM_GROUPS), F32)
    zero_prefix = jnp.zeros((b, CONV_W - 1, D_INNER + BC_DIM), F32)
    outs = [[] for _ in range(10)]
    for l in range(depth):
        lw = _layer_consts(conv_w[l], conv_b[l], dt_bias[l], a_log[l], d_skip[l], ssm_norm_w[l])
        proj = _in_proj(_rmsnorm(h, norm1_w[l], MXU_DTYPE), _regroup_w_in(w_in[l]))

        yp, hp_t = _ssd(proj, 0, b, SSD_CHUNK, lp // SSD_CHUNK, lv, zero_state, zero_prefix, lw, MXU_DTYPE)
        ysm, hs_t = _ssd(proj, rp, bd, ns, 1, ns, _state_to_t(state_ssm[l]), state_conv[l], lw, F32)
        ap = _prompt_attention(proj, b, lp, topk_p)
        asm = _sample_attention(proj, rp, page_table, ck_t, cv_t, cki_t, l, ns, topk_s)
        tail = r - rp - rs
        y_ssm = jnp.concatenate([yp, ysm.astype(MXU_DTYPE), jnp.zeros((tail, D_INNER), MXU_DTYPE)], axis=0)
        y_att = jnp.concatenate([ap, asm.astype(MXU_DTYPE), jnp.zeros((tail, ATT_DIM), MXU_DTYPE)], axis=0)

        h = _merge(h, y_ssm, y_att, proj, w_ssm_proj[l].astype(MXU_DTYPE), w_attn_proj[l].astype(MXU_DTYPE),
                   w_out[l].astype(MXU_DTYPE))
        h = _mlp(h, norm2_w[l], w_up[l].astype(MXU_DTYPE), w_down[l].astype(MXU_DTYPE))

        def pcols(c0, w):
            return proj[:rp, c0:c0 + w].reshape(b, lp, w)[:, :lv]

        def scols(c0, w):
            return proj[rp:rp + rs, c0:c0 + w].reshape(bd, ns, w)

        xbc_w = D_INNER + BC_DIM
        p_conv = jnp.stack([proj[s * lp + lv - (CONV_W - 1):s * lp + lv, C_XS:C_XS + xbc_w] for s in range(b)])
        s_conv = jnp.concatenate([state_conv[l], scols(C_XS, xbc_w)], axis=1)[:, -(CONV_W - 1):]
        for acc, t in zip(outs, (
                pcols(C_K, KV_DIM).reshape(b, lv, ATT_KV_HEADS, HEAD_DIM),
                pcols(C_V, KV_DIM).reshape(b, lv, ATT_KV_HEADS, HEAD_DIM),
                pcols(C_SM + SM_KI, IDX_DIM),
                _state_from_t(hp_t),
                p_conv,
                scols(C_K, KV_DIM).reshape(bd, ns, ATT_KV_HEADS, HEAD_DIM),
                scols(C_V, KV_DIM).reshape(bd, ns, ATT_KV_HEADS, HEAD_DIM),
                scols(C_SM + SM_KI, IDX_DIM),
                _state_from_t(hs_t),
                s_conv)):
            acc.append(t)

    y = _rmsnorm(h, final_norm_w, F32)
    y_prompt = y[:rp].reshape(b, lp, d)[:, N_META:lv]
    y_sample = y[rp:rp + rs].reshape(bd, ns, d)
    return (y_prompt, y_sample) + tuple(jnp.stack(o) for o in outs)
```

```python
import functools

import jax
import jax.numpy as jnp
from jax import lax
from jax.experimental import pallas as pl
from jax.experimental.pallas import tpu as pltpu

F32 = jnp.float32
MXU_DTYPE = jnp.bfloat16

N_META = 16
NORM_EPS = 1e-6
SSM_HEAD_DIM = 64
SSM_GROUPS = 4
SSM_STATE = 128
CONV_W = 4
ATT_HEADS = 16
ATT_KV_HEADS = 4
ATT_GROUP = ATT_HEADS // ATT_KV_HEADS
HEAD_DIM = 64
IDX_HEADS = 8
IDX_DIM = 64
TOPK_MAX = 256

LANES = 128
Q_BLOCK = 128
SSD_CHUNK = 128
VMEM_LIMIT = 56 << 20
MASK_BIAS = -1e30
INT_MIN = -(2 ** 31)
LOG2_E = 1.4426950408889634
V_ROWS = HEAD_DIM + 16

D_MODEL = 1024
D_INNER = 2 * D_MODEL
BC_DIM = 2 * SSM_GROUPS * SSM_STATE
SSM_HEADS = D_INNER // SSM_HEAD_DIM
ATT_DIM = ATT_HEADS * HEAD_DIM
KV_DIM = ATT_KV_HEADS * HEAD_DIM
QI_DIM = IDX_HEADS * IDX_DIM
C_Z = 0
C_XS = C_Z + D_INNER
C_BC = C_XS + D_INNER
C_Q = C_BC + BC_DIM
C_K = C_Q + ATT_DIM
C_V = C_K + KV_DIM
C_QI = C_V + KV_DIM
C_GS = C_QI + QI_DIM
C_GA = C_GS + D_MODEL
C_SM = C_GA + D_MODEL
SM_KI = 0
SM_DT = IDX_DIM
SM_WI = IDX_DIM + SSM_HEADS
PROJ_TN = 512
PROJ_COLS = -(-(C_SM + LANES) // PROJ_TN) * PROJ_TN


def _divisor_tile(n, max_tile, align):
    best = None
    for t in range(align, min(n, max_tile) + 1, align):
        if n % t == 0:
            best = t
    assert best is not None, (n, max_tile, align)
    return best


def _params(*sem):
    return pltpu.CompilerParams(dimension_semantics=sem, vmem_limit_bytes=VMEM_LIMIT)


def _dot(a, b):
    return jnp.dot(a.astype(MXU_DTYPE), b.astype(MXU_DTYPE), preferred_element_type=F32)


def _dot_nt(a, b):
    return lax.dot_general(a.astype(MXU_DTYPE), b.astype(MXU_DTYPE), (((1,), (1,)), ((), ())),
                           preferred_element_type=F32)


def _bf16_pieces(x):
    hi = x.astype(jnp.bfloat16)
    r = x - hi.astype(F32)
    mid = r.astype(jnp.bfloat16)
    return [hi, mid, (r - mid.astype(F32)).astype(jnp.bfloat16)]


def _dot_exact(a, b):
    return jnp.dot(a, b, preferred_element_type=F32, precision=lax.Precision.HIGHEST)


def _rmsnorm_kernel(x_ref, w_ref, o_ref):
    x = x_ref[...]
    ms = jnp.mean(x * x, axis=-1, keepdims=True)
    o_ref[...] = (x * lax.rsqrt(ms + NORM_EPS) * w_ref[...]).astype(o_ref.dtype)


def _rmsnorm(x, w, out_dtype):
    r, d = x.shape
    tm = _divisor_tile(r, 512, 16)
    return pl.pallas_call(
        _rmsnorm_kernel,
        grid=(r // tm,),
        in_specs=[pl.BlockSpec((tm, d), lambda i: (i, 0)), pl.BlockSpec((1, d), lambda i: (0, 0))],
        out_specs=pl.BlockSpec((tm, d), lambda i: (i, 0)),
        out_shape=jax.ShapeDtypeStruct((r, d), out_dtype),
        compiler_params=_params("parallel"),
        name="rmsnorm",
    )(x, w.reshape(1, d))


def _in_proj_kernel(x_ref, w_ref, o_ref):
    o_ref[...] = jnp.dot(x_ref[...], w_ref[pl.program_id(1)], preferred_element_type=F32)


def _in_proj(u, w):
    r, k = u.shape
    nt = w.shape[0]
    tm = _divisor_tile(r, 1100, 16)
    return pl.pallas_call(
        _in_proj_kernel,
        grid=(r // tm, nt),
        in_specs=[pl.BlockSpec((tm, k), lambda i, j: (i, 0)), pl.BlockSpec(w.shape, lambda i, j: (0, 0, 0))],
        out_specs=pl.BlockSpec((tm, PROJ_TN), lambda i, j: (i, j)),
        out_shape=jax.ShapeDtypeStruct((r, nt * PROJ_TN), F32),
        compiler_params=_params("parallel", "arbitrary"),
        name="in_proj",
    )(u, w)


def _merge_kernel(h_ref, ys_ref, ya_ref, gs_ref, ga_ref, wsp_ref, wap_ref, wo_ref, o_ref):
    a = jnp.dot(ys_ref[...], wsp_ref[...], preferred_element_type=F32)
    b = jnp.dot(ya_ref[...], wap_ref[...], preferred_element_type=F32)
    m = jax.nn.sigmoid(gs_ref[...]) * a + jax.nn.sigmoid(ga_ref[...]) * b
    o_ref[...] = h_ref[...] + _dot(m, wo_ref[...])


def _merge(h, y_ssm, y_att, proj, w_sp, w_ap, w_o):
    r, d = h.shape
    tm = _divisor_tile(r, 256, 16)
    const = lambda i: (0, 0)
    return pl.pallas_call(
        _merge_kernel,
        grid=(r // tm,),
        in_specs=[pl.BlockSpec((tm, d), lambda i: (i, 0)),
                  pl.BlockSpec((tm, D_INNER), lambda i: (i, 0)),
                  pl.BlockSpec((tm, ATT_DIM), lambda i: (i, 0)),
                  pl.BlockSpec((tm, d), lambda i: (i, C_GS // D_MODEL)),
                  pl.BlockSpec((tm, d), lambda i: (i, C_GA // D_MODEL)),
                  pl.BlockSpec(w_sp.shape, const), pl.BlockSpec(w_ap.shape, const), pl.BlockSpec(w_o.shape, const)],
        out_specs=pl.BlockSpec((tm, d), lambda i: (i, 0)),
        out_shape=jax.ShapeDtypeStruct((r, d), F32),
        compiler_params=_params("parallel"),
        name="merge",
    )(h, y_ssm, y_att, proj, proj, w_sp, w_ap, w_o)


def _mlp_kernel(h_ref, nw_ref, wu_ref, wd_ref, o_ref):
    x = h_ref[...]
    ms = jnp.mean(x * x, axis=-1, keepdims=True)
    u = x * lax.rsqrt(ms + NORM_EPS) * nw_ref[...]
    a = _dot(u, wu_ref[...])
    a = jnp.square(jnp.maximum(a, 0.0))
    o_ref[...] = x + _dot(a, wd_ref[...])


def _mlp(h, norm_w, w_up, w_down):
    r, d = h.shape
    tm = _divisor_tile(r, 256, 16)
    const = lambda i: (0, 0)
    return pl.pallas_call(
        _mlp_kernel,
        grid=(r // tm,),
        in_specs=[pl.BlockSpec((tm, d), lambda i: (i, 0)), pl.BlockSpec((1, d), const),
                  pl.BlockSpec(w_up.shape, const), pl.BlockSpec(w_down.shape, const)],
        out_specs=pl.BlockSpec((tm, d), lambda i: (i, 0)),
        out_shape=jax.ShapeDtypeStruct((r, d), F32),
        compiler_params=_params("parallel"),
        name="mlp",
    )(h, norm_w.reshape(1, d), w_up, w_down)


def _softplus(x):
    return jnp.maximum(x, 0.0) + jnp.log1p(jnp.exp(-jnp.abs(x)))


def _conv_silu(xp_sc, x, w_ref, b_ref, t):
    xp_sc[pl.ds(8, t), :] = x
    acc = b_ref[...] + x * w_ref[CONV_W - 1:CONV_W, :]
    for k in range(CONV_W - 1):
        acc = acc + xp_sc[pl.ds(5 + k, t), :] * w_ref[k:k + 1, :]
    xp_sc[pl.ds(5, CONV_W - 1), :] = xp_sc[pl.ds(t + 5, CONV_W - 1), :]
    return acc * jax.nn.sigmoid(acc)


def _ssd_kernel(z_ref, xs_ref, bc_ref, sm_ref, h0_ref, pxs_ref, pbc_ref,
                cwx_ref, cbx_ref, cwb_ref, cbb_ref, dtb_ref, aneg_ref, dskip_ref, normw_ref, expand_ref,
                y_ref, hout_ref, state_sc, xpx_sc, xpb_sc, *, t_in, valid_len):
    t = SSD_CHUNK
    c = pl.program_id(1)

    @pl.when(c == 0)
    def _():
        state_sc[...] = h0_ref[...]
        xpx_sc[pl.ds(5, CONV_W - 1), :] = pxs_ref[...]
        xpb_sc[pl.ds(5, CONV_W - 1), :] = pbc_ref[...]

    def rows(ref):
        x = ref[...]
        if t_in < t:
            x = jnp.concatenate([x, jnp.zeros((t - t_in, x.shape[1]), x.dtype)], axis=0)
        return x

    xs = _conv_silu(xpx_sc, rows(xs_ref), cwx_ref, cbx_ref, t)
    bcm = _conv_silu(xpb_sc, rows(bc_ref), cwb_ref, cbb_ref, t)
    z = rows(z_ref)

    row = lax.broadcasted_iota(jnp.int32, (t, LANES), 0)
    n_valid = jnp.minimum(valid_len - c * t, t_in)
    dt = jnp.where(row < n_valid, _softplus(rows(sm_ref) + dtb_ref[...]), 0.0)
    a = dt * aneg_ref[...]
    ri = lax.broadcasted_iota(jnp.int32, (t, t), 0)
    ci = lax.broadcasted_iota(jnp.int32, (t, t), 1)
    causal = ri >= ci
    acum = _dot_exact(jnp.where(causal, 1.0, 0.0), a)
    acum_t = acum.T
    ex = jnp.dot(jnp.concatenate(_bf16_pieces(dt) + _bf16_pieces(acum), axis=0), expand_ref[...],
                 preferred_element_type=F32)
    dtx = ex[:t] + ex[t:2 * t] + ex[2 * t:3 * t]
    acx = ex[3 * t:4 * t] + ex[4 * t:5 * t] + ex[5 * t:]
    last = acx[t - 1:t, :]
    xdt = xs * dtx
    xw = xdt * jnp.exp(last - acx)
    eacx = jnp.exp(acx)
    elast = jnp.exp(last)

    gw = D_INNER // SSM_GROUPS
    hpg = SSM_HEADS // SSM_GROUPS
    ys = []
    for g in range(SSM_GROUPS):
        bg = bcm[:, g * SSM_STATE:(g + 1) * SSM_STATE]
        cg = bcm[:, (SSM_GROUPS + g) * SSM_STATE:(SSM_GROUPS + g + 1) * SSM_STATE]
        cb = _dot_nt(cg, bg)
        h_t = state_sc[g]
        yg = [_dot(cg, h_t) * eacx[:, g * gw:(g + 1) * gw]]
        intra = []
        for e in range(hpg):
            hd = g * hpg + e
            seg = acum[:, SM_DT + hd:SM_DT + hd + 1] - acum_t[SM_DT + hd:SM_DT + hd + 1, :]
            m = cb * jnp.exp(jnp.where(causal, seg, -jnp.inf))
            intra.append(_dot(m, xdt[:, hd * SSM_HEAD_DIM:(hd + 1) * SSM_HEAD_DIM]))
        ys.append(yg[0] + jnp.concatenate(intra, axis=1))
        state_sc[g] = h_t * elast[:, g * gw:(g + 1) * gw] + _dot(bg.T, xw[:, g * gw:(g + 1) * gw])
    y = jnp.concatenate(ys, axis=1) + xs * dskip_ref[...]
    y = y * (z * jax.nn.sigmoid(z))
    outs = []
    for g in range(SSM_GROUPS):
        yg = y[:, g * gw:(g + 1) * gw]
        outs.append(yg * lax.rsqrt(jnp.mean(yg * yg, axis=-1, keepdims=True) + NORM_EPS))
    y = jnp.concatenate(outs, axis=1) * normw_ref[...]
    y_ref[...] = y[:t_in].astype(y_ref.dtype)

    @pl.when(c == pl.num_programs(1) - 1)
    def _():
        hout_ref[...] = state_sc[...]


def _ssd(proj, row0, n_seq, t_in, n_chunks, valid_len, h0_t, prefix, lw, out_dtype):
    assert row0 % t_in == 0
    rb0 = row0 // t_in
    rowblk = lambda w, col: pl.BlockSpec((t_in, w), lambda s, c: (rb0 + s * n_chunks + c, col // w))
    per_seq = lambda shape: pl.BlockSpec((None,) + shape, lambda s, c: (s,) + (0,) * len(shape))
    const = lambda arr: pl.BlockSpec(arr.shape, lambda s, c: (0,) * arr.ndim)
    gw = D_INNER // SSM_GROUPS
    pxs, pbc = prefix[:, :, :D_INNER], prefix[:, :, D_INNER:]
    consts = (lw["cwx"], lw["cbx"], lw["cwb"], lw["cbb"], lw["dtb"], lw["aneg"], lw["dskip"], lw["normw"], lw["expand"])
    y, h_t = pl.pallas_call(
        functools.partial(_ssd_kernel, t_in=t_in, valid_len=valid_len),
        grid=(n_seq, n_chunks),
        in_specs=[rowblk(D_INNER, C_Z), rowblk(D_INNER, C_XS), rowblk(BC_DIM, C_BC), rowblk(LANES, C_SM),
                  per_seq((SSM_GROUPS, SSM_STATE, gw)), per_seq((CONV_W - 1, D_INNER)), per_seq((CONV_W - 1, BC_DIM))]
                 + [const(a) for a in consts],
        out_specs=[pl.BlockSpec((t_in, D_INNER), lambda s, c: (s * n_chunks + c, 0)),
                   per_seq((SSM_GROUPS, SSM_STATE, gw))],
        out_shape=[jax.ShapeDtypeStruct((n_seq * n_chunks * t_in, D_INNER), out_dtype),
                   jax.ShapeDtypeStruct((n_seq, SSM_GROUPS, SSM_STATE, gw), F32)],
        scratch_shapes=[pltpu.VMEM((SSM_GROUPS, SSM_STATE, gw), F32),
                        pltpu.VMEM((SSD_CHUNK + 8, D_INNER), F32),
                        pltpu.VMEM((SSD_CHUNK + 8, BC_DIM), F32)],
        compiler_params=_params("parallel", "arbitrary"),
        name="ssd",
    )(proj, proj, proj, proj, h0_t, pxs, pbc, *consts)
    return y, h_t


def _state_to_t(h):
    n = h.shape[0]
    hpg = SSM_HEADS // SSM_GROUPS
    return h.reshape(n, SSM_GROUPS, hpg, SSM_HEAD_DIM, SSM_STATE).transpose(0, 1, 4, 2, 3).reshape(
        n, SSM_GROUPS, SSM_STATE, hpg * SSM_HEAD_DIM)


def _state_from_t(h_t):
    n = h_t.shape[0]
    hpg = SSM_HEADS // SSM_GROUPS
    return h_t.reshape(n, SSM_GROUPS, SSM_STATE, hpg, SSM_HEAD_DIM).transpose(0, 1, 3, 4, 2).reshape(
        n, SSM_HEADS, SSM_HEAD_DIM, SSM_STATE)


def _sort_key(x):
    bits = lax.bitcast_convert_type(x, jnp.int32)
    bits = jnp.where(bits == INT_MIN, 0, bits)
    return bits ^ ((bits >> 31) & 0x7FFFFFFF)


def _fold_rows(x, group=8):
    x = x.reshape(x.shape[0] // group, group, x.shape[1])
    while x.shape[0] > 1:
        half, odd = divmod(x.shape[0], 2)
        folded = x[:half] + x[half:2 * half]
        if odd:
            folded = jnp.concatenate([folded[:1] + x[2 * half:], folded[1:]], axis=0) if half > 1 else folded + x[2 * half:]
        x = folded
    return x[0]


def _count_t(keys_sc, n_chunks, pred):
    def body(c, cnt):
        return cnt + _fold_rows(jnp.where(pred(keys_sc[c]), 1.0, 0.0))

    cnt = lax.fori_loop(0, n_chunks, body, jnp.zeros((8, LANES), F32))
    return jnp.sum(cnt, axis=0, keepdims=True)


I16_MIN, I16_MAX = -(2 ** 15), 2 ** 15 - 1


SEARCH_UNROLLS = (2, 4, 6)


def _search_bounds(nkc):
    return sorted({min(b, nkc) for b in SEARCH_UNROLLS} | {nkc})


def _search_chunks(n_chunks, nkc):
    bounds = _search_bounds(nkc)
    n = jnp.int32(bounds[-1])
    for b in reversed(bounds[:-1]):
        n = jnp.where(n_chunks <= b, b, n)
    return n


def _kth_half_t(half_sc, n, k):
    def count(cand):
        c16 = cand.astype(jnp.int16)
        cnt = None
        for c in range(n):
            hit = _fold_rows(jnp.where(half_sc[c] >= c16, jnp.int16(1), jnp.int16(0)), 16)
            cnt = hit if cnt is None else cnt + hit
        return jnp.sum(cnt.astype(jnp.int32).astype(F32), axis=0, keepdims=True)

    def bit_body(i, u):
        cand = u | lax.shift_left(jnp.int32(1), 15 - i)
        return jnp.where(count(cand + I16_MIN) >= k, cand, u)

    return lax.fori_loop(0, 16, bit_body, jnp.zeros((1, LANES), jnp.int32)) + I16_MIN


def _kth_half_dispatch(half_sc, n_chunks, k):
    bounds = _search_bounds(half_sc.shape[0])

    def build(i):
        if i == len(bounds) - 1:
            return lambda: _kth_half_t(half_sc, bounds[i], k)
        return lambda: lax.cond(n_chunks <= bounds[i], lambda: _kth_half_t(half_sc, bounds[i], k), build(i + 1))

    return build(0)()


def _kth_key_t(keys_sc, hi_sc, lo_sc, n_chunks, k):
    n_search = _search_chunks(n_chunks, keys_sc.shape[0])
    t_hi = _kth_half_dispatch(hi_sc, n_chunks, k)
    t_hi16 = t_hi.astype(jnp.int16)

    def lo_body(c, carry):
        hi = hi_sc[c]
        lo = ((keys_sc[c] & 0xFFFF) + I16_MIN).astype(jnp.int16)
        lo_sc[c] = jnp.where(hi == t_hi16, lo, jnp.where(hi > t_hi16, jnp.int16(I16_MAX), jnp.int16(I16_MIN)))
        return carry

    lax.fori_loop(0, n_search, lo_body, 0)
    t_lo = _kth_half_dispatch(lo_sc, n_chunks, k)
    return t_hi * 65536 + (t_lo - I16_MIN)


def _select_t(keys_sc, hi_sc, lo_sc, bias_sc, n_chunks, k):
    kc = keys_sc.shape[1]
    thr = _kth_key_t(keys_sc, hi_sc, lo_sc, n_chunks, k)
    n_ge = _count_t(keys_sc, n_chunks, lambda key: key >= thr)
    surplus = jnp.max(jnp.where((n_ge > k) & (thr != INT_MIN), 1.0, 0.0))

    @pl.when(surplus == 0.0)
    def _():
        floor = jnp.maximum(thr, INT_MIN + 1)

        def body(c, carry):
            bias_sc[c] = jnp.where(keys_sc[c] >= floor, 0.0, MASK_BIAS)
            return carry

        lax.fori_loop(0, n_chunks, body, 0)

    @pl.when(surplus != 0.0)
    def _():
        need = k - _count_t(keys_sc, n_chunks, lambda key: key > thr)
        ri = lax.broadcasted_iota(jnp.int32, (kc, kc), 0)
        ci = lax.broadcasted_iota(jnp.int32, (kc, kc), 1)
        lower = jnp.where(ci <= ri, 1.0, 0.0).astype(jnp.bfloat16)

        def body(c, run):
            key = keys_sc[c]
            eqf = jnp.where((key == thr) & (key != INT_MIN), 1.0, 0.0)
            rank = jnp.dot(lower, eqf.astype(jnp.bfloat16), preferred_element_type=F32) + run
            sel = (key > thr) | ((eqf > 0.0) & (rank <= need))
            bias_sc[c] = jnp.where(sel, 0.0, MASK_BIAS)
            return run + jnp.sum(_fold_rows(eqf), axis=0, keepdims=True)

        lax.fori_loop(0, n_chunks, body, jnp.zeros((1, LANES), F32))


def _select_s(keys_sc, bias_ref, k):
    n_chunks, rows, width = keys_sc.shape

    def count(pred):
        return jnp.sum(jnp.sum(jnp.where(pred(keys_sc[...]), 1.0, 0.0), axis=0), axis=1, keepdims=True)

    def bit_body(i, res):
        cand = res ^ lax.shift_left(jnp.int32(1), 31 - i)
        return jnp.where(count(lambda key: key >= cand[None]) >= k, cand, res)

    thr = lax.fori_loop(0, 32, bit_body, jnp.full((rows, 1), INT_MIN, jnp.int32))
    n_ge = count(lambda key: key >= thr[None])
    surplus = jnp.max(jnp.where((n_ge > k) & (thr != INT_MIN), 1.0, 0.0))

    @pl.when(surplus == 0.0)
    def _():
        floor = jnp.maximum(thr, INT_MIN + 1)
        bias_ref[...] = jnp.where(keys_sc[...] >= floor[None], 0.0, MASK_BIAS)

    @pl.when(surplus != 0.0)
    def _():
        need = k - count(lambda key: key > thr[None])
        ri = lax.broadcasted_iota(jnp.int32, (width, width), 0)
        ci = lax.broadcasted_iota(jnp.int32, (width, width), 1)
        upper = jnp.where(ri <= ci, 1.0, 0.0).astype(jnp.bfloat16)

        def body(c, run):
            key = keys_sc[c]
            eqf = jnp.where((key == thr) & (key != INT_MIN), 1.0, 0.0)
            rank = jnp.dot(eqf.astype(jnp.bfloat16), upper, preferred_element_type=F32) + run
            sel = (key > thr) | ((eqf > 0.0) & (rank <= need))
            bias_ref[c] = jnp.where(sel, 0.0, MASK_BIAS)
            return run + jnp.sum(eqf, axis=1, keepdims=True)

        lax.fori_loop(0, n_chunks, body, jnp.zeros((rows, 1), F32))


def _pattn_kernel(qi_ref, smq_ref, q_ref, smk_ref, k_ref, v_ref, o_ref,
                  kb_sc, kib_sc, vt_sc, keys_sc, hi_sc, lo_sc, bias_sc, qz_sc, m_sc, acc_sc, *, kc, lp, topk):
    j = pl.program_id(1)
    nq = Q_BLOCK
    nkc = kb_sc.shape[0]

    @pl.when(j == 0)
    def _():
        for c in range(nkc):
            r0 = c * kc
            n = min(kc, lp - r0)

            def chunk(ref):
                x = ref[r0:r0 + n, :]
                return x if n == kc else jnp.concatenate([x, jnp.zeros((kc - n, x.shape[1]), x.dtype)], axis=0)

            kb_sc[c] = chunk(k_ref).astype(MXU_DTYPE)
            kib_sc[c] = chunk(smk_ref).astype(MXU_DTYPE)
            v_t = chunk(v_ref).T
            pad = jnp.concatenate([jnp.ones((1, kc), F32), jnp.zeros((V_ROWS - HEAD_DIM - 1, kc), F32)], axis=0)
            vt_sc[c] = jnp.concatenate([x for hk in range(ATT_KV_HEADS)
                                        for x in (v_t[hk * HEAD_DIM:(hk + 1) * HEAD_DIM], pad)], axis=0).astype(MXU_DTYPE)

    n_chunks = ((j + 1) * nq - 1) // kc + 1

    qi_t = (qi_ref[...] * IDX_DIM ** -0.5).T
    qiz = jnp.concatenate([qi_t[h * IDX_DIM:(h + 1) * IDX_DIM] for h in range(IDX_HEADS)], axis=1)
    qiz = jnp.concatenate([qiz, jnp.zeros((LANES - IDX_DIM, IDX_HEADS * nq), F32)], axis=0).astype(MXU_DTYPE)
    smq_t = smq_ref[...].T
    wi_row = jnp.concatenate([smq_t[SM_WI + h:SM_WI + h + 1] for h in range(IDX_HEADS)], axis=1) * IDX_HEADS ** -0.5
    kpos = lax.broadcasted_iota(jnp.int32, (kc, nq), 0)
    qpos = j * nq + lax.broadcasted_iota(jnp.int32, (kc, nq), 1)

    def score_body(c, carry):
        s = jnp.maximum(jnp.dot(kib_sc[c], qiz, preferred_element_type=F32), 0.0) * wi_row
        acc = s[:, :nq]
        for h in range(1, IDX_HEADS):
            acc = acc + s[:, h * nq:(h + 1) * nq]
        key = jnp.where(c * kc + kpos <= qpos, _sort_key(acc), INT_MIN)
        keys_sc[c] = key
        hi_sc[c] = (key >> 16).astype(jnp.int16)
        return carry

    lax.fori_loop(0, n_chunks, score_body, 0)

    def blank_body(c, carry):
        keys_sc[c] = jnp.full((kc, nq), INT_MIN, jnp.int32)
        hi_sc[c] = jnp.full((kc, nq), I16_MIN, jnp.int16)
        return carry

    lax.fori_loop(n_chunks, _search_chunks(n_chunks, nkc), blank_body, 0)
    _select_t(keys_sc, hi_sc, lo_sc, bias_sc, n_chunks, topk)

    q_t = (q_ref[...] * (HEAD_DIM ** -0.5 * LOG2_E)).T
    gw = ATT_GROUP * nq
    zero = jnp.zeros((HEAD_DIM, gw), F32)
    for hk in range(ATT_KV_HEADS):
        own = jnp.concatenate([q_t[(hk * ATT_GROUP + g) * HEAD_DIM:(hk * ATT_GROUP + g + 1) * HEAD_DIM]
                               for g in range(ATT_GROUP)], axis=1)
        qz_sc[hk] = jnp.concatenate([own if i == hk else zero for i in range(ATT_KV_HEADS)],
                                    axis=0).astype(MXU_DTYPE)
    m_sc[...] = jnp.full(m_sc.shape, MASK_BIAS, F32)
    acc_sc[...] = jnp.zeros(acc_sc.shape, F32)

    def att_body(c, carry):
        kch = kb_sc[c]
        vt = vt_sc[c]
        b = bias_sc[c]
        b4 = jnp.concatenate([b] * ATT_GROUP, axis=1)

        def qk(hk):
            return jnp.dot(kch, qz_sc[hk], preferred_element_type=F32) + b4

        def soft(hk, s):
            m_old = m_sc[hk]
            m_new = jnp.maximum(m_old, jnp.max(s, axis=0, keepdims=True))
            m_sc[hk] = m_new
            return jnp.exp2(m_old - m_new), jnp.exp2(s - m_new).astype(MXU_DTYPE)

        def pv(hk, alpha, p):
            acc_sc[hk] = alpha * acc_sc[hk] + jnp.dot(vt[hk * V_ROWS:(hk + 1) * V_ROWS], p,
                                                      preferred_element_type=F32)

        nh = ATT_KV_HEADS
        s = {0: qk(0), 1: qk(1)}
        ap = {}
        for hk in range(nh):
            ap[hk] = soft(hk, s.pop(hk))
            if hk + 2 < nh:
                s[hk + 2] = qk(hk + 2)
            if hk >= 1:
                pv(hk - 1, *ap.pop(hk - 1))
        pv(nh - 1, *ap.pop(nh - 1))
        return carry

    lax.fori_loop(0, n_chunks, att_body, 0)

    heads = []
    for hk in range(ATT_KV_HEADS):
        acc = acc_sc[hk]
        o = acc[:HEAD_DIM] / acc[HEAD_DIM:HEAD_DIM + 1]
        heads += [o[:, g * nq:(g + 1) * nq] for g in range(ATT_GROUP)]
    o_ref[...] = jnp.concatenate(heads, axis=0).T.astype(o_ref.dtype)


def _prompt_attention(proj, n_seq, lp, topk):
    nqb = lp // Q_BLOCK
    kc = 512
    nkc = -(-lp // kc)
    qblk = lambda w, col: pl.BlockSpec((Q_BLOCK, w), lambda b, j: (b * nqb + j, col // w))
    seqblk = lambda w, col: pl.BlockSpec((lp, w), lambda b, j: (b, col // w))
    gw = ATT_GROUP * Q_BLOCK
    return pl.pallas_call(
        functools.partial(_pattn_kernel, kc=kc, lp=lp, topk=topk),
        grid=(n_seq, nqb),
        in_specs=[qblk(QI_DIM, C_QI), qblk(LANES, C_SM), qblk(ATT_DIM, C_Q),
                  seqblk(LANES, C_SM), seqblk(KV_DIM, C_K), seqblk(KV_DIM, C_V)],
        out_specs=pl.BlockSpec((Q_BLOCK, ATT_DIM), lambda b, j: (b * nqb + j, 0)),
        out_shape=jax.ShapeDtypeStruct((n_seq * lp, ATT_DIM), MXU_DTYPE),
        scratch_shapes=[pltpu.VMEM((nkc, kc, KV_DIM), MXU_DTYPE), pltpu.VMEM((nkc, kc, LANES), MXU_DTYPE),
                        pltpu.VMEM((nkc, ATT_KV_HEADS * V_ROWS, kc), MXU_DTYPE),
                        pltpu.VMEM((nkc, kc, Q_BLOCK), jnp.int32), pltpu.VMEM((nkc, kc, Q_BLOCK), jnp.int16),
                        pltpu.VMEM((nkc, kc, Q_BLOCK), jnp.int16), pltpu.VMEM((nkc, kc, Q_BLOCK), F32),
                        pltpu.VMEM((ATT_KV_HEADS, KV_DIM, gw), MXU_DTYPE),
                        pltpu.VMEM((ATT_KV_HEADS, 1, gw), F32),
                        pltpu.VMEM((ATT_KV_HEADS, V_ROWS, gw), F32)],
        compiler_params=_params("parallel", "arbitrary"),
        name="prompt_attention",
    )(proj, proj, proj, proj, proj, proj)


def _pad_rows(x, n):
    return jnp.concatenate([x, jnp.zeros((n - x.shape[0], x.shape[1]), x.dtype)], axis=0)


def _sselect_kernel(pt_ref, qi_ref, sm_ref, *rest, pages_per_step, n_pages, page, ns, topk):
    page_refs = rest[:pages_per_step]
    bias_ref, keys_sc = rest[pages_per_step:]
    g = pl.program_id(1)
    qi = (qi_ref[...] * IDX_DIM ** -0.5).astype(MXU_DTYPE)
    qs = jnp.concatenate([qi[:, h * IDX_DIM:(h + 1) * IDX_DIM] for h in range(IDX_HEADS)], axis=0)
    sm = sm_ref[...]
    wi = sm[:, SM_WI:SM_WI + IDX_HEADS] * IDX_HEADS ** -0.5

    def weigh(s):
        s = jnp.maximum(s, 0.0)
        acc = s[:ns] * wi[:, 0:1]
        for h in range(1, IDX_HEADS):
            acc = acc + s[h * ns:(h + 1) * ns] * wi[:, h:h + 1]
        return _sort_key(acc)

    keys = weigh(_dot(qs, jnp.concatenate([r[...] for r in page_refs], axis=1)))
    for i in range(pages_per_step):
        keys_sc[g * pages_per_step + i] = keys[:, i * page:(i + 1) * page]

    @pl.when(g == pl.num_programs(1) - 1)
    def _():
        qrow = lax.broadcasted_iota(jnp.int32, (ns, page), 0)
        lane = lax.broadcasted_iota(jnp.int32, (ns, page), 1)
        new = weigh(_dot_nt(qs, _pad_rows(sm[:, SM_KI:SM_KI + IDX_DIM], page)))
        keys_sc[n_pages] = jnp.where(lane <= qrow, new, INT_MIN)
        _select_s(keys_sc, bias_ref, topk)


def _sattn_kernel(pt_ref, q_ref, kn_ref, vn_ref, bias_ref, *rest, pages_per_step, n_pages, page, ns):
    k_refs = rest[:pages_per_step]
    v_refs = rest[pages_per_step:2 * pages_per_step]
    o_ref, qbd_sc, m_sc, l_sc, acc_sc = rest[2 * pages_per_step:]
    g = pl.program_id(1)

    @pl.when(g == 0)
    def _():
        q = q_ref[...] * HEAD_DIM ** -0.5
        rows = []
        for h in range(ATT_HEADS):
            hk = h // ATT_GROUP
            parts = [jnp.zeros((ns, HEAD_DIM), F32)] * ATT_KV_HEADS
            parts[hk] = q[:, h * HEAD_DIM:(h + 1) * HEAD_DIM]
            rows.append(jnp.concatenate(parts, axis=1))
        qbd_sc[...] = jnp.concatenate(rows, axis=0).astype(MXU_DTYPE)
        m_sc[...] = jnp.full(m_sc.shape, MASK_BIAS, F32)
        l_sc[...] = jnp.zeros(l_sc.shape, F32)
        acc_sc[...] = jnp.zeros(acc_sc.shape, F32)

    def attend(s, b, pv):
        s = s + jnp.concatenate([b] * ATT_HEADS, axis=0)
        m_old = m_sc[...]
        m_new = jnp.maximum(m_old, jnp.max(s, axis=1, keepdims=True))
        alpha = jnp.exp(m_old - m_new)
        p = jnp.exp(s - m_new)
        l_sc[...] = alpha * l_sc[...] + jnp.sum(p, axis=1, keepdims=True)
        acc_sc[...] = alpha * acc_sc[...] + pv(p)
        m_sc[...] = m_new

    kcat = jnp.concatenate([r[...] for r in k_refs], axis=1)
    vcat = jnp.concatenate([r[...] for r in v_refs], axis=1)
    bcat = jnp.concatenate([bias_ref[g * pages_per_step + i] for i in range(pages_per_step)], axis=1)
    attend(_dot(qbd_sc[...], kcat), bcat, lambda p: _dot_nt(p, vcat))

    @pl.when(g == pl.num_programs(1) - 1)
    def _():
        vn = _pad_rows(vn_ref[...], page)
        attend(_dot_nt(qbd_sc[...], _pad_rows(kn_ref[...], page)), bias_ref[n_pages], lambda p: _dot(p, vn))
        o = acc_sc[...] / l_sc[...]
        o_ref[...] = jnp.concatenate(
            [o[h * ns:(h + 1) * ns, (h // ATT_GROUP) * HEAD_DIM:(h // ATT_GROUP + 1) * HEAD_DIM]
             for h in range(ATT_HEADS)], axis=1)


def _key_minor(cache):
    nd = cache.ndim
    t = jnp.transpose(cache, (0, 1) + tuple(range(3, nd)) + (2,))
    return t.reshape(t.shape[:2] + (-1, t.shape[-1]))


def _sample_attention(proj, row0, page_table, ck_t, cv_t, cki_t, layer, ns, topk):
    bd, n_pages = page_table.shape
    page = cki_t.shape[3]
    assert page == LANES and row0 % ns == 0
    pps = max(d for d in range(1, 17) if n_pages % d == 0)
    ng = n_pages // pps
    rb0 = row0 // ns
    rowblk = lambda w, col: pl.BlockSpec((ns, w), lambda s, g, pt: (rb0 + s, col // w))
    pageblk = lambda w, i: pl.BlockSpec((None, None, w, page), lambda s, g, pt: (layer, pt[s, g * pps + i], 0, 0))
    biasblk = pl.BlockSpec((None, n_pages + 1, ns, page), lambda s, g, pt: (s, 0, 0, 0))
    static = dict(pages_per_step=pps, n_pages=n_pages, page=page, ns=ns)

    bias = pl.pallas_call(
        functools.partial(_sselect_kernel, topk=topk, **static),
        grid_spec=pltpu.PrefetchScalarGridSpec(
            num_scalar_prefetch=1, grid=(bd, ng),
            in_specs=[rowblk(QI_DIM, C_QI), rowblk(LANES, C_SM)] + [pageblk(IDX_DIM, i) for i in range(pps)],
            out_specs=biasblk,
            scratch_shapes=[pltpu.VMEM((n_pages + 1, ns, page), jnp.int32)]),
        out_shape=jax.ShapeDtypeStruct((bd, n_pages + 1, ns, page), F32),
        compiler_params=_params("parallel", "arbitrary"),
        name="sample_select",
    )(page_table, proj, proj, *([cki_t] * pps))

    rows = ATT_HEADS * ns
    return pl.pallas_call(
        functools.partial(_sattn_kernel, **static),
        grid_spec=pltpu.PrefetchScalarGridSpec(
            num_scalar_prefetch=1, grid=(bd, ng),
            in_specs=[rowblk(ATT_DIM, C_Q), rowblk(KV_DIM, C_K), rowblk(KV_DIM, C_V), biasblk]
                     + [pageblk(KV_DIM, i) for i in range(pps)] * 2,
            out_specs=pl.BlockSpec((ns, ATT_DIM), lambda s, g, pt: (s, 0)),
            scratch_shapes=[pltpu.VMEM((rows, KV_DIM), MXU_DTYPE), pltpu.VMEM((rows, 1), F32),
                            pltpu.VMEM((rows, 1), F32), pltpu.VMEM((rows, KV_DIM), F32)]),
        out_shape=jax.ShapeDtypeStruct((bd * ns, ATT_DIM), F32),
        compiler_params=_params("parallel", "arbitrary"),
        name="sample_attention",
    )(page_table, proj, proj, proj, bias, *([ck_t] * pps), *([cv_t] * pps))


def _regroup_w_in(w):
    o_dt = D_INNER + D_INNER + BC_DIM
    o_q = o_dt + SSM_HEADS
    o_ki = o_q + ATT_DIM + 2 * KV_DIM + QI_DIM
    o_wi = o_ki + IDX_DIM
    o_g = o_wi + IDX_HEADS
    parts = [w[:, :o_dt], w[:, o_q:o_ki], w[:, o_g:o_g + 2 * D_MODEL], w[:, o_ki:o_wi], w[:, o_dt:o_q], w[:, o_wi:o_g]]
    used = sum(p.shape[1] for p in parts)
    parts.append(jnp.zeros((w.shape[0], PROJ_COLS - used), w.dtype))
    w = jnp.concatenate(parts, axis=1).astype(MXU_DTYPE)
    return w.reshape(w.shape[0], PROJ_COLS // PROJ_TN, PROJ_TN).transpose(1, 0, 2)


def _small_lanes(v):
    return jnp.zeros((1, LANES), F32).at[0, SM_DT:SM_DT + SSM_HEADS].set(v.astype(F32))


def _layer_consts(conv_w, conv_b, dt_bias, a_log, d_skip, ssm_norm_w):
    head_of_lane = jnp.arange(D_INNER) // SSM_HEAD_DIM
    expand = (jnp.arange(LANES)[:, None] == (SM_DT + head_of_lane)[None, :]).astype(jnp.bfloat16)
    return dict(
        cwx=conv_w[:, :D_INNER], cbx=conv_b[None, :D_INNER], cwb=conv_w[:, D_INNER:], cbb=conv_b[None, D_INNER:],
        dtb=_small_lanes(dt_bias), aneg=_small_lanes(-jnp.exp(a_log.astype(F32))),
        dskip=jnp.repeat(d_skip.astype(F32), SSM_HEAD_DIM)[None, :], normw=ssm_norm_w.astype(F32)[None, :],
        expand=expand)


def kernel(x_prompt, x_sample, cache_k, cache_v, cache_kidx, state_ssm, state_conv, page_table, meta_tokens, norm1_w, w_in, conv_w, conv_b, dt_bias, a_log, d_skip, ssm_norm_w, w_ssm_proj, w_attn_proj, w_out, norm2_w, w_up, w_down, final_norm_w):
    b, seq, d = x_prompt.shape
    bd, ns = x_sample.shape[:2]
    depth = w_in.shape[0]
    assert d == D_MODEL and ns % 8 == 0
    past = page_table.shape[1] * cache_kidx.shape[2]
    lv = N_META + seq
    lp = -(-lv // Q_BLOCK) * Q_BLOCK
    rp = b * lp
    rs = bd * ns
    r = -(-(rp + rs) // LANES) * LANES
    topk_p = min(TOPK_MAX, seq // 4)
    topk_s = min(TOPK_MAX, (past + ns) // 4)

    hp = jnp.concatenate([jnp.broadcast_to(meta_tokens[None].astype(F32), (b, N_META, d)), x_prompt,
                          jnp.zeros((b, lp - lv, d), F32)], axis=1).reshape(rp, d)
    h = jnp.concatenate([hp, x_sample.reshape(rs, d), jnp.zeros((r - rp - rs, d), F32)], axis=0)

    ck_t, cv_t, cki_t = _key_minor(cache_k), _key_minor(cache_v), _key_minor(cache_kidx)
    zero_state = jnp.zeros((b, SSM_GROUPS, SSM_STATE, D_INNER // SSM_GROUPS), F32)
    zero_prefix = jnp.zeros((b, CONV_W - 1, D_INNER + BC_DIM), F32)
    outs = [[] for _ in range(10)]
    for l in range(depth):
        lw = _layer_consts(conv_w[l], conv_b[l], dt_bias[l], a_log[l], d_skip[l], ssm_norm_w[l])
        proj = _in_proj(_rmsnorm(h, norm1_w[l], MXU_DTYPE), _regroup_w_in(w_in[l]))

        yp, hp_t = _ssd(proj, 0, b, SSD_CHUNK, lp // SSD_CHUNK, lv, zero_state, zero_prefix, lw, MXU_DTYPE)
        ysm, hs_t = _ssd(proj, rp, bd, ns, 1, ns, _state_to_t(state_ssm[l]), state_conv[l], lw, F32)
        ap = _prompt_attention(proj, b, lp, topk_p)
        asm = _sample_attention(proj, rp, page_table, ck_t, cv_t, cki_t, l, ns, topk_s)
        tail = r - rp - rs
        y_ssm = jnp.concatenate([yp, ysm.astype(MXU_DTYPE), jnp.zeros((tail, D_INNER), MXU_DTYPE)], axis=0)
        y_att = jnp.concatenate([ap, asm.astype(MXU_DTYPE), jnp.zeros((tail, ATT_DIM), MXU_DTYPE)], axis=0)

        h = _merge(h, y_ssm, y_att, proj, w_ssm_proj[l].astype(MXU_DTYPE), w_attn_proj[l].astype(MXU_DTYPE),
                   w_out[l].astype(MXU_DTYPE))
        h = _mlp(h, norm2_w[l], w_up[l].astype(MXU_DTYPE), w_down[l].astype(MXU_DTYPE))

        def pcols(c0, w):
            return proj[:rp, c0:c0 + w].reshape(b, lp, w)[:, :lv]

        def scols(c0, w):
            return proj[rp:rp + rs, c0:c0 + w].reshape(bd, ns, w)

        xbc_w = D_INNER + BC_DIM
        p_conv = jnp.stack([proj[s * lp + lv - (CONV_W - 1):s * lp + lv, C_XS:C_XS + xbc_w] for s in range(b)])
        s_conv = jnp.concatenate([state_conv[l], scols(C_XS, xbc_w)], axis=1)[:, -(CONV_W - 1):]
        for acc, t in zip(outs, (
                pcols(C_K, KV_DIM).reshape(b, lv, ATT_KV_HEADS, HEAD_DIM),
                pcols(C_V, KV_DIM).reshape(b, lv, ATT_KV_HEADS, HEAD_DIM),
                pcols(C_SM + SM_KI, IDX_DIM),
                _state_from_t(hp_t),
                p_conv,
                scols(C_K, KV_DIM).reshape(bd, ns, ATT_KV_HEADS, HEAD_DIM),
                scols(C_V, KV_DIM).reshape(bd, ns, ATT_KV_HEADS, HEAD_DIM),
                scols(C_SM + SM_KI, IDX_DIM),
                _state_from_t(hs_t),
                s_conv)):
            acc.append(t)

    y = _rmsnorm(h, final_norm_w, F32)
    y_prompt = y[:rp].reshape(b, lp, d)[:, N_META:lv]
    y_sample = y[rp:rp + rs].reshape(bd, ns, d)
    return (y_prompt, y_sample) + tuple(jnp.stack(o) for o in outs)
```

```python
import functools

import jax
import jax.numpy as jnp
from jax import lax
from jax.experimental import pallas as pl
from jax.experimental.pallas import tpu as pltpu

F32 = jnp.float32
MXU_DTYPE = jnp.bfloat16

N_META = 16
NORM_EPS = 1e-6
SSM_HEAD_DIM = 64
SSM_GROUPS = 4
SSM_STATE = 128
CONV_W = 4
ATT_HEADS = 16
ATT_KV_HEADS = 4
ATT_GROUP = ATT_HEADS // ATT_KV_HEADS
HEAD_DIM = 64
IDX_HEADS = 8
IDX_DIM = 64
TOPK_MAX = 256

LANES = 128
Q_BLOCK = 128
SSD_CHUNK = 128
VMEM_LIMIT = 56 << 20
MASK_BIAS = -1e30
INT_MIN = -(2 ** 31)
LOG2_E = 1.4426950408889634
V_ROWS = HEAD_DIM + 16

D_MODEL = 1024
D_INNER = 2 * D_MODEL
BC_DIM = 2 * SSM_GROUPS * SSM_STATE
SSM_HEADS = D_INNER // SSM_HEAD_DIM
ATT_DIM = ATT_HEADS * HEAD_DIM
KV_DIM = ATT_KV_HEADS * HEAD_DIM
QI_DIM = IDX_HEADS * IDX_DIM
C_Z = 0
C_XS = C_Z + D_INNER
C_BC = C_XS + D_INNER
C_Q = C_BC + BC_DIM
C_K = C_Q + ATT_DIM
C_V = C_K + KV_DIM
C_QI = C_V + KV_DIM
C_GS = C_QI + QI_DIM
C_GA = C_GS + D_MODEL
C_SM = C_GA + D_MODEL
SM_KI = 0
SM_DT = IDX_DIM
SM_WI = IDX_DIM + SSM_HEADS
PROJ_TN = 512
PROJ_COLS = -(-(C_SM + LANES) // PROJ_TN) * PROJ_TN


def _divisor_tile(n, max_tile, align):
    best = None
    for t in range(align, min(n, max_tile) + 1, align):
        if n % t == 0:
            best = t
    assert best is not None, (n, max_tile, align)
    return best


def _params(*sem):
    return pltpu.CompilerParams(dimension_semantics=sem, vmem_limit_bytes=VMEM_LIMIT)


def _dot(a, b):
    return jnp.dot(a.astype(MXU_DTYPE), b.astype(MXU_DTYPE), preferred_element_type=F32)


def _dot_nt(a, b):
    return lax.dot_general(a.astype(MXU_DTYPE), b.astype(MXU_DTYPE), (((1,), (1,)), ((), ())),
                           preferred_element_type=F32)


def _bf16_pieces(x):
    hi = x.astype(jnp.bfloat16)
    r = x - hi.astype(F32)
    mid = r.astype(jnp.bfloat16)
    return [hi, mid, (r - mid.astype(F32)).astype(jnp.bfloat16)]


def _dot_exact(a, b):
    return jnp.dot(a, b, preferred_element_type=F32, precision=lax.Precision.HIGHEST)


def _rmsnorm_kernel(x_ref, w_ref, o_ref):
    x = x_ref[...]
    ms = jnp.mean(x * x, axis=-1, keepdims=True)
    o_ref[...] = (x * lax.rsqrt(ms + NORM_EPS) * w_ref[...]).astype(o_ref.dtype)


def _rmsnorm(x, w, out_dtype):
    r, d = x.shape
    tm = _divisor_tile(r, 512, 16)
    return pl.pallas_call(
        _rmsnorm_kernel,
        grid=(r // tm,),
        in_specs=[pl.BlockSpec((tm, d), lambda i: (i, 0)), pl.BlockSpec((1, d), lambda i: (0, 0))],
        out_specs=pl.BlockSpec((tm, d), lambda i: (i, 0)),
        out_shape=jax.ShapeDtypeStruct((r, d), out_dtype),
        compiler_params=_params("parallel"),
        name="rmsnorm",
    )(x, w.reshape(1, d))


def _in_proj_kernel(x_ref, w_ref, o_ref):
    o_ref[...] = jnp.dot(x_ref[...], w_ref[pl.program_id(1)], preferred_element_type=F32)


def _in_proj(u, w):
    r, k = u.shape
    nt = w.shape[0]
    tm = _divisor_tile(r, 1100, 16)
    return pl.pallas_call(
        _in_proj_kernel,
        grid=(r // tm, nt),
        in_specs=[pl.BlockSpec((tm, k), lambda i, j: (i, 0)), pl.BlockSpec(w.shape, lambda i, j: (0, 0, 0))],
        out_specs=pl.BlockSpec((tm, PROJ_TN), lambda i, j: (i, j)),
        out_shape=jax.ShapeDtypeStruct((r, nt * PROJ_TN), F32),
        compiler_params=_params("parallel", "arbitrary"),
        name="in_proj",
    )(u, w)


def _merge_kernel(h_ref, ys_ref, ya_ref, gs_ref, ga_ref, wsp_ref, wap_ref, wo_ref, o_ref):
    a = jnp.dot(ys_ref[...], wsp_ref[...], preferred_element_type=F32)
    b = jnp.dot(ya_ref[...], wap_ref[...], preferred_element_type=F32)
    m = jax.nn.sigmoid(gs_ref[...]) * a + jax.nn.sigmoid(ga_ref[...]) * b
    o_ref[...] = h_ref[...] + _dot(m, wo_ref[...])


def _merge(h, y_ssm, y_att, proj, w_sp, w_ap, w_o):
    r, d = h.shape
    tm = _divisor_tile(r, 256, 16)
    const = lambda i: (0, 0)
    return pl.pallas_call(
        _merge_kernel,
        grid=(r // tm,),
        in_specs=[pl.BlockSpec((tm, d), lambda i: (i, 0)),
                  pl.BlockSpec((tm, D_INNER), lambda i: (i, 0)),
                  pl.BlockSpec((tm, ATT_DIM), lambda i: (i, 0)),
                  pl.BlockSpec((tm, d), lambda i: (i, C_GS // D_MODEL)),
                  pl.BlockSpec((tm, d), lambda i: (i, C_GA // D_MODEL)),
                  pl.BlockSpec(w_sp.shape, const), pl.BlockSpec(w_ap.shape, const), pl.BlockSpec(w_o.shape, const)],
        out_specs=pl.BlockSpec((tm, d), lambda i: (i, 0)),
        out_shape=jax.ShapeDtypeStruct((r, d), F32),
        compiler_params=_params("parallel"),
        name="merge",
    )(h, y_ssm, y_att, proj, proj, w_sp, w_ap, w_o)


def _mlp_kernel(h_ref, nw_ref, wu_ref, wd_ref, o_ref):
    x = h_ref[...]
    ms = jnp.mean(x * x, axis=-1, keepdims=True)
    u = x * lax.rsqrt(ms + NORM_EPS) * nw_ref[...]
    a = _dot(u, wu_ref[...])
    a = jnp.square(jnp.maximum(a, 0.0))
    o_ref[...] = x + _dot(a, wd_ref[...])


def _mlp(h, norm_w, w_up, w_down):
    r, d = h.shape
    tm = _divisor_tile(r, 256, 16)
    const = lambda i: (0, 0)
    return pl.pallas_call(
        _mlp_kernel,
        grid=(r // tm,),
        in_specs=[pl.BlockSpec((tm, d), lambda i: (i, 0)), pl.BlockSpec((1, d), const),
                  pl.BlockSpec(w_up.shape, const), pl.BlockSpec(w_down.shape, const)],
        out_specs=pl.BlockSpec((tm, d), lambda i: (i, 0)),
        out_shape=jax.ShapeDtypeStruct((r, d), F32),
        compiler_params=_params("parallel"),
        name="mlp",
    )(h, norm_w.reshape(1, d), w_up, w_down)


def _softplus(x):
    return jnp.maximum(x, 0.0) + jnp.log1p(jnp.exp(-jnp.abs(x)))


def _conv_silu(xp_sc, x, w_ref, b_ref, t):
    xp_sc[pl.ds(8, t), :] = x
    acc = b_ref[...] + x * w_ref[CONV_W - 1:CONV_W, :]
    for k in range(CONV_W - 1):
        acc = acc + xp_sc[pl.ds(5 + k, t), :] * w_ref[k:k + 1, :]
    xp_sc[pl.ds(5, CONV_W - 1), :] = xp_sc[pl.ds(t + 5, CONV_W - 1), :]
    return acc * jax.nn.sigmoid(acc)


def _ssd_kernel(z_ref, xs_ref, bc_ref, sm_ref, h0_ref, pxs_ref, pbc_ref,
                cwx_ref, cbx_ref, cwb_ref, cbb_ref, dtb_ref, aneg_ref, dskip_ref, normw_ref, expand_ref,
                y_ref, hout_ref, state_sc, xpx_sc, xpb_sc, *, t_in, valid_len):
    t = SSD_CHUNK
    c = pl.program_id(1)

    @pl.when(c == 0)
    def _():
        state_sc[...] = h0_ref[...]
        xpx_sc[pl.ds(5, CONV_W - 1), :] = pxs_ref[...]
        xpb_sc[pl.ds(5, CONV_W - 1), :] = pbc_ref[...]

    def rows(ref):
        x = ref[...]
        if t_in < t:
            x = jnp.concatenate([x, jnp.zeros((t - t_in, x.shape[1]), x.dtype)], axis=0)
        return x

    xs = _conv_silu(xpx_sc, rows(xs_ref), cwx_ref, cbx_ref, t)
    bcm = _conv_silu(xpb_sc, rows(bc_ref), cwb_ref, cbb_ref, t)
    z = rows(z_ref)

    row = lax.broadcasted_iota(jnp.int32, (t, LANES), 0)
    n_valid = jnp.minimum(valid_len - c * t, t_in)
    dt = jnp.where(row < n_valid, _softplus(rows(sm_ref) + dtb_ref[...]), 0.0)
    a = dt * aneg_ref[...]
    ri = lax.broadcasted_iota(jnp.int32, (t, t), 0)
    ci = lax.broadcasted_iota(jnp.int32, (t, t), 1)
    causal = ri >= ci
    acum = _dot_exact(jnp.where(causal, 1.0, 0.0), a)
    acum_t = acum.T
    ex = jnp.dot(jnp.concatenate(_bf16_pieces(dt) + _bf16_pieces(acum), axis=0), expand_ref[...],
                 preferred_element_type=F32)
    dtx = ex[:t] + ex[t:2 * t] + ex[2 * t:3 * t]
    acx = ex[3 * t:4 * t] + ex[4 * t:5 * t] + ex[5 * t:]
    last = acx[t - 1:t, :]
    xdt = xs * dtx
    xw = xdt * jnp.exp(last - acx)
    eacx = jnp.exp(acx)
    elast = jnp.exp(last)

    gw = D_INNER // SSM_GROUPS
    hpg = SSM_HEADS // SSM_GROUPS
    ys = []
    for g in range(SSM_GROUPS):
        bg = bcm[:, g * SSM_STATE:(g + 1) * SSM_STATE]
        cg = bcm[:, (SSM_GROUPS + g) * SSM_STATE:(SSM_GROUPS + g + 1) * SSM_STATE]
        cb = _dot_nt(cg, bg)
        h_t = state_sc[g]
        yg = [_dot(cg, h_t) * eacx[:, g * gw:(g + 1) * gw]]
        intra = []
        for e in range(hpg):
            hd = g * hpg + e
            seg = acum[:, SM_DT + hd:SM_DT + hd + 1] - acum_t[SM_DT + hd:SM_DT + hd + 1, :]
            m = cb * jnp.exp(jnp.where(causal, seg, -jnp.inf))
            intra.append(_dot(m, xdt[:, hd * SSM_HEAD_DIM:(hd + 1) * SSM_HEAD_DIM]))
        ys.append(yg[0] + jnp.concatenate(intra, axis=1))
        state_sc[g] = h_t * elast[:, g * gw:(g + 1) * gw] + _dot(bg.T, xw[:, g * gw:(g + 1) * gw])
    y = jnp.concatenate(ys, axis=1) + xs * dskip_ref[...]
    y = y * (z * jax.nn.sigmoid(z))
    outs = []
    for g in range(SSM_GROUPS):
        yg = y[:, g * gw:(g + 1) * gw]
        outs.append(yg * lax.rsqrt(jnp.mean(yg * yg, axis=-1, keepdims=True) + NORM_EPS))
    y = jnp.concatenate(outs, axis=1) * normw_ref[...]
    y_ref[...] = y[:t_in].astype(y_ref.dtype)

    @pl.when(c == pl.num_programs(1) - 1)
    def _():
        hout_ref[...] = state_sc[...]


def _ssd(proj, row0, n_seq, t_in, n_chunks, valid_len, h0_t, prefix, lw, out_dtype):
    assert row0 % t_in == 0
    rb0 = row0 // t_in
    rowblk = lambda w, col: pl.BlockSpec((t_in, w), lambda s, c: (rb0 + s * n_chunks + c, col // w))
    per_seq = lambda shape: pl.BlockSpec((None,) + shape, lambda s, c: (s,) + (0,) * len(shape))
    const = lambda arr: pl.BlockSpec(arr.shape, lambda s, c: (0,) * arr.ndim)
    gw = D_INNER // SSM_GROUPS
    pxs, pbc = prefix[:, :, :D_INNER], prefix[:, :, D_INNER:]
    consts = (lw["cwx"], lw["cbx"], lw["cwb"], lw["cbb"], lw["dtb"], lw["aneg"], lw["dskip"], lw["normw"], lw["expand"])
    y, h_t = pl.pallas_call(
        functools.partial(_ssd_kernel, t_in=t_in, valid_len=valid_len),
        grid=(n_seq, n_chunks),
        in_specs=[rowblk(D_INNER, C_Z), rowblk(D_INNER, C_XS), rowblk(BC_DIM, C_BC), rowblk(LANES, C_SM),
                  per_seq((SSM_GROUPS, SSM_STATE, gw)), per_seq((CONV_W - 1, D_INNER)), per_seq((CONV_W - 1, BC_DIM))]
                 + [const(a) for a in consts],
        out_specs=[pl.BlockSpec((t_in, D_INNER), lambda s, c: (s * n_chunks + c, 0)),
                   per_seq((SSM_GROUPS, SSM_STATE, gw))],
        out_shape=[jax.ShapeDtypeStruct((n_seq * n_chunks * t_in, D_INNER), out_dtype),
                   jax.ShapeDtypeStruct((n_seq, SSM_GROUPS, SSM_STATE, gw), F32)],
        scratch_shapes=[pltpu.VMEM((SSM_GROUPS, SSM_STATE, gw), F32),
                        pltpu.VMEM((SSD_CHUNK + 8, D_INNER), F32),
                        pltpu.VMEM((SSD_CHUNK + 8, BC_DIM), F32)],
        compiler_params=_params("parallel", "arbitrary"),
        name="ssd",
    )(proj, proj, proj, proj, h0_t, pxs, pbc, *consts)
    return y, h_t


def _state_to_t(h):
    n = h.shape[0]
    hpg = SSM_HEADS // SSM_GROUPS
    return h.reshape(n, SSM_GROUPS, hpg, SSM_HEAD_DIM, SSM_STATE).transpose(0, 1, 4, 2, 3).reshape(
        n, SSM_GROUPS, SSM_STATE, hpg * SSM_HEAD_DIM)


def _state_from_t(h_t):
    n = h_t.shape[0]
    hpg = SSM_HEADS // SSM_GROUPS
    return h_t.reshape(n, SSM_GROUPS, SSM_STATE, hpg, SSM_HEAD_DIM).transpose(0, 1, 3, 4, 2).reshape(
        n, SSM_HEADS, SSM_HEAD_DIM, SSM_STATE)


def _sort_key(x):
    bits = lax.bitcast_convert_type(x, jnp.int32)
    bits = jnp.where(bits == INT_MIN, 0, bits)
    return bits ^ ((bits >> 31) & 0x7FFFFFFF)


def _fold_rows(x, group=8):
    x = x.reshape(x.shape[0] // group, group, x.shape[1])
    while x.shape[0] > 1:
        half, odd = divmod(x.shape[0], 2)
        folded = x[:half] + x[half:2 * half]
        if odd:
            folded = jnp.concatenate([folded[:1] + x[2 * half:], folded[1:]], axis=0) if half > 1 else folded + x[2 * half:]
        x = folded
    return x[0]


def _count_t(keys_sc, n_chunks, pred):
    def body(c, cnt):
        return cnt + _fold_rows(jnp.where(pred(keys_sc[c]), 1.0, 0.0))

    cnt = lax.fori_loop(0, n_chunks, body, jnp.zeros((8, LANES), F32))
    return jnp.sum(cnt, axis=0, keepdims=True)


HALF_MIN = -(2 ** 30)
SEARCH_UNROLLS = (2, 4, 6)


def _search_bounds(nkc):
    return sorted({min(b, nkc) for b in SEARCH_UNROLLS} | {nkc})


def _search_chunks(n_chunks, nkc):
    bounds = _search_bounds(nkc)
    n = jnp.int32(bounds[-1])
    for b in reversed(bounds[:-1]):
        n = jnp.where(n_chunks <= b, b, n)
    return n


def _kth_halved_t(kh_sc, n, k):
    total = n * kh_sc.shape[1]

    def count_ge(cand):
        below = None
        for c in range(n):
            neg = _fold_rows((kh_sc[c] - cand) >> 31)
            below = neg if below is None else below + neg
        return total + jnp.sum(below.astype(F32), axis=0, keepdims=True)

    def bit_body(i, u):
        cand = u | lax.shift_left(jnp.int32(1), 30 - i)
        return jnp.where(count_ge(cand + HALF_MIN) >= k, cand, u)

    return lax.fori_loop(0, 31, bit_body, jnp.zeros((1, LANES), jnp.int32)) + HALF_MIN


def _kth_key_t(keys_sc, kh_sc, n_chunks, k):
    bounds = _search_bounds(kh_sc.shape[0])

    def build(i):
        if i == len(bounds) - 1:
            return lambda: _kth_halved_t(kh_sc, bounds[i], k)
        return lambda: lax.cond(n_chunks <= bounds[i], lambda: _kth_halved_t(kh_sc, bounds[i], k), build(i + 1))

    odd = build(0)() * 2 + 1
    return jnp.where(_count_t(keys_sc, n_chunks, lambda key: key >= odd) >= k, odd, odd - 1)


def _select_t(keys_sc, kh_sc, bias_sc, n_chunks, k):
    kc = keys_sc.shape[1]
    thr = _kth_key_t(keys_sc, kh_sc, n_chunks, k)
    n_ge = _count_t(keys_sc, n_chunks, lambda key: key >= thr)
    surplus = jnp.max(jnp.where((n_ge > k) & (thr != INT_MIN), 1.0, 0.0))

    @pl.when(surplus == 0.0)
    def _():
        floor = jnp.maximum(thr, INT_MIN + 1)

        def body(c, carry):
            bias_sc[c] = jnp.where(keys_sc[c] >= floor, 0.0, MASK_BIAS)
            return carry

        lax.fori_loop(0, n_chunks, body, 0)

    @pl.when(surplus != 0.0)
    def _():
        need = k - _count_t(keys_sc, n_chunks, lambda key: key > thr)
        ri = lax.broadcasted_iota(jnp.int32, (kc, kc), 0)
        ci = lax.broadcasted_iota(jnp.int32, (kc, kc), 1)
        lower = jnp.where(ci <= ri, 1.0, 0.0).astype(jnp.bfloat16)

        def body(c, run):
            key = keys_sc[c]
            eqf = jnp.where((key == thr) & (key != INT_MIN), 1.0, 0.0)
            rank = jnp.dot(lower, eqf.astype(jnp.bfloat16), preferred_element_type=F32) + run
            sel = (key > thr) | ((eqf > 0.0) & (rank <= need))
            bias_sc[c] = jnp.where(sel, 0.0, MASK_BIAS)
            return run + jnp.sum(_fold_rows(eqf), axis=0, keepdims=True)

        lax.fori_loop(0, n_chunks, body, jnp.zeros((1, LANES), F32))


def _select_s(keys_sc, bias_ref, k):
    n_chunks, rows, width = keys_sc.shape

    def count(pred):
        return jnp.sum(jnp.sum(jnp.where(pred(keys_sc[...]), 1.0, 0.0), axis=0), axis=1, keepdims=True)

    def bit_body(i, res):
        cand = res ^ lax.shift_left(jnp.int32(1), 31 - i)
        return jnp.where(count(lambda key: key >= cand[None]) >= k, cand, res)

    thr = lax.fori_loop(0, 32, bit_body, jnp.full((rows, 1), INT_MIN, jnp.int32))
    n_ge = count(lambda key: key >= thr[None])
    surplus = jnp.max(jnp.where((n_ge > k) & (thr != INT_MIN), 1.0, 0.0))

    @pl.when(surplus == 0.0)
    def _():
        floor = jnp.maximum(thr, INT_MIN + 1)
        bias_ref[...] = jnp.where(keys_sc[...] >= floor[None], 0.0, MASK_BIAS)

    @pl.when(surplus != 0.0)
    def _():
        need = k - count(lambda key: key > thr[None])
        ri = lax.broadcasted_iota(jnp.int32, (width, width), 0)
        ci = lax.broadcasted_iota(jnp.int32, (width, width), 1)
        upper = jnp.where(ri <= ci, 1.0, 0.0).astype(jnp.bfloat16)

        def body(c, run):
            key = keys_sc[c]
            eqf = jnp.where((key == thr) & (key != INT_MIN), 1.0, 0.0)
            rank = jnp.dot(eqf.astype(jnp.bfloat16), upper, preferred_element_type=F32) + run
            sel = (key > thr) | ((eqf > 0.0) & (rank <= need))
            bias_ref[c] = jnp.where(sel, 0.0, MASK_BIAS)
            return run + jnp.sum(eqf, axis=1, keepdims=True)

        lax.fori_loop(0, n_chunks, body, jnp.zeros((rows, 1), F32))


def _pattn_kernel(qi_ref, smq_ref, q_ref, smk_ref, k_ref, v_ref, o_ref,
                  kb_sc, kib_sc, vt_sc, keys_sc, kh_sc, bias_sc, qz_sc, m_sc, acc_sc, *, kc, lp, topk):
    j = pl.program_id(1)
    nq = Q_BLOCK
    nkc = kb_sc.shape[0]

    @pl.when(j == 0)
    def _():
        for c in range(nkc):
            r0 = c * kc
            n = min(kc, lp - r0)

            def chunk(ref):
                x = ref[r0:r0 + n, :]
                return x if n == kc else jnp.concatenate([x, jnp.zeros((kc - n, x.shape[1]), x.dtype)], axis=0)

            kb_sc[c] = chunk(k_ref).astype(MXU_DTYPE)
            kib_sc[c] = chunk(smk_ref).astype(MXU_DTYPE)
            v_t = chunk(v_ref).T
            pad = jnp.concatenate([jnp.ones((1, kc), F32), jnp.zeros((V_ROWS - HEAD_DIM - 1, kc), F32)], axis=0)
            vt_sc[c] = jnp.concatenate([x for hk in range(ATT_KV_HEADS)
                                        for x in (v_t[hk * HEAD_DIM:(hk + 1) * HEAD_DIM], pad)], axis=0).astype(MXU_DTYPE)

    n_chunks = ((j + 1) * nq - 1) // kc + 1

    qi_t = (qi_ref[...] * IDX_DIM ** -0.5).T
    qiz = jnp.concatenate([qi_t[h * IDX_DIM:(h + 1) * IDX_DIM] for h in range(IDX_HEADS)], axis=1)
    qiz = jnp.concatenate([qiz, jnp.zeros((LANES - IDX_DIM, IDX_HEADS * nq), F32)], axis=0).astype(MXU_DTYPE)
    smq_t = smq_ref[...].T
    wi_row = jnp.concatenate([smq_t[SM_WI + h:SM_WI + h + 1] for h in range(IDX_HEADS)], axis=1) * IDX_HEADS ** -0.5
    kpos = lax.broadcasted_iota(jnp.int32, (kc, nq), 0)
    qpos = j * nq + lax.broadcasted_iota(jnp.int32, (kc, nq), 1)

    def score_body(c, carry):
        s = jnp.maximum(jnp.dot(kib_sc[c], qiz, preferred_element_type=F32), 0.0) * wi_row
        acc = s[:, :nq]
        for h in range(1, IDX_HEADS):
            acc = acc + s[:, h * nq:(h + 1) * nq]
        key = jnp.where(c * kc + kpos <= qpos, _sort_key(acc), INT_MIN)
        keys_sc[c] = key
        kh_sc[c] = key >> 1
        return carry

    lax.fori_loop(0, n_chunks, score_body, 0)

    def blank_body(c, carry):
        kh_sc[c] = jnp.full((kc, nq), HALF_MIN, jnp.int32)
        return carry

    lax.fori_loop(n_chunks, _search_chunks(n_chunks, nkc), blank_body, 0)
    _select_t(keys_sc, kh_sc, bias_sc, n_chunks, topk)

    q_t = (q_ref[...] * (HEAD_DIM ** -0.5 * LOG2_E)).T
    gw = ATT_GROUP * nq
    zero = jnp.zeros((HEAD_DIM, gw), F32)
    for hk in range(ATT_KV_HEADS):
        own = jnp.concatenate([q_t[(hk * ATT_GROUP + g) * HEAD_DIM:(hk * ATT_GROUP + g + 1) * HEAD_DIM]
                               for g in range(ATT_GROUP)], axis=1)
        qz_sc[hk] = jnp.concatenate([own if i == hk else zero for i in range(ATT_KV_HEADS)],
                                    axis=0).astype(MXU_DTYPE)
    m_sc[...] = jnp.full(m_sc.shape, MASK_BIAS, F32)
    acc_sc[...] = jnp.zeros(acc_sc.shape, F32)

    def att_body(c, carry):
        kch = kb_sc[c]
        vt = vt_sc[c]
        b = bias_sc[c]
        b4 = jnp.concatenate([b] * ATT_GROUP, axis=1)

        def qk(hk):
            return jnp.dot(kch, qz_sc[hk], preferred_element_type=F32) + b4

        def soft(hk, s):
            m_old = m_sc[hk]
            m_new = jnp.maximum(m_old, jnp.max(s, axis=0, keepdims=True))
            m_sc[hk] = m_new
            return jnp.exp2(m_old - m_new), jnp.exp2(s - m_new).astype(MXU_DTYPE)

        def pv(hk, alpha, p):
            acc_sc[hk] = alpha * acc_sc[hk] + jnp.dot(vt[hk * V_ROWS:(hk + 1) * V_ROWS], p,
                                                      preferred_element_type=F32)

        nh = ATT_KV_HEADS
        s = {0: qk(0), 1: qk(1)}
        ap = {}
        for hk in range(nh):
            ap[hk] = soft(hk, s.pop(hk))
            if hk + 2 < nh:
                s[hk + 2] = qk(hk + 2)
            if hk >= 1:
                pv(hk - 1, *ap.pop(hk - 1))
        pv(nh - 1, *ap.pop(nh - 1))
        return carry

    lax.fori_loop(0, n_chunks, att_body, 0)

    heads = []
    for hk in range(ATT_KV_HEADS):
        acc = acc_sc[hk]
        o = acc[:HEAD_DIM] / acc[HEAD_DIM:HEAD_DIM + 1]
        heads += [o[:, g * nq:(g + 1) * nq] for g in range(ATT_GROUP)]
    o_ref[...] = jnp.concatenate(heads, axis=0).T.astype(o_ref.dtype)


def _prompt_attention(proj, n_seq, lp, topk):
    nqb = lp // Q_BLOCK
    kc = 512
    nkc = -(-lp // kc)
    qblk = lambda w, col: pl.BlockSpec((Q_BLOCK, w), lambda b, j: (b * nqb + j, col // w))
    seqblk = lambda w, col: pl.BlockSpec((lp, w), lambda b, j: (b, col // w))
    gw = ATT_GROUP * Q_BLOCK
    return pl.pallas_call(
        functools.partial(_pattn_kernel, kc=kc, lp=lp, topk=topk),
        grid=(n_seq, nqb),
        in_specs=[qblk(QI_DIM, C_QI), qblk(LANES, C_SM), qblk(ATT_DIM, C_Q),
                  seqblk(LANES, C_SM), seqblk(KV_DIM, C_K), seqblk(KV_DIM, C_V)],
        out_specs=pl.BlockSpec((Q_BLOCK, ATT_DIM), lambda b, j: (b * nqb + j, 0)),
        out_shape=jax.ShapeDtypeStruct((n_seq * lp, ATT_DIM), MXU_DTYPE),
        scratch_shapes=[pltpu.VMEM((nkc, kc, KV_DIM), MXU_DTYPE), pltpu.VMEM((nkc, kc, LANES), MXU_DTYPE),
                        pltpu.VMEM((nkc, ATT_KV_HEADS * V_ROWS, kc), MXU_DTYPE),
                        pltpu.VMEM((nkc, kc, Q_BLOCK), jnp.int32), pltpu.VMEM((nkc, kc, Q_BLOCK), jnp.int32),
                        pltpu.VMEM((nkc, kc, Q_BLOCK), F32),
                        pltpu.VMEM((ATT_KV_HEADS, KV_DIM, gw), MXU_DTYPE),
                        pltpu.VMEM((ATT_KV_HEADS, 1, gw), F32),
                        pltpu.VMEM((ATT_KV_HEADS, V_ROWS, gw), F32)],
        compiler_params=_params("parallel", "arbitrary"),
        name="prompt_attention",
    )(proj, proj, proj, proj, proj, proj)


def _pad_rows(x, n):
    return jnp.concatenate([x, jnp.zeros((n - x.shape[0], x.shape[1]), x.dtype)], axis=0)


def _sselect_kernel(pt_ref, qi_ref, sm_ref, *rest, pages_per_step, n_pages, page, ns, topk):
    page_refs = rest[:pages_per_step]
    bias_ref, keys_sc = rest[pages_per_step:]
    g = pl.program_id(1)
    qi = (qi_ref[...] * IDX_DIM ** -0.5).astype(MXU_DTYPE)
    qs = jnp.concatenate([qi[:, h * IDX_DIM:(h + 1) * IDX_DIM] for h in range(IDX_HEADS)], axis=0)
    sm = sm_ref[...]
    wi = sm[:, SM_WI:SM_WI + IDX_HEADS] * IDX_HEADS ** -0.5

    def weigh(s):
        s = jnp.maximum(s, 0.0)
        acc = s[:ns] * wi[:, 0:1]
        for h in range(1, IDX_HEADS):
            acc = acc + s[h * ns:(h + 1) * ns] * wi[:, h:h + 1]
        return _sort_key(acc)

    keys = weigh(_dot(qs, jnp.concatenate([r[...] for r in page_refs], axis=1)))
    for i in range(pages_per_step):
        keys_sc[g * pages_per_step + i] = keys[:, i * page:(i + 1) * page]

    @pl.when(g == pl.num_programs(1) - 1)
    def _():
        qrow = lax.broadcasted_iota(jnp.int32, (ns, page), 0)
        lane = lax.broadcasted_iota(jnp.int32, (ns, page), 1)
        new = weigh(_dot_nt(qs, _pad_rows(sm[:, SM_KI:SM_KI + IDX_DIM], page)))
        keys_sc[n_pages] = jnp.where(lane <= qrow, new, INT_MIN)
        _select_s(keys_sc, bias_ref, topk)


def _sattn_kernel(pt_ref, q_ref, kn_ref, vn_ref, bias_ref, *rest, pages_per_step, n_pages, page, ns):
    k_refs = rest[:pages_per_step]
    v_refs = rest[pages_per_step:2 * pages_per_step]
    o_ref, qbd_sc, m_sc, l_sc, acc_sc = rest[2 * pages_per_step:]
    g = pl.program_id(1)

    @pl.when(g == 0)
    def _():
        q = q_ref[...] * HEAD_DIM ** -0.5
        rows = []
        for h in range(ATT_HEADS):
            hk = h // ATT_GROUP
            parts = [jnp.zeros((ns, HEAD_DIM), F32)] * ATT_KV_HEADS
            parts[hk] = q[:, h * HEAD_DIM:(h + 1) * HEAD_DIM]
            rows.append(jnp.concatenate(parts, axis=1))
        qbd_sc[...] = jnp.concatenate(rows, axis=0).astype(MXU_DTYPE)
        m_sc[...] = jnp.full(m_sc.shape, MASK_BIAS, F32)
        l_sc[...] = jnp.zeros(l_sc.shape, F32)
        acc_sc[...] = jnp.zeros(acc_sc.shape, F32)

    def attend(s, b, pv):
        s = s + jnp.concatenate([b] * ATT_HEADS, axis=0)
        m_old = m_sc[...]
        m_new = jnp.maximum(m_old, jnp.max(s, axis=1, keepdims=True))
        alpha = jnp.exp(m_old - m_new)
        p = jnp.exp(s - m_new)
        l_sc[...] = alpha * l_sc[...] + jnp.sum(p, axis=1, keepdims=True)
        acc_sc[...] = alpha * acc_sc[...] + pv(p)
        m_sc[...] = m_new

    kcat = jnp.concatenate([r[...] for r in k_refs], axis=1)
    vcat = jnp.concatenate([r[...] for r in v_refs], axis=1)
    bcat = jnp.concatenate([bias_ref[g * pages_per_step + i] for i in range(pages_per_step)], axis=1)
    attend(_dot(qbd_sc[...], kcat), bcat, lambda p: _dot_nt(p, vcat))

    @pl.when(g == pl.num_programs(1) - 1)
    def _():
        vn = _pad_rows(vn_ref[...], page)
        attend(_dot_nt(qbd_sc[...], _pad_rows(kn_ref[...], page)), bias_ref[n_pages], lambda p: _dot(p, vn))
        o = acc_sc[...] / l_sc[...]
        o_ref[...] = jnp.concatenate(
            [o[h * ns:(h + 1) * ns, (h // ATT_GROUP) * HEAD_DIM:(h // ATT_GROUP + 1) * HEAD_DIM]
             for h in range(ATT_HEADS)], axis=1)


def _key_minor(cache):
    nd = cache.ndim
    t = jnp.transpose(cache, (0, 1) + tuple(range(3, nd)) + (2,))
    return t.reshape(t.shape[:2] + (-1, t.shape[-1]))


def _sample_attention(proj, row0, page_table, ck_t, cv_t, cki_t, layer, ns, topk):
    bd, n_pages = page_table.shape
    page = cki_t.shape[3]
    assert page == LANES and row0 % ns == 0
    pps = max(d for d in range(1, 17) if n_pages % d == 0)
    ng = n_pages // pps
    rb0 = row0 // ns
    rowblk = lambda w, col: pl.BlockSpec((ns, w), lambda s, g, pt: (rb0 + s, col // w))
    pageblk = lambda w, i: pl.BlockSpec((None, None, w, page), lambda s, g, pt: (layer, pt[s, g * pps + i], 0, 0))
    biasblk = pl.BlockSpec((None, n_pages + 1, ns, page), lambda s, g, pt: (s, 0, 0, 0))
    static = dict(pages_per_step=pps, n_pages=n_pages, page=page, ns=ns)

    bias = pl.pallas_call(
        functools.partial(_sselect_kernel, topk=topk, **static),
        grid_spec=pltpu.PrefetchScalarGridSpec(
            num_scalar_prefetch=1, grid=(bd, ng),
            in_specs=[rowblk(QI_DIM, C_QI), rowblk(LANES, C_SM)] + [pageblk(IDX_DIM, i) for i in range(pps)],
            out_specs=biasblk,
            scratch_shapes=[pltpu.VMEM((n_pages + 1, ns, page), jnp.int32)]),
        out_shape=jax.ShapeDtypeStruct((bd, n_pages + 1, ns, page), F32),
        compiler_params=_params("parallel", "arbitrary"),
        name="sample_select",
    )(page_table, proj, proj, *([cki_t] * pps))

    rows = ATT_HEADS * ns
    return pl.pallas_call(
        functools.partial(_sattn_kernel, **static),
        grid_spec=pltpu.PrefetchScalarGridSpec(
            num_scalar_prefetch=1, grid=(bd, ng),
            in_specs=[rowblk(ATT_DIM, C_Q), rowblk(KV_DIM, C_K), rowblk(KV_DIM, C_V), biasblk]
                     + [pageblk(KV_DIM, i) for i in range(pps)] * 2,
            out_specs=pl.BlockSpec((ns, ATT_DIM), lambda s, g, pt: (s, 0)),
            scratch_shapes=[pltpu.VMEM((rows, KV_DIM), MXU_DTYPE), pltpu.VMEM((rows, 1), F32),
                            pltpu.VMEM((rows, 1), F32), pltpu.VMEM((rows, KV_DIM), F32)]),
        out_shape=jax.ShapeDtypeStruct((bd * ns, ATT_DIM), F32),
        compiler_params=_params("parallel", "arbitrary"),
        name="sample_attention",
    )(page_table, proj, proj, proj, bias, *([ck_t] * pps), *([cv_t] * pps))


def _regroup_w_in(w):
    o_dt = D_INNER + D_INNER + BC_DIM
    o_q = o_dt + SSM_HEADS
    o_ki = o_q + ATT_DIM + 2 * KV_DIM + QI_DIM
    o_wi = o_ki + IDX_DIM
    o_g = o_wi + IDX_HEADS
    parts = [w[:, :o_dt], w[:, o_q:o_ki], w[:, o_g:o_g + 2 * D_MODEL], w[:, o_ki:o_wi], w[:, o_dt:o_q], w[:, o_wi:o_g]]
    used = sum(p.shape[1] for p in parts)
    parts.append(jnp.zeros((w.shape[0], PROJ_COLS - used), w.dtype))
    w = jnp.concatenate(parts, axis=1).astype(MXU_DTYPE)
    return w.reshape(w.shape[0], PROJ_COLS // PROJ_TN, PROJ_TN).transpose(1, 0, 2)


def _small_lanes(v):
    return jnp.zeros((1, LANES), F32).at[0, SM_DT:SM_DT + SSM_HEADS].set(v.astype(F32))


def _layer_consts(conv_w, conv_b, dt_bias, a_log, d_skip, ssm_norm_w):
    head_of_lane = jnp.arange(D_INNER) // SSM_HEAD_DIM
    expand = (jnp.arange(LANES)[:, None] == (SM_DT + head_of_lane)[None, :]).astype(jnp.bfloat16)
    return dict(
        cwx=conv_w[:, :D_INNER], cbx=conv_b[None, :D_INNER], cwb=conv_w[:, D_INNER:], cbb=conv_b[None, D_INNER:],
        dtb=_small_lanes(dt_bias), aneg=_small_lanes(-jnp.exp(a_log.astype(F32))),
        dskip=jnp.repeat(d_skip.astype(F32), SSM_HEAD_DIM)[None, :], normw=ssm_norm_w.astype(F32)[None, :],
        expand=expand)


def kernel(x_prompt, x_sample, cache_k, cache_v, cache_kidx, state_ssm, state_conv, page_table, meta_tokens, norm1_w, w_in, conv_w, conv_b, dt_bias, a_log, d_skip, ssm_norm_w, w_ssm_proj, w_attn_proj, w_out, norm2_w, w_up, w_down, final_norm_w):
    b, seq, d = x_prompt.shape
    bd, ns = x_sample.shape[:2]
    depth = w_in.shape[0]
    assert d == D_MODEL and ns % 8 == 0
    past = page_table.shape[1] * cache_kidx.shape[2]
    lv = N_META + seq
    lp = -(-lv // Q_BLOCK) * Q_BLOCK
    rp = b * lp
    rs = bd * ns
    r = -(-(rp + rs) // LANES) * LANES
    topk_p = min(TOPK_MAX, seq // 4)
    topk_s = min(TOPK_MAX, (past + ns) // 4)

    hp = jnp.concatenate([jnp.broadcast_to(meta_tokens[None].astype(F32), (b, N_META, d)), x_prompt,
                          jnp.zeros((b, lp - lv, d), F32)], axis=1).reshape(rp, d)
    h = jnp.concatenate([hp, x_sample.reshape(rs, d), jnp.zeros((r - rp - rs, d), F32)], axis=0)

    ck_t, cv_t, cki_t = _key_minor(cache_k), _key_minor(cache_v), _key_minor(cache_kidx)
    zero_state = jnp.zeros((b, SSM_GROUPS, SSM_STATE, D_INNER // SSM_GROUPS), F32)
    zero_prefix = jnp.zeros((b, CONV_W - 1, D_INNER + BC_DIM), F32)
    outs = [[] for _ in range(10)]
    for l in range(depth):
        lw = _layer_consts(conv_w[l], conv_b[l], dt_bias[l], a_log[l], d_skip[l], ssm_norm_w[l])
        proj = _in_proj(_rmsnorm(h, norm1_w[l], MXU_DTYPE), _regroup_w_in(w_in[l]))

        yp, hp_t = _ssd(proj, 0, b, SSD_CHUNK, lp // SSD_CHUNK, lv, zero_state, zero_prefix, lw, MXU_DTYPE)
        ysm, hs_t = _ssd(proj, rp, bd, ns, 1, ns, _state_to_t(state_ssm[l]), state_conv[l], lw, F32)
        ap = _prompt_attention(proj, b, lp, topk_p)
        asm = _sample_attention(proj, rp, page_table, ck_t, cv_t, cki_t, l, ns, topk_s)
        tail = r - rp - rs
        y_ssm = jnp.concatenate([yp, ysm.astype(MXU_DTYPE), jnp.zeros((tail, D_INNER), MXU_DTYPE)], axis=0)
        y_att = jnp.concatenate([ap, asm.astype(MXU_DTYPE), jnp.zeros((tail, ATT_DIM), MXU_DTYPE)], axis=0)

        h = _merge(h, y_ssm, y_att, proj, w_ssm_proj[l].astype(MXU_DTYPE), w_attn_proj[l].astype(MXU_DTYPE),
                   w_out[l].astype(MXU_DTYPE))
        h = _mlp(h, norm2_w[l], w_up[l].astype(MXU_DTYPE), w_down[l].astype(MXU_DTYPE))

        def pcols(c0, w):
            return proj[:rp, c0:c0 + w].reshape(b, lp, w)[:, :lv]

        def scols(c0, w):
            return proj[rp:rp + rs, c0:c0 + w].reshape(bd, ns, w)

        xbc_w = D_INNER + BC_DIM
        p_conv = jnp.stack([proj[s * lp + lv - (CONV_W - 1):s * lp + lv, C_XS:C_XS + xbc_w] for s in range(b)])
        s_conv = jnp.concatenate([state_conv[l], scols(C_XS, xbc_w)], axis=1)[:, -(CONV_W - 1):]
        for acc, t in zip(outs, (
                pcols(C_K, KV_DIM).reshape(b, lv, ATT_KV_HEADS, HEAD_DIM),
                pcols(C_V, KV_DIM).reshape(b, lv, ATT_KV_HEADS, HEAD_DIM),
                pcols(C_SM + SM_KI, IDX_DIM),
                _state_from_t(hp_t),
                p_conv,
                scols(C_K, KV_DIM).reshape(bd, ns, ATT_KV_HEADS, HEAD_DIM),
                scols(C_V, KV_DIM).reshape(bd, ns, ATT_KV_HEADS, HEAD_DIM),
                scols(C_SM + SM_KI, IDX_DIM),
                _state_from_t(hs_t),
                s_conv)):
            acc.append(t)

    y = _rmsnorm(h, final_norm_w, F32)
    y_prompt = y[:rp].reshape(b, lp, d)[:, N_META:lv]
    y_sample = y[rp:rp + rs].reshape(bd, ns, d)
    return (y_prompt, y_sample) + tuple(jnp.stack(o) for o in outs)
```

```python
import functools
import math

import jax
import jax.numpy as jnp
from jax import lax
from jax.experimental import pallas as pl
from jax.experimental.pallas import tpu as pltpu

F32 = jnp.float32
MXU_DTYPE = jnp.bfloat16

N_META = 16
NORM_EPS = 1e-6
SSM_HEAD_DIM = 64
SSM_GROUPS = 4
SSM_STATE = 128
CONV_W = 4
ATT_HEADS = 16
ATT_KV_HEADS = 4
ATT_GROUP = ATT_HEADS // ATT_KV_HEADS
HEAD_DIM = 64
IDX_HEADS = 8
IDX_DIM = 64
TOPK_MAX = 256

LANES = 128
Q_BLOCK = 128
SSD_CHUNK = 128
VMEM_LIMIT = 56 << 20
MASK_BIAS = -1e30
INT_MIN = -(2 ** 31)
LOG2_E = 1.4426950408889634
V_ROWS = HEAD_DIM + 16

D_MODEL = 1024
D_INNER = 2 * D_MODEL
BC_DIM = 2 * SSM_GROUPS * SSM_STATE
SSM_HEADS = D_INNER // SSM_HEAD_DIM
ATT_DIM = ATT_HEADS * HEAD_DIM
KV_DIM = ATT_KV_HEADS * HEAD_DIM
QI_DIM = IDX_HEADS * IDX_DIM
C_Z = 0
C_XS = C_Z + D_INNER
C_BC = C_XS + D_INNER
C_Q = C_BC + BC_DIM
C_K = C_Q + ATT_DIM
C_V = C_K + KV_DIM
C_QI = C_V + KV_DIM
C_GS = C_QI + QI_DIM
C_GA = C_GS + D_MODEL
C_SM = C_GA + D_MODEL
SM_KI = 0
SM_DT = IDX_DIM
SM_WI = IDX_DIM + SSM_HEADS
PROJ_TN = 512
PROJ_COLS = -(-(C_SM + LANES) // PROJ_TN) * PROJ_TN


def _divisor_tile(n, max_tile, align):
    best = None
    for t in range(align, min(n, max_tile) + 1, align):
        if n % t == 0:
            best = t
    assert best is not None, (n, max_tile, align)
    return best


def _params(*sem):
    return pltpu.CompilerParams(dimension_semantics=sem, vmem_limit_bytes=VMEM_LIMIT)


def _dot(a, b):
    return jnp.dot(a.astype(MXU_DTYPE), b.astype(MXU_DTYPE), preferred_element_type=F32)


def _dot_nt(a, b):
    return lax.dot_general(a.astype(MXU_DTYPE), b.astype(MXU_DTYPE), (((1,), (1,)), ((), ())),
                           preferred_element_type=F32)


def _bf16_pieces(x):
    hi = x.astype(jnp.bfloat16)
    r = x - hi.astype(F32)
    mid = r.astype(jnp.bfloat16)
    return [hi, mid, (r - mid.astype(F32)).astype(jnp.bfloat16)]


def _dot_exact(a, b):
    return jnp.dot(a, b, preferred_element_type=F32, precision=lax.Precision.HIGHEST)


def _rmsnorm_kernel(x_ref, w_ref, o_ref):
    x = x_ref[...]
    ms = jnp.mean(x * x, axis=-1, keepdims=True)
    o_ref[...] = (x * lax.rsqrt(ms + NORM_EPS) * w_ref[...]).astype(o_ref.dtype)


def _rmsnorm(x, w, out_dtype):
    r, d = x.shape
    tm = _divisor_tile(r, 512, 16)
    return pl.pallas_call(
        _rmsnorm_kernel,
        grid=(r // tm,),
        in_specs=[pl.BlockSpec((tm, d), lambda i: (i, 0)), pl.BlockSpec((1, d), lambda i: (0, 0))],
        out_specs=pl.BlockSpec((tm, d), lambda i: (i, 0)),
        out_shape=jax.ShapeDtypeStruct((r, d), out_dtype),
        compiler_params=_params("parallel"),
        name="rmsnorm",
    )(x, w.reshape(1, d))


def _in_proj_kernel(x_ref, w_ref, o_ref):
    o_ref[...] = jnp.dot(x_ref[...], w_ref[pl.program_id(1)], preferred_element_type=F32)


def _in_proj(u, w):
    r, k = u.shape
    nt = w.shape[0]
    tm = _divisor_tile(r, 1100, 16)
    return pl.pallas_call(
        _in_proj_kernel,
        grid=(r // tm, nt),
        in_specs=[pl.BlockSpec((tm, k), lambda i, j: (i, 0)), pl.BlockSpec(w.shape, lambda i, j: (0, 0, 0))],
        out_specs=pl.BlockSpec((tm, PROJ_TN), lambda i, j: (i, j)),
        out_shape=jax.ShapeDtypeStruct((r, nt * PROJ_TN), F32),
        compiler_params=_params("parallel", "arbitrary"),
        name="in_proj",
    )(u, w)


def _merge_kernel(h_ref, ysp_ref, yss_ref, yap_ref, yas_ref, gs_ref, ga_ref, wsp_ref, wap_ref, wo_ref, o_ref, *, n_p):
    prompt = pl.program_id(0) < n_p
    ys = jnp.where(prompt, ysp_ref[...], yss_ref[...].astype(MXU_DTYPE))
    ya = jnp.where(prompt, yap_ref[...], yas_ref[...].astype(MXU_DTYPE))
    a = jnp.dot(ys, wsp_ref[...], preferred_element_type=F32)
    b = jnp.dot(ya, wap_ref[...], preferred_element_type=F32)
    m = jax.nn.sigmoid(gs_ref[...]) * a + jax.nn.sigmoid(ga_ref[...]) * b
    o_ref[...] = h_ref[...] + _dot(m, wo_ref[...])


def _merge(h, rp, ys_p, ys_s, ya_p, ya_s, proj, w_sp, w_ap, w_o):
    r, d = h.shape
    tm = _divisor_tile(math.gcd(rp, r - rp), 256, 16)
    n_p = rp // tm
    const = lambda i: (0, 0)
    pblk = lambda w: pl.BlockSpec((tm, w), lambda i: (jnp.minimum(i, n_p - 1), 0))
    sblk = lambda w: pl.BlockSpec((tm, w), lambda i: (jnp.maximum(i - n_p, 0), 0))
    return pl.pallas_call(
        functools.partial(_merge_kernel, n_p=n_p),
        grid=(r // tm,),
        in_specs=[pl.BlockSpec((tm, d), lambda i: (i, 0)),
                  pblk(D_INNER), sblk(D_INNER), pblk(ATT_DIM), sblk(ATT_DIM),
                  pl.BlockSpec((tm, d), lambda i: (i, C_GS // D_MODEL)),
                  pl.BlockSpec((tm, d), lambda i: (i, C_GA // D_MODEL)),
                  pl.BlockSpec(w_sp.shape, const), pl.BlockSpec(w_ap.shape, const), pl.BlockSpec(w_o.shape, const)],
        out_specs=pl.BlockSpec((tm, d), lambda i: (i, 0)),
        out_shape=jax.ShapeDtypeStruct((r, d), F32),
        compiler_params=_params("parallel"),
        name="merge",
    )(h, ys_p, ys_s, ya_p, ya_s, proj, proj, w_sp, w_ap, w_o)


def _mlp_kernel(h_ref, nw_ref, wu_ref, wd_ref, o_ref):
    x = h_ref[...]
    ms = jnp.mean(x * x, axis=-1, keepdims=True)
    u = x * lax.rsqrt(ms + NORM_EPS) * nw_ref[...]
    a = _dot(u, wu_ref[...])
    a = jnp.square(jnp.maximum(a, 0.0))
    o_ref[...] = x + _dot(a, wd_ref[...])


def _mlp(h, norm_w, w_up, w_down):
    r, d = h.shape
    tm = _divisor_tile(r, 256, 16)
    const = lambda i: (0, 0)
    return pl.pallas_call(
        _mlp_kernel,
        grid=(r // tm,),
        in_specs=[pl.BlockSpec((tm, d), lambda i: (i, 0)), pl.BlockSpec((1, d), const),
                  pl.BlockSpec(w_up.shape, const), pl.BlockSpec(w_down.shape, const)],
        out_specs=pl.BlockSpec((tm, d), lambda i: (i, 0)),
        out_shape=jax.ShapeDtypeStruct((r, d), F32),
        compiler_params=_params("parallel"),
        name="mlp",
    )(h, norm_w.reshape(1, d), w_up, w_down)


def _softplus(x):
    return jnp.maximum(x, 0.0) + jnp.log1p(jnp.exp(-jnp.abs(x)))


def _conv_silu(xp_sc, x, w_ref, b_ref, t):
    xp_sc[pl.ds(8, t), :] = x
    acc = b_ref[...] + x * w_ref[CONV_W - 1:CONV_W, :]
    for k in range(CONV_W - 1):
        acc = acc + xp_sc[pl.ds(5 + k, t), :] * w_ref[k:k + 1, :]
    xp_sc[pl.ds(5, CONV_W - 1), :] = xp_sc[pl.ds(t + 5, CONV_W - 1), :]
    return acc * jax.nn.sigmoid(acc)


def _ssd_kernel(z_ref, xs_ref, bc_ref, sm_ref, h0_ref, pxs_ref, pbc_ref,
                cwx_ref, cbx_ref, cwb_ref, cbb_ref, dtb_ref, aneg_ref, dskip_ref, normw_ref, expand_ref,
                y_ref, hout_ref, state_sc, xpx_sc, xpb_sc, *, t_in, valid_len):
    t = SSD_CHUNK
    c = pl.program_id(1)

    @pl.when(c == 0)
    def _():
        state_sc[...] = h0_ref[...]
        xpx_sc[pl.ds(5, CONV_W - 1), :] = pxs_ref[...]
        xpb_sc[pl.ds(5, CONV_W - 1), :] = pbc_ref[...]

    def rows(ref):
        x = ref[...]
        if t_in < t:
            x = jnp.concatenate([x, jnp.zeros((t - t_in, x.shape[1]), x.dtype)], axis=0)
        return x

    xs = _conv_silu(xpx_sc, rows(xs_ref), cwx_ref, cbx_ref, t)
    bcm = _conv_silu(xpb_sc, rows(bc_ref), cwb_ref, cbb_ref, t)
    z = rows(z_ref)

    row = lax.broadcasted_iota(jnp.int32, (t, LANES), 0)
    n_valid = jnp.minimum(valid_len - c * t, t_in)
    dt = jnp.where(row < n_valid, _softplus(rows(sm_ref) + dtb_ref[...]), 0.0)
    a = dt * aneg_ref[...]
    ri = lax.broadcasted_iota(jnp.int32, (t, t), 0)
    ci = lax.broadcasted_iota(jnp.int32, (t, t), 1)
    causal = ri >= ci
    acum = _dot_exact(jnp.where(causal, 1.0, 0.0), a)
    acum_t = acum.T
    ex = jnp.dot(jnp.concatenate(_bf16_pieces(dt) + _bf16_pieces(acum), axis=0), expand_ref[...],
                 preferred_element_type=F32)
    dtx = ex[:t] + ex[t:2 * t] + ex[2 * t:3 * t]
    acx = ex[3 * t:4 * t] + ex[4 * t:5 * t] + ex[5 * t:]
    last = acx[t - 1:t, :]
    xdt = xs * dtx
    xw = xdt * jnp.exp(last - acx)
    eacx = jnp.exp(acx)
    elast = jnp.exp(last)

    gw = D_INNER // SSM_GROUPS
    hpg = SSM_HEADS // SSM_GROUPS
    ys = []
    for g in range(SSM_GROUPS):
        bg = bcm[:, g * SSM_STATE:(g + 1) * SSM_STATE]
        cg = bcm[:, (SSM_GROUPS + g) * SSM_STATE:(SSM_GROUPS + g + 1) * SSM_STATE]
        cb = _dot_nt(cg, bg)
        h_t = state_sc[g]
        yg = [_dot(cg, h_t) * eacx[:, g * gw:(g + 1) * gw]]
        intra = []
        for e in range(hpg):
            hd = g * hpg + e
            seg = acum[:, SM_DT + hd:SM_DT + hd + 1] - acum_t[SM_DT + hd:SM_DT + hd + 1, :]
            m = cb * jnp.exp(jnp.where(causal, seg, -jnp.inf))
            intra.append(_dot(m, xdt[:, hd * SSM_HEAD_DIM:(hd + 1) * SSM_HEAD_DIM]))
        ys.append(yg[0] + jnp.concatenate(intra, axis=1))
        state_sc[g] = h_t * elast[:, g * gw:(g + 1) * gw] + _dot(bg.T, xw[:, g * gw:(g + 1) * gw])
    y = jnp.concatenate(ys, axis=1) + xs * dskip_ref[...]
    y = y * (z * jax.nn.sigmoid(z))
    outs = []
    for g in range(SSM_GROUPS):
        yg = y[:, g * gw:(g + 1) * gw]
        outs.append(yg * lax.rsqrt(jnp.mean(yg * yg, axis=-1, keepdims=True) + NORM_EPS))
    y = jnp.concatenate(outs, axis=1) * normw_ref[...]
    y_ref[...] = y[:t_in].astype(y_ref.dtype)

    @pl.when(c == pl.num_programs(1) - 1)
    def _():
        hout_ref[...] = state_sc[...]


def _ssd(proj, row0, n_seq, t_in, n_chunks, valid_len, h0_t, prefix, lw, out_dtype):
    assert row0 % t_in == 0
    rb0 = row0 // t_in
    rowblk = lambda w, col: pl.BlockSpec((t_in, w), lambda s, c: (rb0 + s * n_chunks + c, col // w))
    per_seq = lambda shape: pl.BlockSpec((None,) + shape, lambda s, c: (s,) + (0,) * len(shape))
    const = lambda arr: pl.BlockSpec(arr.shape, lambda s, c: (0,) * arr.ndim)
    gw = D_INNER // SSM_GROUPS
    pxs, pbc = prefix[:, :, :D_INNER], prefix[:, :, D_INNER:]
    consts = (lw["cwx"], lw["cbx"], lw["cwb"], lw["cbb"], lw["dtb"], lw["aneg"], lw["dskip"], lw["normw"], lw["expand"])
    y, h_t = pl.pallas_call(
        functools.partial(_ssd_kernel, t_in=t_in, valid_len=valid_len),
        grid=(n_seq, n_chunks),
        in_specs=[rowblk(D_INNER, C_Z), rowblk(D_INNER, C_XS), rowblk(BC_DIM, C_BC), rowblk(LANES, C_SM),
                  per_seq((SSM_GROUPS, SSM_STATE, gw)), per_seq((CONV_W - 1, D_INNER)), per_seq((CONV_W - 1, BC_DIM))]
                 + [const(a) for a in consts],
        out_specs=[pl.BlockSpec((t_in, D_INNER), lambda s, c: (s * n_chunks + c, 0)),
                   per_seq((SSM_GROUPS, SSM_STATE, gw))],
        out_shape=[jax.ShapeDtypeStruct((n_seq * n_chunks * t_in, D_INNER), out_dtype),
                   jax.ShapeDtypeStruct((n_seq, SSM_GROUPS, SSM_STATE, gw), F32)],
        scratch_shapes=[pltpu.VMEM((SSM_GROUPS, SSM_STATE, gw), F32),
                        pltpu.VMEM((SSD_CHUNK + 8, D_INNER), F32),
                        pltpu.VMEM((SSD_CHUNK + 8, BC_DIM), F32)],
        compiler_params=_params("parallel", "arbitrary"),
        name="ssd",
    )(proj, proj, proj, proj, h0_t, pxs, pbc, *consts)
    return y, h_t


def _state_to_t(h):
    n = h.shape[0]
    hpg = SSM_HEADS // SSM_GROUPS
    return h.reshape(n, SSM_GROUPS, hpg, SSM_HEAD_DIM, SSM_STATE).transpose(0, 1, 4, 2, 3).reshape(
        n, SSM_GROUPS, SSM_STATE, hpg * SSM_HEAD_DIM)


def _state_from_t(h_t):
    n = h_t.shape[0]
    hpg = SSM_HEADS // SSM_GROUPS
    return h_t.reshape(n, SSM_GROUPS, SSM_STATE, hpg, SSM_HEAD_DIM).transpose(0, 1, 3, 4, 2).reshape(
        n, SSM_HEADS, SSM_HEAD_DIM, SSM_STATE)


def _sort_key(x):
    bits = lax.bitcast_convert_type(x, jnp.int32)
    bits = jnp.where(bits == INT_MIN, 0, bits)
    return bits ^ ((bits >> 31) & 0x7FFFFFFF)


def _fold_rows(x, group=8):
    x = x.reshape(x.shape[0] // group, group, x.shape[1])
    while x.shape[0] > 1:
        half, odd = divmod(x.shape[0], 2)
        folded = x[:half] + x[half:2 * half]
        if odd:
            folded = jnp.concatenate([folded[:1] + x[2 * half:], folded[1:]], axis=0) if half > 1 else folded + x[2 * half:]
        x = folded
    return x[0]


def _count_t(keys_sc, n_chunks, pred):
    def body(c, cnt):
        return cnt + _fold_rows(jnp.where(pred(keys_sc[c]), 1.0, 0.0))

    cnt = lax.fori_loop(0, n_chunks, body, jnp.zeros((8, LANES), F32))
    return jnp.sum(cnt, axis=0, keepdims=True)


HALF_MIN = -(2 ** 30)
SEARCH_UNROLLS = tuple(range(1, 17))


def _search_bounds(nkc):
    return sorted({min(b, nkc) for b in SEARCH_UNROLLS} | {nkc})


def _search_chunks(n_chunks, nkc):
    bounds = _search_bounds(nkc)
    n = jnp.int32(bounds[-1])
    for b in reversed(bounds[:-1]):
        n = jnp.where(n_chunks <= b, b, n)
    return n


def _kth_halved_t(kh_sc, n, k):
    total = n * kh_sc.shape[1]

    def count_ge(cand):
        below = None
        for c in range(n):
            neg = _fold_rows((kh_sc[c] - cand) >> 31)
            below = neg if below is None else below + neg
        return total + jnp.sum(below.astype(F32), axis=0, keepdims=True)

    def bit_body(i, u):
        cand = u | lax.shift_left(jnp.int32(1), 30 - i)
        return jnp.where(count_ge(cand + HALF_MIN) >= k, cand, u)

    return lax.fori_loop(0, 31, bit_body, jnp.zeros((1, LANES), jnp.int32)) + HALF_MIN


def _kth_key_t(keys_sc, kh_sc, n_chunks, k):
    bounds = _search_bounds(kh_sc.shape[0])

    def build(i):
        if i == len(bounds) - 1:
            return lambda: _kth_halved_t(kh_sc, bounds[i], k)
        return lambda: lax.cond(n_chunks <= bounds[i], lambda: _kth_halved_t(kh_sc, bounds[i], k), build(i + 1))

    odd = build(0)() * 2 + 1
    return jnp.where(_count_t(keys_sc, n_chunks, lambda key: key >= odd) >= k, odd, odd - 1)


def _select_t(keys_sc, kh_sc, bias_sc, n_chunks, k):
    kc = keys_sc.shape[1]
    thr = _kth_key_t(keys_sc, kh_sc, n_chunks, k)
    n_ge = _count_t(keys_sc, n_chunks, lambda key: key >= thr)
    surplus = jnp.max(jnp.where((n_ge > k) & (thr != INT_MIN), 1.0, 0.0))

    @pl.when(surplus == 0.0)
    def _():
        floor = jnp.maximum(thr, INT_MIN + 1)

        def body(c, carry):
            bias_sc[c] = jnp.where(keys_sc[c] >= floor, 0.0, MASK_BIAS)
            return carry

        lax.fori_loop(0, n_chunks, body, 0)

    @pl.when(surplus != 0.0)
    def _():
        need = k - _count_t(keys_sc, n_chunks, lambda key: key > thr)
        ri = lax.broadcasted_iota(jnp.int32, (kc, kc), 0)
        ci = lax.broadcasted_iota(jnp.int32, (kc, kc), 1)
        lower = jnp.where(ci <= ri, 1.0, 0.0).astype(jnp.bfloat16)

        def body(c, run):
            key = keys_sc[c]
            eqf = jnp.where((key == thr) & (key != INT_MIN), 1.0, 0.0)
            rank = jnp.dot(lower, eqf.astype(jnp.bfloat16), preferred_element_type=F32) + run
            sel = (key > thr) | ((eqf > 0.0) & (rank <= need))
            bias_sc[c] = jnp.where(sel, 0.0, MASK_BIAS)
            return run + jnp.sum(_fold_rows(eqf), axis=0, keepdims=True)

        lax.fori_loop(0, n_chunks, body, jnp.zeros((1, LANES), F32))


def _select_s(keys_sc, kh_sc, bias_ref, k):
    n_chunks, rows, width = keys_sc.shape

    def count(pred):
        return jnp.sum(jnp.sum(jnp.where(pred(keys_sc[...]), 1.0, 0.0), axis=0), axis=1, keepdims=True)

    def bit_body(i, u):
        cand = u | lax.shift_left(jnp.int32(1), 30 - i)
        below = jnp.sum((kh_sc[...] - (cand + HALF_MIN)[None]) >> 31, axis=0).astype(F32)
        return jnp.where(n_chunks * width + jnp.sum(below, axis=1, keepdims=True) >= k, cand, u)

    odd = (lax.fori_loop(0, 31, bit_body, jnp.zeros((rows, 1), jnp.int32)) + HALF_MIN) * 2 + 1
    thr = jnp.where(count(lambda key: key >= odd[None]) >= k, odd, odd - 1)
    n_ge = count(lambda key: key >= thr[None])
    surplus = jnp.max(jnp.where((n_ge > k) & (thr != INT_MIN), 1.0, 0.0))

    @pl.when(surplus == 0.0)
    def _():
        floor = jnp.maximum(thr, INT_MIN + 1)
        bias_ref[...] = jnp.where(keys_sc[...] >= floor[None], 0.0, MASK_BIAS)

    @pl.when(surplus != 0.0)
    def _():
        need = k - count(lambda key: key > thr[None])
        ri = lax.broadcasted_iota(jnp.int32, (width, width), 0)
        ci = lax.broadcasted_iota(jnp.int32, (width, width), 1)
        upper = jnp.where(ri <= ci, 1.0, 0.0).astype(jnp.bfloat16)

        def body(c, run):
            key = keys_sc[c]
            eqf = jnp.where((key == thr) & (key != INT_MIN), 1.0, 0.0)
            rank = jnp.dot(eqf.astype(jnp.bfloat16), upper, preferred_element_type=F32) + run
            sel = (key > thr) | ((eqf > 0.0) & (rank <= need))
            bias_ref[c] = jnp.where(sel, 0.0, MASK_BIAS)
            return run + jnp.sum(eqf, axis=1, keepdims=True)

        lax.fori_loop(0, n_chunks, body, jnp.zeros((rows, 1), F32))


def _pattn_kernel(qi_ref, smq_ref, q_ref, smk_ref, k_ref, v_ref, o_ref,
                  kb_sc, kib_sc, vt_sc, keys_sc, kh_sc, bias_sc, qz_sc, m_sc, acc_sc, *, kc, lp, topk):
    j = pl.program_id(1)
    nq = Q_BLOCK
    nkc = kb_sc.shape[0]

    @pl.when(j == 0)
    def _():
        for c in range(nkc):
            r0 = c * kc
            n = min(kc, lp - r0)

            def chunk(ref):
                x = ref[r0:r0 + n, :]
                return x if n == kc else jnp.concatenate([x, jnp.zeros((kc - n, x.shape[1]), x.dtype)], axis=0)

            kb_sc[c] = chunk(k_ref).astype(MXU_DTYPE)
            kib_sc[c] = chunk(smk_ref).astype(MXU_DTYPE)
            v_t = chunk(v_ref).T
            pad = jnp.concatenate([jnp.ones((1, kc), F32), jnp.zeros((V_ROWS - HEAD_DIM - 1, kc), F32)], axis=0)
            vt_sc[c] = jnp.concatenate([x for hk in range(ATT_KV_HEADS)
                                        for x in (v_t[hk * HEAD_DIM:(hk + 1) * HEAD_DIM], pad)], axis=0).astype(MXU_DTYPE)

    n_chunks = ((j + 1) * nq - 1) // kc + 1

    qi_t = (qi_ref[...] * IDX_DIM ** -0.5).T
    qiz = jnp.concatenate([qi_t[h * IDX_DIM:(h + 1) * IDX_DIM] for h in range(IDX_HEADS)], axis=1)
    qiz = jnp.concatenate([qiz, jnp.zeros((LANES - IDX_DIM, IDX_HEADS * nq), F32)], axis=0).astype(MXU_DTYPE)
    smq_t = smq_ref[...].T
    wi_row = jnp.concatenate([smq_t[SM_WI + h:SM_WI + h + 1] for h in range(IDX_HEADS)], axis=1) * IDX_HEADS ** -0.5
    kpos = lax.broadcasted_iota(jnp.int32, (kc, nq), 0)
    qpos = j * nq + lax.broadcasted_iota(jnp.int32, (kc, nq), 1)

    def score_body(c, carry):
        s = jnp.maximum(jnp.dot(kib_sc[c], qiz, preferred_element_type=F32), 0.0) * wi_row
        acc = s[:, :nq]
        for h in range(1, IDX_HEADS):
            acc = acc + s[:, h * nq:(h + 1) * nq]
        key = jnp.where(c * kc + kpos <= qpos, _sort_key(acc), INT_MIN)
        keys_sc[c] = key
        kh_sc[c] = key >> 1
        return carry

    lax.fori_loop(0, n_chunks, score_body, 0)

    def blank_body(c, carry):
        kh_sc[c] = jnp.full((kc, nq), HALF_MIN, jnp.int32)
        return carry

    lax.fori_loop(n_chunks, _search_chunks(n_chunks, nkc), blank_body, 0)
    _select_t(keys_sc, kh_sc, bias_sc, n_chunks, topk)

    q_t = (q_ref[...] * (HEAD_DIM ** -0.5 * LOG2_E)).T
    gw = ATT_GROUP * nq
    zero = jnp.zeros((HEAD_DIM, gw), F32)
    for hk in range(ATT_KV_HEADS):
        own = jnp.concatenate([q_t[(hk * ATT_GROUP + g) * HEAD_DIM:(hk * ATT_GROUP + g + 1) * HEAD_DIM]
                               for g in range(ATT_GROUP)], axis=1)
        qz_sc[hk] = jnp.concatenate([own if i == hk else zero for i in range(ATT_KV_HEADS)],
                                    axis=0).astype(MXU_DTYPE)
    m_sc[...] = jnp.full(m_sc.shape, MASK_BIAS, F32)
    acc_sc[...] = jnp.zeros(acc_sc.shape, F32)

    def att_body(c, carry):
        kch = kb_sc[c]
        vt = vt_sc[c]
        b = bias_sc[c]
        b4 = jnp.concatenate([b] * ATT_GROUP, axis=1)

        def qk(hk):
            return jnp.dot(kch, qz_sc[hk], preferred_element_type=F32) + b4

        def soft(hk, s):
            m_old = m_sc[hk]
            m_new = jnp.maximum(m_old, jnp.max(s, axis=0, keepdims=True))
            m_sc[hk] = m_new
            return jnp.exp2(m_old - m_new), jnp.exp2(s - m_new).astype(MXU_DTYPE)

        def pv(hk, alpha, p):
            acc_sc[hk] = alpha * acc_sc[hk] + jnp.dot(vt[hk * V_ROWS:(hk + 1) * V_ROWS], p,
                                                      preferred_element_type=F32)

        nh = ATT_KV_HEADS
        s = {0: qk(0), 1: qk(1)}
        ap = {}
        for hk in range(nh):
            ap[hk] = soft(hk, s.pop(hk))
            if hk + 2 < nh:
                s[hk + 2] = qk(hk + 2)
            if hk >= 1:
                pv(hk - 1, *ap.pop(hk - 1))
        pv(nh - 1, *ap.pop(nh - 1))
        return carry

    lax.fori_loop(0, n_chunks, att_body, 0)

    heads = []
    for hk in range(ATT_KV_HEADS):
        acc = acc_sc[hk]
        o = acc[:HEAD_DIM] / acc[HEAD_DIM:HEAD_DIM + 1]
        heads += [o[:, g * nq:(g + 1) * nq] for g in range(ATT_GROUP)]
    o_ref[...] = jnp.concatenate(heads, axis=0).T.astype(o_ref.dtype)


def _prompt_attention(proj, n_seq, lp, topk):
    nqb = lp // Q_BLOCK
    kc = 512
    nkc = -(-lp // kc)
    qblk = lambda w, col: pl.BlockSpec((Q_BLOCK, w), lambda b, j: (b * nqb + j, col // w))
    seqblk = lambda w, col: pl.BlockSpec((lp, w), lambda b, j: (b, col // w))
    gw = ATT_GROUP * Q_BLOCK
    return pl.pallas_call(
        functools.partial(_pattn_kernel, kc=kc, lp=lp, topk=topk),
        grid=(n_seq, nqb),
        in_specs=[qblk(QI_DIM, C_QI), qblk(LANES, C_SM), qblk(ATT_DIM, C_Q),
                  seqblk(LANES, C_SM), seqblk(KV_DIM, C_K), seqblk(KV_DIM, C_V)],
        out_specs=pl.BlockSpec((Q_BLOCK, ATT_DIM), lambda b, j: (b * nqb + j, 0)),
        out_shape=jax.ShapeDtypeStruct((n_seq * lp, ATT_DIM), MXU_DTYPE),
        scratch_shapes=[pltpu.VMEM((nkc, kc, KV_DIM), MXU_DTYPE), pltpu.VMEM((nkc, kc, LANES), MXU_DTYPE),
                        pltpu.VMEM((nkc, ATT_KV_HEADS * V_ROWS, kc), MXU_DTYPE),
                        pltpu.VMEM((nkc, kc, Q_BLOCK), jnp.int32), pltpu.VMEM((nkc, kc, Q_BLOCK), jnp.int32),
                        pltpu.VMEM((nkc, kc, Q_BLOCK), F32),
                        pltpu.VMEM((ATT_KV_HEADS, KV_DIM, gw), MXU_DTYPE),
                        pltpu.VMEM((ATT_KV_HEADS, 1, gw), F32),
                        pltpu.VMEM((ATT_KV_HEADS, V_ROWS, gw), F32)],
        compiler_params=_params("parallel", "arbitrary"),
        name="prompt_attention",
    )(proj, proj, proj, proj, proj, proj)


def _pad_rows(x, n):
    return jnp.concatenate([x, jnp.zeros((n - x.shape[0], x.shape[1]), x.dtype)], axis=0)


def _sselect_kernel(pt_ref, qi_ref, sm_ref, *rest, pages_per_step, n_pages, page, ns, topk):
    page_refs = rest[:pages_per_step]
    bias_ref, keys_sc, kh_sc = rest[pages_per_step:]
    g = pl.program_id(1)
    qi = (qi_ref[...] * IDX_DIM ** -0.5).astype(MXU_DTYPE)
    qs = jnp.concatenate([qi[:, h * IDX_DIM:(h + 1) * IDX_DIM] for h in range(IDX_HEADS)], axis=0)
    sm = sm_ref[...]
    wi = sm[:, SM_WI:SM_WI + IDX_HEADS] * IDX_HEADS ** -0.5

    def weigh(s):
        s = jnp.maximum(s, 0.0)
        acc = s[:ns] * wi[:, 0:1]
        for h in range(1, IDX_HEADS):
            acc = acc + s[h * ns:(h + 1) * ns] * wi[:, h:h + 1]
        return _sort_key(acc)

    keys = weigh(_dot(qs, jnp.concatenate([r[...] for r in page_refs], axis=1)))
    for i in range(pages_per_step):
        keys_sc[g * pages_per_step + i] = keys[:, i * page:(i + 1) * page]
        kh_sc[g * pages_per_step + i] = keys[:, i * page:(i + 1) * page] >> 1

    @pl.when(g == pl.num_programs(1) - 1)
    def _():
        qrow = lax.broadcasted_iota(jnp.int32, (ns, page), 0)
        lane = lax.broadcasted_iota(jnp.int32, (ns, page), 1)
        new = weigh(_dot_nt(qs, _pad_rows(sm[:, SM_KI:SM_KI + IDX_DIM], page)))
        new = jnp.where(lane <= qrow, new, INT_MIN)
        keys_sc[n_pages] = new
        kh_sc[n_pages] = new >> 1
        _select_s(keys_sc, kh_sc, bias_ref, topk)


def _sattn_kernel(pt_ref, q_ref, kn_ref, vn_ref, bias_ref, *rest, pages_per_step, n_pages, page, ns):
    k_refs = rest[:pages_per_step]
    v_refs = rest[pages_per_step:2 * pages_per_step]
    o_ref, qbd_sc, m_sc, l_sc, acc_sc = rest[2 * pages_per_step:]
    g = pl.program_id(1)

    @pl.when(g == 0)
    def _():
        q = q_ref[...] * HEAD_DIM ** -0.5
        rows = []
        for h in range(ATT_HEADS):
            hk = h // ATT_GROUP
            parts = [jnp.zeros((ns, HEAD_DIM), F32)] * ATT_KV_HEADS
            parts[hk] = q[:, h * HEAD_DIM:(h + 1) * HEAD_DIM]
            rows.append(jnp.concatenate(parts, axis=1))
        qbd_sc[...] = jnp.concatenate(rows, axis=0).astype(MXU_DTYPE)
        m_sc[...] = jnp.full(m_sc.shape, MASK_BIAS, F32)
        l_sc[...] = jnp.zeros(l_sc.shape, F32)
        acc_sc[...] = jnp.zeros(acc_sc.shape, F32)

    def attend(s, b, pv):
        s = s + jnp.concatenate([b] * ATT_HEADS, axis=0)
        m_old = m_sc[...]
        m_new = jnp.maximum(m_old, jnp.max(s, axis=1, keepdims=True))
        alpha = jnp.exp(m_old - m_new)
        p = jnp.exp(s - m_new)
        l_sc[...] = alpha * l_sc[...] + jnp.sum(p, axis=1, keepdims=True)
        acc_sc[...] = alpha * acc_sc[...] + pv(p)
        m_sc[...] = m_new

    kcat = jnp.concatenate([r[...] for r in k_refs], axis=1)
    vcat = jnp.concatenate([r[...] for r in v_refs], axis=1)
    bcat = jnp.concatenate([bias_ref[g * pages_per_step + i] for i in range(pages_per_step)], axis=1)
    attend(_dot(qbd_sc[...], kcat), bcat, lambda p: _dot_nt(p, vcat))

    @pl.when(g == pl.num_programs(1) - 1)
    def _():
        vn = _pad_rows(vn_ref[...], page)
        attend(_dot_nt(qbd_sc[...], _pad_rows(kn_ref[...], page)), bias_ref[n_pages], lambda p: _dot(p, vn))
        o = acc_sc[...] / l_sc[...]
        o_ref[...] = jnp.concatenate(
            [o[h * ns:(h + 1) * ns, (h // ATT_GROUP) * HEAD_DIM:(h // ATT_GROUP + 1) * HEAD_DIM]
             for h in range(ATT_HEADS)], axis=1)


def _key_minor(cache):
    nd = cache.ndim
    t = jnp.transpose(cache, (0, 1) + tuple(range(3, nd)) + (2,))
    return t.reshape(t.shape[:2] + (-1, t.shape[-1]))


def _sample_attention(proj, row0, page_table, ck_t, cv_t, cki_t, layer, ns, topk):
    bd, n_pages = page_table.shape
    page = cki_t.shape[3]
    assert page == LANES and row0 % ns == 0
    pps = max(d for d in range(1, 17) if n_pages % d == 0)
    ng = n_pages // pps
    rb0 = row0 // ns
    rowblk = lambda w, col: pl.BlockSpec((ns, w), lambda s, g, pt: (rb0 + s, col // w))
    pageblk = lambda w, i: pl.BlockSpec((None, None, w, page), lambda s, g, pt: (layer, pt[s, g * pps + i], 0, 0))
    biasblk = pl.BlockSpec((None, n_pages + 1, ns, page), lambda s, g, pt: (s, 0, 0, 0))
    static = dict(pages_per_step=pps, n_pages=n_pages, page=page, ns=ns)

    bias = pl.pallas_call(
        functools.partial(_sselect_kernel, topk=topk, **static),
        grid_spec=pltpu.PrefetchScalarGridSpec(
            num_scalar_prefetch=1, grid=(bd, ng),
            in_specs=[rowblk(QI_DIM, C_QI), rowblk(LANES, C_SM)] + [pageblk(IDX_DIM, i) for i in range(pps)],
            out_specs=biasblk,
            scratch_shapes=[pltpu.VMEM((n_pages + 1, ns, page), jnp.int32)] * 2),
        out_shape=jax.ShapeDtypeStruct((bd, n_pages + 1, ns, page), F32),
        compiler_params=_params("parallel", "arbitrary"),
        name="sample_select",
    )(page_table, proj, proj, *([cki_t] * pps))

    rows = ATT_HEADS * ns
    return pl.pallas_call(
        functools.partial(_sattn_kernel, **static),
        grid_spec=pltpu.PrefetchScalarGridSpec(
            num_scalar_prefetch=1, grid=(bd, ng),
            in_specs=[rowblk(ATT_DIM, C_Q), rowblk(KV_DIM, C_K), rowblk(KV_DIM, C_V), biasblk]
                     + [pageblk(KV_DIM, i) for i in range(pps)] * 2,
            out_specs=pl.BlockSpec((ns, ATT_DIM), lambda s, g, pt: (s, 0)),
            scratch_shapes=[pltpu.VMEM((rows, KV_DIM), MXU_DTYPE), pltpu.VMEM((rows, 1), F32),
                            pltpu.VMEM((rows, 1), F32), pltpu.VMEM((rows, KV_DIM), F32)]),
        out_shape=jax.ShapeDtypeStruct((bd * ns, ATT_DIM), F32),
        compiler_params=_params("parallel", "arbitrary"),
        name="sample_attention",
    )(page_table, proj, proj, proj, bias, *([ck_t] * pps), *([cv_t] * pps))


def _regroup_w_in(w):
    o_dt = D_INNER + D_INNER + BC_DIM
    o_q = o_dt + SSM_HEADS
    o_ki = o_q + ATT_DIM + 2 * KV_DIM + QI_DIM
    o_wi = o_ki + IDX_DIM
    o_g = o_wi + IDX_HEADS
    parts = [w[:, :o_dt], w[:, o_q:o_ki], w[:, o_g:o_g + 2 * D_MODEL], w[:, o_ki:o_wi], w[:, o_dt:o_q], w[:, o_wi:o_g]]
    used = sum(p.shape[1] for p in parts)
    parts.append(jnp.zeros((w.shape[0], PROJ_COLS - used), w.dtype))
    w = jnp.concatenate(parts, axis=1).astype(MXU_DTYPE)
    return w.reshape(w.shape[0], PROJ_COLS // PROJ_TN, PROJ_TN).transpose(1, 0, 2)


def _small_lanes(v):
    return jnp.zeros((1, LANES), F32).at[0, SM_DT:SM_DT + SSM_HEADS].set(v.astype(F32))


def _layer_consts(conv_w, conv_b, dt_bias, a_log, d_skip, ssm_norm_w):
    head_of_lane = jnp.arange(D_INNER) // SSM_HEAD_DIM
    expand = (jnp.arange(LANES)[:, None] == (SM_DT + head_of_lane)[None, :]).astype(jnp.bfloat16)
    return dict(
        cwx=conv_w[:, :D_INNER], cbx=conv_b[None, :D_INNER], cwb=conv_w[:, D_INNER:], cbb=conv_b[None, D_INNER:],
        dtb=_small_lanes(dt_bias), aneg=_small_lanes(-jnp.exp(a_log.astype(F32))),
        dskip=jnp.repeat(d_skip.astype(F32), SSM_HEAD_DIM)[None, :], normw=ssm_norm_w.astype(F32)[None, :],
        expand=expand)


def kernel(x_prompt, x_sample, cache_k, cache_v, cache_kidx, state_ssm, state_conv, page_table, meta_tokens, norm1_w, w_in, conv_w, conv_b, dt_bias, a_log, d_skip, ssm_norm_w, w_ssm_proj, w_attn_proj, w_out, norm2_w, w_up, w_down, final_norm_w):
    b, seq, d = x_prompt.shape
    bd, ns = x_sample.shape[:2]
    depth = w_in.shape[0]
    assert d == D_MODEL and ns % 8 == 0
    past = page_table.shape[1] * cache_kidx.shape[2]
    lv = N_META + seq
    lp = -(-lv // Q_BLOCK) * Q_BLOCK
    rp = b * lp
    rs = bd * ns
    r = -(-(rp + rs) // LANES) * LANES
    topk_p = min(TOPK_MAX, seq // 4)
    topk_s = min(TOPK_MAX, (past + ns) // 4)

    hp = jnp.concatenate([jnp.broadcast_to(meta_tokens[None].astype(F32), (b, N_META, d)), x_prompt,
                          jnp.zeros((b, lp - lv, d), F32)], axis=1).reshape(rp, d)
    h = jnp.concatenate([hp, x_sample.reshape(rs, d), jnp.zeros((r - rp - rs, d), F32)], axis=0)

    ck_t, cv_t, cki_t = _key_minor(cache_k), _key_minor(cache_v), _key_minor(cache_kidx)
    zero_state = jnp.zeros((b, SSM_GROUPS, SSM_STATE, D_INNER // SSM_GROUPS), F32)
    zero_prefix = jnp.zeros((b, CONV_W - 1, D_INNER + BC_DIM), F32)
    outs = [[] for _ in range(10)]
    for l in range(depth):
        lw = _layer_consts(conv_w[l], conv_b[l], dt_bias[l], a_log[l], d_skip[l], ssm_norm_w[l])
        proj = _in_proj(_rmsnorm(h, norm1_w[l], MXU_DTYPE), _regroup_w_in(w_in[l]))

        yp, hp_t = _ssd(proj, 0, b, SSD_CHUNK, lp // SSD_CHUNK, lv, zero_state, zero_prefix, lw, MXU_DTYPE)
        ysm, hs_t = _ssd(proj, rp, bd, ns, 1, ns, _state_to_t(state_ssm[l]), state_conv[l], lw, F32)
        ap = _prompt_attention(proj, b, lp, topk_p)
        asm = _sample_attention(proj, rp, page_table, ck_t, cv_t, cki_t, l, ns, topk_s)
        tail = r - rp - rs
        if tail:
            ysm = jnp.concatenate([ysm, jnp.zeros((tail, D_INNER), F32)], axis=0)
            asm = jnp.concatenate([asm, jnp.zeros((tail, ATT_DIM), F32)], axis=0)

        h = _merge(h, rp, yp, ysm, ap, asm, proj, w_ssm_proj[l].astype(MXU_DTYPE), w_attn_proj[l].astype(MXU_DTYPE),
                   w_out[l].astype(MXU_DTYPE))
        h = _mlp(h, norm2_w[l], w_up[l].astype(MXU_DTYPE), w_down[l].astype(MXU_DTYPE))

        def pcols(c0, w):
            return proj[:rp, c0:c0 + w].reshape(b, lp, w)[:, :lv]

        def scols(c0, w):
            return proj[rp:rp + rs, c0:c0 + w].reshape(bd, ns, w)

        xbc_w = D_INNER + BC_DIM
        p_conv = jnp.stack([proj[s * lp + lv - (CONV_W - 1):s * lp + lv, C_XS:C_XS + xbc_w] for s in range(b)])
        s_conv = jnp.concatenate([state_conv[l], scols(C_XS, xbc_w)], axis=1)[:, -(CONV_W - 1):]
        for acc, t in zip(outs, (
                pcols(C_K, KV_DIM).reshape(b, lv, ATT_KV_HEADS, HEAD_DIM),
                pcols(C_V, KV_DIM).reshape(b, lv, ATT_KV_HEADS, HEAD_DIM),
                pcols(C_SM + SM_KI, IDX_DIM),
                _state_from_t(hp_t),
                p_conv,
                scols(C_K, KV_DIM).reshape(bd, ns, ATT_KV_HEADS, HEAD_DIM),
                scols(C_V, KV_DIM).reshape(bd, ns, ATT_KV_HEADS, HEAD_DIM),
                scols(C_SM + SM_KI, IDX_DIM),
                _state_from_t(hs_t),
                s_conv)):
            acc.append(t)

    y = _rmsnorm(h, final_norm_w, F32)
    y_prompt = y[:rp].reshape(b, lp, d)[:, N_META:lv]
    y_sample = y[rp:rp + rs].reshape(bd, ns, d)
    return (y_prompt, y_sample) + tuple(jnp.stack(o) for o in outs)
```

```python
import functools
import math

import jax
import jax.numpy as jnp
from jax import lax
from jax.experimental import pallas as pl
from jax.experimental.pallas import tpu as pltpu

F32 = jnp.float32
MXU_DTYPE = jnp.bfloat16

N_META = 16
NORM_EPS = 1e-6
SSM_HEAD_DIM = 64
SSM_GROUPS = 4
SSM_STATE = 128
CONV_W = 4
ATT_HEADS = 16
ATT_KV_HEADS = 4
ATT_GROUP = ATT_HEADS // ATT_KV_HEADS
HEAD_DIM = 64
IDX_HEADS = 8
IDX_DIM = 64
TOPK_MAX = 256

LANES = 128
Q_BLOCK = 128
SSD_CHUNK = 128
VMEM_LIMIT = 56 << 20
MASK_BIAS = -1e30
INT_MIN = -(2 ** 31)
LOG2_E = 1.4426950408889634
V_ROWS = HEAD_DIM + 16

D_MODEL = 1024
D_INNER = 2 * D_MODEL
BC_DIM = 2 * SSM_GROUPS * SSM_STATE
SSM_HEADS = D_INNER // SSM_HEAD_DIM
ATT_DIM = ATT_HEADS * HEAD_DIM
KV_DIM = ATT_KV_HEADS * HEAD_DIM
QI_DIM = IDX_HEADS * IDX_DIM
C_Z = 0
C_XS = C_Z + D_INNER
C_BC = C_XS + D_INNER
C_Q = C_BC + BC_DIM
C_K = C_Q + ATT_DIM
C_V = C_K + KV_DIM
C_QI = C_V + KV_DIM
C_GS = C_QI + QI_DIM
C_GA = C_GS + D_MODEL
C_SM = C_GA + D_MODEL
SM_KI = 0
SM_DT = IDX_DIM
SM_WI = IDX_DIM + SSM_HEADS
PROJ_TN = 512
PROJ_COLS = -(-(C_SM + LANES) // PROJ_TN) * PROJ_TN


def _divisor_tile(n, max_tile, align):
    best = None
    for t in range(align, min(n, max_tile) + 1, align):
        if n % t == 0:
            best = t
    assert best is not None, (n, max_tile, align)
    return best


def _params(*sem):
    return pltpu.CompilerParams(dimension_semantics=sem, vmem_limit_bytes=VMEM_LIMIT)


def _dot(a, b):
    return jnp.dot(a.astype(MXU_DTYPE), b.astype(MXU_DTYPE), preferred_element_type=F32)


def _dot_nt(a, b):
    return lax.dot_general(a.astype(MXU_DTYPE), b.astype(MXU_DTYPE), (((1,), (1,)), ((), ())),
                           preferred_element_type=F32)


def _bf16_pieces(x):
    hi = x.astype(jnp.bfloat16)
    r = x - hi.astype(F32)
    mid = r.astype(jnp.bfloat16)
    return [hi, mid, (r - mid.astype(F32)).astype(jnp.bfloat16)]


def _dot_exact(a, b):
    return jnp.dot(a, b, preferred_element_type=F32, precision=lax.Precision.HIGHEST)


def _row_rmsnorm(x, w):
    return x * lax.rsqrt(jnp.mean(x * x, axis=-1, keepdims=True) + NORM_EPS) * w


def _in_proj_kernel(x_ref, nw_ref, w_ref, o_ref, u_sc):
    j = pl.program_id(1)

    @pl.when(j == 0)
    def _():
        u_sc[...] = _row_rmsnorm(x_ref[...], nw_ref[...]).astype(u_sc.dtype)

    o_ref[...] = jnp.dot(u_sc[...], w_ref[j], preferred_element_type=F32)


def _in_proj(h, norm_w, w):
    r, k = h.shape
    nt = w.shape[0]
    tm = _divisor_tile(r, 1100, 16)
    return pl.pallas_call(
        _in_proj_kernel,
        grid=(r // tm, nt),
        in_specs=[pl.BlockSpec((tm, k), lambda i, j: (i, 0)), pl.BlockSpec((1, k), lambda i, j: (0, 0)),
                  pl.BlockSpec(w.shape, lambda i, j: (0, 0, 0), pipeline_mode=pl.Buffered(1))],
        out_specs=pl.BlockSpec((tm, PROJ_TN), lambda i, j: (i, j)),
        out_shape=jax.ShapeDtypeStruct((r, nt * PROJ_TN), F32),
        scratch_shapes=[pltpu.VMEM((tm, k), MXU_DTYPE)],
        compiler_params=_params("parallel", "arbitrary"),
        name="in_proj",
    )(h, norm_w.reshape(1, k), w)


def _merge_kernel(h_ref, ysp_ref, yss_ref, yap_ref, yas_ref, gs_ref, ga_ref, wsp_ref, wap_ref, wo_ref, o_ref, *, n_p):
    prompt = pl.program_id(0) < n_p
    ys = jnp.where(prompt, ysp_ref[...], yss_ref[...].astype(MXU_DTYPE))
    ya = jnp.where(prompt, yap_ref[...], yas_ref[...].astype(MXU_DTYPE))
    a = jnp.dot(ys, wsp_ref[...], preferred_element_type=F32)
    b = jnp.dot(ya, wap_ref[...], preferred_element_type=F32)
    m = jax.nn.sigmoid(gs_ref[...]) * a + jax.nn.sigmoid(ga_ref[...]) * b
    o_ref[...] = h_ref[...] + _dot(m, wo_ref[...])


def _merge(h, rp, ys_p, ys_s, ya_p, ya_s, proj, w_sp, w_ap, w_o):
    r, d = h.shape
    tm = _divisor_tile(math.gcd(rp, r - rp), 256, 16)
    n_p = rp // tm
    const = lambda i: (0, 0)
    pblk = lambda w: pl.BlockSpec((tm, w), lambda i: (jnp.minimum(i, n_p - 1), 0))
    sblk = lambda w: pl.BlockSpec((tm, w), lambda i: (jnp.maximum(i - n_p, 0), 0))
    return pl.pallas_call(
        functools.partial(_merge_kernel, n_p=n_p),
        grid=(r // tm,),
        in_specs=[pl.BlockSpec((tm, d), lambda i: (i, 0)),
                  pblk(D_INNER), sblk(D_INNER), pblk(ATT_DIM), sblk(ATT_DIM),
                  pl.BlockSpec((tm, d), lambda i: (i, C_GS // D_MODEL)),
                  pl.BlockSpec((tm, d), lambda i: (i, C_GA // D_MODEL)),
                  pl.BlockSpec(w_sp.shape, const), pl.BlockSpec(w_ap.shape, const), pl.BlockSpec(w_o.shape, const)],
        out_specs=pl.BlockSpec((tm, d), lambda i: (i, 0)),
        out_shape=jax.ShapeDtypeStruct((r, d), F32),
        compiler_params=_params("parallel"),
        name="merge",
    )(h, ys_p, ys_s, ya_p, ya_s, proj, proj, w_sp, w_ap, w_o)


def _mlp_kernel(h_ref, nw_ref, wu_ref, wd_ref, *rest):
    x = h_ref[...]
    a = _dot(_row_rmsnorm(x, nw_ref[...]), wu_ref[...])
    a = jnp.square(jnp.maximum(a, 0.0))
    y = x + _dot(a, wd_ref[...])
    if len(rest) == 2:
        fw_ref, o_ref = rest
        y = _row_rmsnorm(y, fw_ref[...])
    else:
        o_ref, = rest
    o_ref[...] = y


def _mlp(h, norm_w, w_up, w_down, final_norm_w=None):
    r, d = h.shape
    tm = _divisor_tile(r, 256, 16)
    const = lambda i: (0, 0)
    extra = [] if final_norm_w is None else [final_norm_w.reshape(1, d)]
    return pl.pallas_call(
        _mlp_kernel,
        grid=(r // tm,),
        in_specs=[pl.BlockSpec((tm, d), lambda i: (i, 0)), pl.BlockSpec((1, d), const),
                  pl.BlockSpec(w_up.shape, const), pl.BlockSpec(w_down.shape, const)]
                 + [pl.BlockSpec((1, d), const)] * len(extra),
        out_specs=pl.BlockSpec((tm, d), lambda i: (i, 0)),
        out_shape=jax.ShapeDtypeStruct((r, d), F32),
        compiler_params=_params("parallel"),
        name="mlp",
    )(h, norm_w.reshape(1, d), w_up, w_down, *extra)


def _softplus(x):
    return jnp.maximum(x, 0.0) + jnp.log1p(jnp.exp(-jnp.abs(x)))


def _conv_silu(xp_sc, x, w_ref, b_ref, t):
    xp_sc[pl.ds(8, t), :] = x
    acc = b_ref[...] + x * w_ref[CONV_W - 1:CONV_W, :]
    for k in range(CONV_W - 1):
        acc = acc + xp_sc[pl.ds(5 + k, t), :] * w_ref[k:k + 1, :]
    xp_sc[pl.ds(5, CONV_W - 1), :] = xp_sc[pl.ds(t + 5, CONV_W - 1), :]
    return acc * jax.nn.sigmoid(acc)


def _ssd_kernel(z_ref, xs_ref, bc_ref, sm_ref, h0_ref, pxs_ref, pbc_ref,
                cwx_ref, cbx_ref, cwb_ref, cbb_ref, dtb_ref, aneg_ref, dskip_ref, normw_ref, expand_ref,
                y_ref, hout_ref, state_sc, xpx_sc, xpb_sc, *, t_in, valid_len):
    t = SSD_CHUNK
    c = pl.program_id(1)

    @pl.when(c == 0)
    def _():
        state_sc[...] = h0_ref[...]
        xpx_sc[pl.ds(5, CONV_W - 1), :] = pxs_ref[...]
        xpb_sc[pl.ds(5, CONV_W - 1), :] = pbc_ref[...]

    def rows(ref):
        x = ref[...]
        if t_in < t:
            x = jnp.concatenate([x, jnp.zeros((t - t_in, x.shape[1]), x.dtype)], axis=0)
        return x

    xs = _conv_silu(xpx_sc, rows(xs_ref), cwx_ref, cbx_ref, t)
    bcm = _conv_silu(xpb_sc, rows(bc_ref), cwb_ref, cbb_ref, t)
    z = rows(z_ref)

    row = lax.broadcasted_iota(jnp.int32, (t, LANES), 0)
    n_valid = jnp.minimum(valid_len - c * t, t_in)
    dt = jnp.where(row < n_valid, _softplus(rows(sm_ref) + dtb_ref[...]), 0.0)
    a = dt * aneg_ref[...]
    ri = lax.broadcasted_iota(jnp.int32, (t, t), 0)
    ci = lax.broadcasted_iota(jnp.int32, (t, t), 1)
    causal = ri >= ci
    acum = _dot_exact(jnp.where(causal, 1.0, 0.0), a)
    acum_t = acum.T
    ex = jnp.dot(jnp.concatenate(_bf16_pieces(dt) + _bf16_pieces(acum), axis=0), expand_ref[...],
                 preferred_element_type=F32)
    dtx = ex[:t] + ex[t:2 * t] + ex[2 * t:3 * t]
    acx = ex[3 * t:4 * t] + ex[4 * t:5 * t] + ex[5 * t:]
    last = acx[t - 1:t, :]
    xdt = xs * dtx
    xw = xdt * jnp.exp(last - acx)
    eacx = jnp.exp(acx)
    elast = jnp.exp(last)

    gw = D_INNER // SSM_GROUPS
    hpg = SSM_HEADS // SSM_GROUPS
    ys = []
    for g in range(SSM_GROUPS):
        bg = bcm[:, g * SSM_STATE:(g + 1) * SSM_STATE]
        cg = bcm[:, (SSM_GROUPS + g) * SSM_STATE:(SSM_GROUPS + g + 1) * SSM_STATE]
        cb = _dot_nt(cg, bg)
        h_t = state_sc[g]
        yg = [_dot(cg, h_t) * eacx[:, g * gw:(g + 1) * gw]]
        intra = []
        for e in range(hpg):
            hd = g * hpg + e
            seg = acum[:, SM_DT + hd:SM_DT + hd + 1] - acum_t[SM_DT + hd:SM_DT + hd + 1, :]
            m = cb * jnp.exp(jnp.where(causal, seg, -jnp.inf))
            intra.append(_dot(m, xdt[:, hd * SSM_HEAD_DIM:(hd + 1) * SSM_HEAD_DIM]))
        ys.append(yg[0] + jnp.concatenate(intra, axis=1))
        state_sc[g] = h_t * elast[:, g * gw:(g + 1) * gw] + _dot(bg.T, xw[:, g * gw:(g + 1) * gw])
    y = jnp.concatenate(ys, axis=1) + xs * dskip_ref[...]
    y = y * (z * jax.nn.sigmoid(z))
    outs = []
    for g in range(SSM_GROUPS):
        yg = y[:, g * gw:(g + 1) * gw]
        outs.append(yg * lax.rsqrt(jnp.mean(yg * yg, axis=-1, keepdims=True) + NORM_EPS))
    y = jnp.concatenate(outs, axis=1) * normw_ref[...]
    y_ref[...] = y[:t_in].astype(y_ref.dtype)

    @pl.when(c == pl.num_programs(1) - 1)
    def _():
        hout_ref[...] = state_sc[...]


def _ssd(proj, row0, n_seq, t_in, n_chunks, valid_len, h0_t, prefix, lw, out_dtype):
    assert row0 % t_in == 0
    rb0 = row0 // t_in
    rowblk = lambda w, col: pl.BlockSpec((t_in, w), lambda s, c: (rb0 + s * n_chunks + c, col // w))
    per_seq = lambda shape: pl.BlockSpec((None,) + shape, lambda s, c: (s,) + (0,) * len(shape))
    const = lambda arr: pl.BlockSpec(arr.shape, lambda s, c: (0,) * arr.ndim)
    gw = D_INNER // SSM_GROUPS
    pxs, pbc = prefix[:, :, :D_INNER], prefix[:, :, D_INNER:]
    consts = (lw["cwx"], lw["cbx"], lw["cwb"], lw["cbb"], lw["dtb"], lw["aneg"], lw["dskip"], lw["normw"], lw["expand"])
    y, h_t = pl.pallas_call(
        functools.partial(_ssd_kernel, t_in=t_in, valid_len=valid_len),
        grid=(n_seq, n_chunks),
        in_specs=[rowblk(D_INNER, C_Z), rowblk(D_INNER, C_XS), rowblk(BC_DIM, C_BC), rowblk(LANES, C_SM),
                  per_seq((SSM_GROUPS, SSM_STATE, gw)), per_seq((CONV_W - 1, D_INNER)), per_seq((CONV_W - 1, BC_DIM))]
                 + [const(a) for a in consts],
        out_specs=[pl.BlockSpec((t_in, D_INNER), lambda s, c: (s * n_chunks + c, 0)),
                   per_seq((SSM_GROUPS, SSM_STATE, gw))],
        out_shape=[jax.ShapeDtypeStruct((n_seq * n_chunks * t_in, D_INNER), out_dtype),
                   jax.ShapeDtypeStruct((n_seq, SSM_GROUPS, SSM_STATE, gw), F32)],
        scratch_shapes=[pltpu.VMEM((SSM_GROUPS, SSM_STATE, gw), F32),
                        pltpu.VMEM((SSD_CHUNK + 8, D_INNER), F32),
                        pltpu.VMEM((SSD_CHUNK + 8, BC_DIM), F32)],
        compiler_params=_params("parallel", "arbitrary"),
        name="ssd",
    )(proj, proj, proj, proj, h0_t, pxs, pbc, *consts)
    return y, h_t


def _state_to_t(h):
    n = h.shape[0]
    hpg = SSM_HEADS // SSM_GROUPS
    return h.reshape(n, SSM_GROUPS, hpg, SSM_HEAD_DIM, SSM_STATE).transpose(0, 1, 4, 2, 3).reshape(
        n, SSM_GROUPS, SSM_STATE, hpg * SSM_HEAD_DIM)


def _state_from_t(h_t):
    n = h_t.shape[0]
    hpg = SSM_HEADS // SSM_GROUPS
    return h_t.reshape(n, SSM_GROUPS, SSM_STATE, hpg, SSM_HEAD_DIM).transpose(0, 1, 3, 4, 2).reshape(
        n, SSM_HEADS, SSM_HEAD_DIM, SSM_STATE)


def _sort_key(x):
    bits = lax.bitcast_convert_type(x, jnp.int32)
    bits = jnp.where(bits == INT_MIN, 0, bits)
    return bits ^ ((bits >> 31) & 0x7FFFFFFF)


def _fold_rows(x, group=8):
    x = x.reshape(x.shape[0] // group, group, x.shape[1])
    while x.shape[0] > 1:
        half, odd = divmod(x.shape[0], 2)
        folded = x[:half] + x[half:2 * half]
        if odd:
            folded = jnp.concatenate([folded[:1] + x[2 * half:], folded[1:]], axis=0) if half > 1 else folded + x[2 * half:]
        x = folded
    return x[0]


def _count_t(keys_sc, n_chunks, pred):
    def body(c, cnt):
        return cnt + _fold_rows(jnp.where(pred(keys_sc[c]), 1.0, 0.0))

    cnt = lax.fori_loop(0, n_chunks, body, jnp.zeros((8, LANES), F32))
    return jnp.sum(cnt, axis=0, keepdims=True)


HALF_MIN = -(2 ** 30)
SEARCH_UNROLLS = tuple(range(1, 17))


def _search_bounds(nkc):
    return sorted({min(b, nkc) for b in SEARCH_UNROLLS} | {nkc})


def _search_chunks(n_chunks, nkc):
    bounds = _search_bounds(nkc)
    n = jnp.int32(bounds[-1])
    for b in reversed(bounds[:-1]):
        n = jnp.where(n_chunks <= b, b, n)
    return n


def _kth_halved_t(kh_sc, n, k):
    total = n * kh_sc.shape[1]

    def count_ge(cand):
        below = None
        for c in range(n):
            neg = _fold_rows((kh_sc[c] - cand) >> 31)
            below = neg if below is None else below + neg
        return total + jnp.sum(below.astype(F32), axis=0, keepdims=True)

    def bit_body(i, u):
        cand = u | lax.shift_left(jnp.int32(1), 30 - i)
        return jnp.where(count_ge(cand + HALF_MIN) >= k, cand, u)

    return lax.fori_loop(0, 31, bit_body, jnp.zeros((1, LANES), jnp.int32)) + HALF_MIN


def _kth_key_t(keys_sc, kh_sc, n_chunks, k):
    bounds = _search_bounds(kh_sc.shape[0])

    def build(i):
        if i == len(bounds) - 1:
            return lambda: _kth_halved_t(kh_sc, bounds[i], k)
        return lambda: lax.cond(n_chunks <= bounds[i], lambda: _kth_halved_t(kh_sc, bounds[i], k), build(i + 1))

    odd = build(0)() * 2 + 1
    return jnp.where(_count_t(keys_sc, n_chunks, lambda key: key >= odd) >= k, odd, odd - 1)


def _select_t(keys_sc, kh_sc, bias_sc, n_chunks, k):
    kc = keys_sc.shape[1]
    thr = _kth_key_t(keys_sc, kh_sc, n_chunks, k)
    n_ge = _count_t(keys_sc, n_chunks, lambda key: key >= thr)
    surplus = jnp.max(jnp.where((n_ge > k) & (thr != INT_MIN), 1.0, 0.0))

    @pl.when(surplus == 0.0)
    def _():
        floor = jnp.maximum(thr, INT_MIN + 1)

        def body(c, carry):
            bias_sc[c] = jnp.where(keys_sc[c] >= floor, 0.0, MASK_BIAS)
            return carry

        lax.fori_loop(0, n_chunks, body, 0)

    @pl.when(surplus != 0.0)
    def _():
        need = k - _count_t(keys_sc, n_chunks, lambda key: key > thr)
        ri = lax.broadcasted_iota(jnp.int32, (kc, kc), 0)
        ci = lax.broadcasted_iota(jnp.int32, (kc, kc), 1)
        lower = jnp.where(ci <= ri, 1.0, 0.0).astype(jnp.bfloat16)

        def body(c, run):
            key = keys_sc[c]
            eqf = jnp.where((key == thr) & (key != INT_MIN), 1.0, 0.0)
            rank = jnp.dot(lower, eqf.astype(jnp.bfloat16), preferred_element_type=F32) + run
            sel = (key > thr) | ((eqf > 0.0) & (rank <= need))
            bias_sc[c] = jnp.where(sel, 0.0, MASK_BIAS)
            return run + jnp.sum(_fold_rows(eqf), axis=0, keepdims=True)

        lax.fori_loop(0, n_chunks, body, jnp.zeros((1, LANES), F32))


def _select_s(keys_sc, kh_sc, bias_ref, k):
    n_chunks, rows, width = keys_sc.shape

    def count(pred):
        return jnp.sum(jnp.sum(jnp.where(pred(keys_sc[...]), 1.0, 0.0), axis=0), axis=1, keepdims=True)

    def bit_body(i, u):
        cand = u | lax.shift_left(jnp.int32(1), 30 - i)
        below = jnp.sum((kh_sc[...] - (cand + HALF_MIN)[None]) >> 31, axis=0).astype(F32)
        return jnp.where(n_chunks * width + jnp.sum(below, axis=1, keepdims=True) >= k, cand, u)

    odd = (lax.fori_loop(0, 31, bit_body, jnp.zeros((rows, 1), jnp.int32)) + HALF_MIN) * 2 + 1
    thr = jnp.where(count(lambda key: key >= odd[None]) >= k, odd, odd - 1)
    n_ge = count(lambda key: key >= thr[None])
    surplus = jnp.max(jnp.where((n_ge > k) & (thr != INT_MIN), 1.0, 0.0))

    @pl.when(surplus == 0.0)
    def _():
        floor = jnp.maximum(thr, INT_MIN + 1)
        bias_ref[...] = jnp.where(keys_sc[...] >= floor[None], 0.0, MASK_BIAS)

    @pl.when(surplus != 0.0)
    def _():
        need = k - count(lambda key: key > thr[None])
        ri = lax.broadcasted_iota(jnp.int32, (width, width), 0)
        ci = lax.broadcasted_iota(jnp.int32, (width, width), 1)
        upper = jnp.where(ri <= ci, 1.0, 0.0).astype(jnp.bfloat16)

        def body(c, run):
            key = keys_sc[c]
            eqf = jnp.where((key == thr) & (key != INT_MIN), 1.0, 0.0)
            rank = jnp.dot(eqf.astype(jnp.bfloat16), upper, preferred_element_type=F32) + run
            sel = (key > thr) | ((eqf > 0.0) & (rank <= need))
            bias_ref[c] = jnp.where(sel, 0.0, MASK_BIAS)
            return run + jnp.sum(eqf, axis=1, keepdims=True)

        lax.fori_loop(0, n_chunks, body, jnp.zeros((rows, 1), F32))


def _pattn_kernel(qi_ref, smq_ref, q_ref, smk_ref, k_ref, v_ref, o_ref,
                  kb_sc, kib_sc, vt_sc, keys_sc, kh_sc, bias_sc, qz_sc, m_sc, acc_sc, *, kc, lp, topk):
    j = pl.program_id(1)
    nq = Q_BLOCK
    nkc = kb_sc.shape[0]

    @pl.when(j == 0)
    def _():
        for c in range(nkc):
            r0 = c * kc
            n = min(kc, lp - r0)

            def chunk(ref):
                x = ref[r0:r0 + n, :]
                return x if n == kc else jnp.concatenate([x, jnp.zeros((kc - n, x.shape[1]), x.dtype)], axis=0)

            kb_sc[c] = chunk(k_ref).astype(MXU_DTYPE)
            kib_sc[c] = chunk(smk_ref).astype(MXU_DTYPE)
            v_t = chunk(v_ref).T
            pad = jnp.concatenate([jnp.ones((1, kc), F32), jnp.zeros((V_ROWS - HEAD_DIM - 1, kc), F32)], axis=0)
            vt_sc[c] = jnp.concatenate([x for hk in range(ATT_KV_HEADS)
                                        for x in (v_t[hk * HEAD_DIM:(hk + 1) * HEAD_DIM], pad)], axis=0).astype(MXU_DTYPE)

    n_chunks = ((j + 1) * nq - 1) // kc + 1

    qi_t = (qi_ref[...] * IDX_DIM ** -0.5).T
    qiz = jnp.concatenate([qi_t[h * IDX_DIM:(h + 1) * IDX_DIM] for h in range(IDX_HEADS)], axis=1)
    qiz = jnp.concatenate([qiz, jnp.zeros((LANES - IDX_DIM, IDX_HEADS * nq), F32)], axis=0).astype(MXU_DTYPE)
    smq_t = smq_ref[...].T
    wi_row = jnp.concatenate([smq_t[SM_WI + h:SM_WI + h + 1] for h in range(IDX_HEADS)], axis=1) * IDX_HEADS ** -0.5
    kpos = lax.broadcasted_iota(jnp.int32, (kc, nq), 0)
    qpos = j * nq + lax.broadcasted_iota(jnp.int32, (kc, nq), 1)

    def score_body(c, carry):
        s = jnp.maximum(jnp.dot(kib_sc[c], qiz, preferred_element_type=F32), 0.0) * wi_row
        acc = s[:, :nq]
        for h in range(1, IDX_HEADS):
            acc = acc + s[:, h * nq:(h + 1) * nq]
        key = jnp.where(c * kc + kpos <= qpos, _sort_key(acc), INT_MIN)
        keys_sc[c] = key
        kh_sc[c] = key >> 1
        return carry

    lax.fori_loop(0, n_chunks, score_body, 0)

    def blank_body(c, carry):
        kh_sc[c] = jnp.full((kc, nq), HALF_MIN, jnp.int32)
        return carry

    lax.fori_loop(n_chunks, _search_chunks(n_chunks, nkc), blank_body, 0)
    _select_t(keys_sc, kh_sc, bias_sc, n_chunks, topk)

    q_t = (q_ref[...] * (HEAD_DIM ** -0.5 * LOG2_E)).T
    gw = ATT_GROUP * nq
    zero = jnp.zeros((HEAD_DIM, gw), F32)
    for hk in range(ATT_KV_HEADS):
        own = jnp.concatenate([q_t[(hk * ATT_GROUP + g) * HEAD_DIM:(hk * ATT_GROUP + g + 1) * HEAD_DIM]
                               for g in range(ATT_GROUP)], axis=1)
        qz_sc[hk] = jnp.concatenate([own if i == hk else zero for i in range(ATT_KV_HEADS)],
                                    axis=0).astype(MXU_DTYPE)
    m_sc[...] = jnp.full(m_sc.shape, MASK_BIAS, F32)
    acc_sc[...] = jnp.zeros(acc_sc.shape, F32)

    def att_body(c, carry):
        kch = kb_sc[c]
        vt = vt_sc[c]
        b = bias_sc[c]
        b4 = jnp.concatenate([b] * ATT_GROUP, axis=1)

        def qk(hk):
            return jnp.dot(kch, qz_sc[hk], preferred_element_type=F32) + b4

        def soft(hk, s):
            m_old = m_sc[hk]
            m_new = jnp.maximum(m_old, jnp.max(s, axis=0, keepdims=True))
            m_sc[hk] = m_new
            return jnp.exp2(m_old - m_new), jnp.exp2(s - m_new).astype(MXU_DTYPE)

        def pv(hk, alpha, p):
            acc_sc[hk] = alpha * acc_sc[hk] + jnp.dot(vt[hk * V_ROWS:(hk + 1) * V_ROWS], p,
                                                      preferred_element_type=F32)

        nh = ATT_KV_HEADS
        s = {0: qk(0), 1: qk(1)}
        ap = {}
        for hk in range(nh):
            ap[hk] = soft(hk, s.pop(hk))
            if hk + 2 < nh:
                s[hk + 2] = qk(hk + 2)
            if hk >= 1:
                pv(hk - 1, *ap.pop(hk - 1))
        pv(nh - 1, *ap.pop(nh - 1))
        return carry

    lax.fori_loop(0, n_chunks, att_body, 0)

    heads = []
    for hk in range(ATT_KV_HEADS):
        acc = acc_sc[hk]
        o = acc[:HEAD_DIM] / acc[HEAD_DIM:HEAD_DIM + 1]
        heads += [o[:, g * nq:(g + 1) * nq] for g in range(ATT_GROUP)]
    o_ref[...] = jnp.concatenate(heads, axis=0).T.astype(o_ref.dtype)


def _prompt_attention(proj, n_seq, lp, topk):
    nqb = lp // Q_BLOCK
    kc = 512
    nkc = -(-lp // kc)
    qblk = lambda w, col: pl.BlockSpec((Q_BLOCK, w), lambda b, j: (b * nqb + j, col // w))
    seqblk = lambda w, col: pl.BlockSpec((lp, w), lambda b, j: (b, col // w))
    gw = ATT_GROUP * Q_BLOCK
    return pl.pallas_call(
        functools.partial(_pattn_kernel, kc=kc, lp=lp, topk=topk),
        grid=(n_seq, nqb),
        in_specs=[qblk(QI_DIM, C_QI), qblk(LANES, C_SM), qblk(ATT_DIM, C_Q),
                  seqblk(LANES, C_SM), seqblk(KV_DIM, C_K), seqblk(KV_DIM, C_V)],
        out_specs=pl.BlockSpec((Q_BLOCK, ATT_DIM), lambda b, j: (b * nqb + j, 0)),
        out_shape=jax.ShapeDtypeStruct((n_seq * lp, ATT_DIM), MXU_DTYPE),
        scratch_shapes=[pltpu.VMEM((nkc, kc, KV_DIM), MXU_DTYPE), pltpu.VMEM((nkc, kc, LANES), MXU_DTYPE),
                        pltpu.VMEM((nkc, ATT_KV_HEADS * V_ROWS, kc), MXU_DTYPE),
                        pltpu.VMEM((nkc, kc, Q_BLOCK), jnp.int32), pltpu.VMEM((nkc, kc, Q_BLOCK), jnp.int32),
                        pltpu.VMEM((nkc, kc, Q_BLOCK), F32),
                        pltpu.VMEM((ATT_KV_HEADS, KV_DIM, gw), MXU_DTYPE),
                        pltpu.VMEM((ATT_KV_HEADS, 1, gw), F32),
                        pltpu.VMEM((ATT_KV_HEADS, V_ROWS, gw), F32)],
        compiler_params=_params("parallel", "arbitrary"),
        name="prompt_attention",
    )(proj, proj, proj, proj, proj, proj)


def _pad_rows(x, n):
    return jnp.concatenate([x, jnp.zeros((n - x.shape[0], x.shape[1]), x.dtype)], axis=0)


def _sselect_kernel(pt_ref, qi_ref, sm_ref, *rest, pages_per_step, n_pages, page, ns, topk):
    page_refs = rest[:pages_per_step]
    bias_ref, keys_sc, kh_sc = rest[pages_per_step:]
    g = pl.program_id(1)
    qi = (qi_ref[...] * IDX_DIM ** -0.5).astype(MXU_DTYPE)
    qs = jnp.concatenate([qi[:, h * IDX_DIM:(h + 1) * IDX_DIM] for h in range(IDX_HEADS)], axis=0)
    sm = sm_ref[...]
    wi = sm[:, SM_WI:SM_WI + IDX_HEADS] * IDX_HEADS ** -0.5

    def weigh(s):
        s = jnp.maximum(s, 0.0)
        acc = s[:ns] * wi[:, 0:1]
        for h in range(1, IDX_HEADS):
            acc = acc + s[h * ns:(h + 1) * ns] * wi[:, h:h + 1]
        return _sort_key(acc)

    keys = weigh(_dot(qs, jnp.concatenate([r[...] for r in page_refs], axis=1)))
    for i in range(pages_per_step):
        keys_sc[g * pages_per_step + i] = keys[:, i * page:(i + 1) * page]
        kh_sc[g * pages_per_step + i] = keys[:, i * page:(i + 1) * page] >> 1

    @pl.when(g == pl.num_programs(1) - 1)
    def _():
        qrow = lax.broadcasted_iota(jnp.int32, (ns, page), 0)
        lane = lax.broadcasted_iota(jnp.int32, (ns, page), 1)
        new = weigh(_dot_nt(qs, _pad_rows(sm[:, SM_KI:SM_KI + IDX_DIM], page)))
        new = jnp.where(lane <= qrow, new, INT_MIN)
        keys_sc[n_pages] = new
        kh_sc[n_pages] = new >> 1
        _select_s(keys_sc, kh_sc, bias_ref, topk)


def _sattn_kernel(pt_ref, q_ref, kn_ref, vn_ref, bias_ref, *rest, pages_per_step, n_pages, page, ns):
    k_refs = rest[:pages_per_step]
    v_refs = rest[pages_per_step:2 * pages_per_step]
    o_ref, qbd_sc, m_sc, l_sc, acc_sc = rest[2 * pages_per_step:]
    g = pl.program_id(1)

    @pl.when(g == 0)
    def _():
        q = q_ref[...] * HEAD_DIM ** -0.5
        rows = []
        for h in range(ATT_HEADS):
            hk = h // ATT_GROUP
            parts = [jnp.zeros((ns, HEAD_DIM), F32)] * ATT_KV_HEADS
            parts[hk] = q[:, h * HEAD_DIM:(h + 1) * HEAD_DIM]
            rows.append(jnp.concatenate(parts, axis=1))
        qbd_sc[...] = jnp.concatenate(rows, axis=0).astype(MXU_DTYPE)
        m_sc[...] = jnp.full(m_sc.shape, MASK_BIAS, F32)
        l_sc[...] = jnp.zeros(l_sc.shape, F32)
        acc_sc[...] = jnp.zeros(acc_sc.shape, F32)

    def attend(s, b, pv):
        s = s + jnp.concatenate([b] * ATT_HEADS, axis=0)
        m_old = m_sc[...]
        m_new = jnp.maximum(m_old, jnp.max(s, axis=1, keepdims=True))
        alpha = jnp.exp(m_old - m_new)
        p = jnp.exp(s - m_new)
        l_sc[...] = alpha * l_sc[...] + jnp.sum(p, axis=1, keepdims=True)
        acc_sc[...] = alpha * acc_sc[...] + pv(p)
        m_sc[...] = m_new

    kcat = jnp.concatenate([r[...] for r in k_refs], axis=1)
    vcat = jnp.concatenate([r[...] for r in v_refs], axis=1)
    bcat = jnp.concatenate([bias_ref[g * pages_per_step + i] for i in range(pages_per_step)], axis=1)
    attend(_dot(qbd_sc[...], kcat), bcat, lambda p: _dot_nt(p, vcat))

    @pl.when(g == pl.num_programs(1) - 1)
    def _():
        vn = _pad_rows(vn_ref[...], page)
        attend(_dot_nt(qbd_sc[...], _pad_rows(kn_ref[...], page)), bias_ref[n_pages], lambda p: _dot(p, vn))
        o = acc_sc[...] / l_sc[...]
        o_ref[...] = jnp.concatenate(
            [o[h * ns:(h + 1) * ns, (h // ATT_GROUP) * HEAD_DIM:(h // ATT_GROUP + 1) * HEAD_DIM]
             for h in range(ATT_HEADS)], axis=1)


def _key_minor(cache):
    nd = cache.ndim
    t = jnp.transpose(cache, (0, 1) + tuple(range(3, nd)) + (2,))
    return t.reshape(t.shape[:2] + (-1, t.shape[-1]))


def _sample_attention(proj, row0, page_table, ck_t, cv_t, cki_t, layer, ns, topk):
    bd, n_pages = page_table.shape
    page = cki_t.shape[3]
    assert page == LANES and row0 % ns == 0
    pps = max(d for d in range(1, 17) if n_pages % d == 0)
    ng = n_pages // pps
    rb0 = row0 // ns
    rowblk = lambda w, col: pl.BlockSpec((ns, w), lambda s, g, pt: (rb0 + s, col // w))
    pageblk = lambda w, i: pl.BlockSpec((None, None, w, page), lambda s, g, pt: (layer, pt[s, g * pps + i], 0, 0))
    biasblk = pl.BlockSpec((None, n_pages + 1, ns, page), lambda s, g, pt: (s, 0, 0, 0))
    static = dict(pages_per_step=pps, n_pages=n_pages, page=page, ns=ns)

    bias = pl.pallas_call(
        functools.partial(_sselect_kernel, topk=topk, **static),
        grid_spec=pltpu.PrefetchScalarGridSpec(
            num_scalar_prefetch=1, grid=(bd, ng),
            in_specs=[rowblk(QI_DIM, C_QI), rowblk(LANES, C_SM)] + [pageblk(IDX_DIM, i) for i in range(pps)],
            out_specs=biasblk,
            scratch_shapes=[pltpu.VMEM((n_pages + 1, ns, page), jnp.int32)] * 2),
        out_shape=jax.ShapeDtypeStruct((bd, n_pages + 1, ns, page), F32),
        compiler_params=_params("parallel", "arbitrary"),
        name="sample_select",
    )(page_table, proj, proj, *([cki_t] * pps))

    rows = ATT_HEADS * ns
    return pl.pallas_call(
        functools.partial(_sattn_kernel, **static),
        grid_spec=pltpu.PrefetchScalarGridSpec(
            num_scalar_prefetch=1, grid=(bd, ng),
            in_specs=[rowblk(ATT_DIM, C_Q), rowblk(KV_DIM, C_K), rowblk(KV_DIM, C_V), biasblk]
                     + [pageblk(KV_DIM, i) for i in range(pps)] * 2,
            out_specs=pl.BlockSpec((ns, ATT_DIM), lambda s, g, pt: (s, 0)),
            scratch_shapes=[pltpu.VMEM((rows, KV_DIM), MXU_DTYPE), pltpu.VMEM((rows, 1), F32),
                            pltpu.VMEM((rows, 1), F32), pltpu.VMEM((rows, KV_DIM), F32)]),
        out_shape=jax.ShapeDtypeStruct((bd * ns, ATT_DIM), F32),
        compiler_params=_params("parallel", "arbitrary"),
        name="sample_attention",
    )(page_table, proj, proj, proj, bias, *([ck_t] * pps), *([cv_t] * pps))


def _regroup_w_in(w):
    o_dt = D_INNER + D_INNER + BC_DIM
    o_q = o_dt + SSM_HEADS
    o_ki = o_q + ATT_DIM + 2 * KV_DIM + QI_DIM
    o_wi = o_ki + IDX_DIM
    o_g = o_wi + IDX_HEADS
    parts = [w[:, :o_dt], w[:, o_q:o_ki], w[:, o_g:o_g + 2 * D_MODEL], w[:, o_ki:o_wi], w[:, o_dt:o_q], w[:, o_wi:o_g]]
    used = sum(p.shape[1] for p in parts)
    parts.append(jnp.zeros((w.shape[0], PROJ_COLS - used), w.dtype))
    w = jnp.concatenate(parts, axis=1).astype(MXU_DTYPE)
    return w.reshape(w.shape[0], PROJ_COLS // PROJ_TN, PROJ_TN).transpose(1, 0, 2)


def _small_lanes(v):
    return jnp.zeros((1, LANES), F32).at[0, SM_DT:SM_DT + SSM_HEADS].set(v.astype(F32))


def _layer_consts(conv_w, conv_b, dt_bias, a_log, d_skip, ssm_norm_w):
    head_of_lane = jnp.arange(D_INNER) // SSM_HEAD_DIM
    expand = (jnp.arange(LANES)[:, None] == (SM_DT + head_of_lane)[None, :]).astype(jnp.bfloat16)
    return dict(
        cwx=conv_w[:, :D_INNER], cbx=conv_b[None, :D_INNER], cwb=conv_w[:, D_INNER:], cbb=conv_b[None, D_INNER:],
        dtb=_small_lanes(dt_bias), aneg=_small_lanes(-jnp.exp(a_log.astype(F32))),
        dskip=jnp.repeat(d_skip.astype(F32), SSM_HEAD_DIM)[None, :], normw=ssm_norm_w.astype(F32)[None, :],
        expand=expand)


def kernel(x_prompt, x_sample, cache_k, cache_v, cache_kidx, state_ssm, state_conv, page_table, meta_tokens, norm1_w, w_in, conv_w, conv_b, dt_bias, a_log, d_skip, ssm_norm_w, w_ssm_proj, w_attn_proj, w_out, norm2_w, w_up, w_down, final_norm_w):
    b, seq, d = x_prompt.shape
    bd, ns = x_sample.shape[:2]
    depth = w_in.shape[0]
    assert d == D_MODEL and ns % 8 == 0
    past = page_table.shape[1] * cache_kidx.shape[2]
    lv = N_META + seq
    lp = -(-lv // Q_BLOCK) * Q_BLOCK
    rp = b * lp
    rs = bd * ns
    r = -(-(rp + rs) // LANES) * LANES
    topk_p = min(TOPK_MAX, seq // 4)
    topk_s = min(TOPK_MAX, (past + ns) // 4)

    hp = jnp.concatenate([jnp.broadcast_to(meta_tokens[None].astype(F32), (b, N_META, d)), x_prompt,
                          jnp.zeros((b, lp - lv, d), F32)], axis=1).reshape(rp, d)
    h = jnp.concatenate([hp, x_sample.reshape(rs, d), jnp.zeros((r - rp - rs, d), F32)], axis=0)

    ck_t, cv_t, cki_t = _key_minor(cache_k), _key_minor(cache_v), _key_minor(cache_kidx)
    zero_state = jnp.zeros((b, SSM_GROUPS, SSM_STATE, D_INNER // SSM_GROUPS), F32)
    zero_prefix = jnp.zeros((b, CONV_W - 1, D_INNER + BC_DIM), F32)
    outs = [[] for _ in range(10)]
    for l in range(depth):
        lw = _layer_consts(conv_w[l], conv_b[l], dt_bias[l], a_log[l], d_skip[l], ssm_norm_w[l])
        proj = _in_proj(h, norm1_w[l], _regroup_w_in(w_in[l]))

        yp, hp_t = _ssd(proj, 0, b, SSD_CHUNK, lp // SSD_CHUNK, lv, zero_state, zero_prefix, lw, MXU_DTYPE)
        ysm, hs_t = _ssd(proj, rp, bd, ns, 1, ns, _state_to_t(state_ssm[l]), state_conv[l], lw, F32)
        ap = _prompt_attention(proj, b, lp, topk_p)
        asm = _sample_attention(proj, rp, page_table, ck_t, cv_t, cki_t, l, ns, topk_s)
        tail = r - rp - rs
        if tail:
            ysm = jnp.concatenate([ysm, jnp.zeros((tail, D_INNER), F32)], axis=0)
            asm = jnp.concatenate([asm, jnp.zeros((tail, ATT_DIM), F32)], axis=0)

        h = _merge(h, rp, yp, ysm, ap, asm, proj, w_ssm_proj[l].astype(MXU_DTYPE), w_attn_proj[l].astype(MXU_DTYPE),
                   w_out[l].astype(MXU_DTYPE))
        h = _mlp(h, norm2_w[l], w_up[l].astype(MXU_DTYPE), w_down[l].astype(MXU_DTYPE),
                 final_norm_w if l == depth - 1 else None)

        def pcols(c0, w):
            return proj[:rp, c0:c0 + w].reshape(b, lp, w)[:, :lv]

        def scols(c0, w):
            return proj[rp:rp + rs, c0:c0 + w].reshape(bd, ns, w)

        xbc_w = D_INNER + BC_DIM
        p_conv = jnp.stack([proj[s * lp + lv - (CONV_W - 1):s * lp + lv, C_XS:C_XS + xbc_w] for s in range(b)])
        s_conv = jnp.concatenate([state_conv[l], scols(C_XS, xbc_w)], axis=1)[:, -(CONV_W - 1):]
        for acc, t in zip(outs, (
                pcols(C_K, KV_DIM).reshape(b, lv, ATT_KV_HEADS, HEAD_DIM),
                pcols(C_V, KV_DIM).reshape(b, lv, ATT_KV_HEADS, HEAD_DIM),
                pcols(C_SM + SM_KI, IDX_DIM),
                _state_from_t(hp_t),
                p_conv,
                scols(C_K, KV_DIM).reshape(bd, ns, ATT_KV_HEADS, HEAD_DIM),
                scols(C_V, KV_DIM).reshape(bd, ns, ATT_KV_HEADS, HEAD_DIM),
                scols(C_SM + SM_KI, IDX_DIM),
                _state_from_t(hs_t),
                s_conv)):
            acc.append(t)

    y = h
    y_prompt = y[:rp].reshape(b, lp, d)[:, N_META:lv]
    y_sample = y[rp:rp + rs].reshape(bd, ns, d)
    return (y_prompt, y_sample) + tuple(jnp.stack(o) for o in outs)
```

```python
import functools
import math

import jax
import jax.numpy as jnp
from jax import lax
from jax.experimental import pallas as pl
from jax.experimental.pallas import tpu as pltpu

F32 = jnp.float32
MXU_DTYPE = jnp.bfloat16

N_META = 16
NORM_EPS = 1e-6
SSM_HEAD_DIM = 64
SSM_GROUPS = 4
SSM_STATE = 128
CONV_W = 4
ATT_HEADS = 16
ATT_KV_HEADS = 4
ATT_GROUP = ATT_HEADS // ATT_KV_HEADS
HEAD_DIM = 64
IDX_HEADS = 8
IDX_DIM = 64
TOPK_MAX = 256

LANES = 128
Q_BLOCK = 128
SSD_CHUNK = 128
VMEM_LIMIT = 56 << 20
MASK_BIAS = -1e30
INT_MIN = -(2 ** 31)
LOG2_E = 1.4426950408889634
V_ROWS = HEAD_DIM + 16

D_MODEL = 1024
D_INNER = 2 * D_MODEL
BC_DIM = 2 * SSM_GROUPS * SSM_STATE
SSM_HEADS = D_INNER // SSM_HEAD_DIM
ATT_DIM = ATT_HEADS * HEAD_DIM
KV_DIM = ATT_KV_HEADS * HEAD_DIM
QI_DIM = IDX_HEADS * IDX_DIM
C_Z = 0
C_XS = C_Z + D_INNER
C_BC = C_XS + D_INNER
C_Q = C_BC + BC_DIM
C_K = C_Q + ATT_DIM
C_V = C_K + KV_DIM
C_QI = C_V + KV_DIM
C_GS = C_QI + QI_DIM
C_GA = C_GS + D_MODEL
C_SM = C_GA + D_MODEL
SM_KI = 0
SM_DT = IDX_DIM
SM_WI = IDX_DIM + SSM_HEADS
PROJ_TN = 512
PROJ_COLS = -(-(C_SM + LANES) // PROJ_TN) * PROJ_TN


def _divisor_tile(n, max_tile, align):
    best = None
    for t in range(align, min(n, max_tile) + 1, align):
        if n % t == 0:
            best = t
    assert best is not None, (n, max_tile, align)
    return best


def _params(*sem):
    return pltpu.CompilerParams(dimension_semantics=sem, vmem_limit_bytes=VMEM_LIMIT)


def _dot(a, b):
    return jnp.dot(a.astype(MXU_DTYPE), b.astype(MXU_DTYPE), preferred_element_type=F32)


def _dot_nt(a, b):
    return lax.dot_general(a.astype(MXU_DTYPE), b.astype(MXU_DTYPE), (((1,), (1,)), ((), ())),
                           preferred_element_type=F32)


def _bf16_pieces(x):
    hi = x.astype(jnp.bfloat16)
    r = x - hi.astype(F32)
    mid = r.astype(jnp.bfloat16)
    return [hi, mid, (r - mid.astype(F32)).astype(jnp.bfloat16)]


def _dot_exact(a, b):
    return jnp.dot(a, b, preferred_element_type=F32, precision=lax.Precision.HIGHEST)


def _sigmoid(x):
    return 0.5 * jnp.tanh(0.5 * x) + 0.5


def _row_rmsnorm(x, w):
    return x * lax.rsqrt(jnp.mean(x * x, axis=-1, keepdims=True) + NORM_EPS) * w


def _in_proj_kernel(x_ref, nw_ref, w_ref, o_ref, u_sc):
    j = pl.program_id(1)

    @pl.when(j == 0)
    def _():
        u_sc[...] = _row_rmsnorm(x_ref[...], nw_ref[...]).astype(u_sc.dtype)

    o_ref[...] = jnp.dot(u_sc[...], w_ref[j], preferred_element_type=F32)


def _in_proj(h, norm_w, w):
    r, k = h.shape
    nt = w.shape[0]
    tm = _divisor_tile(r, 1100, 16)
    return pl.pallas_call(
        _in_proj_kernel,
        grid=(r // tm, nt),
        in_specs=[pl.BlockSpec((tm, k), lambda i, j: (i, 0)), pl.BlockSpec((1, k), lambda i, j: (0, 0)),
                  pl.BlockSpec(w.shape, lambda i, j: (0, 0, 0), pipeline_mode=pl.Buffered(1))],
        out_specs=pl.BlockSpec((tm, PROJ_TN), lambda i, j: (i, j)),
        out_shape=jax.ShapeDtypeStruct((r, nt * PROJ_TN), F32),
        scratch_shapes=[pltpu.VMEM((tm, k), MXU_DTYPE)],
        compiler_params=_params("parallel", "arbitrary"),
        name="in_proj",
    )(h, norm_w.reshape(1, k), w)


def _merge_kernel(h_ref, ysp_ref, yss_ref, yap_ref, yas_ref, gs_ref, ga_ref, wsp_ref, wap_ref, wo_ref, o_ref, *, n_p):
    prompt = pl.program_id(0) < n_p
    ys = jnp.where(prompt, ysp_ref[...], yss_ref[...].astype(MXU_DTYPE))
    ya = jnp.where(prompt, yap_ref[...], yas_ref[...].astype(MXU_DTYPE))
    a = jnp.dot(ys, wsp_ref[...], preferred_element_type=F32)
    b = jnp.dot(ya, wap_ref[...], preferred_element_type=F32)
    m = _sigmoid(gs_ref[...]) * a + _sigmoid(ga_ref[...]) * b
    o_ref[...] = h_ref[...] + _dot(m, wo_ref[...])


def _merge(h, rp, ys_p, ys_s, ya_p, ya_s, proj, w_sp, w_ap, w_o):
    r, d = h.shape
    tm = _divisor_tile(math.gcd(rp, r - rp), 256, 16)
    n_p = rp // tm
    const = lambda i: (0, 0)
    pblk = lambda w: pl.BlockSpec((tm, w), lambda i: (jnp.minimum(i, n_p - 1), 0))
    sblk = lambda w: pl.BlockSpec((tm, w), lambda i: (jnp.maximum(i - n_p, 0), 0))
    return pl.pallas_call(
        functools.partial(_merge_kernel, n_p=n_p),
        grid=(r // tm,),
        in_specs=[pl.BlockSpec((tm, d), lambda i: (i, 0)),
                  pblk(D_INNER), sblk(D_INNER), pblk(ATT_DIM), sblk(ATT_DIM),
                  pl.BlockSpec((tm, d), lambda i: (i, C_GS // D_MODEL)),
                  pl.BlockSpec((tm, d), lambda i: (i, C_GA // D_MODEL)),
                  pl.BlockSpec(w_sp.shape, const), pl.BlockSpec(w_ap.shape, const), pl.BlockSpec(w_o.shape, const)],
        out_specs=pl.BlockSpec((tm, d), lambda i: (i, 0)),
        out_shape=jax.ShapeDtypeStruct((r, d), F32),
        compiler_params=_params("parallel"),
        name="merge",
    )(h, ys_p, ys_s, ya_p, ya_s, proj, proj, w_sp, w_ap, w_o)


def _mlp_kernel(h_ref, nw_ref, wu_ref, wd_ref, *rest):
    x = h_ref[...]
    a = _dot(_row_rmsnorm(x, nw_ref[...]), wu_ref[...])
    a = jnp.square(jnp.maximum(a, 0.0))
    y = x + _dot(a, wd_ref[...])
    if len(rest) == 2:
        fw_ref, o_ref = rest
        y = _row_rmsnorm(y, fw_ref[...])
    else:
        o_ref, = rest
    o_ref[...] = y


def _mlp(h, norm_w, w_up, w_down, final_norm_w=None):
    r, d = h.shape
    tm = _divisor_tile(r, 256, 16)
    const = lambda i: (0, 0)
    extra = [] if final_norm_w is None else [final_norm_w.reshape(1, d)]
    return pl.pallas_call(
        _mlp_kernel,
        grid=(r // tm,),
        in_specs=[pl.BlockSpec((tm, d), lambda i: (i, 0)), pl.BlockSpec((1, d), const),
                  pl.BlockSpec(w_up.shape, const), pl.BlockSpec(w_down.shape, const)]
                 + [pl.BlockSpec((1, d), const)] * len(extra),
        out_specs=pl.BlockSpec((tm, d), lambda i: (i, 0)),
        out_shape=jax.ShapeDtypeStruct((r, d), F32),
        compiler_params=_params("parallel"),
        name="mlp",
    )(h, norm_w.reshape(1, d), w_up, w_down, *extra)


def _softplus(x):
    return jnp.maximum(x, 0.0) + jnp.log1p(jnp.exp(-jnp.abs(x)))


def _conv_silu(xp_sc, x, w_ref, b_ref, t):
    xp_sc[pl.ds(8, t), :] = x
    acc = b_ref[...] + x * w_ref[CONV_W - 1:CONV_W, :]
    for k in range(CONV_W - 1):
        acc = acc + xp_sc[pl.ds(5 + k, t), :] * w_ref[k:k + 1, :]
    xp_sc[pl.ds(5, CONV_W - 1), :] = xp_sc[pl.ds(t + 5, CONV_W - 1), :]
    return acc * _sigmoid(acc)


def _ssd_kernel(z_ref, xs_ref, bc_ref, sm_ref, h0_ref, pxs_ref, pbc_ref,
                cwx_ref, cbx_ref, cwb_ref, cbb_ref, dtb_ref, aneg_ref, dskip_ref, normw_ref, expand_ref,
                y_ref, hout_ref, state_sc, xpx_sc, xpb_sc, *, t_in, valid_len):
    t = SSD_CHUNK
    c = pl.program_id(1)

    @pl.when(c == 0)
    def _():
        state_sc[...] = h0_ref[...]
        xpx_sc[pl.ds(5, CONV_W - 1), :] = pxs_ref[...]
        xpb_sc[pl.ds(5, CONV_W - 1), :] = pbc_ref[...]

    def rows(ref):
        x = ref[...]
        if t_in < t:
            x = jnp.concatenate([x, jnp.zeros((t - t_in, x.shape[1]), x.dtype)], axis=0)
        return x

    xs = _conv_silu(xpx_sc, rows(xs_ref), cwx_ref, cbx_ref, t)
    bcm = _conv_silu(xpb_sc, rows(bc_ref), cwb_ref, cbb_ref, t)
    z = rows(z_ref)

    row = lax.broadcasted_iota(jnp.int32, (t, LANES), 0)
    n_valid = jnp.minimum(valid_len - c * t, t_in)
    dt = jnp.where(row < n_valid, _softplus(rows(sm_ref) + dtb_ref[...]), 0.0)
    a = dt * aneg_ref[...]
    ri = lax.broadcasted_iota(jnp.int32, (t, t), 0)
    ci = lax.broadcasted_iota(jnp.int32, (t, t), 1)
    causal = ri >= ci
    acum = _dot_exact(jnp.where(causal, 1.0, 0.0), a)
    acum_t = acum.T
    ex = jnp.dot(jnp.concatenate(_bf16_pieces(dt) + _bf16_pieces(acum), axis=0), expand_ref[...],
                 preferred_element_type=F32)
    dtx = ex[:t] + ex[t:2 * t] + ex[2 * t:3 * t]
    acx = ex[3 * t:4 * t] + ex[4 * t:5 * t] + ex[5 * t:]
    last = acx[t - 1:t, :]
    xdt = xs * dtx
    xw = xdt * jnp.exp(last - acx)
    eacx = jnp.exp(acx)
    elast = jnp.exp(last)

    gw = D_INNER // SSM_GROUPS
    hpg = SSM_HEADS // SSM_GROUPS
    ys = []
    for g in range(SSM_GROUPS):
        bg = bcm[:, g * SSM_STATE:(g + 1) * SSM_STATE]
        cg = bcm[:, (SSM_GROUPS + g) * SSM_STATE:(SSM_GROUPS + g + 1) * SSM_STATE]
        cb = _dot_nt(cg, bg)
        h_t = state_sc[g]
        yg = [_dot(cg, h_t) * eacx[:, g * gw:(g + 1) * gw]]
        intra = []
        for e in range(hpg):
            hd = g * hpg + e
            seg = acum[:, SM_DT + hd:SM_DT + hd + 1] - acum_t[SM_DT + hd:SM_DT + hd + 1, :]
            m = cb * jnp.exp(jnp.where(causal, seg, -jnp.inf))
            intra.append(_dot(m, xdt[:, hd * SSM_HEAD_DIM:(hd + 1) * SSM_HEAD_DIM]))
        ys.append(yg[0] + jnp.concatenate(intra, axis=1))
        state_sc[g] = h_t * elast[:, g * gw:(g + 1) * gw] + _dot(bg.T, xw[:, g * gw:(g + 1) * gw])
    y = jnp.concatenate(ys, axis=1) + xs * dskip_ref[...]
    y = y * (z * _sigmoid(z))
    outs = []
    for g in range(SSM_GROUPS):
        yg = y[:, g * gw:(g + 1) * gw]
        outs.append(yg * lax.rsqrt(jnp.mean(yg * yg, axis=-1, keepdims=True) + NORM_EPS))
    y = jnp.concatenate(outs, axis=1) * normw_ref[...]
    y_ref[...] = y[:t_in].astype(y_ref.dtype)

    @pl.when(c == pl.num_programs(1) - 1)
    def _():
        hout_ref[...] = state_sc[...]


def _ssd(proj, row0, n_seq, t_in, n_chunks, valid_len, h0_t, prefix, lw, out_dtype):
    assert row0 % t_in == 0
    rb0 = row0 // t_in
    rowblk = lambda w, col: pl.BlockSpec((t_in, w), lambda s, c: (rb0 + s * n_chunks + c, col // w))
    per_seq = lambda shape: pl.BlockSpec((None,) + shape, lambda s, c: (s,) + (0,) * len(shape))
    const = lambda arr: pl.BlockSpec(arr.shape, lambda s, c: (0,) * arr.ndim)
    gw = D_INNER // SSM_GROUPS
    pxs, pbc = prefix[:, :, :D_INNER], prefix[:, :, D_INNER:]
    consts = (lw["cwx"], lw["cbx"], lw["cwb"], lw["cbb"], lw["dtb"], lw["aneg"], lw["dskip"], lw["normw"], lw["expand"])
    y, h_t = pl.pallas_call(
        functools.partial(_ssd_kernel, t_in=t_in, valid_len=valid_len),
        grid=(n_seq, n_chunks),
        in_specs=[rowblk(D_INNER, C_Z), rowblk(D_INNER, C_XS), rowblk(BC_DIM, C_BC), rowblk(LANES, C_SM),
                  per_seq((SSM_GROUPS, SSM_STATE, gw)), per_seq((CONV_W - 1, D_INNER)), per_seq((CONV_W - 1, BC_DIM))]
                 + [const(a) for a in consts],
        out_specs=[pl.BlockSpec((t_in, D_INNER), lambda s, c: (s * n_chunks + c, 0)),
                   per_seq((SSM_GROUPS, SSM_STATE, gw))],
        out_shape=[jax.ShapeDtypeStruct((n_seq * n_chunks * t_in, D_INNER), out_dtype),
                   jax.ShapeDtypeStruct((n_seq, SSM_GROUPS, SSM_STATE, gw), F32)],
        scratch_shapes=[pltpu.VMEM((SSM_GROUPS, SSM_STATE, gw), F32),
                        pltpu.VMEM((SSD_CHUNK + 8, D_INNER), F32),
                        pltpu.VMEM((SSD_CHUNK + 8, BC_DIM), F32)],
        compiler_params=_params("parallel", "arbitrary"),
        name="ssd",
    )(proj, proj, proj, proj, h0_t, pxs, pbc, *consts)
    return y, h_t


def _state_to_t(h):
    n = h.shape[0]
    hpg = SSM_HEADS // SSM_GROUPS
    return h.reshape(n, SSM_GROUPS, hpg, SSM_HEAD_DIM, SSM_STATE).transpose(0, 1, 4, 2, 3).reshape(
        n, SSM_GROUPS, SSM_STATE, hpg * SSM_HEAD_DIM)


def _state_from_t(h_t):
    n = h_t.shape[0]
    hpg = SSM_HEADS // SSM_GROUPS
    return h_t.reshape(n, SSM_GROUPS, SSM_STATE, hpg, SSM_HEAD_DIM).transpose(0, 1, 3, 4, 2).reshape(
        n, SSM_HEADS, SSM_HEAD_DIM, SSM_STATE)


def _sort_key(x):
    bits = lax.bitcast_convert_type(x, jnp.int32)
    bits = jnp.where(bits == INT_MIN, 0, bits)
    return bits ^ ((bits >> 31) & 0x7FFFFFFF)


def _fold_rows(x, group=8):
    x = x.reshape(x.shape[0] // group, group, x.shape[1])
    while x.shape[0] > 1:
        half, odd = divmod(x.shape[0], 2)
        folded = x[:half] + x[half:2 * half]
        if odd:
            folded = jnp.concatenate([folded[:1] + x[2 * half:], folded[1:]], axis=0) if half > 1 else folded + x[2 * half:]
        x = folded
    return x[0]


def _count_t(keys_sc, n_chunks, pred):
    def body(c, cnt):
        return cnt + _fold_rows(jnp.where(pred(keys_sc[c]), 1.0, 0.0))

    cnt = lax.fori_loop(0, n_chunks, body, jnp.zeros((8, LANES), F32))
    return jnp.sum(cnt, axis=0, keepdims=True)


HALF_MIN = -(2 ** 30)
SEARCH_UNROLLS = tuple(range(1, 17))


def _search_bounds(nkc):
    return sorted({min(b, nkc) for b in SEARCH_UNROLLS} | {nkc})


def _search_chunks(n_chunks, nkc):
    bounds = _search_bounds(nkc)
    n = jnp.int32(bounds[-1])
    for b in reversed(bounds[:-1]):
        n = jnp.where(n_chunks <= b, b, n)
    return n


def _kth_halved_t(kh_sc, n, k):
    total = n * kh_sc.shape[1]

    def count_ge(cand):
        below = None
        for c in range(n):
            neg = _fold_rows((kh_sc[c] - cand) >> 31)
            below = neg if below is None else below + neg
        return total + jnp.sum(below.astype(F32), axis=0, keepdims=True)

    def bit_body(i, u):
        cand = u | lax.shift_left(jnp.int32(1), 30 - i)
        return jnp.where(count_ge(cand + HALF_MIN) >= k, cand, u)

    return lax.fori_loop(0, 31, bit_body, jnp.zeros((1, LANES), jnp.int32)) + HALF_MIN


def _kth_key_t(keys_sc, kh_sc, n_chunks, k):
    bounds = _search_bounds(kh_sc.shape[0])

    def build(i):
        if i == len(bounds) - 1:
            return lambda: _kth_halved_t(kh_sc, bounds[i], k)
        return lambda: lax.cond(n_chunks <= bounds[i], lambda: _kth_halved_t(kh_sc, bounds[i], k), build(i + 1))

    odd = build(0)() * 2 + 1
    return jnp.where(_count_t(keys_sc, n_chunks, lambda key: key >= odd) >= k, odd, odd - 1)


def _select_t(keys_sc, kh_sc, bias_sc, n_chunks, k):
    kc = keys_sc.shape[1]
    thr = _kth_key_t(keys_sc, kh_sc, n_chunks, k)
    n_ge = _count_t(keys_sc, n_chunks, lambda key: key >= thr)
    surplus = jnp.max(jnp.where((n_ge > k) & (thr != INT_MIN), 1.0, 0.0))

    @pl.when(surplus == 0.0)
    def _():
        floor = jnp.maximum(thr, INT_MIN + 1)

        def body(c, carry):
            bias_sc[c] = jnp.where(keys_sc[c] >= floor, 0.0, MASK_BIAS)
            return carry

        lax.fori_loop(0, n_chunks, body, 0)

    @pl.when(surplus != 0.0)
    def _():
        need = k - _count_t(keys_sc, n_chunks, lambda key: key > thr)
        ri = lax.broadcasted_iota(jnp.int32, (kc, kc), 0)
        ci = lax.broadcasted_iota(jnp.int32, (kc, kc), 1)
        lower = jnp.where(ci <= ri, 1.0, 0.0).astype(jnp.bfloat16)

        def body(c, run):
            key = keys_sc[c]
            eqf = jnp.where((key == thr) & (key != INT_MIN), 1.0, 0.0)
            rank = jnp.dot(lower, eqf.astype(jnp.bfloat16), preferred_element_type=F32) + run
            sel = (key > thr) | ((eqf > 0.0) & (rank <= need))
            bias_sc[c] = jnp.where(sel, 0.0, MASK_BIAS)
            return run + jnp.sum(_fold_rows(eqf), axis=0, keepdims=True)

        lax.fori_loop(0, n_chunks, body, jnp.zeros((1, LANES), F32))


def _select_s(keys_sc, kh_sc, bias_ref, k):
    n_chunks, rows, width = keys_sc.shape

    def count(pred):
        return jnp.sum(jnp.sum(jnp.where(pred(keys_sc[...]), 1.0, 0.0), axis=0), axis=1, keepdims=True)

    def bit_body(i, u):
        cand = u | lax.shift_left(jnp.int32(1), 30 - i)
        below = jnp.sum((kh_sc[...] - (cand + HALF_MIN)[None]) >> 31, axis=0).astype(F32)
        return jnp.where(n_chunks * width + jnp.sum(below, axis=1, keepdims=True) >= k, cand, u)

    odd = (lax.fori_loop(0, 31, bit_body, jnp.zeros((rows, 1), jnp.int32)) + HALF_MIN) * 2 + 1
    thr = jnp.where(count(lambda key: key >= odd[None]) >= k, odd, odd - 1)
    n_ge = count(lambda key: key >= thr[None])
    surplus = jnp.max(jnp.where((n_ge > k) & (thr != INT_MIN), 1.0, 0.0))

    @pl.when(surplus == 0.0)
    def _():
        floor = jnp.maximum(thr, INT_MIN + 1)
        bias_ref[...] = jnp.where(keys_sc[...] >= floor[None], 0.0, MASK_BIAS)

    @pl.when(surplus != 0.0)
    def _():
        need = k - count(lambda key: key > thr[None])
        ri = lax.broadcasted_iota(jnp.int32, (width, width), 0)
        ci = lax.broadcasted_iota(jnp.int32, (width, width), 1)
        upper = jnp.where(ri <= ci, 1.0, 0.0).astype(jnp.bfloat16)

        def body(c, run):
            key = keys_sc[c]
            eqf = jnp.where((key == thr) & (key != INT_MIN), 1.0, 0.0)
            rank = jnp.dot(eqf.astype(jnp.bfloat16), upper, preferred_element_type=F32) + run
            sel = (key > thr) | ((eqf > 0.0) & (rank <= need))
            bias_ref[c] = jnp.where(sel, 0.0, MASK_BIAS)
            return run + jnp.sum(eqf, axis=1, keepdims=True)

        lax.fori_loop(0, n_chunks, body, jnp.zeros((rows, 1), F32))


def _pattn_kernel(qi_ref, smq_ref, q_ref, smk_ref, k_ref, v_ref, o_ref,
                  kb_sc, kib_sc, vt_sc, keys_sc, kh_sc, bias_sc, qz_sc, m_sc, acc_sc, *, kc, lp, topk):
    j = pl.program_id(1)
    nq = Q_BLOCK
    nkc = kb_sc.shape[0]

    @pl.when(j == 0)
    def _():
        for c in range(nkc):
            r0 = c * kc
            n = min(kc, lp - r0)

            def chunk(ref):
                x = ref[r0:r0 + n, :]
                return x if n == kc else jnp.concatenate([x, jnp.zeros((kc - n, x.shape[1]), x.dtype)], axis=0)

            kb_sc[c] = chunk(k_ref).astype(MXU_DTYPE)
            kib_sc[c] = chunk(smk_ref).astype(MXU_DTYPE)
            v_t = chunk(v_ref).T
            pad = jnp.concatenate([jnp.ones((1, kc), F32), jnp.zeros((V_ROWS - HEAD_DIM - 1, kc), F32)], axis=0)
            vt_sc[c] = jnp.concatenate([x for hk in range(ATT_KV_HEADS)
                                        for x in (v_t[hk * HEAD_DIM:(hk + 1) * HEAD_DIM], pad)], axis=0).astype(MXU_DTYPE)

    n_chunks = ((j + 1) * nq - 1) // kc + 1

    qi_t = (qi_ref[...] * IDX_DIM ** -0.5).T
    qiz = jnp.concatenate([qi_t[h * IDX_DIM:(h + 1) * IDX_DIM] for h in range(IDX_HEADS)], axis=1)
    qiz = jnp.concatenate([qiz, jnp.zeros((LANES - IDX_DIM, IDX_HEADS * nq), F32)], axis=0).astype(MXU_DTYPE)
    smq_t = smq_ref[...].T
    wi_row = jnp.concatenate([smq_t[SM_WI + h:SM_WI + h + 1] for h in range(IDX_HEADS)], axis=1) * IDX_HEADS ** -0.5
    kpos = lax.broadcasted_iota(jnp.int32, (kc, nq), 0)
    qpos = j * nq + lax.broadcasted_iota(jnp.int32, (kc, nq), 1)

    def score_body(c, carry):
        s = jnp.maximum(jnp.dot(kib_sc[c], qiz, preferred_element_type=F32), 0.0) * wi_row
        acc = s[:, :nq]
        for h in range(1, IDX_HEADS):
            acc = acc + s[:, h * nq:(h + 1) * nq]
        key = jnp.where(c * kc + kpos <= qpos, _sort_key(acc), INT_MIN)
        keys_sc[c] = key
        kh_sc[c] = key >> 1
        return carry

    lax.fori_loop(0, n_chunks, score_body, 0)

    def blank_body(c, carry):
        kh_sc[c] = jnp.full((kc, nq), HALF_MIN, jnp.int32)
        return carry

    lax.fori_loop(n_chunks, _search_chunks(n_chunks, nkc), blank_body, 0)
    _select_t(keys_sc, kh_sc, bias_sc, n_chunks, topk)

    q_t = (q_ref[...] * (HEAD_DIM ** -0.5 * LOG2_E)).T
    gw = ATT_GROUP * nq
    zero = jnp.zeros((HEAD_DIM, gw), F32)
    for hk in range(ATT_KV_HEADS):
        own = jnp.concatenate([q_t[(hk * ATT_GROUP + g) * HEAD_DIM:(hk * ATT_GROUP + g + 1) * HEAD_DIM]
                               for g in range(ATT_GROUP)], axis=1)
        qz_sc[hk] = jnp.concatenate([own if i == hk else zero for i in range(ATT_KV_HEADS)],
                                    axis=0).astype(MXU_DTYPE)
    m_sc[...] = jnp.full(m_sc.shape, MASK_BIAS, F32)
    acc_sc[...] = jnp.zeros(acc_sc.shape, F32)

    def attend(chunks):
        stages = [(c, hk) for c in chunks for hk in range(ATT_KV_HEADS)]
        loaded = {}

        def operands(c):
            if id(c) not in loaded:
                loaded[id(c)] = (kb_sc[c], vt_sc[c], jnp.concatenate([bias_sc[c]] * ATT_GROUP, axis=1))
            return loaded[id(c)]

        def qk(t):
            c, hk = stages[t]
            kch, _, b4 = operands(c)
            return jnp.dot(kch, qz_sc[hk], preferred_element_type=F32) + b4

        def soft(t, s):
            hk = stages[t][1]
            m_old = m_sc[hk]
            m_new = jnp.maximum(m_old, jnp.max(s, axis=0, keepdims=True))
            m_sc[hk] = m_new
            return jnp.exp2(m_old - m_new), jnp.exp2(s - m_new).astype(MXU_DTYPE)

        def pv(t, alpha, p):
            c, hk = stages[t]
            vt = operands(c)[1]
            acc_sc[hk] = alpha * acc_sc[hk] + jnp.dot(vt[hk * V_ROWS:(hk + 1) * V_ROWS], p,
                                                      preferred_element_type=F32)

        n = len(stages)
        s = {0: qk(0), 1: qk(1)}
        ap = {}
        for t in range(n):
            ap[t] = soft(t, s.pop(t))
            if t + 2 < n:
                s[t + 2] = qk(t + 2)
            if t >= 1:
                pv(t - 1, *ap.pop(t - 1))
        pv(n - 1, *ap.pop(n - 1))

    def pair_body(i, carry):
        attend([2 * i, 2 * i + 1])
        return carry

    lax.fori_loop(0, n_chunks // 2, pair_body, 0)

    @pl.when(n_chunks % 2 == 1)
    def _():
        attend([n_chunks - 1])

    heads = []
    for hk in range(ATT_KV_HEADS):
        acc = acc_sc[hk]
        o = acc[:HEAD_DIM] / acc[HEAD_DIM:HEAD_DIM + 1]
        heads += [o[:, g * nq:(g + 1) * nq] for g in range(ATT_GROUP)]
    o_ref[...] = jnp.concatenate(heads, axis=0).T.astype(o_ref.dtype)


def _prompt_attention(proj, n_seq, lp, topk):
    nqb = lp // Q_BLOCK
    kc = 512
    nkc = -(-lp // kc)
    qblk = lambda w, col: pl.BlockSpec((Q_BLOCK, w), lambda b, j: (b * nqb + j, col // w))
    seqblk = lambda w, col: pl.BlockSpec((lp, w), lambda b, j: (b, col // w))
    gw = ATT_GROUP * Q_BLOCK
    return pl.pallas_call(
        functools.partial(_pattn_kernel, kc=kc, lp=lp, topk=topk),
        grid=(n_seq, nqb),
        in_specs=[qblk(QI_DIM, C_QI), qblk(LANES, C_SM), qblk(ATT_DIM, C_Q),
                  seqblk(LANES, C_SM), seqblk(KV_DIM, C_K), seqblk(KV_DIM, C_V)],
        out_specs=pl.BlockSpec((Q_BLOCK, ATT_DIM), lambda b, j: (b * nqb + j, 0)),
        out_shape=jax.ShapeDtypeStruct((n_seq * lp, ATT_DIM), MXU_DTYPE),
        scratch_shapes=[pltpu.VMEM((nkc, kc, KV_DIM), MXU_DTYPE), pltpu.VMEM((nkc, kc, LANES), MXU_DTYPE),
                        pltpu.VMEM((nkc, ATT_KV_HEADS * V_ROWS, kc), MXU_DTYPE),
                        pltpu.VMEM((nkc, kc, Q_BLOCK), jnp.int32), pltpu.VMEM((nkc, kc, Q_BLOCK), jnp.int32),
                        pltpu.VMEM((nkc, kc, Q_BLOCK), F32),
                        pltpu.VMEM((ATT_KV_HEADS, KV_DIM, gw), MXU_DTYPE),
                        pltpu.VMEM((ATT_KV_HEADS, 1, gw), F32),
                        pltpu.VMEM((ATT_KV_HEADS, V_ROWS, gw), F32)],
        compiler_params=_params("parallel", "arbitrary"),
        name="prompt_attention",
    )(proj, proj, proj, proj, proj, proj)


def _pad_rows(x, n):
    return jnp.concatenate([x, jnp.zeros((n - x.shape[0], x.shape[1]), x.dtype)], axis=0)


def _sselect_kernel(pt_ref, qi_ref, sm_ref, *rest, pages_per_step, n_pages, page, ns, topk):
    page_refs = rest[:pages_per_step]
    bias_ref, keys_sc, kh_sc = rest[pages_per_step:]
    g = pl.program_id(1)
    qi = (qi_ref[...] * IDX_DIM ** -0.5).astype(MXU_DTYPE)
    qs = jnp.concatenate([qi[:, h * IDX_DIM:(h + 1) * IDX_DIM] for h in range(IDX_HEADS)], axis=0)
    sm = sm_ref[...]
    wi = sm[:, SM_WI:SM_WI + IDX_HEADS] * IDX_HEADS ** -0.5

    def weigh(s):
        s = jnp.maximum(s, 0.0)
        acc = s[:ns] * wi[:, 0:1]
        for h in range(1, IDX_HEADS):
            acc = acc + s[h * ns:(h + 1) * ns] * wi[:, h:h + 1]
        return _sort_key(acc)

    keys = weigh(_dot(qs, jnp.concatenate([r[...] for r in page_refs], axis=1)))
    for i in range(pages_per_step):
        keys_sc[g * pages_per_step + i] = keys[:, i * page:(i + 1) * page]
        kh_sc[g * pages_per_step + i] = keys[:, i * page:(i + 1) * page] >> 1

    @pl.when(g == pl.num_programs(1) - 1)
    def _():
        qrow = lax.broadcasted_iota(jnp.int32, (ns, page), 0)
        lane = lax.broadcasted_iota(jnp.int32, (ns, page), 1)
        new = weigh(_dot_nt(qs, _pad_rows(sm[:, SM_KI:SM_KI + IDX_DIM], page)))
        new = jnp.where(lane <= qrow, new, INT_MIN)
        keys_sc[n_pages] = new
        kh_sc[n_pages] = new >> 1
        _select_s(keys_sc, kh_sc, bias_ref, topk)


def _sattn_kernel(pt_ref, q_ref, kn_ref, vn_ref, bias_ref, *rest, pages_per_step, n_pages, page, ns):
    k_refs = rest[:pages_per_step]
    v_refs = rest[pages_per_step:2 * pages_per_step]
    o_ref, qbd_sc, m_sc, l_sc, acc_sc = rest[2 * pages_per_step:]
    g = pl.program_id(1)

    @pl.when(g == 0)
    def _():
        q = q_ref[...] * HEAD_DIM ** -0.5
        rows = []
        for h in range(ATT_HEADS):
            hk = h // ATT_GROUP
            parts = [jnp.zeros((ns, HEAD_DIM), F32)] * ATT_KV_HEADS
            parts[hk] = q[:, h * HEAD_DIM:(h + 1) * HEAD_DIM]
            rows.append(jnp.concatenate(parts, axis=1))
        qbd_sc[...] = jnp.concatenate(rows, axis=0).astype(MXU_DTYPE)
        m_sc[...] = jnp.full(m_sc.shape, MASK_BIAS, F32)
        l_sc[...] = jnp.zeros(l_sc.shape, F32)
        acc_sc[...] = jnp.zeros(acc_sc.shape, F32)

    def attend(s, b, pv):
        s = s + jnp.concatenate([b] * ATT_HEADS, axis=0)
        m_old = m_sc[...]
        m_new = jnp.maximum(m_old, jnp.max(s, axis=1, keepdims=True))
        alpha = jnp.exp(m_old - m_new)
        p = jnp.exp(s - m_new)
        l_sc[...] = alpha * l_sc[...] + jnp.sum(p, axis=1, keepdims=True)
        acc_sc[...] = alpha * acc_sc[...] + pv(p)
        m_sc[...] = m_new

    kcat = jnp.concatenate([r[...] for r in k_refs], axis=1)
    vcat = jnp.concatenate([r[...] for r in v_refs], axis=1)
    bcat = jnp.concatenate([bias_ref[g * pages_per_step + i] for i in range(pages_per_step)], axis=1)
    attend(_dot(qbd_sc[...], kcat), bcat, lambda p: _dot_nt(p, vcat))

    @pl.when(g == pl.num_programs(1) - 1)
    def _():
        vn = _pad_rows(vn_ref[...], page)
        attend(_dot_nt(qbd_sc[...], _pad_rows(kn_ref[...], page)), bias_ref[n_pages], lambda p: _dot(p, vn))
        o = acc_sc[...] / l_sc[...]
        o_ref[...] = jnp.concatenate(
            [o[h * ns:(h + 1) * ns, (h // ATT_GROUP) * HEAD_DIM:(h // ATT_GROUP + 1) * HEAD_DIM]
             for h in range(ATT_HEADS)], axis=1)


def _key_minor(cache):
    nd = cache.ndim
    t = jnp.transpose(cache, (0, 1) + tuple(range(3, nd)) + (2,))
    return t.reshape(t.shape[:2] + (-1, t.shape[-1]))


def _sample_attention(proj, row0, page_table, ck_t, cv_t, cki_t, layer, ns, topk):
    bd, n_pages = page_table.shape
    page = cki_t.shape[3]
    assert page == LANES and row0 % ns == 0
    pps = max(d for d in range(1, 17) if n_pages % d == 0)
    ng = n_pages // pps
    rb0 = row0 // ns
    rowblk = lambda w, col: pl.BlockSpec((ns, w), lambda s, g, pt: (rb0 + s, col // w))
    pageblk = lambda w, i: pl.BlockSpec((None, None, w, page), lambda s, g, pt: (layer, pt[s, g * pps + i], 0, 0))
    biasblk = pl.BlockSpec((None, n_pages + 1, ns, page), lambda s, g, pt: (s, 0, 0, 0))
    static = dict(pages_per_step=pps, n_pages=n_pages, page=page, ns=ns)

    bias = pl.pallas_call(
        functools.partial(_sselect_kernel, topk=topk, **static),
        grid_spec=pltpu.PrefetchScalarGridSpec(
            num_scalar_prefetch=1, grid=(bd, ng),
            in_specs=[rowblk(QI_DIM, C_QI), rowblk(LANES, C_SM)] + [pageblk(IDX_DIM, i) for i in range(pps)],
            out_specs=biasblk,
            scratch_shapes=[pltpu.VMEM((n_pages + 1, ns, page), jnp.int32)] * 2),
        out_shape=jax.ShapeDtypeStruct((bd, n_pages + 1, ns, page), F32),
        compiler_params=_params("parallel", "arbitrary"),
        name="sample_select",
    )(page_table, proj, proj, *([cki_t] * pps))

    rows = ATT_HEADS * ns
    return pl.pallas_call(
        functools.partial(_sattn_kernel, **static),
        grid_spec=pltpu.PrefetchScalarGridSpec(
            num_scalar_prefetch=1, grid=(bd, ng),
            in_specs=[rowblk(ATT_DIM, C_Q), rowblk(KV_DIM, C_K), rowblk(KV_DIM, C_V), biasblk]
                     + [pageblk(KV_DIM, i) for i in range(pps)] * 2,
            out_specs=pl.BlockSpec((ns, ATT_DIM), lambda s, g, pt: (s, 0)),
            scratch_shapes=[pltpu.VMEM((rows, KV_DIM), MXU_DTYPE), pltpu.VMEM((rows, 1), F32),
                            pltpu.VMEM((rows, 1), F32), pltpu.VMEM((rows, KV_DIM), F32)]),
        out_shape=jax.ShapeDtypeStruct((bd * ns, ATT_DIM), F32),
        compiler_params=_params("parallel", "arbitrary"),
        name="sample_attention",
    )(page_table, proj, proj, proj, bias, *([ck_t] * pps), *([cv_t] * pps))


def _regroup_w_in(w):
    o_dt = D_INNER + D_INNER + BC_DIM
    o_q = o_dt + SSM_HEADS
    o_ki = o_q + ATT_DIM + 2 * KV_DIM + QI_DIM
    o_wi = o_ki + IDX_DIM
    o_g = o_wi + IDX_HEADS
    parts = [w[:, :o_dt], w[:, o_q:o_ki], w[:, o_g:o_g + 2 * D_MODEL], w[:, o_ki:o_wi], w[:, o_dt:o_q], w[:, o_wi:o_g]]
    used = sum(p.shape[1] for p in parts)
    parts.append(jnp.zeros((w.shape[0], PROJ_COLS - used), w.dtype))
    w = jnp.concatenate(parts, axis=1).astype(MXU_DTYPE)
    return w.reshape(w.shape[0], PROJ_COLS // PROJ_TN, PROJ_TN).transpose(1, 0, 2)


def _small_lanes(v):
    return jnp.zeros((1, LANES), F32).at[0, SM_DT:SM_DT + SSM_HEADS].set(v.astype(F32))


def _layer_consts(conv_w, conv_b, dt_bias, a_log, d_skip, ssm_norm_w):
    head_of_lane = jnp.arange(D_INNER) // SSM_HEAD_DIM
    expand = (jnp.arange(LANES)[:, None] == (SM_DT + head_of_lane)[None, :]).astype(jnp.bfloat16)
    return dict(
        cwx=conv_w[:, :D_INNER], cbx=conv_b[None, :D_INNER], cwb=conv_w[:, D_INNER:], cbb=conv_b[None, D_INNER:],
        dtb=_small_lanes(dt_bias), aneg=_small_lanes(-jnp.exp(a_log.astype(F32))),
        dskip=jnp.repeat(d_skip.astype(F32), SSM_HEAD_DIM)[None, :], normw=ssm_norm_w.astype(F32)[None, :],
        expand=expand)


def kernel(x_prompt, x_sample, cache_k, cache_v, cache_kidx, state_ssm, state_conv, page_table, meta_tokens, norm1_w, w_in, conv_w, conv_b, dt_bias, a_log, d_skip, ssm_norm_w, w_ssm_proj, w_attn_proj, w_out, norm2_w, w_up, w_down, final_norm_w):
    b, seq, d = x_prompt.shape
    bd, ns = x_sample.shape[:2]
    depth = w_in.shape[0]
    assert d == D_MODEL and ns % 8 == 0
    past = page_table.shape[1] * cache_kidx.shape[2]
    lv = N_META + seq
    lp = -(-lv // Q_BLOCK) * Q_BLOCK
    rp = b * lp
    rs = bd * ns
    r = -(-(rp + rs) // LANES) * LANES
    topk_p = min(TOPK_MAX, seq // 4)
    topk_s = min(TOPK_MAX, (past + ns) // 4)

    hp = jnp.concatenate([jnp.broadcast_to(meta_tokens[None].astype(F32), (b, N_META, d)), x_prompt,
                          jnp.zeros((b, lp - lv, d), F32)], axis=1).reshape(rp, d)
    h = jnp.concatenate([hp, x_sample.reshape(rs, d), jnp.zeros((r - rp - rs, d), F32)], axis=0)

    ck_t, cv_t, cki_t = _key_minor(cache_k), _key_minor(cache_v), _key_minor(cache_kidx)
    zero_state = jnp.zeros((b, SSM_GROUPS, SSM_STATE, D_INNER // SSM_GROUPS), F32)
    zero_prefix = jnp.zeros((b, CONV_W - 1, D_INNER + BC_DIM), F32)
    outs = [[] for _ in range(10)]
    for l in range(depth):
        lw = _layer_consts(conv_w[l], conv_b[l], dt_bias[l], a_log[l], d_skip[l], ssm_norm_w[l])
        proj = _in_proj(h, norm1_w[l], _regroup_w_in(w_in[l]))

        yp, hp_t = _ssd(proj, 0, b, SSD_CHUNK, lp // SSD_CHUNK, lv, zero_state, zero_prefix, lw, MXU_DTYPE)
        ysm, hs_t = _ssd(proj, rp, bd, ns, 1, ns, _state_to_t(state_ssm[l]), state_conv[l], lw, F32)
        ap = _prompt_attention(proj, b, lp, topk_p)
        asm = _sample_attention(proj, rp, page_table, ck_t, cv_t, cki_t, l, ns, topk_s)
        tail = r - rp - rs
        if tail:
            ysm = jnp.concatenate([ysm, jnp.zeros((tail, D_INNER), F32)], axis=0)
            asm = jnp.concatenate([asm, jnp.zeros((tail, ATT_DIM), F32)], axis=0)

        h = _merge(h, rp, yp, ysm, ap, asm, proj, w_ssm_proj[l].astype(MXU_DTYPE), w_attn_proj[l].astype(MXU_DTYPE),
                   w_out[l].astype(MXU_DTYPE))
        h = _mlp(h, norm2_w[l], w_up[l].astype(MXU_DTYPE), w_down[l].astype(MXU_DTYPE),
                 final_norm_w if l == depth - 1 else None)

        def pcols(c0, w):
            return proj[:rp, c0:c0 + w].reshape(b, lp, w)[:, :lv]

        def scols(c0, w):
            return proj[rp:rp + rs, c0:c0 + w].reshape(bd, ns, w)

        xbc_w = D_INNER + BC_DIM
        p_conv = jnp.stack([proj[s * lp + lv - (CONV_W - 1):s * lp + lv, C_XS:C_XS + xbc_w] for s in range(b)])
        s_conv = jnp.concatenate([state_conv[l], scols(C_XS, xbc_w)], axis=1)[:, -(CONV_W - 1):]
        for acc, t in zip(outs, (
                pcols(C_K, KV_DIM).reshape(b, lv, ATT_KV_HEADS, HEAD_DIM),
                pcols(C_V, KV_DIM).reshape(b, lv, ATT_KV_HEADS, HEAD_DIM),
                pcols(C_SM + SM_KI, IDX_DIM),
                _state_from_t(hp_t),
                p_conv,
                scols(C_K, KV_DIM).reshape(bd, ns, ATT_KV_HEADS, HEAD_DIM),
                scols(C_V, KV_DIM).reshape(bd, ns, ATT_KV_HEADS, HEAD_DIM),
                scols(C_SM + SM_KI, IDX_DIM),
                _state_from_t(hs_t),
                s_conv)):
            acc.append(t)

    y = h
    y_prompt = jnp.stack([y[s * lp + N_META:s * lp + lv] for s in range(b)])
    y_sample = y[rp:rp + rs].reshape(bd, ns, d)
    return (y_prompt, y_sample) + tuple(jnp.stack(o) for o in outs)
```

```python
import functools
import math

import jax
import jax.numpy as jnp
from jax import lax
from jax.experimental import pallas as pl
from jax.experimental.pallas import tpu as pltpu

F32 = jnp.float32
MXU_DTYPE = jnp.bfloat16

N_META = 16
NORM_EPS = 1e-6
SSM_HEAD_DIM = 64
SSM_GROUPS = 4
SSM_STATE = 128
CONV_W = 4
ATT_HEADS = 16
ATT_KV_HEADS = 4
ATT_GROUP = ATT_HEADS // ATT_KV_HEADS
HEAD_DIM = 64
IDX_HEADS = 8
IDX_DIM = 64
TOPK_MAX = 256

LANES = 128
Q_BLOCK = 128
SSD_CHUNK = 128
VMEM_LIMIT = 56 << 20
MASK_BIAS = -1e30
INT_MIN = -(2 ** 31)
LOG2_E = 1.4426950408889634
V_ROWS = HEAD_DIM + 16

D_MODEL = 1024
D_INNER = 2 * D_MODEL
BC_DIM = 2 * SSM_GROUPS * SSM_STATE
SSM_HEADS = D_INNER // SSM_HEAD_DIM
ATT_DIM = ATT_HEADS * HEAD_DIM
KV_DIM = ATT_KV_HEADS * HEAD_DIM
QI_DIM = IDX_HEADS * IDX_DIM
C_Z = 0
C_XS = C_Z + D_INNER
C_BC = C_XS + D_INNER
C_Q = C_BC + BC_DIM
C_K = C_Q + ATT_DIM
C_V = C_K + KV_DIM
C_QI = C_V + KV_DIM
C_GS = C_QI + QI_DIM
C_GA = C_GS + D_MODEL
C_SM = C_GA + D_MODEL
SM_KI = 0
SM_DT = IDX_DIM
SM_WI = IDX_DIM + SSM_HEADS
PROJ_TN = 512
PROJ_COLS = -(-(C_SM + LANES) // PROJ_TN) * PROJ_TN


def _divisor_tile(n, max_tile, align):
    best = None
    for t in range(align, min(n, max_tile) + 1, align):
        if n % t == 0:
            best = t
    assert best is not None, (n, max_tile, align)
    return best


def _params(*sem):
    return pltpu.CompilerParams(dimension_semantics=sem, vmem_limit_bytes=VMEM_LIMIT)


def _dot(a, b):
    return jnp.dot(a.astype(MXU_DTYPE), b.astype(MXU_DTYPE), preferred_element_type=F32)


def _dot_nt(a, b):
    return lax.dot_general(a.astype(MXU_DTYPE), b.astype(MXU_DTYPE), (((1,), (1,)), ((), ())),
                           preferred_element_type=F32)


def _bf16_pieces(x):
    hi = x.astype(jnp.bfloat16)
    r = x - hi.astype(F32)
    mid = r.astype(jnp.bfloat16)
    return [hi, mid, (r - mid.astype(F32)).astype(jnp.bfloat16)]


def _dot_exact(a, b):
    return jnp.dot(a, b, preferred_element_type=F32, precision=lax.Precision.HIGHEST)


def _sigmoid(x):
    return 0.5 * jnp.tanh(0.5 * x) + 0.5


def _row_rmsnorm(x, w):
    return x * lax.rsqrt(jnp.mean(x * x, axis=-1, keepdims=True) + NORM_EPS) * w


def _in_proj_kernel(x_ref, nw_ref, w_ref, o_ref, u_sc):
    j = pl.program_id(1)

    @pl.when(j == 0)
    def _():
        u_sc[...] = _row_rmsnorm(x_ref[...], nw_ref[...]).astype(u_sc.dtype)

    o_ref[...] = jnp.dot(u_sc[...], w_ref[j], preferred_element_type=F32)


def _in_proj(h, norm_w, w):
    r, k = h.shape
    nt = w.shape[0]
    tm = _divisor_tile(r, 1100, 16)
    return pl.pallas_call(
        _in_proj_kernel,
        grid=(r // tm, nt),
        in_specs=[pl.BlockSpec((tm, k), lambda i, j: (i, 0)), pl.BlockSpec((1, k), lambda i, j: (0, 0)),
                  pl.BlockSpec(w.shape, lambda i, j: (0, 0, 0), pipeline_mode=pl.Buffered(1))],
        out_specs=pl.BlockSpec((tm, PROJ_TN), lambda i, j: (i, j)),
        out_shape=jax.ShapeDtypeStruct((r, nt * PROJ_TN), F32),
        scratch_shapes=[pltpu.VMEM((tm, k), MXU_DTYPE)],
        compiler_params=_params("parallel", "arbitrary"),
        name="in_proj",
    )(h, norm_w.reshape(1, k), w)


def _merge_kernel(h_ref, ysp_ref, yss_ref, yap_ref, yas_ref, gs_ref, ga_ref, wsp_ref, wap_ref, wo_ref, o_ref, *, n_p):
    prompt = pl.program_id(0) < n_p
    ys = jnp.where(prompt, ysp_ref[...], yss_ref[...].astype(MXU_DTYPE))
    ya = jnp.where(prompt, yap_ref[...], yas_ref[...].astype(MXU_DTYPE))
    a = jnp.dot(ys, wsp_ref[...], preferred_element_type=F32)
    b = jnp.dot(ya, wap_ref[...], preferred_element_type=F32)
    m = _sigmoid(gs_ref[...]) * a + _sigmoid(ga_ref[...]) * b
    o_ref[...] = h_ref[...] + _dot(m, wo_ref[...])


def _merge(h, rp, ys_p, ys_s, ya_p, ya_s, proj, w_sp, w_ap, w_o):
    r, d = h.shape
    tm = _divisor_tile(math.gcd(rp, r - rp), 256, 16)
    n_p = rp // tm
    const = lambda i: (0, 0)
    pblk = lambda w: pl.BlockSpec((tm, w), lambda i: (jnp.minimum(i, n_p - 1), 0))
    sblk = lambda w: pl.BlockSpec((tm, w), lambda i: (jnp.maximum(i - n_p, 0), 0))
    return pl.pallas_call(
        functools.partial(_merge_kernel, n_p=n_p),
        grid=(r // tm,),
        in_specs=[pl.BlockSpec((tm, d), lambda i: (i, 0)),
                  pblk(D_INNER), sblk(D_INNER), pblk(ATT_DIM), sblk(ATT_DIM),
                  pl.BlockSpec((tm, d), lambda i: (i, C_GS // D_MODEL)),
                  pl.BlockSpec((tm, d), lambda i: (i, C_GA // D_MODEL)),
                  pl.BlockSpec(w_sp.shape, const), pl.BlockSpec(w_ap.shape, const), pl.BlockSpec(w_o.shape, const)],
        out_specs=pl.BlockSpec((tm, d), lambda i: (i, 0)),
        out_shape=jax.ShapeDtypeStruct((r, d), F32),
        compiler_params=_params("parallel"),
        name="merge",
    )(h, ys_p, ys_s, ya_p, ya_s, proj, proj, w_sp, w_ap, w_o)


def _mlp_kernel(h_ref, nw_ref, wu_ref, wd_ref, *rest):
    x = h_ref[...]
    a = _dot(_row_rmsnorm(x, nw_ref[...]), wu_ref[...])
    a = jnp.square(jnp.maximum(a, 0.0))
    y = x + _dot(a, wd_ref[...])
    if len(rest) == 2:
        fw_ref, o_ref = rest
        y = _row_rmsnorm(y, fw_ref[...])
    else:
        o_ref, = rest
    o_ref[...] = y


def _mlp(h, norm_w, w_up, w_down, final_norm_w=None):
    r, d = h.shape
    tm = _divisor_tile(r, 256, 16)
    const = lambda i: (0, 0)
    extra = [] if final_norm_w is None else [final_norm_w.reshape(1, d)]
    return pl.pallas_call(
        _mlp_kernel,
        grid=(r // tm,),
        in_specs=[pl.BlockSpec((tm, d), lambda i: (i, 0)), pl.BlockSpec((1, d), const),
                  pl.BlockSpec(w_up.shape, const), pl.BlockSpec(w_down.shape, const)]
                 + [pl.BlockSpec((1, d), const)] * len(extra),
        out_specs=pl.BlockSpec((tm, d), lambda i: (i, 0)),
        out_shape=jax.ShapeDtypeStruct((r, d), F32),
        compiler_params=_params("parallel"),
        name="mlp",
    )(h, norm_w.reshape(1, d), w_up, w_down, *extra)


def _softplus(x):
    return jnp.maximum(x, 0.0) + jnp.log1p(jnp.exp(-jnp.abs(x)))


def _conv_silu(xp_sc, x, w_ref, b_ref, t):
    xp_sc[pl.ds(8, t), :] = x
    acc = b_ref[...] + x * w_ref[CONV_W - 1:CONV_W, :]
    for k in range(CONV_W - 1):
        acc = acc + xp_sc[pl.ds(5 + k, t), :] * w_ref[k:k + 1, :]
    xp_sc[pl.ds(5, CONV_W - 1), :] = xp_sc[pl.ds(t + 5, CONV_W - 1), :]
    return acc * _sigmoid(acc)


def _ssd_kernel(z_ref, xs_ref, bc_ref, sm_ref, *rest, t_in, valid_len, has_h0, has_prev):
    h0_ref = rest[0] if has_h0 else None
    rest = rest[int(has_h0):]
    pxs_ref, pbc_ref, cwx_ref, cbx_ref, cwb_ref, cbb_ref, dtb_ref, aneg_ref, dskip_ref, normw_ref, expand_ref = rest[:11]
    y_ref, hout_ref, state_sc, xpx_sc, xpb_sc = rest[11 + int(has_prev):]
    t = SSD_CHUNK
    c = pl.program_id(1)
    hpg = SSM_HEADS // SSM_GROUPS
    pair = 2 * SSM_HEAD_DIM

    @pl.when(c == 0)
    def _():
        if has_h0:
            for g in range(SSM_GROUPS):
                state_sc[g] = jnp.concatenate(
                    [jnp.concatenate([h0_ref[g * hpg + e], h0_ref[g * hpg + e + 1]], axis=0).T
                     for e in range(0, hpg, 2)], axis=1)
        else:
            state_sc[...] = jnp.zeros(state_sc.shape, F32)
        xpx_sc[pl.ds(5, CONV_W - 1), :] = pxs_ref[...]
        xpb_sc[pl.ds(5, CONV_W - 1), :] = pbc_ref[...]

    def rows(ref):
        x = ref[...]
        if t_in < t:
            x = jnp.concatenate([x, jnp.zeros((t - t_in, x.shape[1]), x.dtype)], axis=0)
        return x

    xs = _conv_silu(xpx_sc, rows(xs_ref), cwx_ref, cbx_ref, t)
    bcm = _conv_silu(xpb_sc, rows(bc_ref), cwb_ref, cbb_ref, t)
    z = rows(z_ref)

    row = lax.broadcasted_iota(jnp.int32, (t, LANES), 0)
    n_valid = jnp.minimum(valid_len - c * t, t_in)
    dt = jnp.where(row < n_valid, _softplus(rows(sm_ref) + dtb_ref[...]), 0.0)
    a = dt * aneg_ref[...]
    ri = lax.broadcasted_iota(jnp.int32, (t, t), 0)
    ci = lax.broadcasted_iota(jnp.int32, (t, t), 1)
    causal = ri >= ci
    acum = _dot_exact(jnp.where(causal, 1.0, 0.0), a)
    acum_t = acum.T
    ex = jnp.dot(jnp.concatenate(_bf16_pieces(dt) + _bf16_pieces(acum), axis=0), expand_ref[...],
                 preferred_element_type=F32)
    dtx = ex[:t] + ex[t:2 * t] + ex[2 * t:3 * t]
    acx = ex[3 * t:4 * t] + ex[4 * t:5 * t] + ex[5 * t:]
    last = acx[t - 1:t, :]
    xdt = xs * dtx
    xw = xdt * jnp.exp(last - acx)
    eacx = jnp.exp(acx)
    elast = jnp.exp(last)

    gw = D_INNER // SSM_GROUPS
    hpg = SSM_HEADS // SSM_GROUPS
    ys = []
    for g in range(SSM_GROUPS):
        bg = bcm[:, g * SSM_STATE:(g + 1) * SSM_STATE]
        cg = bcm[:, (SSM_GROUPS + g) * SSM_STATE:(SSM_GROUPS + g + 1) * SSM_STATE]
        cb = _dot_nt(cg, bg)
        h_t = state_sc[g]
        yg = [_dot(cg, h_t) * eacx[:, g * gw:(g + 1) * gw]]
        intra = []
        for e in range(hpg):
            hd = g * hpg + e
            seg = acum[:, SM_DT + hd:SM_DT + hd + 1] - acum_t[SM_DT + hd:SM_DT + hd + 1, :]
            m = cb * jnp.exp(jnp.where(causal, seg, -jnp.inf))
            intra.append(_dot(m, xdt[:, hd * SSM_HEAD_DIM:(hd + 1) * SSM_HEAD_DIM]))
        ys.append(yg[0] + jnp.concatenate(intra, axis=1))
        state_sc[g] = h_t * elast[:, g * gw:(g + 1) * gw] + _dot(bg.T, xw[:, g * gw:(g + 1) * gw])
    y = jnp.concatenate(ys, axis=1) + xs * dskip_ref[...]
    y = y * (z * _sigmoid(z))
    outs = []
    for g in range(SSM_GROUPS):
        yg = y[:, g * gw:(g + 1) * gw]
        outs.append(yg * lax.rsqrt(jnp.mean(yg * yg, axis=-1, keepdims=True) + NORM_EPS))
    y = jnp.concatenate(outs, axis=1) * normw_ref[...]
    y_ref[...] = y[:t_in].astype(y_ref.dtype)

    @pl.when(c == pl.num_programs(1) - 1)
    def _():
        for g in range(SSM_GROUPS):
            st = state_sc[g]
            for e in range(0, hpg, 2):
                both = st[:, e * SSM_HEAD_DIM:e * SSM_HEAD_DIM + pair].T
                hout_ref[g * hpg + e] = both[:SSM_HEAD_DIM]
                hout_ref[g * hpg + e + 1] = both[SSM_HEAD_DIM:]


def _ssd(proj, row0, n_seq, t_in, n_chunks, valid_len, layer, depth, state_in, state_prev, prefix, lw, out_dtype):
    assert row0 % t_in == 0
    rb0 = row0 // t_in
    rowblk = lambda w, col: pl.BlockSpec((t_in, w), lambda s, c: (rb0 + s * n_chunks + c, col // w))
    per_seq = lambda shape: pl.BlockSpec((None,) + shape, lambda s, c: (s,) + (0,) * len(shape))
    const = lambda arr: pl.BlockSpec(arr.shape, lambda s, c: (0,) * arr.ndim)
    state_blk = pl.BlockSpec((None, None, SSM_HEADS, SSM_HEAD_DIM, SSM_STATE), lambda s, c: (layer, s, 0, 0, 0))
    gw = D_INNER // SSM_GROUPS
    pxs, pbc = prefix[:, :, :D_INNER], prefix[:, :, D_INNER:]
    consts = (lw["cwx"], lw["cbx"], lw["cwb"], lw["cbb"], lw["dtb"], lw["aneg"], lw["dskip"], lw["normw"], lw["expand"])
    has_h0, has_prev = state_in is not None, state_prev is not None
    operands = [proj, proj, proj, proj] + [state_in] * has_h0 + [pxs, pbc, *consts] + [state_prev] * has_prev
    in_specs = ([rowblk(D_INNER, C_Z), rowblk(D_INNER, C_XS), rowblk(BC_DIM, C_BC), rowblk(LANES, C_SM)]
                + [state_blk] * has_h0
                + [per_seq((CONV_W - 1, D_INNER)), per_seq((CONV_W - 1, BC_DIM))] + [const(a) for a in consts]
                + [pl.BlockSpec(memory_space=pl.ANY)] * has_prev)
    return pl.pallas_call(
        functools.partial(_ssd_kernel, t_in=t_in, valid_len=valid_len, has_h0=has_h0, has_prev=has_prev),
        grid=(n_seq, n_chunks),
        in_specs=in_specs,
        out_specs=[pl.BlockSpec((t_in, D_INNER), lambda s, c: (s * n_chunks + c, 0)), state_blk],
        out_shape=[jax.ShapeDtypeStruct((n_seq * n_chunks * t_in, D_INNER), out_dtype),
                   jax.ShapeDtypeStruct((depth, n_seq, SSM_HEADS, SSM_HEAD_DIM, SSM_STATE), F32)],
        scratch_shapes=[pltpu.VMEM((SSM_GROUPS, SSM_STATE, gw), F32),
                        pltpu.VMEM((SSD_CHUNK + 8, D_INNER), F32),
                        pltpu.VMEM((SSD_CHUNK + 8, BC_DIM), F32)],
        input_output_aliases={len(operands) - 1: 1} if has_prev else {},
        compiler_params=_params("parallel", "arbitrary"),
        name="ssd",
    )(*operands)


def _sort_key(x):
    bits = lax.bitcast_convert_type(x, jnp.int32)
    bits = jnp.where(bits == INT_MIN, 0, bits)
    return bits ^ ((bits >> 31) & 0x7FFFFFFF)


def _fold_rows(x, group=8):
    x = x.reshape(x.shape[0] // group, group, x.shape[1])
    while x.shape[0] > 1:
        half, odd = divmod(x.shape[0], 2)
        folded = x[:half] + x[half:2 * half]
        if odd:
            folded = jnp.concatenate([folded[:1] + x[2 * half:], folded[1:]], axis=0) if half > 1 else folded + x[2 * half:]
        x = folded
    return x[0]


def _count_t(keys_sc, n_chunks, pred):
    def body(c, cnt):
        return cnt + _fold_rows(jnp.where(pred(keys_sc[c]), 1.0, 0.0))

    cnt = lax.fori_loop(0, n_chunks, body, jnp.zeros((8, LANES), F32))
    return jnp.sum(cnt, axis=0, keepdims=True)


HALF_MIN = -(2 ** 30)
SEARCH_UNROLLS = tuple(range(1, 17))


def _search_bounds(nkc):
    return sorted({min(b, nkc) for b in SEARCH_UNROLLS} | {nkc})


def _search_chunks(n_chunks, nkc):
    bounds = _search_bounds(nkc)
    n = jnp.int32(bounds[-1])
    for b in reversed(bounds[:-1]):
        n = jnp.where(n_chunks <= b, b, n)
    return n


def _kth_halved_t(kh_sc, n, k):
    total = n * kh_sc.shape[1]

    def count_ge(cand):
        below = None
        for c in range(n):
            neg = _fold_rows((kh_sc[c] - cand) >> 31)
            below = neg if below is None else below + neg
        return total + jnp.sum(below.astype(F32), axis=0, keepdims=True)

    def bit_body(i, u):
        cand = u | lax.shift_left(jnp.int32(1), 30 - i)
        return jnp.where(count_ge(cand + HALF_MIN) >= k, cand, u)

    return lax.fori_loop(0, 31, bit_body, jnp.zeros((1, LANES), jnp.int32)) + HALF_MIN


def _kth_key_t(keys_sc, kh_sc, n_chunks, k):
    bounds = _search_bounds(kh_sc.shape[0])

    def build(i):
        if i == len(bounds) - 1:
            return lambda: _kth_halved_t(kh_sc, bounds[i], k)
        return lambda: lax.cond(n_chunks <= bounds[i], lambda: _kth_halved_t(kh_sc, bounds[i], k), build(i + 1))

    odd = build(0)() * 2 + 1
    return jnp.where(_count_t(keys_sc, n_chunks, lambda key: key >= odd) >= k, odd, odd - 1)


def _select_t(keys_sc, kh_sc, bias_sc, n_chunks, k):
    kc = keys_sc.shape[1]
    thr = _kth_key_t(keys_sc, kh_sc, n_chunks, k)
    n_ge = _count_t(keys_sc, n_chunks, lambda key: key >= thr)
    surplus = jnp.max(jnp.where((n_ge > k) & (thr != INT_MIN), 1.0, 0.0))

    @pl.when(surplus == 0.0)
    def _():
        floor = jnp.maximum(thr, INT_MIN + 1)

        def body(c, carry):
            bias_sc[c] = jnp.where(keys_sc[c] >= floor, 0.0, MASK_BIAS)
            return carry

        lax.fori_loop(0, n_chunks, body, 0)

    @pl.when(surplus != 0.0)
    def _():
        need = k - _count_t(keys_sc, n_chunks, lambda key: key > thr)
        ri = lax.broadcasted_iota(jnp.int32, (kc, kc), 0)
        ci = lax.broadcasted_iota(jnp.int32, (kc, kc), 1)
        lower = jnp.where(ci <= ri, 1.0, 0.0).astype(jnp.bfloat16)

        def body(c, run):
            key = keys_sc[c]
            eqf = jnp.where((key == thr) & (key != INT_MIN), 1.0, 0.0)
            rank = jnp.dot(lower, eqf.astype(jnp.bfloat16), preferred_element_type=F32) + run
            sel = (key > thr) | ((eqf > 0.0) & (rank <= need))
            bias_sc[c] = jnp.where(sel, 0.0, MASK_BIAS)
            return run + jnp.sum(_fold_rows(eqf), axis=0, keepdims=True)

        lax.fori_loop(0, n_chunks, body, jnp.zeros((1, LANES), F32))


def _select_s(keys_sc, kh_sc, bias_ref, k):
    n_chunks, rows, width = keys_sc.shape

    def count(pred):
        return jnp.sum(jnp.sum(jnp.where(pred(keys_sc[...]), 1.0, 0.0), axis=0), axis=1, keepdims=True)

    def bit_body(i, u):
        cand = u | lax.shift_left(jnp.int32(1), 30 - i)
        below = jnp.sum((kh_sc[...] - (cand + HALF_MIN)[None]) >> 31, axis=0).astype(F32)
        return jnp.where(n_chunks * width + jnp.sum(below, axis=1, keepdims=True) >= k, cand, u)

    odd = (lax.fori_loop(0, 31, bit_body, jnp.zeros((rows, 1), jnp.int32)) + HALF_MIN) * 2 + 1
    thr = jnp.where(count(lambda key: key >= odd[None]) >= k, odd, odd - 1)
    n_ge = count(lambda key: key >= thr[None])
    surplus = jnp.max(jnp.where((n_ge > k) & (thr != INT_MIN), 1.0, 0.0))

    @pl.when(surplus == 0.0)
    def _():
        floor = jnp.maximum(thr, INT_MIN + 1)
        bias_ref[...] = jnp.where(keys_sc[...] >= floor[None], 0.0, MASK_BIAS)

    @pl.when(surplus != 0.0)
    def _():
        need = k - count(lambda key: key > thr[None])
        ri = lax.broadcasted_iota(jnp.int32, (width, width), 0)
        ci = lax.broadcasted_iota(jnp.int32, (width, width), 1)
        upper = jnp.where(ri <= ci, 1.0, 0.0).astype(jnp.bfloat16)

        def body(c, run):
            key = keys_sc[c]
            eqf = jnp.where((key == thr) & (key != INT_MIN), 1.0, 0.0)
            rank = jnp.dot(eqf.astype(jnp.bfloat16), upper, preferred_element_type=F32) + run
            sel = (key > thr) | ((eqf > 0.0) & (rank <= need))
            bias_ref[c] = jnp.where(sel, 0.0, MASK_BIAS)
            return run + jnp.sum(eqf, axis=1, keepdims=True)

        lax.fori_loop(0, n_chunks, body, jnp.zeros((rows, 1), F32))


def _pattn_kernel(qi_ref, smq_ref, q_ref, smk_ref, k_ref, v_ref, o_ref,
                  kb_sc, kib_sc, vt_sc, keys_sc, kh_sc, bias_sc, qz_sc, m_sc, acc_sc, *, kc, lp, topk):
    j = pl.program_id(1)
    nq = Q_BLOCK
    nkc = kb_sc.shape[0]

    @pl.when(j == 0)
    def _():
        for c in range(nkc):
            r0 = c * kc
            n = min(kc, lp - r0)

            def chunk(ref):
                x = ref[r0:r0 + n, :]
                return x if n == kc else jnp.concatenate([x, jnp.zeros((kc - n, x.shape[1]), x.dtype)], axis=0)

            kb_sc[c] = chunk(k_ref).astype(MXU_DTYPE)
            kib_sc[c] = chunk(smk_ref).astype(MXU_DTYPE)
            v_t = chunk(v_ref).T
            pad = jnp.concatenate([jnp.ones((1, kc), F32), jnp.zeros((V_ROWS - HEAD_DIM - 1, kc), F32)], axis=0)
            vt_sc[c] = jnp.concatenate([x for hk in range(ATT_KV_HEADS)
                                        for x in (v_t[hk * HEAD_DIM:(hk + 1) * HEAD_DIM], pad)], axis=0).astype(MXU_DTYPE)

    n_chunks = ((j + 1) * nq - 1) // kc + 1

    qi_t = (qi_ref[...] * IDX_DIM ** -0.5).T
    qiz = jnp.concatenate([qi_t[h * IDX_DIM:(h + 1) * IDX_DIM] for h in range(IDX_HEADS)], axis=1)
    qiz = jnp.concatenate([qiz, jnp.zeros((LANES - IDX_DIM, IDX_HEADS * nq), F32)], axis=0).astype(MXU_DTYPE)
    smq_t = smq_ref[...].T
    wi_row = jnp.concatenate([smq_t[SM_WI + h:SM_WI + h + 1] for h in range(IDX_HEADS)], axis=1) * IDX_HEADS ** -0.5
    kpos = lax.broadcasted_iota(jnp.int32, (kc, nq), 0)
    qpos = j * nq + lax.broadcasted_iota(jnp.int32, (kc, nq), 1)

    def score_body(c, carry):
        s = jnp.maximum(jnp.dot(kib_sc[c], qiz, preferred_element_type=F32), 0.0) * wi_row
        acc = s[:, :nq]
        for h in range(1, IDX_HEADS):
            acc = acc + s[:, h * nq:(h + 1) * nq]
        key = jnp.where(c * kc + kpos <= qpos, _sort_key(acc), INT_MIN)
        keys_sc[c] = key
        kh_sc[c] = key >> 1
        return carry

    lax.fori_loop(0, n_chunks, score_body, 0)

    def blank_body(c, carry):
        kh_sc[c] = jnp.full((kc, nq), HALF_MIN, jnp.int32)
        return carry

    lax.fori_loop(n_chunks, _search_chunks(n_chunks, nkc), blank_body, 0)
    _select_t(keys_sc, kh_sc, bias_sc, n_chunks, topk)

    q_t = (q_ref[...] * (HEAD_DIM ** -0.5 * LOG2_E)).T
    gw = ATT_GROUP * nq
    zero = jnp.zeros((HEAD_DIM, gw), F32)
    for hk in range(ATT_KV_HEADS):
        own = jnp.concatenate([q_t[(hk * ATT_GROUP + g) * HEAD_DIM:(hk * ATT_GROUP + g + 1) * HEAD_DIM]
                               for g in range(ATT_GROUP)], axis=1)
        qz_sc[hk] = jnp.concatenate([own if i == hk else zero for i in range(ATT_KV_HEADS)],
                                    axis=0).astype(MXU_DTYPE)
    m_sc[...] = jnp.full(m_sc.shape, MASK_BIAS, F32)
    acc_sc[...] = jnp.zeros(acc_sc.shape, F32)

    def attend(chunks):
        stages = [(c, hk) for c in chunks for hk in range(ATT_KV_HEADS)]
        loaded = {}

        def operands(c):
            if id(c) not in loaded:
                loaded[id(c)] = (kb_sc[c], vt_sc[c], jnp.concatenate([bias_sc[c]] * ATT_GROUP, axis=1))
            return loaded[id(c)]

        def qk(t):
            c, hk = stages[t]
            kch, _, b4 = operands(c)
            return jnp.dot(kch, qz_sc[hk], preferred_element_type=F32) + b4

        def soft(t, s):
            hk = stages[t][1]
            m_old = m_sc[hk]
            m_new = jnp.maximum(m_old, jnp.max(s, axis=0, keepdims=True))
            m_sc[hk] = m_new
            return jnp.exp2(m_old - m_new), jnp.exp2(s - m_new).astype(MXU_DTYPE)

        def pv(t, alpha, p):
            c, hk = stages[t]
            vt = operands(c)[1]
            acc_sc[hk] = alpha * acc_sc[hk] + jnp.dot(vt[hk * V_ROWS:(hk + 1) * V_ROWS], p,
                                                      preferred_element_type=F32)

        n = len(stages)
        s = {0: qk(0), 1: qk(1)}
        ap = {}
        for t in range(n):
            ap[t] = soft(t, s.pop(t))
            if t + 2 < n:
                s[t + 2] = qk(t + 2)
            if t >= 1:
                pv(t - 1, *ap.pop(t - 1))
        pv(n - 1, *ap.pop(n - 1))

    def pair_body(i, carry):
        attend([2 * i, 2 * i + 1])
        return carry

    lax.fori_loop(0, n_chunks // 2, pair_body, 0)

    @pl.when(n_chunks % 2 == 1)
    def _():
        attend([n_chunks - 1])

    heads = []
    for hk in range(ATT_KV_HEADS):
        acc = acc_sc[hk]
        o = acc[:HEAD_DIM] / acc[HEAD_DIM:HEAD_DIM + 1]
        heads += [o[:, g * nq:(g + 1) * nq] for g in range(ATT_GROUP)]
    o_ref[...] = jnp.concatenate(heads, axis=0).T.astype(o_ref.dtype)


def _prompt_attention(proj, n_seq, lp, topk):
    nqb = lp // Q_BLOCK
    kc = 512
    nkc = -(-lp // kc)
    qblk = lambda w, col: pl.BlockSpec((Q_BLOCK, w), lambda b, j: (b * nqb + j, col // w))
    seqblk = lambda w, col: pl.BlockSpec((lp, w), lambda b, j: (b, col // w))
    gw = ATT_GROUP * Q_BLOCK
    return pl.pallas_call(
        functools.partial(_pattn_kernel, kc=kc, lp=lp, topk=topk),
        grid=(n_seq, nqb),
        in_specs=[qblk(QI_DIM, C_QI), qblk(LANES, C_SM), qblk(ATT_DIM, C_Q),
                  seqblk(LANES, C_SM), seqblk(KV_DIM, C_K), seqblk(KV_DIM, C_V)],
        out_specs=pl.BlockSpec((Q_BLOCK, ATT_DIM), lambda b, j: (b * nqb + j, 0)),
        out_shape=jax.ShapeDtypeStruct((n_seq * lp, ATT_DIM), MXU_DTYPE),
        scratch_shapes=[pltpu.VMEM((nkc, kc, KV_DIM), MXU_DTYPE), pltpu.VMEM((nkc, kc, LANES), MXU_DTYPE),
                        pltpu.VMEM((nkc, ATT_KV_HEADS * V_ROWS, kc), MXU_DTYPE),
                        pltpu.VMEM((nkc, kc, Q_BLOCK), jnp.int32), pltpu.VMEM((nkc, kc, Q_BLOCK), jnp.int32),
                        pltpu.VMEM((nkc, kc, Q_BLOCK), F32),
                        pltpu.VMEM((ATT_KV_HEADS, KV_DIM, gw), MXU_DTYPE),
                        pltpu.VMEM((ATT_KV_HEADS, 1, gw), F32),
                        pltpu.VMEM((ATT_KV_HEADS, V_ROWS, gw), F32)],
        compiler_params=_params("parallel", "arbitrary"),
        name="prompt_attention",
    )(proj, proj, proj, proj, proj, proj)


def _pad_rows(x, n):
    return jnp.concatenate([x, jnp.zeros((n - x.shape[0], x.shape[1]), x.dtype)], axis=0)


def _sselect_kernel(pt_ref, qi_ref, sm_ref, *rest, pages_per_step, n_pages, page, ns, topk):
    page_refs = rest[:pages_per_step]
    bias_ref, keys_sc, kh_sc = rest[pages_per_step:]
    g = pl.program_id(1)
    qi = (qi_ref[...] * IDX_DIM ** -0.5).astype(MXU_DTYPE)
    qs = jnp.concatenate([qi[:, h * IDX_DIM:(h + 1) * IDX_DIM] for h in range(IDX_HEADS)], axis=0)
    sm = sm_ref[...]
    wi = sm[:, SM_WI:SM_WI + IDX_HEADS] * IDX_HEADS ** -0.5

    def weigh(s):
        s = jnp.maximum(s, 0.0)
        acc = s[:ns] * wi[:, 0:1]
        for h in range(1, IDX_HEADS):
            acc = acc + s[h * ns:(h + 1) * ns] * wi[:, h:h + 1]
        return _sort_key(acc)

    keys = weigh(_dot(qs, jnp.concatenate([r[...] for r in page_refs], axis=1)))
    for i in range(pages_per_step):
        keys_sc[g * pages_per_step + i] = keys[:, i * page:(i + 1) * page]
        kh_sc[g * pages_per_step + i] = keys[:, i * page:(i + 1) * page] >> 1

    @pl.when(g == pl.num_programs(1) - 1)
    def _():
        qrow = lax.broadcasted_iota(jnp.int32, (ns, page), 0)
        lane = lax.broadcasted_iota(jnp.int32, (ns, page), 1)
        new = weigh(_dot_nt(qs, _pad_rows(sm[:, SM_KI:SM_KI + IDX_DIM], page)))
        new = jnp.where(lane <= qrow, new, INT_MIN)
        keys_sc[n_pages] = new
        kh_sc[n_pages] = new >> 1
        _select_s(keys_sc, kh_sc, bias_ref, topk)


def _sattn_kernel(pt_ref, q_ref, kn_ref, vn_ref, bias_ref, *rest, pages_per_step, n_pages, page, ns):
    k_refs = rest[:pages_per_step]
    v_refs = rest[pages_per_step:2 * pages_per_step]
    o_ref, qbd_sc, m_sc, l_sc, acc_sc = rest[2 * pages_per_step:]
    g = pl.program_id(1)

    @pl.when(g == 0)
    def _():
        q = q_ref[...] * HEAD_DIM ** -0.5
        rows = []
        for h in range(ATT_HEADS):
            hk = h // ATT_GROUP
            parts = [jnp.zeros((ns, HEAD_DIM), F32)] * ATT_KV_HEADS
            parts[hk] = q[:, h * HEAD_DIM:(h + 1) * HEAD_DIM]
            rows.append(jnp.concatenate(parts, axis=1))
        qbd_sc[...] = jnp.concatenate(rows, axis=0).astype(MXU_DTYPE)
        m_sc[...] = jnp.full(m_sc.shape, MASK_BIAS, F32)
        l_sc[...] = jnp.zeros(l_sc.shape, F32)
        acc_sc[...] = jnp.zeros(acc_sc.shape, F32)

    def attend(s, b, pv):
        s = s + jnp.concatenate([b] * ATT_HEADS, axis=0)
        m_old = m_sc[...]
        m_new = jnp.maximum(m_old, jnp.max(s, axis=1, keepdims=True))
        alpha = jnp.exp(m_old - m_new)
        p = jnp.exp(s - m_new)
        l_sc[...] = alpha * l_sc[...] + jnp.sum(p, axis=1, keepdims=True)
        acc_sc[...] = alpha * acc_sc[...] + pv(p)
        m_sc[...] = m_new

    kcat = jnp.concatenate([r[...] for r in k_refs], axis=1)
    vcat = jnp.concatenate([r[...] for r in v_refs], axis=1)
    bcat = jnp.concatenate([bias_ref[g * pages_per_step + i] for i in range(pages_per_step)], axis=1)
    attend(_dot(qbd_sc[...], kcat), bcat, lambda p: _dot_nt(p, vcat))

    @pl.when(g == pl.num_programs(1) - 1)
    def _():
        vn = _pad_rows(vn_ref[...], page)
        attend(_dot_nt(qbd_sc[...], _pad_rows(kn_ref[...], page)), bias_ref[n_pages], lambda p: _dot(p, vn))
        o = acc_sc[...] / l_sc[...]
        o_ref[...] = jnp.concatenate(
            [o[h * ns:(h + 1) * ns, (h // ATT_GROUP) * HEAD_DIM:(h // ATT_GROUP + 1) * HEAD_DIM]
             for h in range(ATT_HEADS)], axis=1)


def _key_minor(cache):
    nd = cache.ndim
    t = jnp.transpose(cache, (0, 1) + tuple(range(3, nd)) + (2,))
    return t.reshape(t.shape[:2] + (-1, t.shape[-1]))


def _sample_attention(proj, row0, page_table, ck_t, cv_t, cki_t, layer, ns, topk):
    bd, n_pages = page_table.shape
    page = cki_t.shape[3]
    assert page == LANES and row0 % ns == 0
    pps = max(d for d in range(1, 17) if n_pages % d == 0)
    ng = n_pages // pps
    rb0 = row0 // ns
    rowblk = lambda w, col: pl.BlockSpec((ns, w), lambda s, g, pt: (rb0 + s, col // w))
    pageblk = lambda w, i: pl.BlockSpec((None, None, w, page), lambda s, g, pt: (layer, pt[s, g * pps + i], 0, 0))
    biasblk = pl.BlockSpec((None, n_pages + 1, ns, page), lambda s, g, pt: (s, 0, 0, 0))
    static = dict(pages_per_step=pps, n_pages=n_pages, page=page, ns=ns)

    bias = pl.pallas_call(
        functools.partial(_sselect_kernel, topk=topk, **static),
        grid_spec=pltpu.PrefetchScalarGridSpec(
            num_scalar_prefetch=1, grid=(bd, ng),
            in_specs=[rowblk(QI_DIM, C_QI), rowblk(LANES, C_SM)] + [pageblk(IDX_DIM, i) for i in range(pps)],
            out_specs=biasblk,
            scratch_shapes=[pltpu.VMEM((n_pages + 1, ns, page), jnp.int32)] * 2),
        out_shape=jax.ShapeDtypeStruct((bd, n_pages + 1, ns, page), F32),
        compiler_params=_params("parallel", "arbitrary"),
        name="sample_select",
    )(page_table, proj, proj, *([cki_t] * pps))

    rows = ATT_HEADS * ns
    return pl.pallas_call(
        functools.partial(_sattn_kernel, **static),
        grid_spec=pltpu.PrefetchScalarGridSpec(
            num_scalar_prefetch=1, grid=(bd, ng),
            in_specs=[rowblk(ATT_DIM, C_Q), rowblk(KV_DIM, C_K), rowblk(KV_DIM, C_V), biasblk]
                     + [pageblk(KV_DIM, i) for i in range(pps)] * 2,
            out_specs=pl.BlockSpec((ns, ATT_DIM), lambda s, g, pt: (s, 0)),
            scratch_shapes=[pltpu.VMEM((rows, KV_DIM), MXU_DTYPE), pltpu.VMEM((rows, 1), F32),
                            pltpu.VMEM((rows, 1), F32), pltpu.VMEM((rows, KV_DIM), F32)]),
        out_shape=jax.ShapeDtypeStruct((bd * ns, ATT_DIM), F32),
        compiler_params=_params("parallel", "arbitrary"),
        name="sample_attention",
    )(page_table, proj, proj, proj, bias, *([ck_t] * pps), *([cv_t] * pps))


def _regroup_w_in(w):
    o_dt = D_INNER + D_INNER + BC_DIM
    o_q = o_dt + SSM_HEADS
    o_ki = o_q + ATT_DIM + 2 * KV_DIM + QI_DIM
    o_wi = o_ki + IDX_DIM
    o_g = o_wi + IDX_HEADS
    parts = [w[:, :o_dt], w[:, o_q:o_ki], w[:, o_g:o_g + 2 * D_MODEL], w[:, o_ki:o_wi], w[:, o_dt:o_q], w[:, o_wi:o_g]]
    used = sum(p.shape[1] for p in parts)
    parts.append(jnp.zeros((w.shape[0], PROJ_COLS - used), w.dtype))
    w = jnp.concatenate(parts, axis=1).astype(MXU_DTYPE)
    return w.reshape(w.shape[0], PROJ_COLS // PROJ_TN, PROJ_TN).transpose(1, 0, 2)


def _small_lanes(v):
    return jnp.zeros((1, LANES), F32).at[0, SM_DT:SM_DT + SSM_HEADS].set(v.astype(F32))


def _layer_consts(conv_w, conv_b, dt_bias, a_log, d_skip, ssm_norm_w):
    head_of_lane = jnp.arange(D_INNER) // SSM_HEAD_DIM
    expand = (jnp.arange(LANES)[:, None] == (SM_DT + head_of_lane)[None, :]).astype(jnp.bfloat16)
    return dict(
        cwx=conv_w[:, :D_INNER], cbx=conv_b[None, :D_INNER], cwb=conv_w[:, D_INNER:], cbb=conv_b[None, D_INNER:],
        dtb=_small_lanes(dt_bias), aneg=_small_lanes(-jnp.exp(a_log.astype(F32))),
        dskip=jnp.repeat(d_skip.astype(F32), SSM_HEAD_DIM)[None, :], normw=ssm_norm_w.astype(F32)[None, :],
        expand=expand)


def kernel(x_prompt, x_sample, cache_k, cache_v, cache_kidx, state_ssm, state_conv, page_table, meta_tokens, norm1_w, w_in, conv_w, conv_b, dt_bias, a_log, d_skip, ssm_norm_w, w_ssm_proj, w_attn_proj, w_out, norm2_w, w_up, w_down, final_norm_w):
    b, seq, d = x_prompt.shape
    bd, ns = x_sample.shape[:2]
    depth = w_in.shape[0]
    assert d == D_MODEL and ns % 8 == 0
    past = page_table.shape[1] * cache_kidx.shape[2]
    lv = N_META + seq
    lp = -(-lv // Q_BLOCK) * Q_BLOCK
    rp = b * lp
    rs = bd * ns
    r = -(-(rp + rs) // LANES) * LANES
    topk_p = min(TOPK_MAX, seq // 4)
    topk_s = min(TOPK_MAX, (past + ns) // 4)

    hp = jnp.concatenate([jnp.broadcast_to(meta_tokens[None].astype(F32), (b, N_META, d)), x_prompt,
                          jnp.zeros((b, lp - lv, d), F32)], axis=1).reshape(rp, d)
    h = jnp.concatenate([hp, x_sample.reshape(rs, d), jnp.zeros((r - rp - rs, d), F32)], axis=0)

    ck_t, cv_t, cki_t = _key_minor(cache_k), _key_minor(cache_v), _key_minor(cache_kidx)
    p_ssm = s_ssm = None
    zero_prefix = jnp.zeros((b, CONV_W - 1, D_INNER + BC_DIM), F32)
    outs = [[] for _ in range(10)]
    for l in range(depth):
        lw = _layer_consts(conv_w[l], conv_b[l], dt_bias[l], a_log[l], d_skip[l], ssm_norm_w[l])
        proj = _in_proj(h, norm1_w[l], _regroup_w_in(w_in[l]))

        yp, p_ssm = _ssd(proj, 0, b, SSD_CHUNK, lp // SSD_CHUNK, lv, l, depth, None, p_ssm, zero_prefix, lw, MXU_DTYPE)
        ysm, s_ssm = _ssd(proj, rp, bd, ns, 1, ns, l, depth, state_ssm, s_ssm, state_conv[l], lw, F32)
        ap = _prompt_attention(proj, b, lp, topk_p)
        asm = _sample_attention(proj, rp, page_table, ck_t, cv_t, cki_t, l, ns, topk_s)
        tail = r - rp - rs
        if tail:
            ysm = jnp.concatenate([ysm, jnp.zeros((tail, D_INNER), F32)], axis=0)
            asm = jnp.concatenate([asm, jnp.zeros((tail, ATT_DIM), F32)], axis=0)

        h = _merge(h, rp, yp, ysm, ap, asm, proj, w_ssm_proj[l].astype(MXU_DTYPE), w_attn_proj[l].astype(MXU_DTYPE),
                   w_out[l].astype(MXU_DTYPE))
        h = _mlp(h, norm2_w[l], w_up[l].astype(MXU_DTYPE), w_down[l].astype(MXU_DTYPE),
                 final_norm_w if l == depth - 1 else None)

        def pcols(c0, w):
            return proj[:rp, c0:c0 + w].reshape(b, lp, w)[:, :lv]

        def scols(c0, w):
            return proj[rp:rp + rs, c0:c0 + w].reshape(bd, ns, w)

        xbc_w = D_INNER + BC_DIM
        p_conv = jnp.stack([proj[s * lp + lv - (CONV_W - 1):s * lp + lv, C_XS:C_XS + xbc_w] for s in range(b)])
        s_conv = jnp.concatenate([state_conv[l], scols(C_XS, xbc_w)], axis=1)[:, -(CONV_W - 1):]
        for acc, t in zip(outs, (
                pcols(C_K, KV_DIM).reshape(b, lv, ATT_KV_HEADS, HEAD_DIM),
                pcols(C_V, KV_DIM).reshape(b, lv, ATT_KV_HEADS, HEAD_DIM),
                pcols(C_SM + SM_KI, IDX_DIM),
                None,
                p_conv,
                scols(C_K, KV_DIM).reshape(bd, ns, ATT_KV_HEADS, HEAD_DIM),
                scols(C_V, KV_DIM).reshape(bd, ns, ATT_KV_HEADS, HEAD_DIM),
                scols(C_SM + SM_KI, IDX_DIM),
                None,
                s_conv)):
            acc.append(t)

    y = h
    y_prompt = jnp.stack([y[s * lp + N_META:s * lp + lv] for s in range(b)])
    y_sample = y[rp:rp + rs].reshape(bd, ns, d)
    stacked = [jnp.stack(o) if o[0] is not None else None for o in outs]
    stacked[3], stacked[8] = p_ssm, s_ssm
    return (y_prompt, y_sample) + tuple(stacked)
```

```python
import functools
import math

import jax
import jax.numpy as jnp
from jax import lax
from jax.experimental import pallas as pl
from jax.experimental.pallas import tpu as pltpu

F32 = jnp.float32
MXU_DTYPE = jnp.bfloat16

N_META = 16
NORM_EPS = 1e-6
SSM_HEAD_DIM = 64
SSM_GROUPS = 4
SSM_STATE = 128
CONV_W = 4
ATT_HEADS = 16
ATT_KV_HEADS = 4
ATT_GROUP = ATT_HEADS // ATT_KV_HEADS
HEAD_DIM = 64
IDX_HEADS = 8
IDX_DIM = 64
TOPK_MAX = 256

LANES = 128
Q_BLOCK = 128
SSD_CHUNK = 128
VMEM_LIMIT = 56 << 20
MASK_BIAS = -1e30
INT_MIN = -(2 ** 31)
LOG2_E = 1.4426950408889634
V_ROWS = HEAD_DIM + 16

D_MODEL = 1024
D_INNER = 2 * D_MODEL
BC_DIM = 2 * SSM_GROUPS * SSM_STATE
SSM_HEADS = D_INNER // SSM_HEAD_DIM
ATT_DIM = ATT_HEADS * HEAD_DIM
KV_DIM = ATT_KV_HEADS * HEAD_DIM
QI_DIM = IDX_HEADS * IDX_DIM
C_Z = 0
C_XS = C_Z + D_INNER
C_BC = C_XS + D_INNER
C_Q = C_BC + BC_DIM
C_K = C_Q + ATT_DIM
C_V = C_K + KV_DIM
C_QI = C_V + KV_DIM
C_GS = C_QI + QI_DIM
C_GA = C_GS + D_MODEL
C_SM = C_GA + D_MODEL
SM_KI = 0
SM_DT = IDX_DIM
SM_WI = IDX_DIM + SSM_HEADS
PROJ_TN = 512
PROJ_COLS = -(-(C_SM + LANES) // PROJ_TN) * PROJ_TN


def _divisor_tile(n, max_tile, align):
    best = None
    for t in range(align, min(n, max_tile) + 1, align):
        if n % t == 0:
            best = t
    assert best is not None, (n, max_tile, align)
    return best


def _params(*sem):
    return pltpu.CompilerParams(dimension_semantics=sem, vmem_limit_bytes=VMEM_LIMIT)


def _dot(a, b):
    return jnp.dot(a.astype(MXU_DTYPE), b.astype(MXU_DTYPE), preferred_element_type=F32)


def _dot_nt(a, b):
    return lax.dot_general(a.astype(MXU_DTYPE), b.astype(MXU_DTYPE), (((1,), (1,)), ((), ())),
                           preferred_element_type=F32)


def _bf16_pieces(x):
    hi = x.astype(jnp.bfloat16)
    r = x - hi.astype(F32)
    mid = r.astype(jnp.bfloat16)
    return [hi, mid, (r - mid.astype(F32)).astype(jnp.bfloat16)]


def _dot_exact(a, b):
    return jnp.dot(a, b, preferred_element_type=F32, precision=lax.Precision.HIGHEST)


def _sigmoid(x):
    return 0.5 * jnp.tanh(0.5 * x) + 0.5


def _row_rmsnorm(x, w):
    return x * lax.rsqrt(jnp.mean(x * x, axis=-1, keepdims=True) + NORM_EPS) * w


def _in_proj_kernel(x_ref, nw_ref, w_ref, o_ref, u_sc):
    j = pl.program_id(1)

    @pl.when(j == 0)
    def _():
        u_sc[...] = _row_rmsnorm(x_ref[...], nw_ref[...]).astype(u_sc.dtype)

    o_ref[...] = jnp.dot(u_sc[...], w_ref[j], preferred_element_type=F32)


def _in_proj(h, norm_w, w):
    r, k = h.shape
    nt = w.shape[0]
    tm = _divisor_tile(r, 1100, 16)
    return pl.pallas_call(
        _in_proj_kernel,
        grid=(r // tm, nt),
        in_specs=[pl.BlockSpec((tm, k), lambda i, j: (i, 0)), pl.BlockSpec((1, k), lambda i, j: (0, 0)),
                  pl.BlockSpec(w.shape, lambda i, j: (0, 0, 0), pipeline_mode=pl.Buffered(1))],
        out_specs=pl.BlockSpec((tm, PROJ_TN), lambda i, j: (i, j)),
        out_shape=jax.ShapeDtypeStruct((r, nt * PROJ_TN), F32),
        scratch_shapes=[pltpu.VMEM((tm, k), MXU_DTYPE)],
        compiler_params=_params("parallel", "arbitrary"),
        name="in_proj",
    )(h, norm_w.reshape(1, k), w)


def _mix_kernel(h_ref, ysp_ref, yss_ref, yap_ref, yas_ref, gs_ref, ga_ref, wsp_ref, wap_ref, wo_ref,
                nw_ref, wu_ref, wd_ref, *rest, n_p):
    prompt = pl.program_id(0) < n_p
    ys = jnp.where(prompt, ysp_ref[...], yss_ref[...].astype(MXU_DTYPE))
    ya = jnp.where(prompt, yap_ref[...], yas_ref[...].astype(MXU_DTYPE))
    a = jnp.dot(ys, wsp_ref[...], preferred_element_type=F32)
    b = jnp.dot(ya, wap_ref[...], preferred_element_type=F32)
    m = _sigmoid(gs_ref[...]) * a + _sigmoid(ga_ref[...]) * b
    x = h_ref[...] + _dot(m, wo_ref[...])
    f = _dot(_row_rmsnorm(x, nw_ref[...]), wu_ref[...])
    f = jnp.square(jnp.maximum(f, 0.0))
    y = x + _dot(f, wd_ref[...])
    if len(rest) == 2:
        fw_ref, o_ref = rest
        y = _row_rmsnorm(y, fw_ref[...])
    else:
        o_ref, = rest
    o_ref[...] = y


def _mix(h, rp, ys_p, ys_s, ya_p, ya_s, proj, w_sp, w_ap, w_o, norm_w, w_up, w_down, final_norm_w=None):
    r, d = h.shape
    tm = _divisor_tile(math.gcd(rp, r - rp), 256, 16)
    n_p = rp // tm
    const = lambda i: (0, 0)
    resident = lambda w: pl.BlockSpec(w.shape, const, pipeline_mode=pl.Buffered(1))
    pblk = lambda w: pl.BlockSpec((tm, w), lambda i: (jnp.minimum(i, n_p - 1), 0))
    sblk = lambda w: pl.BlockSpec((tm, w), lambda i: (jnp.maximum(i - n_p, 0), 0))
    extra = [] if final_norm_w is None else [final_norm_w.reshape(1, d)]
    return pl.pallas_call(
        functools.partial(_mix_kernel, n_p=n_p),
        grid=(r // tm,),
        in_specs=[pl.BlockSpec((tm, d), lambda i: (i, 0)),
                  pblk(D_INNER), sblk(D_INNER), pblk(ATT_DIM), sblk(ATT_DIM),
                  pl.BlockSpec((tm, d), lambda i: (i, C_GS // D_MODEL)),
                  pl.BlockSpec((tm, d), lambda i: (i, C_GA // D_MODEL)),
                  resident(w_sp), resident(w_ap), resident(w_o),
                  pl.BlockSpec((1, d), const), resident(w_up), resident(w_down)]
                 + [pl.BlockSpec((1, d), const)] * len(extra),
        out_specs=pl.BlockSpec((tm, d), lambda i: (i, 0)),
        out_shape=jax.ShapeDtypeStruct((r, d), F32),
        compiler_params=_params("parallel"),
        name="mix",
    )(h, ys_p, ys_s, ya_p, ya_s, proj, proj, w_sp, w_ap, w_o, norm_w.reshape(1, d), w_up, w_down, *extra)


def _softplus(x):
    return jnp.maximum(x, 0.0) + jnp.log1p(jnp.exp(-jnp.abs(x)))


def _conv_silu(xp_sc, x, w_ref, b_ref, t):
    xp_sc[pl.ds(8, t), :] = x
    acc = b_ref[...] + x * w_ref[CONV_W - 1:CONV_W, :]
    for k in range(CONV_W - 1):
        acc = acc + xp_sc[pl.ds(5 + k, t), :] * w_ref[k:k + 1, :]
    xp_sc[pl.ds(5, CONV_W - 1), :] = xp_sc[pl.ds(t + 5, CONV_W - 1), :]
    return acc * _sigmoid(acc)


def _ssd_kernel(z_ref, xs_ref, bc_ref, sm_ref, *rest, t_in, valid_len, has_h0, has_prev):
    h0_ref = rest[0] if has_h0 else None
    rest = rest[int(has_h0):]
    pxs_ref, pbc_ref, cwx_ref, cbx_ref, cwb_ref, cbb_ref, dtb_ref, aneg_ref, dskip_ref, normw_ref, expand_ref = rest[:11]
    y_ref, hout_ref, state_sc, xpx_sc, xpb_sc = rest[11 + int(has_prev):]
    t = SSD_CHUNK
    c = pl.program_id(1)
    hpg = SSM_HEADS // SSM_GROUPS
    pair = 2 * SSM_HEAD_DIM

    @pl.when(c == 0)
    def _():
        if has_h0:
            for g in range(SSM_GROUPS):
                state_sc[g] = jnp.concatenate(
                    [jnp.concatenate([h0_ref[g * hpg + e], h0_ref[g * hpg + e + 1]], axis=0).T
                     for e in range(0, hpg, 2)], axis=1)
        else:
            state_sc[...] = jnp.zeros(state_sc.shape, F32)
        xpx_sc[pl.ds(5, CONV_W - 1), :] = pxs_ref[...]
        xpb_sc[pl.ds(5, CONV_W - 1), :] = pbc_ref[...]

    def rows(ref):
        x = ref[...]
        if t_in < t:
            x = jnp.concatenate([x, jnp.zeros((t - t_in, x.shape[1]), x.dtype)], axis=0)
        return x

    xs = _conv_silu(xpx_sc, rows(xs_ref), cwx_ref, cbx_ref, t)
    bcm = _conv_silu(xpb_sc, rows(bc_ref), cwb_ref, cbb_ref, t)
    z = rows(z_ref)

    row = lax.broadcasted_iota(jnp.int32, (t, LANES), 0)
    n_valid = jnp.minimum(valid_len - c * t, t_in)
    dt = jnp.where(row < n_valid, _softplus(rows(sm_ref) + dtb_ref[...]), 0.0)
    a = dt * aneg_ref[...]
    ri = lax.broadcasted_iota(jnp.int32, (t, t), 0)
    ci = lax.broadcasted_iota(jnp.int32, (t, t), 1)
    causal = ri >= ci
    acum = _dot_exact(jnp.where(causal, 1.0, 0.0), a)
    acum_t = acum.T
    ex = jnp.dot(jnp.concatenate(_bf16_pieces(dt) + _bf16_pieces(acum), axis=0), expand_ref[...],
                 preferred_element_type=F32)
    dtx = ex[:t] + ex[t:2 * t] + ex[2 * t:3 * t]
    acx = ex[3 * t:4 * t] + ex[4 * t:5 * t] + ex[5 * t:]
    last = acx[t - 1:t, :]
    xdt = xs * dtx
    xw = xdt * jnp.exp(last - acx)
    eacx = jnp.exp(acx)
    elast = jnp.exp(last)

    gw = D_INNER // SSM_GROUPS
    hpg = SSM_HEADS // SSM_GROUPS
    ys = []
    for g in range(SSM_GROUPS):
        bg = bcm[:, g * SSM_STATE:(g + 1) * SSM_STATE]
        cg = bcm[:, (SSM_GROUPS + g) * SSM_STATE:(SSM_GROUPS + g + 1) * SSM_STATE]
        cb = _dot_nt(cg, bg)
        h_t = state_sc[g]
        yg = [_dot(cg, h_t) * eacx[:, g * gw:(g + 1) * gw]]
        intra = []
        for e in range(hpg):
            hd = g * hpg + e
            seg = acum[:, SM_DT + hd:SM_DT + hd + 1] - acum_t[SM_DT + hd:SM_DT + hd + 1, :]
            m = cb * jnp.exp(jnp.where(causal, seg, -jnp.inf))
            intra.append(_dot(m, xdt[:, hd * SSM_HEAD_DIM:(hd + 1) * SSM_HEAD_DIM]))
        ys.append(yg[0] + jnp.concatenate(intra, axis=1))
        state_sc[g] = h_t * elast[:, g * gw:(g + 1) * gw] + _dot(bg.T, xw[:, g * gw:(g + 1) * gw])
    y = jnp.concatenate(ys, axis=1) + xs * dskip_ref[...]
    y = y * (z * _sigmoid(z))
    outs = []
    for g in range(SSM_GROUPS):
        yg = y[:, g * gw:(g + 1) * gw]
        outs.append(yg * lax.rsqrt(jnp.mean(yg * yg, axis=-1, keepdims=True) + NORM_EPS))
    y = jnp.concatenate(outs, axis=1) * normw_ref[...]
    y_ref[...] = y[:t_in].astype(y_ref.dtype)

    @pl.when(c == pl.num_programs(1) - 1)
    def _():
        for g in range(SSM_GROUPS):
            st = state_sc[g]
            for e in range(0, hpg, 2):
                both = st[:, e * SSM_HEAD_DIM:e * SSM_HEAD_DIM + pair].T
                hout_ref[g * hpg + e] = both[:SSM_HEAD_DIM]
                hout_ref[g * hpg + e + 1] = both[SSM_HEAD_DIM:]


def _ssd(proj, row0, n_seq, t_in, n_chunks, valid_len, layer, depth, state_in, state_prev, prefix, lw, out_dtype):
    assert row0 % t_in == 0
    rb0 = row0 // t_in
    rowblk = lambda w, col: pl.BlockSpec((t_in, w), lambda s, c: (rb0 + s * n_chunks + c, col // w))
    per_seq = lambda shape: pl.BlockSpec((None,) + shape, lambda s, c: (s,) + (0,) * len(shape))
    const = lambda arr: pl.BlockSpec(arr.shape, lambda s, c: (0,) * arr.ndim)
    state_blk = pl.BlockSpec((None, None, SSM_HEADS, SSM_HEAD_DIM, SSM_STATE), lambda s, c: (layer, s, 0, 0, 0))
    gw = D_INNER // SSM_GROUPS
    pxs, pbc = prefix[:, :, :D_INNER], prefix[:, :, D_INNER:]
    consts = (lw["cwx"], lw["cbx"], lw["cwb"], lw["cbb"], lw["dtb"], lw["aneg"], lw["dskip"], lw["normw"], lw["expand"])
    has_h0, has_prev = state_in is not None, state_prev is not None
    operands = [proj, proj, proj, proj] + [state_in] * has_h0 + [pxs, pbc, *consts] + [state_prev] * has_prev
    in_specs = ([rowblk(D_INNER, C_Z), rowblk(D_INNER, C_XS), rowblk(BC_DIM, C_BC), rowblk(LANES, C_SM)]
                + [state_blk] * has_h0
                + [per_seq((CONV_W - 1, D_INNER)), per_seq((CONV_W - 1, BC_DIM))] + [const(a) for a in consts]
                + [pl.BlockSpec(memory_space=pl.ANY)] * has_prev)
    return pl.pallas_call(
        functools.partial(_ssd_kernel, t_in=t_in, valid_len=valid_len, has_h0=has_h0, has_prev=has_prev),
        grid=(n_seq, n_chunks),
        in_specs=in_specs,
        out_specs=[pl.BlockSpec((t_in, D_INNER), lambda s, c: (s * n_chunks + c, 0)), state_blk],
        out_shape=[jax.ShapeDtypeStruct((n_seq * n_chunks * t_in, D_INNER), out_dtype),
                   jax.ShapeDtypeStruct((depth, n_seq, SSM_HEADS, SSM_HEAD_DIM, SSM_STATE), F32)],
        scratch_shapes=[pltpu.VMEM((SSM_GROUPS, SSM_STATE, gw), F32),
                        pltpu.VMEM((SSD_CHUNK + 8, D_INNER), F32),
                        pltpu.VMEM((SSD_CHUNK + 8, BC_DIM), F32)],
        input_output_aliases={len(operands) - 1: 1} if has_prev else {},
        compiler_params=_params("parallel", "arbitrary"),
        name="ssd",
    )(*operands)


def _sort_key(x):
    bits = lax.bitcast_convert_type(x, jnp.int32)
    bits = jnp.where(bits == INT_MIN, 0, bits)
    return bits ^ ((bits >> 31) & 0x7FFFFFFF)


def _fold_rows(x, group=8):
    x = x.reshape(x.shape[0] // group, group, x.shape[1])
    while x.shape[0] > 1:
        half, odd = divmod(x.shape[0], 2)
        folded = x[:half] + x[half:2 * half]
        if odd:
            folded = jnp.concatenate([folded[:1] + x[2 * half:], folded[1:]], axis=0) if half > 1 else folded + x[2 * half:]
        x = folded
    return x[0]


def _count_t(keys_sc, n_chunks, pred):
    def body(c, cnt):
        return cnt + _fold_rows(jnp.where(pred(keys_sc[c]), 1.0, 0.0))

    cnt = lax.fori_loop(0, n_chunks, body, jnp.zeros((8, LANES), F32))
    return jnp.sum(cnt, axis=0, keepdims=True)


HALF_MIN = -(2 ** 30)
SEARCH_UNROLLS = tuple(range(1, 17))


def _search_bounds(nkc):
    return sorted({min(b, nkc) for b in SEARCH_UNROLLS} | {nkc})


def _search_chunks(n_chunks, nkc):
    bounds = _search_bounds(nkc)
    n = jnp.int32(bounds[-1])
    for b in reversed(bounds[:-1]):
        n = jnp.where(n_chunks <= b, b, n)
    return n


def _kth_halved_t(kh_sc, n, k):
    total = n * kh_sc.shape[1]

    def count_ge(cand):
        below = None
        for c in range(n):
            neg = _fold_rows((kh_sc[c] - cand) >> 31)
            below = neg if below is None else below + neg
        return total + jnp.sum(below.astype(F32), axis=0, keepdims=True)

    def bit_body(i, u):
        cand = u | lax.shift_left(jnp.int32(1), 30 - i)
        return jnp.where(count_ge(cand + HALF_MIN) >= k, cand, u)

    return lax.fori_loop(0, 31, bit_body, jnp.zeros((1, LANES), jnp.int32)) + HALF_MIN


def _kth_key_t(keys_sc, kh_sc, n_chunks, k):
    bounds = _search_bounds(kh_sc.shape[0])

    def build(i):
        if i == len(bounds) - 1:
            return lambda: _kth_halved_t(kh_sc, bounds[i], k)
        return lambda: lax.cond(n_chunks <= bounds[i], lambda: _kth_halved_t(kh_sc, bounds[i], k), build(i + 1))

    odd = build(0)() * 2 + 1
    return jnp.where(_count_t(keys_sc, n_chunks, lambda key: key >= odd) >= k, odd, odd - 1)


def _select_t(keys_sc, kh_sc, bias_sc, n_chunks, k):
    kc = keys_sc.shape[1]
    thr = _kth_key_t(keys_sc, kh_sc, n_chunks, k)
    n_ge = _count_t(keys_sc, n_chunks, lambda key: key >= thr)
    surplus = jnp.max(jnp.where((n_ge > k) & (thr != INT_MIN), 1.0, 0.0))

    @pl.when(surplus == 0.0)
    def _():
        floor = jnp.maximum(thr, INT_MIN + 1)

        def body(c, carry):
            bias_sc[c] = jnp.where(keys_sc[c] >= floor, 0.0, MASK_BIAS)
            return carry

        lax.fori_loop(0, n_chunks, body, 0)

    @pl.when(surplus != 0.0)
    def _():
        need = k - _count_t(keys_sc, n_chunks, lambda key: key > thr)
        ri = lax.broadcasted_iota(jnp.int32, (kc, kc), 0)
        ci = lax.broadcasted_iota(jnp.int32, (kc, kc), 1)
        lower = jnp.where(ci <= ri, 1.0, 0.0).astype(jnp.bfloat16)

        def body(c, run):
            key = keys_sc[c]
            eqf = jnp.where((key == thr) & (key != INT_MIN), 1.0, 0.0)
            rank = jnp.dot(lower, eqf.astype(jnp.bfloat16), preferred_element_type=F32) + run
            sel = (key > thr) | ((eqf > 0.0) & (rank <= need))
            bias_sc[c] = jnp.where(sel, 0.0, MASK_BIAS)
            return run + jnp.sum(_fold_rows(eqf), axis=0, keepdims=True)

        lax.fori_loop(0, n_chunks, body, jnp.zeros((1, LANES), F32))


def _select_s(keys_sc, kh_sc, bias_ref, k):
    n_chunks, rows, width = keys_sc.shape

    def count(pred):
        return jnp.sum(jnp.sum(jnp.where(pred(keys_sc[...]), 1.0, 0.0), axis=0), axis=1, keepdims=True)

    def bit_body(i, u):
        cand = u | lax.shift_left(jnp.int32(1), 30 - i)
        below = jnp.sum((kh_sc[...] - (cand + HALF_MIN)[None]) >> 31, axis=0).astype(F32)
        return jnp.where(n_chunks * width + jnp.sum(below, axis=1, keepdims=True) >= k, cand, u)

    odd = (lax.fori_loop(0, 31, bit_body, jnp.zeros((rows, 1), jnp.int32)) + HALF_MIN) * 2 + 1
    thr = jnp.where(count(lambda key: key >= odd[None]) >= k, odd, odd - 1)
    n_ge = count(lambda key: key >= thr[None])
    surplus = jnp.max(jnp.where((n_ge > k) & (thr != INT_MIN), 1.0, 0.0))

    @pl.when(surplus == 0.0)
    def _():
        floor = jnp.maximum(thr, INT_MIN + 1)
        bias_ref[...] = jnp.where(keys_sc[...] >= floor[None], 0.0, MASK_BIAS)

    @pl.when(surplus != 0.0)
    def _():
        need = k - count(lambda key: key > thr[None])
        ri = lax.broadcasted_iota(jnp.int32, (width, width), 0)
        ci = lax.broadcasted_iota(jnp.int32, (width, width), 1)
        upper = jnp.where(ri <= ci, 1.0, 0.0).astype(jnp.bfloat16)

        def body(c, run):
            key = keys_sc[c]
            eqf = jnp.where((key == thr) & (key != INT_MIN), 1.0, 0.0)
            rank = jnp.dot(eqf.astype(jnp.bfloat16), upper, preferred_element_type=F32) + run
            sel = (key > thr) | ((eqf > 0.0) & (rank <= need))
            bias_ref[c] = jnp.where(sel, 0.0, MASK_BIAS)
            return run + jnp.sum(eqf, axis=1, keepdims=True)

        lax.fori_loop(0, n_chunks, body, jnp.zeros((rows, 1), F32))


def _pattn_kernel(qi_ref, smq_ref, q_ref, smk_ref, k_ref, v_ref, o_ref,
                  kb_sc, kib_sc, vt_sc, keys_sc, kh_sc, bias_sc, qz_sc, m_sc, acc_sc, *, kc, lp, topk):
    j = pl.program_id(1)
    nq = Q_BLOCK
    nkc = kb_sc.shape[0]

    @pl.when(j == 0)
    def _():
        for c in range(nkc):
            r0 = c * kc
            n = min(kc, lp - r0)

            def chunk(ref):
                x = ref[r0:r0 + n, :]
                return x if n == kc else jnp.concatenate([x, jnp.zeros((kc - n, x.shape[1]), x.dtype)], axis=0)

            kb_sc[c] = chunk(k_ref).astype(MXU_DTYPE)
            kib_sc[c] = chunk(smk_ref).astype(MXU_DTYPE)
            v_t = chunk(v_ref).T
            pad = jnp.concatenate([jnp.ones((1, kc), F32), jnp.zeros((V_ROWS - HEAD_DIM - 1, kc), F32)], axis=0)
            vt_sc[c] = jnp.concatenate([x for hk in range(ATT_KV_HEADS)
                                        for x in (v_t[hk * HEAD_DIM:(hk + 1) * HEAD_DIM], pad)], axis=0).astype(MXU_DTYPE)

    n_chunks = ((j + 1) * nq - 1) // kc + 1

    qi_t = (qi_ref[...] * IDX_DIM ** -0.5).T
    qiz = jnp.concatenate([qi_t[h * IDX_DIM:(h + 1) * IDX_DIM] for h in range(IDX_HEADS)], axis=1)
    qiz = jnp.concatenate([qiz, jnp.zeros((LANES - IDX_DIM, IDX_HEADS * nq), F32)], axis=0).astype(MXU_DTYPE)
    smq_t = smq_ref[...].T
    wi_row = jnp.concatenate([smq_t[SM_WI + h:SM_WI + h + 1] for h in range(IDX_HEADS)], axis=1) * IDX_HEADS ** -0.5
    kpos = lax.broadcasted_iota(jnp.int32, (kc, nq), 0)
    qpos = j * nq + lax.broadcasted_iota(jnp.int32, (kc, nq), 1)

    def score_body(c, carry):
        s = jnp.maximum(jnp.dot(kib_sc[c], qiz, preferred_element_type=F32), 0.0) * wi_row
        acc = s[:, :nq]
        for h in range(1, IDX_HEADS):
            acc = acc + s[:, h * nq:(h + 1) * nq]
        key = jnp.where(c * kc + kpos <= qpos, _sort_key(acc), INT_MIN)
        keys_sc[c] = key
        kh_sc[c] = key >> 1
        return carry

    lax.fori_loop(0, n_chunks, score_body, 0)

    def blank_body(c, carry):
        kh_sc[c] = jnp.full((kc, nq), HALF_MIN, jnp.int32)
        return carry

    lax.fori_loop(n_chunks, _search_chunks(n_chunks, nkc), blank_body, 0)
    _select_t(keys_sc, kh_sc, bias_sc, n_chunks, topk)

    q_t = (q_ref[...] * (HEAD_DIM ** -0.5 * LOG2_E)).T
    gw = ATT_GROUP * nq
    zero = jnp.zeros((HEAD_DIM, gw), F32)
    for hk in range(ATT_KV_HEADS):
        own = jnp.concatenate([q_t[(hk * ATT_GROUP + g) * HEAD_DIM:(hk * ATT_GROUP + g + 1) * HEAD_DIM]
                               for g in range(ATT_GROUP)], axis=1)
        qz_sc[hk] = jnp.concatenate([own if i == hk else zero for i in range(ATT_KV_HEADS)],
                                    axis=0).astype(MXU_DTYPE)
    m_sc[...] = jnp.full(m_sc.shape, MASK_BIAS, F32)
    acc_sc[...] = jnp.zeros(acc_sc.shape, F32)

    def attend(chunks):
        stages = [(c, hk) for c in chunks for hk in range(ATT_KV_HEADS)]
        loaded = {}

        def operands(c):
            if id(c) not in loaded:
                loaded[id(c)] = (kb_sc[c], vt_sc[c], jnp.concatenate([bias_sc[c]] * ATT_GROUP, axis=1))
            return loaded[id(c)]

        def qk(t):
            c, hk = stages[t]
            kch, _, b4 = operands(c)
            return jnp.dot(kch, qz_sc[hk], preferred_element_type=F32) + b4

        def soft(t, s):
            hk = stages[t][1]
            m_old = m_sc[hk]
            m_new = jnp.maximum(m_old, jnp.max(s, axis=0, keepdims=True))
            m_sc[hk] = m_new
            return jnp.exp2(m_old - m_new), jnp.exp2(s - m_new).astype(MXU_DTYPE)

        def pv(t, alpha, p):
            c, hk = stages[t]
            vt = operands(c)[1]
            acc_sc[hk] = alpha * acc_sc[hk] + jnp.dot(vt[hk * V_ROWS:(hk + 1) * V_ROWS], p,
                                                      preferred_element_type=F32)

        n = len(stages)
        s = {0: qk(0), 1: qk(1)}
        ap = {}
        for t in range(n):
            ap[t] = soft(t, s.pop(t))
            if t + 2 < n:
                s[t + 2] = qk(t + 2)
            if t >= 1:
                pv(t - 1, *ap.pop(t - 1))
        pv(n - 1, *ap.pop(n - 1))

    def pair_body(i, carry):
        attend([2 * i, 2 * i + 1])
        return carry

    lax.fori_loop(0, n_chunks // 2, pair_body, 0)

    @pl.when(n_chunks % 2 == 1)
    def _():
        attend([n_chunks - 1])

    heads = []
    for hk in range(ATT_KV_HEADS):
        acc = acc_sc[hk]
        o = acc[:HEAD_DIM] / acc[HEAD_DIM:HEAD_DIM + 1]
        heads += [o[:, g * nq:(g + 1) * nq] for g in range(ATT_GROUP)]
    o_ref[...] = jnp.concatenate(heads, axis=0).T.astype(o_ref.dtype)


def _prompt_attention(proj, n_seq, lp, topk):
    nqb = lp // Q_BLOCK
    kc = 512
    nkc = -(-lp // kc)
    qblk = lambda w, col: pl.BlockSpec((Q_BLOCK, w), lambda b, j: (b * nqb + j, col // w))
    seqblk = lambda w, col: pl.BlockSpec((lp, w), lambda b, j: (b, col // w))
    gw = ATT_GROUP * Q_BLOCK
    return pl.pallas_call(
        functools.partial(_pattn_kernel, kc=kc, lp=lp, topk=topk),
        grid=(n_seq, nqb),
        in_specs=[qblk(QI_DIM, C_QI), qblk(LANES, C_SM), qblk(ATT_DIM, C_Q),
                  seqblk(LANES, C_SM), seqblk(KV_DIM, C_K), seqblk(KV_DIM, C_V)],
        out_specs=pl.BlockSpec((Q_BLOCK, ATT_DIM), lambda b, j: (b * nqb + j, 0)),
        out_shape=jax.ShapeDtypeStruct((n_seq * lp, ATT_DIM), MXU_DTYPE),
        scratch_shapes=[pltpu.VMEM((nkc, kc, KV_DIM), MXU_DTYPE), pltpu.VMEM((nkc, kc, LANES), MXU_DTYPE),
                        pltpu.VMEM((nkc, ATT_KV_HEADS * V_ROWS, kc), MXU_DTYPE),
                        pltpu.VMEM((nkc, kc, Q_BLOCK), jnp.int32), pltpu.VMEM((nkc, kc, Q_BLOCK), jnp.int32),
                        pltpu.VMEM((nkc, kc, Q_BLOCK), F32),
                        pltpu.VMEM((ATT_KV_HEADS, KV_DIM, gw), MXU_DTYPE),
                        pltpu.VMEM((ATT_KV_HEADS, 1, gw), F32),
                        pltpu.VMEM((ATT_KV_HEADS, V_ROWS, gw), F32)],
        compiler_params=_params("parallel", "arbitrary"),
        name="prompt_attention",
    )(proj, proj, proj, proj, proj, proj)


def _pad_rows(x, n):
    return jnp.concatenate([x, jnp.zeros((n - x.shape[0], x.shape[1]), x.dtype)], axis=0)


def _sselect_kernel(pt_ref, qi_ref, sm_ref, *rest, pages_per_step, n_pages, page, ns, topk):
    page_refs = rest[:pages_per_step]
    bias_ref, keys_sc, kh_sc = rest[pages_per_step:]
    g = pl.program_id(1)
    qi = (qi_ref[...] * IDX_DIM ** -0.5).astype(MXU_DTYPE)
    qs = jnp.concatenate([qi[:, h * IDX_DIM:(h + 1) * IDX_DIM] for h in range(IDX_HEADS)], axis=0)
    sm = sm_ref[...]
    wi = sm[:, SM_WI:SM_WI + IDX_HEADS] * IDX_HEADS ** -0.5

    def weigh(s):
        s = jnp.maximum(s, 0.0)
        acc = s[:ns] * wi[:, 0:1]
        for h in range(1, IDX_HEADS):
            acc = acc + s[h * ns:(h + 1) * ns] * wi[:, h:h + 1]
        return _sort_key(acc)

    keys = weigh(_dot(qs, jnp.concatenate([r[...] for r in page_refs], axis=1)))
    for i in range(pages_per_step):
        keys_sc[g * pages_per_step + i] = keys[:, i * page:(i + 1) * page]
        kh_sc[g * pages_per_step + i] = keys[:, i * page:(i + 1) * page] >> 1

    @pl.when(g == pl.num_programs(1) - 1)
    def _():
        qrow = lax.broadcasted_iota(jnp.int32, (ns, page), 0)
        lane = lax.broadcasted_iota(jnp.int32, (ns, page), 1)
        new = weigh(_dot_nt(qs, _pad_rows(sm[:, SM_KI:SM_KI + IDX_DIM], page)))
        new = jnp.where(lane <= qrow, new, INT_MIN)
        keys_sc[n_pages] = new
        kh_sc[n_pages] = new >> 1
        _select_s(keys_sc, kh_sc, bias_ref, topk)


def _sattn_kernel(pt_ref, q_ref, kn_ref, vn_ref, bias_ref, *rest, pages_per_step, n_pages, page, ns):
    k_refs = rest[:pages_per_step]
    v_refs = rest[pages_per_step:2 * pages_per_step]
    o_ref, qbd_sc, m_sc, l_sc, acc_sc = rest[2 * pages_per_step:]
    g = pl.program_id(1)

    @pl.when(g == 0)
    def _():
        q = q_ref[...] * HEAD_DIM ** -0.5
        rows = []
        for h in range(ATT_HEADS):
            hk = h // ATT_GROUP
            parts = [jnp.zeros((ns, HEAD_DIM), F32)] * ATT_KV_HEADS
            parts[hk] = q[:, h * HEAD_DIM:(h + 1) * HEAD_DIM]
            rows.append(jnp.concatenate(parts, axis=1))
        qbd_sc[...] = jnp.concatenate(rows, axis=0).astype(MXU_DTYPE)
        m_sc[...] = jnp.full(m_sc.shape, MASK_BIAS, F32)
        l_sc[...] = jnp.zeros(l_sc.shape, F32)
        acc_sc[...] = jnp.zeros(acc_sc.shape, F32)

    def attend(s, b, pv):
        s = s + jnp.concatenate([b] * ATT_HEADS, axis=0)
        m_old = m_sc[...]
        m_new = jnp.maximum(m_old, jnp.max(s, axis=1, keepdims=True))
        alpha = jnp.exp(m_old - m_new)
        p = jnp.exp(s - m_new)
        l_sc[...] = alpha * l_sc[...] + jnp.sum(p, axis=1, keepdims=True)
        acc_sc[...] = alpha * acc_sc[...] + pv(p)
        m_sc[...] = m_new

    kcat = jnp.concatenate([r[...] for r in k_refs], axis=1)
    vcat = jnp.concatenate([r[...] for r in v_refs], axis=1)
    bcat = jnp.concatenate([bias_ref[g * pages_per_step + i] for i in range(pages_per_step)], axis=1)
    attend(_dot(qbd_sc[...], kcat), bcat, lambda p: _dot_nt(p, vcat))

    @pl.when(g == pl.num_programs(1) - 1)
    def _():
        vn = _pad_rows(vn_ref[...], page)
        attend(_dot_nt(qbd_sc[...], _pad_rows(kn_ref[...], page)), bias_ref[n_pages], lambda p: _dot(p, vn))
        o = acc_sc[...] / l_sc[...]
        o_ref[...] = jnp.concatenate(
            [o[h * ns:(h + 1) * ns, (h // ATT_GROUP) * HEAD_DIM:(h // ATT_GROUP + 1) * HEAD_DIM]
             for h in range(ATT_HEADS)], axis=1)


def _key_minor(cache):
    nd = cache.ndim
    t = jnp.transpose(cache, (0, 1) + tuple(range(3, nd)) + (2,))
    return t.reshape(t.shape[:2] + (-1, t.shape[-1]))


def _sample_attention(proj, row0, page_table, ck_t, cv_t, cki_t, layer, ns, topk):
    bd, n_pages = page_table.shape
    page = cki_t.shape[3]
    assert page == LANES and row0 % ns == 0
    pps = max(d for d in range(1, 17) if n_pages % d == 0)
    ng = n_pages // pps
    rb0 = row0 // ns
    rowblk = lambda w, col: pl.BlockSpec((ns, w), lambda s, g, pt: (rb0 + s, col // w))
    pageblk = lambda w, i: pl.BlockSpec((None, None, w, page), lambda s, g, pt: (layer, pt[s, g * pps + i], 0, 0))
    biasblk = pl.BlockSpec((None, n_pages + 1, ns, page), lambda s, g, pt: (s, 0, 0, 0))
    static = dict(pages_per_step=pps, n_pages=n_pages, page=page, ns=ns)

    bias = pl.pallas_call(
        functools.partial(_sselect_kernel, topk=topk, **static),
        grid_spec=pltpu.PrefetchScalarGridSpec(
            num_scalar_prefetch=1, grid=(bd, ng),
            in_specs=[rowblk(QI_DIM, C_QI), rowblk(LANES, C_SM)] + [pageblk(IDX_DIM, i) for i in range(pps)],
            out_specs=biasblk,
            scratch_shapes=[pltpu.VMEM((n_pages + 1, ns, page), jnp.int32)] * 2),
        out_shape=jax.ShapeDtypeStruct((bd, n_pages + 1, ns, page), F32),
        compiler_params=_params("parallel", "arbitrary"),
        name="sample_select",
    )(page_table, proj, proj, *([cki_t] * pps))

    rows = ATT_HEADS * ns
    return pl.pallas_call(
        functools.partial(_sattn_kernel, **static),
        grid_spec=pltpu.PrefetchScalarGridSpec(
            num_scalar_prefetch=1, grid=(bd, ng),
            in_specs=[rowblk(ATT_DIM, C_Q), rowblk(KV_DIM, C_K), rowblk(KV_DIM, C_V), biasblk]
                     + [pageblk(KV_DIM, i) for i in range(pps)] * 2,
            out_specs=pl.BlockSpec((ns, ATT_DIM), lambda s, g, pt: (s, 0)),
            scratch_shapes=[pltpu.VMEM((rows, KV_DIM), MXU_DTYPE), pltpu.VMEM((rows, 1), F32),
                            pltpu.VMEM((rows, 1), F32), pltpu.VMEM((rows, KV_DIM), F32)]),
        out_shape=jax.ShapeDtypeStruct((bd * ns, ATT_DIM), F32),
        compiler_params=_params("parallel", "arbitrary"),
        name="sample_attention",
    )(page_table, proj, proj, proj, bias, *([ck_t] * pps), *([cv_t] * pps))


def _regroup_w_in(w):
    o_dt = D_INNER + D_INNER + BC_DIM
    o_q = o_dt + SSM_HEADS
    o_ki = o_q + ATT_DIM + 2 * KV_DIM + QI_DIM
    o_wi = o_ki + IDX_DIM
    o_g = o_wi + IDX_HEADS
    parts = [w[:, :o_dt], w[:, o_q:o_ki], w[:, o_g:o_g + 2 * D_MODEL], w[:, o_ki:o_wi], w[:, o_dt:o_q], w[:, o_wi:o_g]]
    used = sum(p.shape[1] for p in parts)
    parts.append(jnp.zeros((w.shape[0], PROJ_COLS - used), w.dtype))
    w = jnp.concatenate(parts, axis=1).astype(MXU_DTYPE)
    return w.reshape(w.shape[0], PROJ_COLS // PROJ_TN, PROJ_TN).transpose(1, 0, 2)


def _small_lanes(v):
    return jnp.zeros((1, LANES), F32).at[0, SM_DT:SM_DT + SSM_HEADS].set(v.astype(F32))


def _layer_consts(conv_w, conv_b, dt_bias, a_log, d_skip, ssm_norm_w):
    head_of_lane = jnp.arange(D_INNER) // SSM_HEAD_DIM
    expand = (jnp.arange(LANES)[:, None] == (SM_DT + head_of_lane)[None, :]).astype(jnp.bfloat16)
    return dict(
        cwx=conv_w[:, :D_INNER], cbx=conv_b[None, :D_INNER], cwb=conv_w[:, D_INNER:], cbb=conv_b[None, D_INNER:],
        dtb=_small_lanes(dt_bias), aneg=_small_lanes(-jnp.exp(a_log.astype(F32))),
        dskip=jnp.repeat(d_skip.astype(F32), SSM_HEAD_DIM)[None, :], normw=ssm_norm_w.astype(F32)[None, :],
        expand=expand)


def kernel(x_prompt, x_sample, cache_k, cache_v, cache_kidx, state_ssm, state_conv, page_table, meta_tokens, norm1_w, w_in, conv_w, conv_b, dt_bias, a_log, d_skip, ssm_norm_w, w_ssm_proj, w_attn_proj, w_out, norm2_w, w_up, w_down, final_norm_w):
    b, seq, d = x_prompt.shape
    bd, ns = x_sample.shape[:2]
    depth = w_in.shape[0]
    assert d == D_MODEL and ns % 8 == 0
    past = page_table.shape[1] * cache_kidx.shape[2]
    lv = N_META + seq
    lp = -(-lv // Q_BLOCK) * Q_BLOCK
    rp = b * lp
    rs = bd * ns
    r = -(-(rp + rs) // LANES) * LANES
    topk_p = min(TOPK_MAX, seq // 4)
    topk_s = min(TOPK_MAX, (past + ns) // 4)

    meta = meta_tokens.astype(F32)
    pad = jnp.zeros((lp - lv, d), F32)
    h = jnp.concatenate([piece for s in range(b) for piece in (meta, x_prompt[s], pad)]
                        + [x_sample.reshape(rs, d), jnp.zeros((r - rp - rs, d), F32)], axis=0)

    ck_t, cv_t, cki_t = _key_minor(cache_k), _key_minor(cache_v), _key_minor(cache_kidx)
    p_ssm = s_ssm = None
    zero_prefix = jnp.zeros((b, CONV_W - 1, D_INNER + BC_DIM), F32)
    outs = [[] for _ in range(10)]
    for l in range(depth):
        lw = _layer_consts(conv_w[l], conv_b[l], dt_bias[l], a_log[l], d_skip[l], ssm_norm_w[l])
        proj = _in_proj(h, norm1_w[l], _regroup_w_in(w_in[l]))

        yp, p_ssm = _ssd(proj, 0, b, SSD_CHUNK, lp // SSD_CHUNK, lv, l, depth, None, p_ssm, zero_prefix, lw, MXU_DTYPE)
        ysm, s_ssm = _ssd(proj, rp, bd, ns, 1, ns, l, depth, state_ssm, s_ssm, state_conv[l], lw, F32)
        ap = _prompt_attention(proj, b, lp, topk_p)
        asm = _sample_attention(proj, rp, page_table, ck_t, cv_t, cki_t, l, ns, topk_s)
        tail = r - rp - rs
        if tail:
            ysm = jnp.concatenate([ysm, jnp.zeros((tail, D_INNER), F32)], axis=0)
            asm = jnp.concatenate([asm, jnp.zeros((tail, ATT_DIM), F32)], axis=0)

        h = _mix(h, rp, yp, ysm, ap, asm, proj, w_ssm_proj[l].astype(MXU_DTYPE), w_attn_proj[l].astype(MXU_DTYPE),
                 w_out[l].astype(MXU_DTYPE), norm2_w[l], w_up[l].astype(MXU_DTYPE), w_down[l].astype(MXU_DTYPE),
                 final_norm_w if l == depth - 1 else None)

        def pcols(c0, w):
            return proj[:rp, c0:c0 + w].reshape(b, lp, w)[:, :lv]

        def scols(c0, w):
            return proj[rp:rp + rs, c0:c0 + w].reshape(bd, ns, w)

        xbc_w = D_INNER + BC_DIM
        p_conv = jnp.stack([proj[s * lp + lv - (CONV_W - 1):s * lp + lv, C_XS:C_XS + xbc_w] for s in range(b)])
        s_conv = jnp.concatenate([state_conv[l], scols(C_XS, xbc_w)], axis=1)[:, -(CONV_W - 1):]
        for acc, t in zip(outs, (
                pcols(C_K, KV_DIM).reshape(b, lv, ATT_KV_HEADS, HEAD_DIM),
                pcols(C_V, KV_DIM).reshape(b, lv, ATT_KV_HEADS, HEAD_DIM),
                pcols(C_SM + SM_KI, IDX_DIM),
                None,
                p_conv,
                scols(C_K, KV_DIM).reshape(bd, ns, ATT_KV_HEADS, HEAD_DIM),
                scols(C_V, KV_DIM).reshape(bd, ns, ATT_KV_HEADS, HEAD_DIM),
                scols(C_SM + SM_KI, IDX_DIM),
                None,
                s_conv)):
            acc.append(t)

    y = h
    y_prompt = jnp.stack([y[s * lp + N_META:s * lp + lv] for s in range(b)])
    y_sample = y[rp:rp + rs].reshape(bd, ns, d)
    stacked = [jnp.stack(o) if o[0] is not None else None for o in outs]
    stacked[3], stacked[8] = p_ssm, s_ssm
    return (y_prompt, y_sample) + tuple(stacked)
```

```python
import functools
import math

import jax
import jax.numpy as jnp
from jax import lax
from jax.experimental import pallas as pl
from jax.experimental.pallas import tpu as pltpu

F32 = jnp.float32
MXU_DTYPE = jnp.bfloat16

N_META = 16
NORM_EPS = 1e-6
SSM_HEAD_DIM = 64
SSM_GROUPS = 4
SSM_STATE = 128
CONV_W = 4
ATT_HEADS = 16
ATT_KV_HEADS = 4
ATT_GROUP = ATT_HEADS // ATT_KV_HEADS
HEAD_DIM = 64
IDX_HEADS = 8
IDX_DIM = 64
TOPK_MAX = 256

LANES = 128
Q_BLOCK = 128
SSD_CHUNK = 128
VMEM_LIMIT = 56 << 20
MASK_BIAS = -1e30
INT_MIN = -(2 ** 31)
LOG2_E = 1.4426950408889634
V_ROWS = HEAD_DIM + 16

D_MODEL = 1024
D_INNER = 2 * D_MODEL
BC_DIM = 2 * SSM_GROUPS * SSM_STATE
SSM_HEADS = D_INNER // SSM_HEAD_DIM
ATT_DIM = ATT_HEADS * HEAD_DIM
KV_DIM = ATT_KV_HEADS * HEAD_DIM
QI_DIM = IDX_HEADS * IDX_DIM
C_Z = 0
C_XS = C_Z + D_INNER
C_BC = C_XS + D_INNER
C_Q = C_BC + BC_DIM
C_K = C_Q + ATT_DIM
C_V = C_K + KV_DIM
C_QI = C_V + KV_DIM
C_GS = C_QI + QI_DIM
C_GA = C_GS + D_MODEL
C_SM = C_GA + D_MODEL
SM_KI = 0
SM_DT = IDX_DIM
SM_WI = IDX_DIM + SSM_HEADS
PROJ_TN = 512
PROJ_COLS = -(-(C_SM + LANES) // PROJ_TN) * PROJ_TN


def _divisor_tile(n, max_tile, align):
    best = None
    for t in range(align, min(n, max_tile) + 1, align):
        if n % t == 0:
            best = t
    assert best is not None, (n, max_tile, align)
    return best


def _params(*sem):
    return pltpu.CompilerParams(dimension_semantics=sem, vmem_limit_bytes=VMEM_LIMIT)


def _dot(a, b):
    return jnp.dot(a.astype(MXU_DTYPE), b.astype(MXU_DTYPE), preferred_element_type=F32)


def _dot_nt(a, b):
    return lax.dot_general(a.astype(MXU_DTYPE), b.astype(MXU_DTYPE), (((1,), (1,)), ((), ())),
                           preferred_element_type=F32)


def _bf16_pieces(x):
    hi = x.astype(jnp.bfloat16)
    r = x - hi.astype(F32)
    mid = r.astype(jnp.bfloat16)
    return [hi, mid, (r - mid.astype(F32)).astype(jnp.bfloat16)]


def _dot_exact(a, b):
    return jnp.dot(a, b, preferred_element_type=F32, precision=lax.Precision.HIGHEST)


def _sigmoid(x):
    return 0.5 * jnp.tanh(0.5 * x) + 0.5


def _row_rmsnorm(x, w):
    return x * lax.rsqrt(jnp.mean(x * x, axis=-1, keepdims=True) + NORM_EPS) * w


def _in_proj_kernel(x_ref, nw_ref, w_ref, o_ref, u_sc):
    j = pl.program_id(1)

    @pl.when(j == 0)
    def _():
        u_sc[...] = _row_rmsnorm(x_ref[...], nw_ref[...]).astype(u_sc.dtype)

    o_ref[...] = jnp.dot(u_sc[...], w_ref[j], preferred_element_type=F32)


def _in_proj(h, norm_w, w):
    r, k = h.shape
    nt = w.shape[0]
    tm = _divisor_tile(r, 1100, 16)
    return pl.pallas_call(
        _in_proj_kernel,
        grid=(r // tm, nt),
        in_specs=[pl.BlockSpec((tm, k), lambda i, j: (i, 0)), pl.BlockSpec((1, k), lambda i, j: (0, 0)),
                  pl.BlockSpec(w.shape, lambda i, j: (0, 0, 0), pipeline_mode=pl.Buffered(1))],
        out_specs=pl.BlockSpec((tm, PROJ_TN), lambda i, j: (i, j)),
        out_shape=jax.ShapeDtypeStruct((r, nt * PROJ_TN), F32),
        scratch_shapes=[pltpu.VMEM((tm, k), MXU_DTYPE)],
        compiler_params=_params("parallel", "arbitrary"),
        name="in_proj",
    )(h, norm_w.reshape(1, k), w)


def _mix_kernel(h_ref, ysp_ref, yss_ref, yap_ref, yas_ref, gs_ref, ga_ref, wsp_ref, wap_ref, wo_ref,
                nw_ref, wu_ref, wd_ref, *rest, n_p):
    prompt = pl.program_id(0) < n_p
    ys = jnp.where(prompt, ysp_ref[...], yss_ref[...].astype(MXU_DTYPE))
    ya = jnp.where(prompt, yap_ref[...], yas_ref[...].astype(MXU_DTYPE))
    a = jnp.dot(ys, wsp_ref[...], preferred_element_type=F32)
    b = jnp.dot(ya, wap_ref[...], preferred_element_type=F32)
    m = _sigmoid(gs_ref[...]) * a + _sigmoid(ga_ref[...]) * b
    x = h_ref[...] + _dot(m, wo_ref[...])
    f = _dot(_row_rmsnorm(x, nw_ref[...]), wu_ref[...])
    f = jnp.square(jnp.maximum(f, 0.0))
    y = x + _dot(f, wd_ref[...])
    if len(rest) == 2:
        fw_ref, o_ref = rest
        y = _row_rmsnorm(y, fw_ref[...])
    else:
        o_ref, = rest
    o_ref[...] = y


def _mix(h, rp, ys_p, ys_s, ya_p, ya_s, proj, w_sp, w_ap, w_o, norm_w, w_up, w_down, final_norm_w=None):
    r, d = h.shape
    tm = _divisor_tile(math.gcd(rp, r - rp), 256, 16)
    n_p = rp // tm
    const = lambda i: (0, 0)
    resident = lambda w: pl.BlockSpec(w.shape, const, pipeline_mode=pl.Buffered(1))
    pblk = lambda w: pl.BlockSpec((tm, w), lambda i: (jnp.minimum(i, n_p - 1), 0))
    sblk = lambda w: pl.BlockSpec((tm, w), lambda i: (jnp.maximum(i - n_p, 0), 0))
    extra = [] if final_norm_w is None else [final_norm_w.reshape(1, d)]
    return pl.pallas_call(
        functools.partial(_mix_kernel, n_p=n_p),
        grid=(r // tm,),
        in_specs=[pl.BlockSpec((tm, d), lambda i: (i, 0)),
                  pblk(D_INNER), sblk(D_INNER), pblk(ATT_DIM), sblk(ATT_DIM),
                  pl.BlockSpec((tm, d), lambda i: (i, C_GS // D_MODEL)),
                  pl.BlockSpec((tm, d), lambda i: (i, C_GA // D_MODEL)),
                  resident(w_sp), resident(w_ap), resident(w_o),
                  pl.BlockSpec((1, d), const), resident(w_up), resident(w_down)]
                 + [pl.BlockSpec((1, d), const)] * len(extra),
        out_specs=pl.BlockSpec((tm, d), lambda i: (i, 0)),
        out_shape=jax.ShapeDtypeStruct((r, d), F32),
        compiler_params=_params("parallel"),
        name="mix",
    )(h, ys_p, ys_s, ya_p, ya_s, proj, proj, w_sp, w_ap, w_o, norm_w.reshape(1, d), w_up, w_down, *extra)


def _softplus(x):
    return jnp.maximum(x, 0.0) + jnp.log1p(jnp.exp(-jnp.abs(x)))


def _conv_silu(xp_sc, x, w_ref, b_ref, t):
    xp_sc[pl.ds(8, t), :] = x
    acc = b_ref[...] + x * w_ref[CONV_W - 1:CONV_W, :]
    for k in range(CONV_W - 1):
        acc = acc + xp_sc[pl.ds(5 + k, t), :] * w_ref[k:k + 1, :]
    xp_sc[pl.ds(5, CONV_W - 1), :] = xp_sc[pl.ds(t + 5, CONV_W - 1), :]
    return acc * _sigmoid(acc)


def _ssd_kernel(z_ref, xs_ref, bc_ref, sm_ref, *rest, t_in, valid_len, has_h0, has_prev):
    h0_ref = rest[0] if has_h0 else None
    rest = rest[int(has_h0):]
    pxs_ref, pbc_ref, cwx_ref, cbx_ref, cwb_ref, cbb_ref, dtb_ref, aneg_ref, dskip_ref, normw_ref, expand_ref = rest[:11]
    y_ref, hout_ref, state_sc, xpx_sc, xpb_sc = rest[11 + int(has_prev):]
    t = SSD_CHUNK
    c = pl.program_id(1)
    hpg = SSM_HEADS // SSM_GROUPS
    pair = 2 * SSM_HEAD_DIM

    @pl.when(c == 0)
    def _():
        if has_h0:
            for g in range(SSM_GROUPS):
                state_sc[g] = jnp.concatenate(
                    [jnp.concatenate([h0_ref[g * hpg + e], h0_ref[g * hpg + e + 1]], axis=0).T
                     for e in range(0, hpg, 2)], axis=1)
        else:
            state_sc[...] = jnp.zeros(state_sc.shape, F32)
        xpx_sc[pl.ds(5, CONV_W - 1), :] = pxs_ref[...]
        xpb_sc[pl.ds(5, CONV_W - 1), :] = pbc_ref[...]

    def rows(ref):
        x = ref[...]
        if t_in < t:
            x = jnp.concatenate([x, jnp.zeros((t - t_in, x.shape[1]), x.dtype)], axis=0)
        return x

    xs = _conv_silu(xpx_sc, rows(xs_ref), cwx_ref, cbx_ref, t)
    bcm = _conv_silu(xpb_sc, rows(bc_ref), cwb_ref, cbb_ref, t)
    z = rows(z_ref)

    row = lax.broadcasted_iota(jnp.int32, (t, LANES), 0)
    n_valid = jnp.minimum(valid_len - c * t, t_in)
    dt = jnp.where(row < n_valid, _softplus(rows(sm_ref) + dtb_ref[...]), 0.0)
    a = dt * aneg_ref[...]
    ri = lax.broadcasted_iota(jnp.int32, (t, t), 0)
    ci = lax.broadcasted_iota(jnp.int32, (t, t), 1)
    causal = ri >= ci
    acum = _dot_exact(jnp.where(causal, 1.0, 0.0), a)
    acum_t = acum.T
    ex = jnp.dot(jnp.concatenate(_bf16_pieces(dt) + _bf16_pieces(acum), axis=0), expand_ref[...],
                 preferred_element_type=F32)
    dtx = ex[:t] + ex[t:2 * t] + ex[2 * t:3 * t]
    acx = ex[3 * t:4 * t] + ex[4 * t:5 * t] + ex[5 * t:]
    last = acx[t - 1:t, :]
    xdt = xs * dtx
    xw = xdt * jnp.exp(last - acx)
    eacx = jnp.exp(acx)
    elast = jnp.exp(last)

    gw = D_INNER // SSM_GROUPS
    hpg = SSM_HEADS // SSM_GROUPS
    ys = []
    for g in range(SSM_GROUPS):
        bg = bcm[:, g * SSM_STATE:(g + 1) * SSM_STATE]
        cg = bcm[:, (SSM_GROUPS + g) * SSM_STATE:(SSM_GROUPS + g + 1) * SSM_STATE]
        cb = _dot_nt(cg, bg)
        h_t = state_sc[g]
        yg = [_dot(cg, h_t) * eacx[:, g * gw:(g + 1) * gw]]
        intra = []
        for e in range(hpg):
            hd = g * hpg + e
            seg = acum[:, SM_DT + hd:SM_DT + hd + 1] - acum_t[SM_DT + hd:SM_DT + hd + 1, :]
            m = cb * jnp.exp(jnp.where(causal, seg, -jnp.inf))
            intra.append(_dot(m, xdt[:, hd * SSM_HEAD_DIM:(hd + 1) * SSM_HEAD_DIM]))
        ys.append(yg[0] + jnp.concatenate(intra, axis=1))
        state_sc[g] = h_t * elast[:, g * gw:(g + 1) * gw] + _dot(bg.T, xw[:, g * gw:(g + 1) * gw])
    y = jnp.concatenate(ys, axis=1) + xs * dskip_ref[...]
    y = y * (z * _sigmoid(z))
    outs = []
    for g in range(SSM_GROUPS):
        yg = y[:, g * gw:(g + 1) * gw]
        outs.append(yg * lax.rsqrt(jnp.mean(yg * yg, axis=-1, keepdims=True) + NORM_EPS))
    y = jnp.concatenate(outs, axis=1) * normw_ref[...]
    y_ref[...] = y[:t_in].astype(y_ref.dtype)

    @pl.when(c == pl.num_programs(1) - 1)
    def _():
        for g in range(SSM_GROUPS):
            st = state_sc[g]
            for e in range(0, hpg, 2):
                both = st[:, e * SSM_HEAD_DIM:e * SSM_HEAD_DIM + pair].T
                hout_ref[g * hpg + e] = both[:SSM_HEAD_DIM]
                hout_ref[g * hpg + e + 1] = both[SSM_HEAD_DIM:]


def _ssd(proj, row0, n_seq, t_in, n_chunks, valid_len, layer, depth, state_in, state_prev, prefix, lw, out_dtype):
    assert row0 % t_in == 0
    rb0 = row0 // t_in
    rowblk = lambda w, col: pl.BlockSpec((t_in, w), lambda s, c: (rb0 + s * n_chunks + c, col // w))
    per_seq = lambda shape: pl.BlockSpec((None,) + shape, lambda s, c: (s,) + (0,) * len(shape))
    const = lambda arr: pl.BlockSpec(arr.shape, lambda s, c: (0,) * arr.ndim)
    state_blk = pl.BlockSpec((None, None, SSM_HEADS, SSM_HEAD_DIM, SSM_STATE), lambda s, c: (layer, s, 0, 0, 0))
    gw = D_INNER // SSM_GROUPS
    pxs, pbc = prefix[:, :, :D_INNER], prefix[:, :, D_INNER:]
    consts = (lw["cwx"], lw["cbx"], lw["cwb"], lw["cbb"], lw["dtb"], lw["aneg"], lw["dskip"], lw["normw"], lw["expand"])
    has_h0, has_prev = state_in is not None, state_prev is not None
    operands = [proj, proj, proj, proj] + [state_in] * has_h0 + [pxs, pbc, *consts] + [state_prev] * has_prev
    in_specs = ([rowblk(D_INNER, C_Z), rowblk(D_INNER, C_XS), rowblk(BC_DIM, C_BC), rowblk(LANES, C_SM)]
                + [state_blk] * has_h0
                + [per_seq((CONV_W - 1, D_INNER)), per_seq((CONV_W - 1, BC_DIM))] + [const(a) for a in consts]
                + [pl.BlockSpec(memory_space=pl.ANY)] * has_prev)
    return pl.pallas_call(
        functools.partial(_ssd_kernel, t_in=t_in, valid_len=valid_len, has_h0=has_h0, has_prev=has_prev),
        grid=(n_seq, n_chunks),
        in_specs=in_specs,
        out_specs=[pl.BlockSpec((t_in, D_INNER), lambda s, c: (s * n_chunks + c, 0)), state_blk],
        out_shape=[jax.ShapeDtypeStruct((n_seq * n_chunks * t_in, D_INNER), out_dtype),
                   jax.ShapeDtypeStruct((depth, n_seq, SSM_HEADS, SSM_HEAD_DIM, SSM_STATE), F32)],
        scratch_shapes=[pltpu.VMEM((SSM_GROUPS, SSM_STATE, gw), F32),
                        pltpu.VMEM((SSD_CHUNK + 8, D_INNER), F32),
                        pltpu.VMEM((SSD_CHUNK + 8, BC_DIM), F32)],
        input_output_aliases={len(operands) - 1: 1} if has_prev else {},
        compiler_params=_params("parallel", "arbitrary"),
        name="ssd",
    )(*operands)


def _sort_key(x):
    bits = lax.bitcast_convert_type(x, jnp.int32)
    bits = jnp.where(bits == INT_MIN, 0, bits)
    return bits ^ ((bits >> 31) & 0x7FFFFFFF)


def _fold_rows(x, group=8):
    x = x.reshape(x.shape[0] // group, group, x.shape[1])
    while x.shape[0] > 1:
        half, odd = divmod(x.shape[0], 2)
        folded = x[:half] + x[half:2 * half]
        if odd:
            folded = jnp.concatenate([folded[:1] + x[2 * half:], folded[1:]], axis=0) if half > 1 else folded + x[2 * half:]
        x = folded
    return x[0]


def _count_t(keys_sc, n_chunks, pred):
    def body(c, cnt):
        return cnt + _fold_rows(jnp.where(pred(keys_sc[c]), 1.0, 0.0))

    cnt = lax.fori_loop(0, n_chunks, body, jnp.zeros((8, LANES), F32))
    return jnp.sum(cnt, axis=0, keepdims=True)


HALF_MIN = -(2 ** 30)
SEARCH_UNROLLS = tuple(range(1, 17))


def _search_bounds(nkc):
    return sorted({min(b, nkc) for b in SEARCH_UNROLLS} | {nkc})


def _search_chunks(n_chunks, nkc):
    bounds = _search_bounds(nkc)
    n = jnp.int32(bounds[-1])
    for b in reversed(bounds[:-1]):
        n = jnp.where(n_chunks <= b, b, n)
    return n


PLANE_ROWS = 32 * 8


def _bit_planes(rows):
    a = list(rows)
    j, m = 16, 0x0000FFFF
    while j:
        k = 0
        while k < 32:
            t = (a[k] ^ lax.shift_right_logical(a[k + j], jnp.int32(j))) & m
            a[k] = a[k] ^ t
            a[k + j] = a[k + j] ^ (t << j)
            k = (k + j + 1) & ~j
        j >>= 1
        m = (m ^ (m << j)) & 0xFFFFFFFF
    return [a[31 - b] for b in range(32)]


def _store_planes(planes_sc, group, key):
    planes = _bit_planes([key[8 * v:8 * v + 8] for v in range(32)])
    planes[31] = ~planes[31]
    for b in range(32):
        planes_sc[b, group] = planes[b]


def _kth_key_planes(planes_sc, n, k):
    def bit_body(i, carry):
        live, above, t = carry
        bit = 31 - i
        ones = live & planes_sc[bit][:n]
        tot = above + jnp.sum(jnp.sum(lax.population_count(ones), axis=0).astype(F32), axis=0, keepdims=True)
        take = tot >= k
        live = jnp.where(take, ones, live ^ ones)
        above = jnp.where(take, above, tot)
        return live, above, t | lax.shift_left(jnp.where(take, 1, 0), bit)

    init = (jnp.full((n, 8, LANES), -1, jnp.int32), jnp.zeros((1, LANES), F32), jnp.zeros((1, LANES), jnp.int32))
    return lax.fori_loop(0, 32, bit_body, init)[2] ^ INT_MIN


def _kth_key_t(planes_sc, n_chunks, nkc, k):
    bounds = _search_bounds(nkc)
    per_chunk = planes_sc.shape[1] // nkc

    def build(i):
        if i == len(bounds) - 1:
            return lambda: _kth_key_planes(planes_sc, bounds[i] * per_chunk, k)
        return lambda: lax.cond(n_chunks <= bounds[i], lambda: _kth_key_planes(planes_sc, bounds[i] * per_chunk, k),
                                build(i + 1))

    return build(0)()


def _select_t(keys_sc, planes_sc, bias_sc, n_chunks, k):
    kc = keys_sc.shape[1]
    thr = _kth_key_t(planes_sc, n_chunks, keys_sc.shape[0], k)
    n_ge = _count_t(keys_sc, n_chunks, lambda key: key >= thr)
    surplus = jnp.max(jnp.where((n_ge > k) & (thr != INT_MIN), 1.0, 0.0))

    @pl.when(surplus == 0.0)
    def _():
        floor = jnp.maximum(thr, INT_MIN + 1)

        def body(c, carry):
            bias_sc[c] = jnp.where(keys_sc[c] >= floor, 0.0, MASK_BIAS)
            return carry

        lax.fori_loop(0, n_chunks, body, 0)

    @pl.when(surplus != 0.0)
    def _():
        need = k - _count_t(keys_sc, n_chunks, lambda key: key > thr)
        ri = lax.broadcasted_iota(jnp.int32, (kc, kc), 0)
        ci = lax.broadcasted_iota(jnp.int32, (kc, kc), 1)
        lower = jnp.where(ci <= ri, 1.0, 0.0).astype(jnp.bfloat16)

        def body(c, run):
            key = keys_sc[c]
            eqf = jnp.where((key == thr) & (key != INT_MIN), 1.0, 0.0)
            rank = jnp.dot(lower, eqf.astype(jnp.bfloat16), preferred_element_type=F32) + run
            sel = (key > thr) | ((eqf > 0.0) & (rank <= need))
            bias_sc[c] = jnp.where(sel, 0.0, MASK_BIAS)
            return run + jnp.sum(_fold_rows(eqf), axis=0, keepdims=True)

        lax.fori_loop(0, n_chunks, body, jnp.zeros((1, LANES), F32))


def _select_s(keys_sc, kh_sc, bias_ref, k):
    n_chunks, rows, width = keys_sc.shape

    def count(pred):
        return jnp.sum(jnp.sum(jnp.where(pred(keys_sc[...]), 1.0, 0.0), axis=0), axis=1, keepdims=True)

    def bit_body(i, u):
        cand = u | lax.shift_left(jnp.int32(1), 30 - i)
        below = jnp.sum((kh_sc[...] - (cand + HALF_MIN)[None]) >> 31, axis=0).astype(F32)
        return jnp.where(n_chunks * width + jnp.sum(below, axis=1, keepdims=True) >= k, cand, u)

    odd = (lax.fori_loop(0, 31, bit_body, jnp.zeros((rows, 1), jnp.int32)) + HALF_MIN) * 2 + 1
    thr = jnp.where(count(lambda key: key >= odd[None]) >= k, odd, odd - 1)
    n_ge = count(lambda key: key >= thr[None])
    surplus = jnp.max(jnp.where((n_ge > k) & (thr != INT_MIN), 1.0, 0.0))

    @pl.when(surplus == 0.0)
    def _():
        floor = jnp.maximum(thr, INT_MIN + 1)
        bias_ref[...] = jnp.where(keys_sc[...] >= floor[None], 0.0, MASK_BIAS)

    @pl.when(surplus != 0.0)
    def _():
        need = k - count(lambda key: key > thr[None])
        ri = lax.broadcasted_iota(jnp.int32, (width, width), 0)
        ci = lax.broadcasted_iota(jnp.int32, (width, width), 1)
        upper = jnp.where(ri <= ci, 1.0, 0.0).astype(jnp.bfloat16)

        def body(c, run):
            key = keys_sc[c]
            eqf = jnp.where((key == thr) & (key != INT_MIN), 1.0, 0.0)
            rank = jnp.dot(eqf.astype(jnp.bfloat16), upper, preferred_element_type=F32) + run
            sel = (key > thr) | ((eqf > 0.0) & (rank <= need))
            bias_ref[c] = jnp.where(sel, 0.0, MASK_BIAS)
            return run + jnp.sum(eqf, axis=1, keepdims=True)

        lax.fori_loop(0, n_chunks, body, jnp.zeros((rows, 1), F32))


def _pattn_kernel(qi_ref, smq_ref, q_ref, smk_ref, k_ref, v_ref, o_ref,
                  kb_sc, kib_sc, vt_sc, keys_sc, planes_sc, bias_sc, qz_sc, m_sc, acc_sc, *, kc, lp, topk):
    j = pl.program_id(1)
    nq = Q_BLOCK
    nkc = kb_sc.shape[0]

    @pl.when(j == 0)
    def _():
        for c in range(nkc):
            r0 = c * kc
            n = min(kc, lp - r0)

            def chunk(ref):
                x = ref[r0:r0 + n, :]
                return x if n == kc else jnp.concatenate([x, jnp.zeros((kc - n, x.shape[1]), x.dtype)], axis=0)

            kb_sc[c] = chunk(k_ref).astype(MXU_DTYPE)
            kib_sc[c] = chunk(smk_ref).astype(MXU_DTYPE)
            v_t = chunk(v_ref).T
            pad = jnp.concatenate([jnp.ones((1, kc), F32), jnp.zeros((V_ROWS - HEAD_DIM - 1, kc), F32)], axis=0)
            vt_sc[c] = jnp.concatenate([x for hk in range(ATT_KV_HEADS)
                                        for x in (v_t[hk * HEAD_DIM:(hk + 1) * HEAD_DIM], pad)], axis=0).astype(MXU_DTYPE)

    n_chunks = ((j + 1) * nq - 1) // kc + 1

    qi_t = (qi_ref[...] * IDX_DIM ** -0.5).T
    qiz = jnp.concatenate([qi_t[h * IDX_DIM:(h + 1) * IDX_DIM] for h in range(IDX_HEADS)], axis=1)
    qiz = jnp.concatenate([qiz, jnp.zeros((LANES - IDX_DIM, IDX_HEADS * nq), F32)], axis=0).astype(MXU_DTYPE)
    smq_t = smq_ref[...].T
    wi_row = jnp.concatenate([smq_t[SM_WI + h:SM_WI + h + 1] for h in range(IDX_HEADS)], axis=1) * IDX_HEADS ** -0.5
    kpos = lax.broadcasted_iota(jnp.int32, (kc, nq), 0)
    qpos = j * nq + lax.broadcasted_iota(jnp.int32, (kc, nq), 1)

    def score_body(c, carry):
        s = jnp.maximum(jnp.dot(kib_sc[c], qiz, preferred_element_type=F32), 0.0) * wi_row
        acc = s[:, :nq]
        for h in range(1, IDX_HEADS):
            acc = acc + s[:, h * nq:(h + 1) * nq]
        key = jnp.where(c * kc + kpos <= qpos, _sort_key(acc), INT_MIN)
        keys_sc[c] = key
        for g in range(kc // PLANE_ROWS):
            _store_planes(planes_sc, c * (kc // PLANE_ROWS) + g, key[g * PLANE_ROWS:(g + 1) * PLANE_ROWS])
        return carry

    lax.fori_loop(0, n_chunks, score_body, 0)

    def blank_body(c, carry):
        for g in range(kc // PLANE_ROWS):
            for b in range(32):
                planes_sc[b, c * (kc // PLANE_ROWS) + g] = jnp.zeros((8, nq), jnp.int32)
        return carry

    lax.fori_loop(n_chunks, _search_chunks(n_chunks, nkc), blank_body, 0)
    _select_t(keys_sc, planes_sc, bias_sc, n_chunks, topk)

    q_t = (q_ref[...] * (HEAD_DIM ** -0.5 * LOG2_E)).T
    gw = ATT_GROUP * nq
    zero = jnp.zeros((HEAD_DIM, gw), F32)
    for hk in range(ATT_KV_HEADS):
        own = jnp.concatenate([q_t[(hk * ATT_GROUP + g) * HEAD_DIM:(hk * ATT_GROUP + g + 1) * HEAD_DIM]
                               for g in range(ATT_GROUP)], axis=1)
        qz_sc[hk] = jnp.concatenate([own if i == hk else zero for i in range(ATT_KV_HEADS)],
                                    axis=0).astype(MXU_DTYPE)
    m_sc[...] = jnp.full(m_sc.shape, MASK_BIAS, F32)
    acc_sc[...] = jnp.zeros(acc_sc.shape, F32)

    def attend(chunks):
        stages = [(c, hk) for c in chunks for hk in range(ATT_KV_HEADS)]
        loaded = {}

        def operands(c):
            if id(c) not in loaded:
                loaded[id(c)] = (kb_sc[c], vt_sc[c], jnp.concatenate([bias_sc[c]] * ATT_GROUP, axis=1))
            return loaded[id(c)]

        def qk(t):
            c, hk = stages[t]
            kch, _, b4 = operands(c)
            return jnp.dot(kch, qz_sc[hk], preferred_element_type=F32) + b4

        def soft(t, s):
            hk = stages[t][1]
            m_old = m_sc[hk]
            m_new = jnp.maximum(m_old, jnp.max(s, axis=0, keepdims=True))
            m_sc[hk] = m_new
            return jnp.exp2(m_old - m_new), jnp.exp2(s - m_new).astype(MXU_DTYPE)

        def pv(t, alpha, p):
            c, hk = stages[t]
            vt = operands(c)[1]
            acc_sc[hk] = alpha * acc_sc[hk] + jnp.dot(vt[hk * V_ROWS:(hk + 1) * V_ROWS], p,
                                                      preferred_element_type=F32)

        n = len(stages)
        s = {0: qk(0), 1: qk(1)}
        ap = {}
        for t in range(n):
            ap[t] = soft(t, s.pop(t))
            if t + 2 < n:
                s[t + 2] = qk(t + 2)
            if t >= 1:
                pv(t - 1, *ap.pop(t - 1))
        pv(n - 1, *ap.pop(n - 1))

    def pair_body(i, carry):
        attend([2 * i, 2 * i + 1])
        return carry

    lax.fori_loop(0, n_chunks // 2, pair_body, 0)

    @pl.when(n_chunks % 2 == 1)
    def _():
        attend([n_chunks - 1])

    heads = []
    for hk in range(ATT_KV_HEADS):
        acc = acc_sc[hk]
        o = acc[:HEAD_DIM] / acc[HEAD_DIM:HEAD_DIM + 1]
        heads += [o[:, g * nq:(g + 1) * nq] for g in range(ATT_GROUP)]
    o_ref[...] = jnp.concatenate(heads, axis=0).T.astype(o_ref.dtype)


def _prompt_attention(proj, n_seq, lp, topk):
    nqb = lp // Q_BLOCK
    kc = 512
    nkc = -(-lp // kc)
    qblk = lambda w, col: pl.BlockSpec((Q_BLOCK, w), lambda b, j: (b * nqb + j, col // w))
    seqblk = lambda w, col: pl.BlockSpec((lp, w), lambda b, j: (b, col // w))
    gw = ATT_GROUP * Q_BLOCK
    return pl.pallas_call(
        functools.partial(_pattn_kernel, kc=kc, lp=lp, topk=topk),
        grid=(n_seq, nqb),
        in_specs=[qblk(QI_DIM, C_QI), qblk(LANES, C_SM), qblk(ATT_DIM, C_Q),
                  seqblk(LANES, C_SM), seqblk(KV_DIM, C_K), seqblk(KV_DIM, C_V)],
        out_specs=pl.BlockSpec((Q_BLOCK, ATT_DIM), lambda b, j: (b * nqb + j, 0)),
        out_shape=jax.ShapeDtypeStruct((n_seq * lp, ATT_DIM), MXU_DTYPE),
        scratch_shapes=[pltpu.VMEM((nkc, kc, KV_DIM), MXU_DTYPE), pltpu.VMEM((nkc, kc, LANES), MXU_DTYPE),
                        pltpu.VMEM((nkc, ATT_KV_HEADS * V_ROWS, kc), MXU_DTYPE),
                        pltpu.VMEM((nkc, kc, Q_BLOCK), jnp.int32),
                        pltpu.VMEM((32, nkc * (kc // PLANE_ROWS), 8, Q_BLOCK), jnp.int32),
                        pltpu.VMEM((nkc, kc, Q_BLOCK), F32),
                        pltpu.VMEM((ATT_KV_HEADS, KV_DIM, gw), MXU_DTYPE),
                        pltpu.VMEM((ATT_KV_HEADS, 1, gw), F32),
                        pltpu.VMEM((ATT_KV_HEADS, V_ROWS, gw), F32)],
        compiler_params=_params("parallel", "arbitrary"),
        name="prompt_attention",
    )(proj, proj, proj, proj, proj, proj)


def _pad_rows(x, n):
    return jnp.concatenate([x, jnp.zeros((n - x.shape[0], x.shape[1]), x.dtype)], axis=0)


def _sselect_kernel(pt_ref, qi_ref, sm_ref, *rest, pages_per_step, n_pages, page, ns, topk):
    page_refs = rest[:pages_per_step]
    bias_ref, keys_sc, kh_sc = rest[pages_per_step:]
    g = pl.program_id(1)
    qi = (qi_ref[...] * IDX_DIM ** -0.5).astype(MXU_DTYPE)
    qs = jnp.concatenate([qi[:, h * IDX_DIM:(h + 1) * IDX_DIM] for h in range(IDX_HEADS)], axis=0)
    sm = sm_ref[...]
    wi = sm[:, SM_WI:SM_WI + IDX_HEADS] * IDX_HEADS ** -0.5

    def weigh(s):
        s = jnp.maximum(s, 0.0)
        acc = s[:ns] * wi[:, 0:1]
        for h in range(1, IDX_HEADS):
            acc = acc + s[h * ns:(h + 1) * ns] * wi[:, h:h + 1]
        return _sort_key(acc)

    keys = weigh(_dot(qs, jnp.concatenate([r[...] for r in page_refs], axis=1)))
    for i in range(pages_per_step):
        keys_sc[g * pages_per_step + i] = keys[:, i * page:(i + 1) * page]
        kh_sc[g * pages_per_step + i] = keys[:, i * page:(i + 1) * page] >> 1

    @pl.when(g == pl.num_programs(1) - 1)
    def _():
        qrow = lax.broadcasted_iota(jnp.int32, (ns, page), 0)
        lane = lax.broadcasted_iota(jnp.int32, (ns, page), 1)
        new = weigh(_dot_nt(qs, _pad_rows(sm[:, SM_KI:SM_KI + IDX_DIM], page)))
        new = jnp.where(lane <= qrow, new, INT_MIN)
        keys_sc[n_pages] = new
        kh_sc[n_pages] = new >> 1
        _select_s(keys_sc, kh_sc, bias_ref, topk)


def _sattn_kernel(pt_ref, q_ref, kn_ref, vn_ref, bias_ref, *rest, pages_per_step, n_pages, page, ns):
    k_refs = rest[:pages_per_step]
    v_refs = rest[pages_per_step:2 * pages_per_step]
    o_ref, qbd_sc, m_sc, l_sc, acc_sc = rest[2 * pages_per_step:]
    g = pl.program_id(1)

    @pl.when(g == 0)
    def _():
        q = q_ref[...] * HEAD_DIM ** -0.5
        rows = []
        for h in range(ATT_HEADS):
            hk = h // ATT_GROUP
            parts = [jnp.zeros((ns, HEAD_DIM), F32)] * ATT_KV_HEADS
            parts[hk] = q[:, h * HEAD_DIM:(h + 1) * HEAD_DIM]
            rows.append(jnp.concatenate(parts, axis=1))
        qbd_sc[...] = jnp.concatenate(rows, axis=0).astype(MXU_DTYPE)
        m_sc[...] = jnp.full(m_sc.shape, MASK_BIAS, F32)
        l_sc[...] = jnp.zeros(l_sc.shape, F32)
        acc_sc[...] = jnp.zeros(acc_sc.shape, F32)

    def attend(s, b, pv):
        s = s + jnp.concatenate([b] * ATT_HEADS, axis=0)
        m_old = m_sc[...]
        m_new = jnp.maximum(m_old, jnp.max(s, axis=1, keepdims=True))
        alpha = jnp.exp(m_old - m_new)
        p = jnp.exp(s - m_new)
        l_sc[...] = alpha * l_sc[...] + jnp.sum(p, axis=1, keepdims=True)
        acc_sc[...] = alpha * acc_sc[...] + pv(p)
        m_sc[...] = m_new

    kcat = jnp.concatenate([r[...] for r in k_refs], axis=1)
    vcat = jnp.concatenate([r[...] for r in v_refs], axis=1)
    bcat = jnp.concatenate([bias_ref[g * pages_per_step + i] for i in range(pages_per_step)], axis=1)
    attend(_dot(qbd_sc[...], kcat), bcat, lambda p: _dot_nt(p, vcat))

    @pl.when(g == pl.num_programs(1) - 1)
    def _():
        vn = _pad_rows(vn_ref[...], page)
        attend(_dot_nt(qbd_sc[...], _pad_rows(kn_ref[...], page)), bias_ref[n_pages], lambda p: _dot(p, vn))
        o = acc_sc[...] / l_sc[...]
        o_ref[...] = jnp.concatenate(
            [o[h * ns:(h + 1) * ns, (h // ATT_GROUP) * HEAD_DIM:(h // ATT_GROUP + 1) * HEAD_DIM]
             for h in range(ATT_HEADS)], axis=1)


def _key_minor(cache):
    nd = cache.ndim
    t = jnp.transpose(cache, (0, 1) + tuple(range(3, nd)) + (2,))
    return t.reshape(t.shape[:2] + (-1, t.shape[-1]))


def _sample_attention(proj, row0, page_table, ck_t, cv_t, cki_t, layer, ns, topk):
    bd, n_pages = page_table.shape
    page = cki_t.shape[3]
    assert page == LANES and row0 % ns == 0
    pps = max(d for d in range(1, 17) if n_pages % d == 0)
    ng = n_pages // pps
    rb0 = row0 // ns
    rowblk = lambda w, col: pl.BlockSpec((ns, w), lambda s, g, pt: (rb0 + s, col // w))
    pageblk = lambda w, i: pl.BlockSpec((None, None, w, page), lambda s, g, pt: (layer, pt[s, g * pps + i], 0, 0))
    biasblk = pl.BlockSpec((None, n_pages + 1, ns, page), lambda s, g, pt: (s, 0, 0, 0))
    static = dict(pages_per_step=pps, n_pages=n_pages, page=page, ns=ns)

    bias = pl.pallas_call(
        functools.partial(_sselect_kernel, topk=topk, **static),
        grid_spec=pltpu.PrefetchScalarGridSpec(
            num_scalar_prefetch=1, grid=(bd, ng),
            in_specs=[rowblk(QI_DIM, C_QI), rowblk(LANES, C_SM)] + [pageblk(IDX_DIM, i) for i in range(pps)],
            out_specs=biasblk,
            scratch_shapes=[pltpu.VMEM((n_pages + 1, ns, page), jnp.int32)] * 2),
        out_shape=jax.ShapeDtypeStruct((bd, n_pages + 1, ns, page), F32),
        compiler_params=_params("parallel", "arbitrary"),
        name="sample_select",
    )(page_table, proj, proj, *([cki_t] * pps))

    rows = ATT_HEADS * ns
    return pl.pallas_call(
        functools.partial(_sattn_kernel, **static),
        grid_spec=pltpu.PrefetchScalarGridSpec(
            num_scalar_prefetch=1, grid=(bd, ng),
            in_specs=[rowblk(ATT_DIM, C_Q), rowblk(KV_DIM, C_K), rowblk(KV_DIM, C_V), biasblk]
                     + [pageblk(KV_DIM, i) for i in range(pps)] * 2,
            out_specs=pl.BlockSpec((ns, ATT_DIM), lambda s, g, pt: (s, 0)),
            scratch_shapes=[pltpu.VMEM((rows, KV_DIM), MXU_DTYPE), pltpu.VMEM((rows, 1), F32),
                            pltpu.VMEM((rows, 1), F32), pltpu.VMEM((rows, KV_DIM), F32)]),
        out_shape=jax.ShapeDtypeStruct((bd * ns, ATT_DIM), F32),
        compiler_params=_params("parallel", "arbitrary"),
        name="sample_attention",
    )(page_table, proj, proj, proj, bias, *([ck_t] * pps), *([cv_t] * pps))


def _regroup_w_in(w):
    o_dt = D_INNER + D_INNER + BC_DIM
    o_q = o_dt + SSM_HEADS
    o_ki = o_q + ATT_DIM + 2 * KV_DIM + QI_DIM
    o_wi = o_ki + IDX_DIM
    o_g = o_wi + IDX_HEADS
    parts = [w[:, :o_dt], w[:, o_q:o_ki], w[:, o_g:o_g + 2 * D_MODEL], w[:, o_ki:o_wi], w[:, o_dt:o_q], w[:, o_wi:o_g]]
    used = sum(p.shape[1] for p in parts)
    parts.append(jnp.zeros((w.shape[0], PROJ_COLS - used), w.dtype))
    w = jnp.concatenate(parts, axis=1).astype(MXU_DTYPE)
    return w.reshape(w.shape[0], PROJ_COLS // PROJ_TN, PROJ_TN).transpose(1, 0, 2)


def _small_lanes(v):
    return jnp.zeros((1, LANES), F32).at[0, SM_DT:SM_DT + SSM_HEADS].set(v.astype(F32))


def _layer_consts(conv_w, conv_b, dt_bias, a_log, d_skip, ssm_norm_w):
    head_of_lane = jnp.arange(D_INNER) // SSM_HEAD_DIM
    expand = (jnp.arange(LANES)[:, None] == (SM_DT + head_of_lane)[None, :]).astype(jnp.bfloat16)
    return dict(
        cwx=conv_w[:, :D_INNER], cbx=conv_b[None, :D_INNER], cwb=conv_w[:, D_INNER:], cbb=conv_b[None, D_INNER:],
        dtb=_small_lanes(dt_bias), aneg=_small_lanes(-jnp.exp(a_log.astype(F32))),
        dskip=jnp.repeat(d_skip.astype(F32), SSM_HEAD_DIM)[None, :], normw=ssm_norm_w.astype(F32)[None, :],
        expand=expand)


def kernel(x_prompt, x_sample, cache_k, cache_v, cache_kidx, state_ssm, state_conv, page_table, meta_tokens, norm1_w, w_in, conv_w, conv_b, dt_bias, a_log, d_skip, ssm_norm_w, w_ssm_proj, w_attn_proj, w_out, norm2_w, w_up, w_down, final_norm_w):
    b, seq, d = x_prompt.shape
    bd, ns = x_sample.shape[:2]
    depth = w_in.shape[0]
    assert d == D_MODEL and ns % 8 == 0
    past = page_table.shape[1] * cache_kidx.shape[2]
    lv = N_META + seq
    lp = -(-lv // Q_BLOCK) * Q_BLOCK
    rp = b * lp
    rs = bd * ns
    r = -(-(rp + rs) // LANES) * LANES
    topk_p = min(TOPK_MAX, seq // 4)
    topk_s = min(TOPK_MAX, (past + ns) // 4)

    meta = meta_tokens.astype(F32)
    pad = jnp.zeros((lp - lv, d), F32)
    h = jnp.concatenate([piece for s in range(b) for piece in (meta, x_prompt[s], pad)]
                        + [x_sample.reshape(rs, d), jnp.zeros((r - rp - rs, d), F32)], axis=0)

    ck_t, cv_t, cki_t = _key_minor(cache_k), _key_minor(cache_v), _key_minor(cache_kidx)
    p_ssm = s_ssm = None
    zero_prefix = jnp.zeros((b, CONV_W - 1, D_INNER + BC_DIM), F32)
    outs = [[] for _ in range(10)]
    for l in range(depth):
        lw = _layer_consts(conv_w[l], conv_b[l], dt_bias[l], a_log[l], d_skip[l], ssm_norm_w[l])
        proj = _in_proj(h, norm1_w[l], _regroup_w_in(w_in[l]))

        yp, p_ssm = _ssd(proj, 0, b, SSD_CHUNK, lp // SSD_CHUNK, lv, l, depth, None, p_ssm, zero_prefix, lw, MXU_DTYPE)
        ysm, s_ssm = _ssd(proj, rp, bd, ns, 1, ns, l, depth, state_ssm, s_ssm, state_conv[l], lw, F32)
        ap = _prompt_attention(proj, b, lp, topk_p)
        asm = _sample_attention(proj, rp, page_table, ck_t, cv_t, cki_t, l, ns, topk_s)
        tail = r - rp - rs
        if tail:
            ysm = jnp.concatenate([ysm, jnp.zeros((tail, D_INNER), F32)], axis=0)
            asm = jnp.concatenate([asm, jnp.zeros((tail, ATT_DIM), F32)], axis=0)

        h = _mix(h, rp, yp, ysm, ap, asm, proj, w_ssm_proj[l].astype(MXU_DTYPE), w_attn_proj[l].astype(MXU_DTYPE),
                 w_out[l].astype(MXU_DTYPE), norm2_w[l], w_up[l].astype(MXU_DTYPE), w_down[l].astype(MXU_DTYPE),
                 final_norm_w if l == depth - 1 else None)

        def pcols(c0, w):
            return proj[:rp, c0:c0 + w].reshape(b, lp, w)[:, :lv]

        def scols(c0, w):
            return proj[rp:rp + rs, c0:c0 + w].reshape(bd, ns, w)

        xbc_w = D_INNER + BC_DIM
        p_conv = jnp.stack([proj[s * lp + lv - (CONV_W - 1):s * lp + lv, C_XS:C_XS + xbc_w] for s in range(b)])
        s_conv = jnp.concatenate([state_conv[l], scols(C_XS, xbc_w)], axis=1)[:, -(CONV_W - 1):]
        for acc, t in zip(outs, (
                pcols(C_K, KV_DIM).reshape(b, lv, ATT_KV_HEADS, HEAD_DIM),
                pcols(C_V, KV_DIM).reshape(b, lv, ATT_KV_HEADS, HEAD_DIM),
                pcols(C_SM + SM_KI, IDX_DIM),
                None,
                p_conv,
                scols(C_K, KV_DIM).reshape(bd, ns, ATT_KV_HEADS, HEAD_DIM),
                scols(C_V, KV_DIM).reshape(bd, ns, ATT_KV_HEADS, HEAD_DIM),
                scols(C_SM + SM_KI, IDX_DIM),
                None,
                s_conv)):
            acc.append(t)

    y = h
    y_prompt = jnp.stack([y[s * lp + N_META:s * lp + lv] for s in range(b)])
    y_sample = y[rp:rp + rs].reshape(bd, ns, d)
    stacked = [jnp.stack(o) if o[0] is not None else None for o in outs]
    stacked[3], stacked[8] = p_ssm, s_ssm
    return (y_prompt, y_sample) + tuple(stacked)
```

```python
import functools
import math

import jax
import jax.numpy as jnp
from jax import lax
from jax.experimental import pallas as pl
from jax.experimental.pallas import tpu as pltpu

F32 = jnp.float32
MXU_DTYPE = jnp.bfloat16

N_META = 16
NORM_EPS = 1e-6
SSM_HEAD_DIM = 64
SSM_GROUPS = 4
SSM_STATE = 128
CONV_W = 4
ATT_HEADS = 16
ATT_KV_HEADS = 4
ATT_GROUP = ATT_HEADS // ATT_KV_HEADS
HEAD_DIM = 64
IDX_HEADS = 8
IDX_DIM = 64
TOPK_MAX = 256

LANES = 128
Q_BLOCK = 128
SSD_CHUNK = 128
VMEM_LIMIT = 56 << 20
MASK_BIAS = -1e30
INT_MIN = -(2 ** 31)
LOG2_E = 1.4426950408889634
V_ROWS = HEAD_DIM + 16

D_MODEL = 1024
D_INNER = 2 * D_MODEL
BC_DIM = 2 * SSM_GROUPS * SSM_STATE
SSM_HEADS = D_INNER // SSM_HEAD_DIM
ATT_DIM = ATT_HEADS * HEAD_DIM
KV_DIM = ATT_KV_HEADS * HEAD_DIM
QI_DIM = IDX_HEADS * IDX_DIM
C_Z = 0
C_XS = C_Z + D_INNER
C_BC = C_XS + D_INNER
C_Q = C_BC + BC_DIM
C_K = C_Q + ATT_DIM
C_V = C_K + KV_DIM
C_QI = C_V + KV_DIM
C_GS = C_QI + QI_DIM
C_GA = C_GS + D_MODEL
C_SM = C_GA + D_MODEL
SM_KI = 0
SM_DT = IDX_DIM
SM_WI = IDX_DIM + SSM_HEADS
PROJ_TN = 512
PROJ_COLS = -(-(C_SM + LANES) // PROJ_TN) * PROJ_TN


def _divisor_tile(n, max_tile, align):
    best = None
    for t in range(align, min(n, max_tile) + 1, align):
        if n % t == 0:
            best = t
    assert best is not None, (n, max_tile, align)
    return best


def _params(*sem):
    return pltpu.CompilerParams(dimension_semantics=sem, vmem_limit_bytes=VMEM_LIMIT)


def _dot(a, b):
    return jnp.dot(a.astype(MXU_DTYPE), b.astype(MXU_DTYPE), preferred_element_type=F32)


def _dot_nt(a, b):
    return lax.dot_general(a.astype(MXU_DTYPE), b.astype(MXU_DTYPE), (((1,), (1,)), ((), ())),
                           preferred_element_type=F32)


def _bf16_pieces(x):
    hi = x.astype(jnp.bfloat16)
    r = x - hi.astype(F32)
    mid = r.astype(jnp.bfloat16)
    return [hi, mid, (r - mid.astype(F32)).astype(jnp.bfloat16)]


def _dot_exact(a, b):
    return jnp.dot(a, b, preferred_element_type=F32, precision=lax.Precision.HIGHEST)


def _sigmoid(x):
    return 0.5 * jnp.tanh(0.5 * x) + 0.5


def _row_rmsnorm(x, w):
    return x * lax.rsqrt(jnp.mean(x * x, axis=-1, keepdims=True) + NORM_EPS) * w


def _in_proj_kernel(x_ref, nw_ref, w_ref, o_ref, u_sc):
    j = pl.program_id(1)

    @pl.when(j == 0)
    def _():
        u_sc[...] = _row_rmsnorm(x_ref[...], nw_ref[...]).astype(u_sc.dtype)

    o_ref[...] = jnp.dot(u_sc[...], w_ref[j], preferred_element_type=F32)


def _in_proj(h, norm_w, w):
    r, k = h.shape
    nt = w.shape[0]
    tm = _divisor_tile(r, 1100, 16)
    return pl.pallas_call(
        _in_proj_kernel,
        grid=(r // tm, nt),
        in_specs=[pl.BlockSpec((tm, k), lambda i, j: (i, 0)), pl.BlockSpec((1, k), lambda i, j: (0, 0)),
                  pl.BlockSpec(w.shape, lambda i, j: (0, 0, 0), pipeline_mode=pl.Buffered(1))],
        out_specs=pl.BlockSpec((tm, PROJ_TN), lambda i, j: (i, j)),
        out_shape=jax.ShapeDtypeStruct((r, nt * PROJ_TN), F32),
        scratch_shapes=[pltpu.VMEM((tm, k), MXU_DTYPE)],
        compiler_params=_params("parallel", "arbitrary"),
        name="in_proj",
    )(h, norm_w.reshape(1, k), w)


def _mix_kernel(h_ref, ysp_ref, yss_ref, yap_ref, yas_ref, gs_ref, ga_ref, wsp_ref, wap_ref, wo_ref,
                nw_ref, wu_ref, wd_ref, *rest, n_p):
    prompt = pl.program_id(0) < n_p
    ys = jnp.where(prompt, ysp_ref[...], yss_ref[...].astype(MXU_DTYPE))
    ya = jnp.where(prompt, yap_ref[...], yas_ref[...].astype(MXU_DTYPE))
    a = jnp.dot(ys, wsp_ref[...], preferred_element_type=F32)
    b = jnp.dot(ya, wap_ref[...], preferred_element_type=F32)
    m = _sigmoid(gs_ref[...]) * a + _sigmoid(ga_ref[...]) * b
    x = h_ref[...] + _dot(m, wo_ref[...])
    f = _dot(_row_rmsnorm(x, nw_ref[...]), wu_ref[...])
    f = jnp.square(jnp.maximum(f, 0.0))
    y = x + _dot(f, wd_ref[...])
    if len(rest) == 2:
        fw_ref, o_ref = rest
        y = _row_rmsnorm(y, fw_ref[...])
    else:
        o_ref, = rest
    o_ref[...] = y


def _mix(h, rp, ys_p, ys_s, ya_p, ya_s, proj, w_sp, w_ap, w_o, norm_w, w_up, w_down, final_norm_w=None):
    r, d = h.shape
    tm = _divisor_tile(math.gcd(rp, r - rp), 256, 16)
    n_p = rp // tm
    const = lambda i: (0, 0)
    resident = lambda w: pl.BlockSpec(w.shape, const, pipeline_mode=pl.Buffered(1))
    pblk = lambda w: pl.BlockSpec((tm, w), lambda i: (jnp.minimum(i, n_p - 1), 0))
    sblk = lambda w: pl.BlockSpec((tm, w), lambda i: (jnp.maximum(i - n_p, 0), 0))
    extra = [] if final_norm_w is None else [final_norm_w.reshape(1, d)]
    return pl.pallas_call(
        functools.partial(_mix_kernel, n_p=n_p),
        grid=(r // tm,),
        in_specs=[pl.BlockSpec((tm, d), lambda i: (i, 0)),
                  pblk(D_INNER), sblk(D_INNER), pblk(ATT_DIM), sblk(ATT_DIM),
                  pl.BlockSpec((tm, d), lambda i: (i, C_GS // D_MODEL)),
                  pl.BlockSpec((tm, d), lambda i: (i, C_GA // D_MODEL)),
                  resident(w_sp), resident(w_ap), resident(w_o),
                  pl.BlockSpec((1, d), const), resident(w_up), resident(w_down)]
                 + [pl.BlockSpec((1, d), const)] * len(extra),
        out_specs=pl.BlockSpec((tm, d), lambda i: (i, 0)),
        out_shape=jax.ShapeDtypeStruct((r, d), F32),
        compiler_params=_params("parallel"),
        name="mix",
    )(h, ys_p, ys_s, ya_p, ya_s, proj, proj, w_sp, w_ap, w_o, norm_w.reshape(1, d), w_up, w_down, *extra)


def _softplus(x):
    return jnp.maximum(x, 0.0) + jnp.log1p(jnp.exp(-jnp.abs(x)))


def _conv_silu(xp_sc, x, w_ref, b_ref, t):
    xp_sc[pl.ds(8, t), :] = x
    acc = b_ref[...] + x * w_ref[CONV_W - 1:CONV_W, :]
    for k in range(CONV_W - 1):
        acc = acc + xp_sc[pl.ds(5 + k, t), :] * w_ref[k:k + 1, :]
    xp_sc[pl.ds(5, CONV_W - 1), :] = xp_sc[pl.ds(t + 5, CONV_W - 1), :]
    return acc * _sigmoid(acc)


def _ssd_kernel(z_ref, xs_ref, bc_ref, sm_ref, *rest, t_in, valid_len, has_h0, has_prev):
    h0_ref = rest[0] if has_h0 else None
    rest = rest[int(has_h0):]
    pxs_ref, pbc_ref, cwx_ref, cbx_ref, cwb_ref, cbb_ref, dtb_ref, aneg_ref, dskip_ref, normw_ref, expand_ref = rest[:11]
    y_ref, hout_ref, state_sc, xpx_sc, xpb_sc = rest[11 + int(has_prev):]
    t = SSD_CHUNK
    c = pl.program_id(1)
    hpg = SSM_HEADS // SSM_GROUPS
    pair = 2 * SSM_HEAD_DIM

    @pl.when(c == 0)
    def _():
        if has_h0:
            for g in range(SSM_GROUPS):
                state_sc[g] = jnp.concatenate(
                    [jnp.concatenate([h0_ref[g * hpg + e], h0_ref[g * hpg + e + 1]], axis=0).T
                     for e in range(0, hpg, 2)], axis=1)
        else:
            state_sc[...] = jnp.zeros(state_sc.shape, F32)
        xpx_sc[pl.ds(5, CONV_W - 1), :] = pxs_ref[...]
        xpb_sc[pl.ds(5, CONV_W - 1), :] = pbc_ref[...]

    def rows(ref):
        x = ref[...]
        if t_in < t:
            x = jnp.concatenate([x, jnp.zeros((t - t_in, x.shape[1]), x.dtype)], axis=0)
        return x

    xs = _conv_silu(xpx_sc, rows(xs_ref), cwx_ref, cbx_ref, t)
    bcm = _conv_silu(xpb_sc, rows(bc_ref), cwb_ref, cbb_ref, t)
    z = rows(z_ref)

    row = lax.broadcasted_iota(jnp.int32, (t, LANES), 0)
    n_valid = jnp.minimum(valid_len - c * t, t_in)
    dt = jnp.where(row < n_valid, _softplus(rows(sm_ref) + dtb_ref[...]), 0.0)
    a = dt * aneg_ref[...]
    ri = lax.broadcasted_iota(jnp.int32, (t, t), 0)
    ci = lax.broadcasted_iota(jnp.int32, (t, t), 1)
    causal = ri >= ci
    acum = _dot_exact(jnp.where(causal, 1.0, 0.0), a)
    acum_t = acum.T
    ex = jnp.dot(jnp.concatenate(_bf16_pieces(dt) + _bf16_pieces(acum), axis=0), expand_ref[...],
                 preferred_element_type=F32)
    dtx = ex[:t] + ex[t:2 * t] + ex[2 * t:3 * t]
    acx = ex[3 * t:4 * t] + ex[4 * t:5 * t] + ex[5 * t:]
    last = acx[t - 1:t, :]
    xdt = xs * dtx
    xw = xdt * jnp.exp(last - acx)
    eacx = jnp.exp(acx)
    elast = jnp.exp(last)

    gw = D_INNER // SSM_GROUPS
    hpg = SSM_HEADS // SSM_GROUPS
    ys = []
    for g in range(SSM_GROUPS):
        bg = bcm[:, g * SSM_STATE:(g + 1) * SSM_STATE]
        cg = bcm[:, (SSM_GROUPS + g) * SSM_STATE:(SSM_GROUPS + g + 1) * SSM_STATE]
        cb = _dot_nt(cg, bg)
        h_t = state_sc[g]
        yg = [_dot(cg, h_t) * eacx[:, g * gw:(g + 1) * gw]]
        intra = []
        for e in range(hpg):
            hd = g * hpg + e
            seg = acum[:, SM_DT + hd:SM_DT + hd + 1] - acum_t[SM_DT + hd:SM_DT + hd + 1, :]
            m = cb * jnp.exp(jnp.where(causal, seg, -jnp.inf))
            intra.append(_dot(m, xdt[:, hd * SSM_HEAD_DIM:(hd + 1) * SSM_HEAD_DIM]))
        ys.append(yg[0] + jnp.concatenate(intra, axis=1))
        state_sc[g] = h_t * elast[:, g * gw:(g + 1) * gw] + _dot(bg.T, xw[:, g * gw:(g + 1) * gw])
    y = jnp.concatenate(ys, axis=1) + xs * dskip_ref[...]
    y = y * (z * _sigmoid(z))
    outs = []
    for g in range(SSM_GROUPS):
        yg = y[:, g * gw:(g + 1) * gw]
        outs.append(yg * lax.rsqrt(jnp.mean(yg * yg, axis=-1, keepdims=True) + NORM_EPS))
    y = jnp.concatenate(outs, axis=1) * normw_ref[...]
    y_ref[...] = y[:t_in].astype(y_ref.dtype)

    @pl.when(c == pl.num_programs(1) - 1)
    def _():
        for g in range(SSM_GROUPS):
            st = state_sc[g]
            for e in range(0, hpg, 2):
                both = st[:, e * SSM_HEAD_DIM:e * SSM_HEAD_DIM + pair].T
                hout_ref[g * hpg + e] = both[:SSM_HEAD_DIM]
                hout_ref[g * hpg + e + 1] = both[SSM_HEAD_DIM:]


def _ssd(proj, row0, n_seq, t_in, n_chunks, valid_len, layer, depth, state_in, state_prev, prefix, lw, out_dtype):
    assert row0 % t_in == 0
    rb0 = row0 // t_in
    rowblk = lambda w, col: pl.BlockSpec((t_in, w), lambda s, c: (rb0 + s * n_chunks + c, col // w))
    per_seq = lambda shape: pl.BlockSpec((None,) + shape, lambda s, c: (s,) + (0,) * len(shape))
    const = lambda arr: pl.BlockSpec(arr.shape, lambda s, c: (0,) * arr.ndim)
    state_blk = pl.BlockSpec((None, None, SSM_HEADS, SSM_HEAD_DIM, SSM_STATE), lambda s, c: (layer, s, 0, 0, 0))
    gw = D_INNER // SSM_GROUPS
    pxs, pbc = prefix[:, :, :D_INNER], prefix[:, :, D_INNER:]
    consts = (lw["cwx"], lw["cbx"], lw["cwb"], lw["cbb"], lw["dtb"], lw["aneg"], lw["dskip"], lw["normw"], lw["expand"])
    has_h0, has_prev = state_in is not None, state_prev is not None
    operands = [proj, proj, proj, proj] + [state_in] * has_h0 + [pxs, pbc, *consts] + [state_prev] * has_prev
    in_specs = ([rowblk(D_INNER, C_Z), rowblk(D_INNER, C_XS), rowblk(BC_DIM, C_BC), rowblk(LANES, C_SM)]
                + [state_blk] * has_h0
                + [per_seq((CONV_W - 1, D_INNER)), per_seq((CONV_W - 1, BC_DIM))] + [const(a) for a in consts]
                + [pl.BlockSpec(memory_space=pl.ANY)] * has_prev)
    return pl.pallas_call(
        functools.partial(_ssd_kernel, t_in=t_in, valid_len=valid_len, has_h0=has_h0, has_prev=has_prev),
        grid=(n_seq, n_chunks),
        in_specs=in_specs,
        out_specs=[pl.BlockSpec((t_in, D_INNER), lambda s, c: (s * n_chunks + c, 0)), state_blk],
        out_shape=[jax.ShapeDtypeStruct((n_seq * n_chunks * t_in, D_INNER), out_dtype),
                   jax.ShapeDtypeStruct((depth, n_seq, SSM_HEADS, SSM_HEAD_DIM, SSM_STATE), F32)],
        scratch_shapes=[pltpu.VMEM((SSM_GROUPS, SSM_STATE, gw), F32),
                        pltpu.VMEM((SSD_CHUNK + 8, D_INNER), F32),
                        pltpu.VMEM((SSD_CHUNK + 8, BC_DIM), F32)],
        input_output_aliases={len(operands) - 1: 1} if has_prev else {},
        compiler_params=_params("parallel", "arbitrary"),
        name="ssd",
    )(*operands)


def _sort_key(x):
    bits = lax.bitcast_convert_type(x, jnp.int32)
    raw = bits ^ ((bits >> 31) & 0x7FFFFFFF)
    return raw - (raw >> 31)


def _fold_rows(x, group=8):
    x = x.reshape(x.shape[0] // group, group, x.shape[1])
    while x.shape[0] > 1:
        half, odd = divmod(x.shape[0], 2)
        folded = x[:half] + x[half:2 * half]
        if odd:
            folded = jnp.concatenate([folded[:1] + x[2 * half:], folded[1:]], axis=0) if half > 1 else folded + x[2 * half:]
        x = folded
    return x[0]


def _count_t(keys_sc, n_chunks, pred):
    def body(c, cnt):
        return cnt + _fold_rows(jnp.where(pred(keys_sc[c]), 1.0, 0.0))

    cnt = lax.fori_loop(0, n_chunks, body, jnp.zeros((8, LANES), F32))
    return jnp.sum(cnt, axis=0, keepdims=True)


HALF_MIN = -(2 ** 30)
SEARCH_UNROLLS = tuple(range(1, 17))


def _search_bounds(nkc):
    return sorted({min(b, nkc) for b in SEARCH_UNROLLS} | {nkc})


def _search_chunks(n_chunks, nkc):
    bounds = _search_bounds(nkc)
    n = jnp.int32(bounds[-1])
    for b in reversed(bounds[:-1]):
        n = jnp.where(n_chunks <= b, b, n)
    return n


PLANE_ROWS = 32 * 8


def _bit_planes(rows):
    a = list(rows)
    j, m = 16, 0x0000FFFF
    while j:
        k = 0
        while k < 32:
            t = (a[k] ^ lax.shift_right_logical(a[k + j], jnp.int32(j))) & m
            a[k] = a[k] ^ t
            a[k + j] = a[k + j] ^ (t << j)
            k = (k + j + 1) & ~j
        j >>= 1
        m = (m ^ (m << j)) & 0xFFFFFFFF
    return [a[31 - b] for b in range(32)]


def _store_planes(planes_sc, group, key):
    planes = _bit_planes([key[8 * v:8 * v + 8] for v in range(32)])
    planes[31] = ~planes[31]
    for b in range(32):
        planes_sc[b, group] = planes[b]


def _kth_key_planes(planes_sc, n, k):
    def count(words):
        return jnp.sum(jnp.sum(lax.population_count(words), axis=0).astype(F32), axis=0, keepdims=True)

    def bit_body(i, carry):
        live, above, t = carry
        bit = 31 - i
        ones = live & planes_sc[bit][:n]
        tot = above + count(ones)
        take = tot >= k
        live = jnp.where(take, ones, live ^ ones)
        above = jnp.where(take, above, tot)
        return live, above, t | lax.shift_left(jnp.where(take, 1, 0), bit)

    init = (jnp.full((n, 8, LANES), -1, jnp.int32), jnp.zeros((1, LANES), F32), jnp.zeros((1, LANES), jnp.int32))
    live, above, t = lax.fori_loop(0, 32, bit_body, init)
    return t ^ INT_MIN, above + count(live)


def _kth_key_t(planes_sc, n_chunks, nkc, k):
    bounds = _search_bounds(nkc)
    per_chunk = planes_sc.shape[1] // nkc

    def build(i):
        if i == len(bounds) - 1:
            return lambda: _kth_key_planes(planes_sc, bounds[i] * per_chunk, k)
        return lambda: lax.cond(n_chunks <= bounds[i], lambda: _kth_key_planes(planes_sc, bounds[i] * per_chunk, k),
                                build(i + 1))

    return build(0)()


def _select_t(keys_sc, planes_sc, bias_sc, n_chunks, k):
    kc = keys_sc.shape[1]
    thr, n_ge = _kth_key_t(planes_sc, n_chunks, keys_sc.shape[0], k)
    surplus = jnp.max(jnp.where((n_ge > k) & (thr != INT_MIN), 1.0, 0.0))

    @pl.when(surplus == 0.0)
    def _():
        floor = jnp.maximum(thr, INT_MIN + 1)

        def body(c, carry):
            bias_sc[c] = jnp.where(keys_sc[c] >= floor, 0.0, MASK_BIAS)
            return carry

        lax.fori_loop(0, n_chunks, body, 0)

    @pl.when(surplus != 0.0)
    def _():
        need = k - _count_t(keys_sc, n_chunks, lambda key: key > thr)
        ri = lax.broadcasted_iota(jnp.int32, (kc, kc), 0)
        ci = lax.broadcasted_iota(jnp.int32, (kc, kc), 1)
        lower = jnp.where(ci <= ri, 1.0, 0.0).astype(jnp.bfloat16)

        def body(c, run):
            key = keys_sc[c]
            eqf = jnp.where((key == thr) & (key != INT_MIN), 1.0, 0.0)
            rank = jnp.dot(lower, eqf.astype(jnp.bfloat16), preferred_element_type=F32) + run
            sel = (key > thr) | ((eqf > 0.0) & (rank <= need))
            bias_sc[c] = jnp.where(sel, 0.0, MASK_BIAS)
            return run + jnp.sum(_fold_rows(eqf), axis=0, keepdims=True)

        lax.fori_loop(0, n_chunks, body, jnp.zeros((1, LANES), F32))


def _select_s(keys_sc, kh_sc, bias_ref, k):
    n_chunks, rows, width = keys_sc.shape

    def count(pred):
        return jnp.sum(jnp.sum(jnp.where(pred(keys_sc[...]), 1.0, 0.0), axis=0), axis=1, keepdims=True)

    def bit_body(i, u):
        cand = u | lax.shift_left(jnp.int32(1), 30 - i)
        below = jnp.sum((kh_sc[...] - (cand + HALF_MIN)[None]) >> 31, axis=0).astype(F32)
        return jnp.where(n_chunks * width + jnp.sum(below, axis=1, keepdims=True) >= k, cand, u)

    odd = (lax.fori_loop(0, 31, bit_body, jnp.zeros((rows, 1), jnp.int32)) + HALF_MIN) * 2 + 1
    thr = jnp.where(count(lambda key: key >= odd[None]) >= k, odd, odd - 1)
    n_ge = count(lambda key: key >= thr[None])
    surplus = jnp.max(jnp.where((n_ge > k) & (thr != INT_MIN), 1.0, 0.0))

    @pl.when(surplus == 0.0)
    def _():
        floor = jnp.maximum(thr, INT_MIN + 1)
        bias_ref[...] = jnp.where(keys_sc[...] >= floor[None], 0.0, MASK_BIAS)

    @pl.when(surplus != 0.0)
    def _():
        need = k - count(lambda key: key > thr[None])
        ri = lax.broadcasted_iota(jnp.int32, (width, width), 0)
        ci = lax.broadcasted_iota(jnp.int32, (width, width), 1)
        upper = jnp.where(ri <= ci, 1.0, 0.0).astype(jnp.bfloat16)

        def body(c, run):
            key = keys_sc[c]
            eqf = jnp.where((key == thr) & (key != INT_MIN), 1.0, 0.0)
            rank = jnp.dot(eqf.astype(jnp.bfloat16), upper, preferred_element_type=F32) + run
            sel = (key > thr) | ((eqf > 0.0) & (rank <= need))
            bias_ref[c] = jnp.where(sel, 0.0, MASK_BIAS)
            return run + jnp.sum(eqf, axis=1, keepdims=True)

        lax.fori_loop(0, n_chunks, body, jnp.zeros((rows, 1), F32))


def _pattn_kernel(qi_ref, smq_ref, q_ref, smk_ref, k_ref, v_ref, o_ref,
                  kb_sc, kib_sc, vt_sc, keys_sc, planes_sc, bias_sc, qz_sc, m_sc, acc_sc, *, kc, lp, topk):
    j = pl.program_id(1)
    nq = Q_BLOCK
    nkc = kb_sc.shape[0]

    @pl.when(j == 0)
    def _():
        for c in range(nkc):
            r0 = c * kc
            n = min(kc, lp - r0)

            def chunk(ref):
                x = ref[r0:r0 + n, :]
                return x if n == kc else jnp.concatenate([x, jnp.zeros((kc - n, x.shape[1]), x.dtype)], axis=0)

            kb_sc[c] = chunk(k_ref).astype(MXU_DTYPE)
            kib_sc[c] = chunk(smk_ref).astype(MXU_DTYPE)
            v_t = chunk(v_ref).T
            pad = jnp.concatenate([jnp.ones((1, kc), F32), jnp.zeros((V_ROWS - HEAD_DIM - 1, kc), F32)], axis=0)
            vt_sc[c] = jnp.concatenate([x for hk in range(ATT_KV_HEADS)
                                        for x in (v_t[hk * HEAD_DIM:(hk + 1) * HEAD_DIM], pad)], axis=0).astype(MXU_DTYPE)

    n_chunks = ((j + 1) * nq - 1) // kc + 1

    qi_t = (qi_ref[...] * IDX_DIM ** -0.5).T
    qiz = jnp.concatenate([qi_t[h * IDX_DIM:(h + 1) * IDX_DIM] for h in range(IDX_HEADS)], axis=1)
    qiz = jnp.concatenate([qiz, jnp.zeros((LANES - IDX_DIM, IDX_HEADS * nq), F32)], axis=0).astype(MXU_DTYPE)
    smq_t = smq_ref[...].T
    wi_row = jnp.concatenate([smq_t[SM_WI + h:SM_WI + h + 1] for h in range(IDX_HEADS)], axis=1) * IDX_HEADS ** -0.5
    kpos = lax.broadcasted_iota(jnp.int32, (kc, nq), 0)
    qpos = j * nq + lax.broadcasted_iota(jnp.int32, (kc, nq), 1)

    def score_body(c, carry):
        s = jnp.maximum(jnp.dot(kib_sc[c], qiz, preferred_element_type=F32), 0.0) * wi_row
        acc = s[:, :nq]
        for h in range(1, IDX_HEADS):
            acc = acc + s[:, h * nq:(h + 1) * nq]
        key = jnp.where(c * kc + kpos <= qpos, _sort_key(acc), INT_MIN)
        keys_sc[c] = key
        for g in range(kc // PLANE_ROWS):
            _store_planes(planes_sc, c * (kc // PLANE_ROWS) + g, key[g * PLANE_ROWS:(g + 1) * PLANE_ROWS])
        return carry

    lax.fori_loop(0, n_chunks, score_body, 0)

    def blank_body(c, carry):
        for g in range(kc // PLANE_ROWS):
            for b in range(32):
                planes_sc[b, c * (kc // PLANE_ROWS) + g] = jnp.zeros((8, nq), jnp.int32)
        return carry

    lax.fori_loop(n_chunks, _search_chunks(n_chunks, nkc), blank_body, 0)
    _select_t(keys_sc, planes_sc, bias_sc, n_chunks, topk)

    q_t = (q_ref[...] * (HEAD_DIM ** -0.5 * LOG2_E)).T
    gw = ATT_GROUP * nq
    zero = jnp.zeros((HEAD_DIM, gw), F32)
    for hk in range(ATT_KV_HEADS):
        own = jnp.concatenate([q_t[(hk * ATT_GROUP + g) * HEAD_DIM:(hk * ATT_GROUP + g + 1) * HEAD_DIM]
                               for g in range(ATT_GROUP)], axis=1)
        qz_sc[hk] = jnp.concatenate([own if i == hk else zero for i in range(ATT_KV_HEADS)],
                                    axis=0).astype(MXU_DTYPE)
    m_sc[...] = jnp.full(m_sc.shape, MASK_BIAS, F32)
    acc_sc[...] = jnp.zeros(acc_sc.shape, F32)

    def attend(chunks):
        stages = [(c, hk) for c in chunks for hk in range(ATT_KV_HEADS)]
        loaded = {}

        def operands(c):
            if id(c) not in loaded:
                loaded[id(c)] = (kb_sc[c], vt_sc[c], jnp.concatenate([bias_sc[c]] * ATT_GROUP, axis=1))
            return loaded[id(c)]

        def qk(t):
            c, hk = stages[t]
            kch, _, b4 = operands(c)
            return jnp.dot(kch, qz_sc[hk], preferred_element_type=F32) + b4

        def soft(t, s):
            hk = stages[t][1]
            m_old = m_sc[hk]
            m_new = jnp.maximum(m_old, jnp.max(s, axis=0, keepdims=True))
            m_sc[hk] = m_new
            return jnp.exp2(m_old - m_new), jnp.exp2(s - m_new).astype(MXU_DTYPE)

        def pv(t, alpha, p):
            c, hk = stages[t]
            vt = operands(c)[1]
            acc_sc[hk] = alpha * acc_sc[hk] + jnp.dot(vt[hk * V_ROWS:(hk + 1) * V_ROWS], p,
                                                      preferred_element_type=F32)

        n = len(stages)
        s = {0: qk(0), 1: qk(1)}
        ap = {}
        for t in range(n):
            ap[t] = soft(t, s.pop(t))
            if t + 2 < n:
                s[t + 2] = qk(t + 2)
            if t >= 1:
                pv(t - 1, *ap.pop(t - 1))
        pv(n - 1, *ap.pop(n - 1))

    def pair_body(i, carry):
        attend([2 * i, 2 * i + 1])
        return carry

    lax.fori_loop(0, n_chunks // 2, pair_body, 0)

    @pl.when(n_chunks % 2 == 1)
    def _():
        attend([n_chunks - 1])

    heads = []
    for hk in range(ATT_KV_HEADS):
        acc = acc_sc[hk]
        o = acc[:HEAD_DIM] / acc[HEAD_DIM:HEAD_DIM + 1]
        heads += [o[:, g * nq:(g + 1) * nq] for g in range(ATT_GROUP)]
    o_ref[...] = jnp.concatenate(heads, axis=0).T.astype(o_ref.dtype)


def _prompt_attention(proj, n_seq, lp, topk):
    nqb = lp // Q_BLOCK
    kc = 512
    nkc = -(-lp // kc)
    qblk = lambda w, col: pl.BlockSpec((Q_BLOCK, w), lambda b, j: (b * nqb + j, col // w))
    seqblk = lambda w, col: pl.BlockSpec((lp, w), lambda b, j: (b, col // w))
    gw = ATT_GROUP * Q_BLOCK
    return pl.pallas_call(
        functools.partial(_pattn_kernel, kc=kc, lp=lp, topk=topk),
        grid=(n_seq, nqb),
        in_specs=[qblk(QI_DIM, C_QI), qblk(LANES, C_SM), qblk(ATT_DIM, C_Q),
                  seqblk(LANES, C_SM), seqblk(KV_DIM, C_K), seqblk(KV_DIM, C_V)],
        out_specs=pl.BlockSpec((Q_BLOCK, ATT_DIM), lambda b, j: (b * nqb + j, 0)),
        out_shape=jax.ShapeDtypeStruct((n_seq * lp, ATT_DIM), MXU_DTYPE),
        scratch_shapes=[pltpu.VMEM((nkc, kc, KV_DIM), MXU_DTYPE), pltpu.VMEM((nkc, kc, LANES), MXU_DTYPE),
                        pltpu.VMEM((nkc, ATT_KV_HEADS * V_ROWS, kc), MXU_DTYPE),
                        pltpu.VMEM((nkc, kc, Q_BLOCK), jnp.int32),
                        pltpu.VMEM((32, nkc * (kc // PLANE_ROWS), 8, Q_BLOCK), jnp.int32),
                        pltpu.VMEM((nkc, kc, Q_BLOCK), F32),
                        pltpu.VMEM((ATT_KV_HEADS, KV_DIM, gw), MXU_DTYPE),
                        pltpu.VMEM((ATT_KV_HEADS, 1, gw), F32),
                        pltpu.VMEM((ATT_KV_HEADS, V_ROWS, gw), F32)],
        compiler_params=_params("parallel", "arbitrary"),
        name="prompt_attention",
    )(proj, proj, proj, proj, proj, proj)


def _pad_rows(x, n):
    return jnp.concatenate([x, jnp.zeros((n - x.shape[0], x.shape[1]), x.dtype)], axis=0)


def _sselect_kernel(pt_ref, qi_ref, sm_ref, *rest, pages_per_step, n_pages, page, ns, topk):
    page_refs = rest[:pages_per_step]
    bias_ref, keys_sc, kh_sc = rest[pages_per_step:]
    g = pl.program_id(1)
    qi = (qi_ref[...] * IDX_DIM ** -0.5).astype(MXU_DTYPE)
    qs = jnp.concatenate([qi[:, h * IDX_DIM:(h + 1) * IDX_DIM] for h in range(IDX_HEADS)], axis=0)
    sm = sm_ref[...]
    wi = sm[:, SM_WI:SM_WI + IDX_HEADS] * IDX_HEADS ** -0.5

    def weigh(s):
        s = jnp.maximum(s, 0.0)
        acc = s[:ns] * wi[:, 0:1]
        for h in range(1, IDX_HEADS):
            acc = acc + s[h * ns:(h + 1) * ns] * wi[:, h:h + 1]
        return _sort_key(acc)

    keys = weigh(_dot(qs, jnp.concatenate([r[...] for r in page_refs], axis=1)))
    for i in range(pages_per_step):
        keys_sc[g * pages_per_step + i] = keys[:, i * page:(i + 1) * page]
        kh_sc[g * pages_per_step + i] = keys[:, i * page:(i + 1) * page] >> 1

    @pl.when(g == pl.num_programs(1) - 1)
    def _():
        qrow = lax.broadcasted_iota(jnp.int32, (ns, page), 0)
        lane = lax.broadcasted_iota(jnp.int32, (ns, page), 1)
        new = weigh(_dot_nt(qs, _pad_rows(sm[:, SM_KI:SM_KI + IDX_DIM], page)))
        new = jnp.where(lane <= qrow, new, INT_MIN)
        keys_sc[n_pages] = new
        kh_sc[n_pages] = new >> 1
        _select_s(keys_sc, kh_sc, bias_ref, topk)


def _sattn_kernel(pt_ref, q_ref, kn_ref, vn_ref, bias_ref, *rest, pages_per_step, n_pages, page, ns):
    k_refs = rest[:pages_per_step]
    v_refs = rest[pages_per_step:2 * pages_per_step]
    o_ref, qbd_sc, m_sc, l_sc, acc_sc = rest[2 * pages_per_step:]
    g = pl.program_id(1)

    @pl.when(g == 0)
    def _():
        q = q_ref[...] * HEAD_DIM ** -0.5
        rows = []
        for h in range(ATT_HEADS):
            hk = h // ATT_GROUP
            parts = [jnp.zeros((ns, HEAD_DIM), F32)] * ATT_KV_HEADS
            parts[hk] = q[:, h * HEAD_DIM:(h + 1) * HEAD_DIM]
            rows.append(jnp.concatenate(parts, axis=1))
        qbd_sc[...] = jnp.concatenate(rows, axis=0).astype(MXU_DTYPE)
        m_sc[...] = jnp.full(m_sc.shape, MASK_BIAS, F32)
        l_sc[...] = jnp.zeros(l_sc.shape, F32)
        acc_sc[...] = jnp.zeros(acc_sc.shape, F32)

    def attend(s, b, pv):
        s = s + jnp.concatenate([b] * ATT_HEADS, axis=0)
        m_old = m_sc[...]
        m_new = jnp.maximum(m_old, jnp.max(s, axis=1, keepdims=True))
        alpha = jnp.exp(m_old - m_new)
        p = jnp.exp(s - m_new)
        l_sc[...] = alpha * l_sc[...] + jnp.sum(p, axis=1, keepdims=True)
        acc_sc[...] = alpha * acc_sc[...] + pv(p)
        m_sc[...] = m_new

    kcat = jnp.concatenate([r[...] for r in k_refs], axis=1)
    vcat = jnp.concatenate([r[...] for r in v_refs], axis=1)
    bcat = jnp.concatenate([bias_ref[g * pages_per_step + i] for i in range(pages_per_step)], axis=1)
    attend(_dot(qbd_sc[...], kcat), bcat, lambda p: _dot_nt(p, vcat))

    @pl.when(g == pl.num_programs(1) - 1)
    def _():
        vn = _pad_rows(vn_ref[...], page)
        attend(_dot_nt(qbd_sc[...], _pad_rows(kn_ref[...], page)), bias_ref[n_pages], lambda p: _dot(p, vn))
        o = acc_sc[...] / l_sc[...]
        o_ref[...] = jnp.concatenate(
            [o[h * ns:(h + 1) * ns, (h // ATT_GROUP) * HEAD_DIM:(h // ATT_GROUP + 1) * HEAD_DIM]
             for h in range(ATT_HEADS)], axis=1)


def _key_minor(cache):
    nd = cache.ndim
    t = jnp.transpose(cache, (0, 1) + tuple(range(3, nd)) + (2,))
    return t.reshape(t.shape[:2] + (-1, t.shape[-1]))


def _sample_attention(proj, row0, page_table, ck_t, cv_t, cki_t, layer, ns, topk):
    bd, n_pages = page_table.shape
    page = cki_t.shape[3]
    assert page == LANES and row0 % ns == 0
    pps = max(d for d in range(1, 17) if n_pages % d == 0)
    ng = n_pages // pps
    rb0 = row0 // ns
    rowblk = lambda w, col: pl.BlockSpec((ns, w), lambda s, g, pt: (rb0 + s, col // w))
    pageblk = lambda w, i: pl.BlockSpec((None, None, w, page), lambda s, g, pt: (layer, pt[s, g * pps + i], 0, 0))
    biasblk = pl.BlockSpec((None, n_pages + 1, ns, page), lambda s, g, pt: (s, 0, 0, 0))
    static = dict(pages_per_step=pps, n_pages=n_pages, page=page, ns=ns)

    bias = pl.pallas_call(
        functools.partial(_sselect_kernel, topk=topk, **static),
        grid_spec=pltpu.PrefetchScalarGridSpec(
            num_scalar_prefetch=1, grid=(bd, ng),
            in_specs=[rowblk(QI_DIM, C_QI), rowblk(LANES, C_SM)] + [pageblk(IDX_DIM, i) for i in range(pps)],
            out_specs=biasblk,
            scratch_shapes=[pltpu.VMEM((n_pages + 1, ns, page), jnp.int32)] * 2),
        out_shape=jax.ShapeDtypeStruct((bd, n_pages + 1, ns, page), F32),
        compiler_params=_params("parallel", "arbitrary"),
        name="sample_select",
    )(page_table, proj, proj, *([cki_t] * pps))

    rows = ATT_HEADS * ns
    return pl.pallas_call(
        functools.partial(_sattn_kernel, **static),
        grid_spec=pltpu.PrefetchScalarGridSpec(
            num_scalar_prefetch=1, grid=(bd, ng),
            in_specs=[rowblk(ATT_DIM, C_Q), rowblk(KV_DIM, C_K), rowblk(KV_DIM, C_V), biasblk]
                     + [pageblk(KV_DIM, i) for i in range(pps)] * 2,
            out_specs=pl.BlockSpec((ns, ATT_DIM), lambda s, g, pt: (s, 0)),
            scratch_shapes=[pltpu.VMEM((rows, KV_DIM), MXU_DTYPE), pltpu.VMEM((rows, 1), F32),
                            pltpu.VMEM((rows, 1), F32), pltpu.VMEM((rows, KV_DIM), F32)]),
        out_shape=jax.ShapeDtypeStruct((bd * ns, ATT_DIM), F32),
        compiler_params=_params("parallel", "arbitrary"),
        name="sample_attention",
    )(page_table, proj, proj, proj, bias, *([ck_t] * pps), *([cv_t] * pps))


def _regroup_w_in(w):
    o_dt = D_INNER + D_INNER + BC_DIM
    o_q = o_dt + SSM_HEADS
    o_ki = o_q + ATT_DIM + 2 * KV_DIM + QI_DIM
    o_wi = o_ki + IDX_DIM
    o_g = o_wi + IDX_HEADS
    parts = [w[:, :o_dt], w[:, o_q:o_ki], w[:, o_g:o_g + 2 * D_MODEL], w[:, o_ki:o_wi], w[:, o_dt:o_q], w[:, o_wi:o_g]]
    used = sum(p.shape[1] for p in parts)
    parts.append(jnp.zeros((w.shape[0], PROJ_COLS - used), w.dtype))
    w = jnp.concatenate(parts, axis=1).astype(MXU_DTYPE)
    return w.reshape(w.shape[0], PROJ_COLS // PROJ_TN, PROJ_TN).transpose(1, 0, 2)


def _small_lanes(v):
    return jnp.zeros((1, LANES), F32).at[0, SM_DT:SM_DT + SSM_HEADS].set(v.astype(F32))


def _layer_consts(conv_w, conv_b, dt_bias, a_log, d_skip, ssm_norm_w):
    head_of_lane = jnp.arange(D_INNER) // SSM_HEAD_DIM
    expand = (jnp.arange(LANES)[:, None] == (SM_DT + head_of_lane)[None, :]).astype(jnp.bfloat16)
    return dict(
        cwx=conv_w[:, :D_INNER], cbx=conv_b[None, :D_INNER], cwb=conv_w[:, D_INNER:], cbb=conv_b[None, D_INNER:],
        dtb=_small_lanes(dt_bias), aneg=_small_lanes(-jnp.exp(a_log.astype(F32))),
        dskip=jnp.repeat(d_skip.astype(F32), SSM_HEAD_DIM)[None, :], normw=ssm_norm_w.astype(F32)[None, :],
        expand=expand)


def kernel(x_prompt, x_sample, cache_k, cache_v, cache_kidx, state_ssm, state_conv, page_table, meta_tokens, norm1_w, w_in, conv_w, conv_b, dt_bias, a_log, d_skip, ssm_norm_w, w_ssm_proj, w_attn_proj, w_out, norm2_w, w_up, w_down, final_norm_w):
    b, seq, d = x_prompt.shape
    bd, ns = x_sample.shape[:2]
    depth = w_in.shape[0]
    assert d == D_MODEL and ns % 8 == 0
    past = page_table.shape[1] * cache_kidx.shape[2]
    lv = N_META + seq
    lp = -(-lv // Q_BLOCK) * Q_BLOCK
    rp = b * lp
    rs = bd * ns
    r = -(-(rp + rs) // LANES) * LANES
    topk_p = min(TOPK_MAX, seq // 4)
    topk_s = min(TOPK_MAX, (past + ns) // 4)

    meta = meta_tokens.astype(F32)
    pad = jnp.zeros((lp - lv, d), F32)
    h = jnp.concatenate([piece for s in range(b) for piece in (meta, x_prompt[s], pad)]
                        + [x_sample.reshape(rs, d), jnp.zeros((r - rp - rs, d), F32)], axis=0)

    ck_t, cv_t, cki_t = _key_minor(cache_k), _key_minor(cache_v), _key_minor(cache_kidx)
    p_ssm = s_ssm = None
    zero_prefix = jnp.zeros((b, CONV_W - 1, D_INNER + BC_DIM), F32)
    outs = [[] for _ in range(10)]
    for l in range(depth):
        lw = _layer_consts(conv_w[l], conv_b[l], dt_bias[l], a_log[l], d_skip[l], ssm_norm_w[l])
        proj = _in_proj(h, norm1_w[l], _regroup_w_in(w_in[l]))

        yp, p_ssm = _ssd(proj, 0, b, SSD_CHUNK, lp // SSD_CHUNK, lv, l, depth, None, p_ssm, zero_prefix, lw, MXU_DTYPE)
        ysm, s_ssm = _ssd(proj, rp, bd, ns, 1, ns, l, depth, state_ssm, s_ssm, state_conv[l], lw, F32)
        ap = _prompt_attention(proj, b, lp, topk_p)
        asm = _sample_attention(proj, rp, page_table, ck_t, cv_t, cki_t, l, ns, topk_s)
        tail = r - rp - rs
        if tail:
            ysm = jnp.concatenate([ysm, jnp.zeros((tail, D_INNER), F32)], axis=0)
            asm = jnp.concatenate([asm, jnp.zeros((tail, ATT_DIM), F32)], axis=0)

        h = _mix(h, rp, yp, ysm, ap, asm, proj, w_ssm_proj[l].astype(MXU_DTYPE), w_attn_proj[l].astype(MXU_DTYPE),
                 w_out[l].astype(MXU_DTYPE), norm2_w[l], w_up[l].astype(MXU_DTYPE), w_down[l].astype(MXU_DTYPE),
                 final_norm_w if l == depth - 1 else None)

        def pcols(c0, w):
            return proj[:rp, c0:c0 + w].reshape(b, lp, w)[:, :lv]

        def scols(c0, w):
            return proj[rp:rp + rs, c0:c0 + w].reshape(bd, ns, w)

        xbc_w = D_INNER + BC_DIM
        p_conv = jnp.stack([proj[s * lp + lv - (CONV_W - 1):s * lp + lv, C_XS:C_XS + xbc_w] for s in range(b)])
        s_conv = jnp.concatenate([state_conv[l], scols(C_XS, xbc_w)], axis=1)[:, -(CONV_W - 1):]
        for acc, t in zip(outs, (
                pcols(C_K, KV_DIM).reshape(b, lv, ATT_KV_HEADS, HEAD_DIM),
                pcols(C_V, KV_DIM).reshape(b, lv, ATT_KV_HEADS, HEAD_DIM),
                pcols(C_SM + SM_KI, IDX_DIM),
                None,
                p_conv,
                scols(C_K, KV_DIM).reshape(bd, ns, ATT_KV_HEADS, HEAD_DIM),
                scols(C_V, KV_DIM).reshape(bd, ns, ATT_KV_HEADS, HEAD_DIM),
                scols(C_SM + SM_KI, IDX_DIM),
                None,
                s_conv)):
            acc.append(t)

    y = h
    y_prompt = jnp.stack([y[s * lp + N_META:s * lp + lv] for s in range(b)])
    y_sample = y[rp:rp + rs].reshape(bd, ns, d)
    stacked = [jnp.stack(o) if o[0] is not None else None for o in outs]
    stacked[3], stacked[8] = p_ssm, s_ssm
    return (y_prompt, y_sample) + tuple(stacked)
```

```python
import functools
import math

import jax
import jax.numpy as jnp
from jax import lax
from jax.experimental import pallas as pl
from jax.experimental.pallas import tpu as pltpu

F32 = jnp.float32
MXU_DTYPE = jnp.bfloat16

N_META = 16
NORM_EPS = 1e-6
SSM_HEAD_DIM = 64
SSM_GROUPS = 4
SSM_STATE = 128
CONV_W = 4
ATT_HEADS = 16
ATT_KV_HEADS = 4
ATT_GROUP = ATT_HEADS // ATT_KV_HEADS
HEAD_DIM = 64
IDX_HEADS = 8
IDX_DIM = 64
TOPK_MAX = 256

LANES = 128
Q_BLOCK = 128
SSD_CHUNK = 128
VMEM_LIMIT = 56 << 20
MASK_BIAS = -1e30
INT_MIN = -(2 ** 31)
LOG2_E = 1.4426950408889634
V_ROWS = HEAD_DIM + 16

D_MODEL = 1024
D_INNER = 2 * D_MODEL
BC_DIM = 2 * SSM_GROUPS * SSM_STATE
SSM_HEADS = D_INNER // SSM_HEAD_DIM
ATT_DIM = ATT_HEADS * HEAD_DIM
KV_DIM = ATT_KV_HEADS * HEAD_DIM
QI_DIM = IDX_HEADS * IDX_DIM
C_Z = 0
C_XS = C_Z + D_INNER
C_BC = C_XS + D_INNER
C_Q = C_BC + BC_DIM
C_K = C_Q + ATT_DIM
C_V = C_K + KV_DIM
C_QI = C_V + KV_DIM
C_GS = C_QI + QI_DIM
C_GA = C_GS + D_MODEL
C_SM = C_GA + D_MODEL
SM_KI = 0
SM_DT = IDX_DIM
SM_WI = IDX_DIM + SSM_HEADS
PROJ_TN = 512
PROJ_COLS = -(-(C_SM + LANES) // PROJ_TN) * PROJ_TN


def _divisor_tile(n, max_tile, align):
    best = None
    for t in range(align, min(n, max_tile) + 1, align):
        if n % t == 0:
            best = t
    assert best is not None, (n, max_tile, align)
    return best


def _params(*sem):
    return pltpu.CompilerParams(dimension_semantics=sem, vmem_limit_bytes=VMEM_LIMIT)


def _dot(a, b):
    return jnp.dot(a.astype(MXU_DTYPE), b.astype(MXU_DTYPE), preferred_element_type=F32)


def _dot_nt(a, b):
    return lax.dot_general(a.astype(MXU_DTYPE), b.astype(MXU_DTYPE), (((1,), (1,)), ((), ())),
                           preferred_element_type=F32)


def _bf16_pieces(x):
    hi = x.astype(jnp.bfloat16)
    r = x - hi.astype(F32)
    mid = r.astype(jnp.bfloat16)
    return [hi, mid, (r - mid.astype(F32)).astype(jnp.bfloat16)]


def _dot_exact(a, b):
    return jnp.dot(a, b, preferred_element_type=F32, precision=lax.Precision.HIGHEST)


def _sigmoid(x):
    return 0.5 * jnp.tanh(0.5 * x) + 0.5


def _row_rmsnorm(x, w):
    return x * lax.rsqrt(jnp.mean(x * x, axis=-1, keepdims=True) + NORM_EPS) * w


def _in_proj_kernel(x_ref, nw_ref, w_ref, o_ref, u_sc):
    j = pl.program_id(1)

    @pl.when(j == 0)
    def _():
        u_sc[...] = _row_rmsnorm(x_ref[...], nw_ref[...]).astype(u_sc.dtype)

    o_ref[...] = jnp.dot(u_sc[...], w_ref[j], preferred_element_type=F32)


def _in_proj(h, norm_w, w):
    r, k = h.shape
    nt = w.shape[0]
    tm = _divisor_tile(r, 1100, 16)
    return pl.pallas_call(
        _in_proj_kernel,
        grid=(r // tm, nt),
        in_specs=[pl.BlockSpec((tm, k), lambda i, j: (i, 0)), pl.BlockSpec((1, k), lambda i, j: (0, 0)),
                  pl.BlockSpec(w.shape, lambda i, j: (0, 0, 0), pipeline_mode=pl.Buffered(1))],
        out_specs=pl.BlockSpec((tm, PROJ_TN), lambda i, j: (i, j)),
        out_shape=jax.ShapeDtypeStruct((r, nt * PROJ_TN), F32),
        scratch_shapes=[pltpu.VMEM((tm, k), MXU_DTYPE)],
        compiler_params=_params("parallel", "arbitrary"),
        name="in_proj",
    )(h, norm_w.reshape(1, k), w)


def _mix_kernel(h_ref, ysp_ref, yss_ref, yap_ref, yas_ref, gs_ref, ga_ref, wsp_ref, wap_ref, wo_ref,
                nw_ref, wu_ref, wd_ref, *rest, n_p):
    prompt = pl.program_id(0) < n_p
    ys = jnp.where(prompt, ysp_ref[...], yss_ref[...].astype(MXU_DTYPE))
    ya = jnp.where(prompt, yap_ref[...], yas_ref[...].astype(MXU_DTYPE))
    a = jnp.dot(ys, wsp_ref[...], preferred_element_type=F32)
    b = jnp.dot(ya, wap_ref[...], preferred_element_type=F32)
    m = _sigmoid(gs_ref[...]) * a + _sigmoid(ga_ref[...]) * b
    x = h_ref[...] + _dot(m, wo_ref[...])
    f = _dot(_row_rmsnorm(x, nw_ref[...]), wu_ref[...])
    f = jnp.square(jnp.maximum(f, 0.0))
    y = x + _dot(f, wd_ref[...])
    if len(rest) == 2:
        fw_ref, o_ref = rest
        y = _row_rmsnorm(y, fw_ref[...])
    else:
        o_ref, = rest
    o_ref[...] = y


def _mix(h, rp, ys_p, ys_s, ya_p, ya_s, proj, w_sp, w_ap, w_o, norm_w, w_up, w_down, final_norm_w=None):
    r, d = h.shape
    tm = _divisor_tile(math.gcd(rp, r - rp), 256, 16)
    n_p = rp // tm
    const = lambda i: (0, 0)
    resident = lambda w: pl.BlockSpec(w.shape, const, pipeline_mode=pl.Buffered(1))
    pblk = lambda w: pl.BlockSpec((tm, w), lambda i: (jnp.minimum(i, n_p - 1), 0))
    sblk = lambda w: pl.BlockSpec((tm, w), lambda i: (jnp.maximum(i - n_p, 0), 0))
    extra = [] if final_norm_w is None else [final_norm_w.reshape(1, d)]
    return pl.pallas_call(
        functools.partial(_mix_kernel, n_p=n_p),
        grid=(r // tm,),
        in_specs=[pl.BlockSpec((tm, d), lambda i: (i, 0)),
                  pblk(D_INNER), sblk(D_INNER), pblk(ATT_DIM), sblk(ATT_DIM),
                  pl.BlockSpec((tm, d), lambda i: (i, C_GS // D_MODEL)),
                  pl.BlockSpec((tm, d), lambda i: (i, C_GA // D_MODEL)),
                  resident(w_sp), resident(w_ap), resident(w_o),
                  pl.BlockSpec((1, d), const), resident(w_up), resident(w_down)]
                 + [pl.BlockSpec((1, d), const)] * len(extra),
        out_specs=pl.BlockSpec((tm, d), lambda i: (i, 0)),
        out_shape=jax.ShapeDtypeStruct((r, d), F32),
        compiler_params=_params("parallel"),
        name="mix",
    )(h, ys_p, ys_s, ya_p, ya_s, proj, proj, w_sp, w_ap, w_o, norm_w.reshape(1, d), w_up, w_down, *extra)


def _softplus(x):
    return jnp.maximum(x, 0.0) + jnp.log1p(jnp.exp(-jnp.abs(x)))


def _conv_silu(xp_sc, x, w_ref, b_ref, t):
    xp_sc[pl.ds(8, t), :] = x
    acc = b_ref[...] + x * w_ref[CONV_W - 1:CONV_W, :]
    for k in range(CONV_W - 1):
        acc = acc + xp_sc[pl.ds(5 + k, t), :] * w_ref[k:k + 1, :]
    xp_sc[pl.ds(5, CONV_W - 1), :] = xp_sc[pl.ds(t + 5, CONV_W - 1), :]
    return acc * _sigmoid(acc)


def _ssd_kernel(z_ref, xs_ref, bc_ref, sm_ref, *rest, t_in, valid_len, has_h0):
    h0_ref = rest[0] if has_h0 else None
    rest = rest[int(has_h0):]
    pxs_ref, pbc_ref, cwx_ref, cbx_ref, cwb_ref, cbb_ref, dtb_ref, aneg_ref, dskip_ref, normw_ref, expand_ref = rest[:11]
    y_ref, hout_ref, state_sc, xpx_sc, xpb_sc = rest[12:]
    t = SSD_CHUNK
    c = pl.program_id(1)
    hpg = SSM_HEADS // SSM_GROUPS
    pair = 2 * SSM_HEAD_DIM

    @pl.when(c == 0)
    def _():
        if has_h0:
            for g in range(SSM_GROUPS):
                state_sc[g] = jnp.concatenate(
                    [jnp.concatenate([h0_ref[g * hpg + e], h0_ref[g * hpg + e + 1]], axis=0).T
                     for e in range(0, hpg, 2)], axis=1)
        else:
            state_sc[...] = jnp.zeros(state_sc.shape, F32)
        xpx_sc[pl.ds(5, CONV_W - 1), :] = pxs_ref[...]
        xpb_sc[pl.ds(5, CONV_W - 1), :] = pbc_ref[...]

    def rows(ref):
        x = ref[...]
        if t_in < t:
            x = jnp.concatenate([x, jnp.zeros((t - t_in, x.shape[1]), x.dtype)], axis=0)
        return x

    xs = _conv_silu(xpx_sc, rows(xs_ref), cwx_ref, cbx_ref, t)
    bcm = _conv_silu(xpb_sc, rows(bc_ref), cwb_ref, cbb_ref, t)
    z = rows(z_ref)

    row = lax.broadcasted_iota(jnp.int32, (t, LANES), 0)
    n_valid = jnp.minimum(valid_len - c * t, t_in)
    dt = jnp.where(row < n_valid, _softplus(rows(sm_ref) + dtb_ref[...]), 0.0)
    a = dt * aneg_ref[...]
    ri = lax.broadcasted_iota(jnp.int32, (t, t), 0)
    ci = lax.broadcasted_iota(jnp.int32, (t, t), 1)
    causal = ri >= ci
    acum = _dot_exact(jnp.where(causal, 1.0, 0.0), a)
    acum_t = acum.T
    ex = jnp.dot(jnp.concatenate(_bf16_pieces(dt) + _bf16_pieces(acum), axis=0), expand_ref[...],
                 preferred_element_type=F32)
    dtx = ex[:t] + ex[t:2 * t] + ex[2 * t:3 * t]
    acx = ex[3 * t:4 * t] + ex[4 * t:5 * t] + ex[5 * t:]
    last = acx[t - 1:t, :]
    xdt = xs * dtx
    xw = xdt * jnp.exp(last - acx)
    eacx = jnp.exp(acx)
    elast = jnp.exp(last)

    gw = D_INNER // SSM_GROUPS
    hpg = SSM_HEADS // SSM_GROUPS
    ys = []
    for g in range(SSM_GROUPS):
        bg = bcm[:, g * SSM_STATE:(g + 1) * SSM_STATE]
        cg = bcm[:, (SSM_GROUPS + g) * SSM_STATE:(SSM_GROUPS + g + 1) * SSM_STATE]
        cb = _dot_nt(cg, bg)
        h_t = state_sc[g]
        yg = [_dot(cg, h_t) * eacx[:, g * gw:(g + 1) * gw]]
        intra = []
        for e in range(hpg):
            hd = g * hpg + e
            seg = acum[:, SM_DT + hd:SM_DT + hd + 1] - acum_t[SM_DT + hd:SM_DT + hd + 1, :]
            m = cb * jnp.exp(jnp.where(causal, seg, -jnp.inf))
            intra.append(_dot(m, xdt[:, hd * SSM_HEAD_DIM:(hd + 1) * SSM_HEAD_DIM]))
        ys.append(yg[0] + jnp.concatenate(intra, axis=1))
        state_sc[g] = h_t * elast[:, g * gw:(g + 1) * gw] + _dot(bg.T, xw[:, g * gw:(g + 1) * gw])
    y = jnp.concatenate(ys, axis=1) + xs * dskip_ref[...]
    y = y * (z * _sigmoid(z))
    outs = []
    for g in range(SSM_GROUPS):
        yg = y[:, g * gw:(g + 1) * gw]
        outs.append(yg * lax.rsqrt(jnp.mean(yg * yg, axis=-1, keepdims=True) + NORM_EPS))
    y = jnp.concatenate(outs, axis=1) * normw_ref[...]
    y_ref[...] = y[:t_in].astype(y_ref.dtype)

    @pl.when(c == pl.num_programs(1) - 1)
    def _():
        for g in range(SSM_GROUPS):
            st = state_sc[g]
            for e in range(0, hpg, 2):
                both = st[:, e * SSM_HEAD_DIM:e * SSM_HEAD_DIM + pair].T
                hout_ref[g * hpg + e] = both[:SSM_HEAD_DIM]
                hout_ref[g * hpg + e + 1] = both[SSM_HEAD_DIM:]


def _ssd(proj, row0, n_seq, t_in, n_chunks, valid_len, layer, state_in, state_out, prefix, lw, out_dtype):
    assert row0 % t_in == 0
    rb0 = row0 // t_in
    rowblk = lambda w, col: pl.BlockSpec((t_in, w), lambda s, c: (rb0 + s * n_chunks + c, col // w))
    per_seq = lambda shape: pl.BlockSpec((None,) + shape, lambda s, c: (s,) + (0,) * len(shape))
    const = lambda arr: pl.BlockSpec(arr.shape, lambda s, c: (0,) * arr.ndim)
    state_blk = pl.BlockSpec((None, None, SSM_HEADS, SSM_HEAD_DIM, SSM_STATE), lambda s, c: (layer, s, 0, 0, 0))
    gw = D_INNER // SSM_GROUPS
    pxs, pbc = prefix[:, :, :D_INNER], prefix[:, :, D_INNER:]
    consts = (lw["cwx"], lw["cbx"], lw["cwb"], lw["cbb"], lw["dtb"], lw["aneg"], lw["dskip"], lw["normw"], lw["expand"])
    has_h0 = state_in is not None
    operands = [proj, proj, proj, proj] + [state_in] * has_h0 + [pxs, pbc, *consts, state_out]
    in_specs = ([rowblk(D_INNER, C_Z), rowblk(D_INNER, C_XS), rowblk(BC_DIM, C_BC), rowblk(LANES, C_SM)]
                + [state_blk] * has_h0
                + [per_seq((CONV_W - 1, D_INNER)), per_seq((CONV_W - 1, BC_DIM))] + [const(a) for a in consts]
                + [pl.BlockSpec(memory_space=pl.ANY)])
    return pl.pallas_call(
        functools.partial(_ssd_kernel, t_in=t_in, valid_len=valid_len, has_h0=has_h0),
        grid=(n_seq, n_chunks),
        in_specs=in_specs,
        out_specs=[pl.BlockSpec((t_in, D_INNER), lambda s, c: (s * n_chunks + c, 0)), state_blk],
        out_shape=[jax.ShapeDtypeStruct((n_seq * n_chunks * t_in, D_INNER), out_dtype),
                   jax.ShapeDtypeStruct(state_out.shape, F32)],
        scratch_shapes=[pltpu.VMEM((SSM_GROUPS, SSM_STATE, gw), F32),
                        pltpu.VMEM((SSD_CHUNK + 8, D_INNER), F32),
                        pltpu.VMEM((SSD_CHUNK + 8, BC_DIM), F32)],
        input_output_aliases={len(operands) - 1: 1},
        compiler_params=_params("parallel", "arbitrary"),
        name="ssd",
    )(*operands)


def _sort_key(x):
    bits = lax.bitcast_convert_type(x, jnp.int32)
    raw = bits ^ ((bits >> 31) & 0x7FFFFFFF)
    return raw - (raw >> 31)


def _fold_rows(x, group=8):
    x = x.reshape(x.shape[0] // group, group, x.shape[1])
    while x.shape[0] > 1:
        half, odd = divmod(x.shape[0], 2)
        folded = x[:half] + x[half:2 * half]
        if odd:
            folded = jnp.concatenate([folded[:1] + x[2 * half:], folded[1:]], axis=0) if half > 1 else folded + x[2 * half:]
        x = folded
    return x[0]


def _count_t(keys_sc, n_chunks, pred):
    def body(c, cnt):
        return cnt + _fold_rows(jnp.where(pred(keys_sc[c]), 1.0, 0.0))

    cnt = lax.fori_loop(0, n_chunks, body, jnp.zeros((8, LANES), F32))
    return jnp.sum(cnt, axis=0, keepdims=True)


HALF_MIN = -(2 ** 30)
SEARCH_UNROLLS = tuple(range(1, 17))


def _search_bounds(nkc):
    return sorted({min(b, nkc) for b in SEARCH_UNROLLS} | {nkc})


def _search_chunks(n_chunks, nkc):
    bounds = _search_bounds(nkc)
    n = jnp.int32(bounds[-1])
    for b in reversed(bounds[:-1]):
        n = jnp.where(n_chunks <= b, b, n)
    return n


PLANE_ROWS = 32 * 8


def _bit_planes(rows):
    a = list(rows)
    j, m = 16, 0x0000FFFF
    while j:
        k = 0
        while k < 32:
            t = (a[k] ^ lax.shift_right_logical(a[k + j], jnp.int32(j))) & m
            a[k] = a[k] ^ t
            a[k + j] = a[k + j] ^ (t << j)
            k = (k + j + 1) & ~j
        j >>= 1
        m = (m ^ (m << j)) & 0xFFFFFFFF
    return [a[31 - b] for b in range(32)]


def _store_planes(planes_sc, group, key):
    planes = _bit_planes([key[8 * v:8 * v + 8] for v in range(32)])
    planes[31] = ~planes[31]
    for b in range(32):
        planes_sc[b, group] = planes[b]


def _kth_key_planes(planes_sc, n, k):
    def count(words):
        return jnp.sum(jnp.sum(lax.population_count(words), axis=0).astype(F32), axis=0, keepdims=True)

    def bit_body(i, carry):
        live, above, t = carry
        bit = 31 - i
        ones = live & planes_sc[bit][:n]
        tot = above + count(ones)
        take = tot >= k
        live = jnp.where(take, ones, live ^ ones)
        above = jnp.where(take, above, tot)
        return live, above, t | lax.shift_left(jnp.where(take, 1, 0), bit)

    init = (jnp.full((n, 8, LANES), -1, jnp.int32), jnp.zeros((1, LANES), F32), jnp.zeros((1, LANES), jnp.int32))
    live, above, t = lax.fori_loop(0, 32, bit_body, init)
    return t ^ INT_MIN, above + count(live)


def _kth_key_t(planes_sc, n_chunks, nkc, k):
    bounds = _search_bounds(nkc)
    per_chunk = planes_sc.shape[1] // nkc

    def build(i):
        if i == len(bounds) - 1:
            return lambda: _kth_key_planes(planes_sc, bounds[i] * per_chunk, k)
        return lambda: lax.cond(n_chunks <= bounds[i], lambda: _kth_key_planes(planes_sc, bounds[i] * per_chunk, k),
                                build(i + 1))

    return build(0)()


def _select_t(keys_sc, planes_sc, bias_sc, n_chunks, k):
    kc = keys_sc.shape[1]
    thr, n_ge = _kth_key_t(planes_sc, n_chunks, keys_sc.shape[0], k)
    surplus = jnp.max(jnp.where((n_ge > k) & (thr != INT_MIN), 1.0, 0.0))

    @pl.when(surplus == 0.0)
    def _():
        floor = jnp.maximum(thr, INT_MIN + 1)

        def body(c, carry):
            bias_sc[c] = jnp.where(keys_sc[c] >= floor, 0.0, MASK_BIAS)
            return carry

        lax.fori_loop(0, n_chunks, body, 0)

    @pl.when(surplus != 0.0)
    def _():
        need = k - _count_t(keys_sc, n_chunks, lambda key: key > thr)
        ri = lax.broadcasted_iota(jnp.int32, (kc, kc), 0)
        ci = lax.broadcasted_iota(jnp.int32, (kc, kc), 1)
        lower = jnp.where(ci <= ri, 1.0, 0.0).astype(jnp.bfloat16)

        def body(c, run):
            key = keys_sc[c]
            eqf = jnp.where((key == thr) & (key != INT_MIN), 1.0, 0.0)
            rank = jnp.dot(lower, eqf.astype(jnp.bfloat16), preferred_element_type=F32) + run
            sel = (key > thr) | ((eqf > 0.0) & (rank <= need))
            bias_sc[c] = jnp.where(sel, 0.0, MASK_BIAS)
            return run + jnp.sum(_fold_rows(eqf), axis=0, keepdims=True)

        lax.fori_loop(0, n_chunks, body, jnp.zeros((1, LANES), F32))


def _select_s(keys_sc, kh_sc, bias_ref, k):
    n_chunks, rows, width = keys_sc.shape

    def count(pred):
        return jnp.sum(jnp.sum(jnp.where(pred(keys_sc[...]), 1.0, 0.0), axis=0), axis=1, keepdims=True)

    def bit_body(i, u):
        cand = u | lax.shift_left(jnp.int32(1), 30 - i)
        below = jnp.sum((kh_sc[...] - (cand + HALF_MIN)[None]) >> 31, axis=0).astype(F32)
        return jnp.where(n_chunks * width + jnp.sum(below, axis=1, keepdims=True) >= k, cand, u)

    odd = (lax.fori_loop(0, 31, bit_body, jnp.zeros((rows, 1), jnp.int32)) + HALF_MIN) * 2 + 1
    thr = jnp.where(count(lambda key: key >= odd[None]) >= k, odd, odd - 1)
    n_ge = count(lambda key: key >= thr[None])
    surplus = jnp.max(jnp.where((n_ge > k) & (thr != INT_MIN), 1.0, 0.0))

    @pl.when(surplus == 0.0)
    def _():
        floor = jnp.maximum(thr, INT_MIN + 1)
        bias_ref[...] = jnp.where(keys_sc[...] >= floor[None], 0.0, MASK_BIAS)

    @pl.when(surplus != 0.0)
    def _():
        need = k - count(lambda key: key > thr[None])
        ri = lax.broadcasted_iota(jnp.int32, (width, width), 0)
        ci = lax.broadcasted_iota(jnp.int32, (width, width), 1)
        upper = jnp.where(ri <= ci, 1.0, 0.0).astype(jnp.bfloat16)

        def body(c, run):
            key = keys_sc[c]
            eqf = jnp.where((key == thr) & (key != INT_MIN), 1.0, 0.0)
            rank = jnp.dot(eqf.astype(jnp.bfloat16), upper, preferred_element_type=F32) + run
            sel = (key > thr) | ((eqf > 0.0) & (rank <= need))
            bias_ref[c] = jnp.where(sel, 0.0, MASK_BIAS)
            return run + jnp.sum(eqf, axis=1, keepdims=True)

        lax.fori_loop(0, n_chunks, body, jnp.zeros((rows, 1), F32))


def _pattn_kernel(qi_ref, smq_ref, q_ref, smk_ref, k_ref, v_ref, o_ref,
                  kb_sc, kib_sc, vt_sc, keys_sc, planes_sc, bias_sc, qz_sc, m_sc, acc_sc, *, kc, lp, topk):
    j = pl.program_id(1)
    nq = Q_BLOCK
    nkc = kb_sc.shape[0]

    @pl.when(j == 0)
    def _():
        for c in range(nkc):
            r0 = c * kc
            n = min(kc, lp - r0)

            def chunk(ref):
                x = ref[r0:r0 + n, :]
                return x if n == kc else jnp.concatenate([x, jnp.zeros((kc - n, x.shape[1]), x.dtype)], axis=0)

            kb_sc[c] = chunk(k_ref).astype(MXU_DTYPE)
            kib_sc[c] = chunk(smk_ref).astype(MXU_DTYPE)
            v_t = chunk(v_ref).T
            pad = jnp.concatenate([jnp.ones((1, kc), F32), jnp.zeros((V_ROWS - HEAD_DIM - 1, kc), F32)], axis=0)
            vt_sc[c] = jnp.concatenate([x for hk in range(ATT_KV_HEADS)
                                        for x in (v_t[hk * HEAD_DIM:(hk + 1) * HEAD_DIM], pad)], axis=0).astype(MXU_DTYPE)

    n_chunks = ((j + 1) * nq - 1) // kc + 1

    qi_t = (qi_ref[...] * IDX_DIM ** -0.5).T
    qiz = jnp.concatenate([qi_t[h * IDX_DIM:(h + 1) * IDX_DIM] for h in range(IDX_HEADS)], axis=1)
    qiz = jnp.concatenate([qiz, jnp.zeros((LANES - IDX_DIM, IDX_HEADS * nq), F32)], axis=0).astype(MXU_DTYPE)
    smq_t = smq_ref[...].T
    wi_row = jnp.concatenate([smq_t[SM_WI + h:SM_WI + h + 1] for h in range(IDX_HEADS)], axis=1) * IDX_HEADS ** -0.5
    kpos = lax.broadcasted_iota(jnp.int32, (kc, nq), 0)
    qpos = j * nq + lax.broadcasted_iota(jnp.int32, (kc, nq), 1)

    def score_body(c, carry):
        s = jnp.maximum(jnp.dot(kib_sc[c], qiz, preferred_element_type=F32), 0.0) * wi_row
        acc = s[:, :nq]
        for h in range(1, IDX_HEADS):
            acc = acc + s[:, h * nq:(h + 1) * nq]
        key = jnp.where(c * kc + kpos <= qpos, _sort_key(acc), INT_MIN)
        keys_sc[c] = key
        for g in range(kc // PLANE_ROWS):
            _store_planes(planes_sc, c * (kc // PLANE_ROWS) + g, key[g * PLANE_ROWS:(g + 1) * PLANE_ROWS])
        return carry

    lax.fori_loop(0, n_chunks, score_body, 0)

    def blank_body(c, carry):
        for g in range(kc // PLANE_ROWS):
            for b in range(32):
                planes_sc[b, c * (kc // PLANE_ROWS) + g] = jnp.zeros((8, nq), jnp.int32)
        return carry

    lax.fori_loop(n_chunks, _search_chunks(n_chunks, nkc), blank_body, 0)
    _select_t(keys_sc, planes_sc, bias_sc, n_chunks, topk)

    q_t = (q_ref[...] * (HEAD_DIM ** -0.5 * LOG2_E)).T
    gw = ATT_GROUP * nq
    zero = jnp.zeros((HEAD_DIM, gw), F32)
    for hk in range(ATT_KV_HEADS):
        own = jnp.concatenate([q_t[(hk * ATT_GROUP + g) * HEAD_DIM:(hk * ATT_GROUP + g + 1) * HEAD_DIM]
                               for g in range(ATT_GROUP)], axis=1)
        qz_sc[hk] = jnp.concatenate([own if i == hk else zero for i in range(ATT_KV_HEADS)],
                                    axis=0).astype(MXU_DTYPE)
    m_sc[...] = jnp.full(m_sc.shape, MASK_BIAS, F32)
    acc_sc[...] = jnp.zeros(acc_sc.shape, F32)

    def attend(chunks):
        stages = [(c, hk) for c in chunks for hk in range(ATT_KV_HEADS)]
        loaded = {}

        def operands(c):
            if id(c) not in loaded:
                loaded[id(c)] = (kb_sc[c], vt_sc[c], jnp.concatenate([bias_sc[c]] * ATT_GROUP, axis=1))
            return loaded[id(c)]

        def qk(t):
            c, hk = stages[t]
            kch, _, b4 = operands(c)
            return jnp.dot(kch, qz_sc[hk], preferred_element_type=F32) + b4

        def soft(t, s):
            hk = stages[t][1]
            m_old = m_sc[hk]
            m_new = jnp.maximum(m_old, jnp.max(s, axis=0, keepdims=True))
            m_sc[hk] = m_new
            return jnp.exp2(m_old - m_new), jnp.exp2(s - m_new).astype(MXU_DTYPE)

        def pv(t, alpha, p):
            c, hk = stages[t]
            vt = operands(c)[1]
            acc_sc[hk] = alpha * acc_sc[hk] + jnp.dot(vt[hk * V_ROWS:(hk + 1) * V_ROWS], p,
                                                      preferred_element_type=F32)

        n = len(stages)
        s = {0: qk(0), 1: qk(1)}
        ap = {}
        for t in range(n):
            ap[t] = soft(t, s.pop(t))
            if t + 2 < n:
                s[t + 2] = qk(t + 2)
            if t >= 1:
                pv(t - 1, *ap.pop(t - 1))
        pv(n - 1, *ap.pop(n - 1))

    def pair_body(i, carry):
        attend([2 * i, 2 * i + 1])
        return carry

    lax.fori_loop(0, n_chunks // 2, pair_body, 0)

    @pl.when(n_chunks % 2 == 1)
    def _():
        attend([n_chunks - 1])

    heads = []
    for hk in range(ATT_KV_HEADS):
        acc = acc_sc[hk]
        o = acc[:HEAD_DIM] / acc[HEAD_DIM:HEAD_DIM + 1]
        heads += [o[:, g * nq:(g + 1) * nq] for g in range(ATT_GROUP)]
    o_ref[...] = jnp.concatenate(heads, axis=0).T.astype(o_ref.dtype)


def _prompt_attention(proj, n_seq, lp, topk):
    nqb = lp // Q_BLOCK
    kc = 512
    nkc = -(-lp // kc)
    qblk = lambda w, col: pl.BlockSpec((Q_BLOCK, w), lambda b, j: (b * nqb + j, col // w))
    seqblk = lambda w, col: pl.BlockSpec((lp, w), lambda b, j: (b, col // w))
    gw = ATT_GROUP * Q_BLOCK
    return pl.pallas_call(
        functools.partial(_pattn_kernel, kc=kc, lp=lp, topk=topk),
        grid=(n_seq, nqb),
        in_specs=[qblk(QI_DIM, C_QI), qblk(LANES, C_SM), qblk(ATT_DIM, C_Q),
                  seqblk(LANES, C_SM), seqblk(KV_DIM, C_K), seqblk(KV_DIM, C_V)],
        out_specs=pl.BlockSpec((Q_BLOCK, ATT_DIM), lambda b, j: (b * nqb + j, 0)),
        out_shape=jax.ShapeDtypeStruct((n_seq * lp, ATT_DIM), MXU_DTYPE),
        scratch_shapes=[pltpu.VMEM((nkc, kc, KV_DIM), MXU_DTYPE), pltpu.VMEM((nkc, kc, LANES), MXU_DTYPE),
                        pltpu.VMEM((nkc, ATT_KV_HEADS * V_ROWS, kc), MXU_DTYPE),
                        pltpu.VMEM((nkc, kc, Q_BLOCK), jnp.int32),
                        pltpu.VMEM((32, nkc * (kc // PLANE_ROWS), 8, Q_BLOCK), jnp.int32),
                        pltpu.VMEM((nkc, kc, Q_BLOCK), F32),
                        pltpu.VMEM((ATT_KV_HEADS, KV_DIM, gw), MXU_DTYPE),
                        pltpu.VMEM((ATT_KV_HEADS, 1, gw), F32),
                        pltpu.VMEM((ATT_KV_HEADS, V_ROWS, gw), F32)],
        compiler_params=_params("parallel", "arbitrary"),
        name="prompt_attention",
    )(proj, proj, proj, proj, proj, proj)


def _pad_rows(x, n):
    return jnp.concatenate([x, jnp.zeros((n - x.shape[0], x.shape[1]), x.dtype)], axis=0)


def _sselect_kernel(pt_ref, qi_ref, sm_ref, cache_ref, bias_ref, keys_sc, kh_sc, pages_sc, sem, *,
                    layer, group, n_pages, page, ns, topk):
    s = pl.program_id(0)
    slot = s % 2

    def page_copies(seq, slot_):
        return [pltpu.make_async_copy(cache_ref.at[layer, pt_ref[seq, p]], pages_sc.at[slot_, p], sem.at[slot_])
                for p in range(n_pages)]

    @pl.when(s == 0)
    def _():
        for cp in page_copies(0, 0):
            cp.start()

    @pl.when(s + 1 < pl.num_programs(0))
    def _():
        for cp in page_copies(s + 1, 1 - slot):
            cp.start()

    qi = (qi_ref[...] * IDX_DIM ** -0.5).astype(MXU_DTYPE)
    qs = jnp.concatenate([qi[:, h * IDX_DIM:(h + 1) * IDX_DIM] for h in range(IDX_HEADS)], axis=0)
    sm = sm_ref[...]
    wi = sm[:, SM_WI:SM_WI + IDX_HEADS] * IDX_HEADS ** -0.5

    def weigh(x):
        x = jnp.maximum(x, 0.0)
        acc = x[:ns] * wi[:, 0:1]
        for h in range(1, IDX_HEADS):
            acc = acc + x[h * ns:(h + 1) * ns] * wi[:, h:h + 1]
        return _sort_key(acc)

    qrow = lax.broadcasted_iota(jnp.int32, (ns, page), 0)
    lane = lax.broadcasted_iota(jnp.int32, (ns, page), 1)
    new = weigh(_dot_nt(qs, _pad_rows(sm[:, SM_KI:SM_KI + IDX_DIM], page)))
    new = jnp.where(lane <= qrow, new, INT_MIN)
    keys_sc[n_pages] = new
    kh_sc[n_pages] = new >> 1

    for cp in page_copies(s, slot):
        cp.wait()

    for g in range(n_pages // group):
        keys = weigh(_dot(qs, jnp.concatenate([pages_sc[slot, g * group + i] for i in range(group)], axis=1)))
        for i in range(group):
            keys_sc[g * group + i] = keys[:, i * page:(i + 1) * page]
            kh_sc[g * group + i] = keys[:, i * page:(i + 1) * page] >> 1
    _select_s(keys_sc, kh_sc, bias_ref, topk)


def _sattn_kernel(pt_ref, q_ref, kn_ref, vn_ref, bias_ref, *rest, pages_per_step, n_pages, page, ns):
    k_refs = rest[:pages_per_step]
    v_refs = rest[pages_per_step:2 * pages_per_step]
    o_ref, qbd_sc, m_sc, l_sc, acc_sc = rest[2 * pages_per_step:]
    g = pl.program_id(1)

    @pl.when(g == 0)
    def _():
        q = q_ref[...] * HEAD_DIM ** -0.5
        rows = []
        for h in range(ATT_HEADS):
            hk = h // ATT_GROUP
            parts = [jnp.zeros((ns, HEAD_DIM), F32)] * ATT_KV_HEADS
            parts[hk] = q[:, h * HEAD_DIM:(h + 1) * HEAD_DIM]
            rows.append(jnp.concatenate(parts, axis=1))
        qbd_sc[...] = jnp.concatenate(rows, axis=0).astype(MXU_DTYPE)
        m_sc[...] = jnp.full(m_sc.shape, MASK_BIAS, F32)
        l_sc[...] = jnp.zeros(l_sc.shape, F32)
        acc_sc[...] = jnp.zeros(acc_sc.shape, F32)

    def attend(s, b, pv):
        s = s + jnp.concatenate([b] * ATT_HEADS, axis=0)
        m_old = m_sc[...]
        m_new = jnp.maximum(m_old, jnp.max(s, axis=1, keepdims=True))
        alpha = jnp.exp(m_old - m_new)
        p = jnp.exp(s - m_new)
        l_sc[...] = alpha * l_sc[...] + jnp.sum(p, axis=1, keepdims=True)
        acc_sc[...] = alpha * acc_sc[...] + pv(p)
        m_sc[...] = m_new

    kcat = jnp.concatenate([r[...] for r in k_refs], axis=1)
    vcat = jnp.concatenate([r[...] for r in v_refs], axis=1)
    bcat = jnp.concatenate([bias_ref[g * pages_per_step + i] for i in range(pages_per_step)], axis=1)
    attend(_dot(qbd_sc[...], kcat), bcat, lambda p: _dot_nt(p, vcat))

    @pl.when(g == pl.num_programs(1) - 1)
    def _():
        vn = _pad_rows(vn_ref[...], page)
        attend(_dot_nt(qbd_sc[...], _pad_rows(kn_ref[...], page)), bias_ref[n_pages], lambda p: _dot(p, vn))
        o = acc_sc[...] / l_sc[...]
        o_ref[...] = jnp.concatenate(
            [o[h * ns:(h + 1) * ns, (h // ATT_GROUP) * HEAD_DIM:(h // ATT_GROUP + 1) * HEAD_DIM]
             for h in range(ATT_HEADS)], axis=1)


def _key_minor(cache):
    nd = cache.ndim
    t = jnp.transpose(cache, (0, 1) + tuple(range(3, nd)) + (2,))
    return t.reshape(t.shape[:2] + (-1, t.shape[-1]))


def _sample_attention(proj, row0, page_table, ck_t, cv_t, cki_t, layer, ns, topk):
    bd, n_pages = page_table.shape
    page = cki_t.shape[3]
    assert page == LANES and row0 % ns == 0
    pps = max(d for d in range(1, 17) if n_pages % d == 0)
    ng = n_pages // pps
    rb0 = row0 // ns
    rowblk = lambda w, col: pl.BlockSpec((ns, w), lambda s, g, pt: (rb0 + s, col // w))
    pageblk = lambda w, i: pl.BlockSpec((None, None, w, page), lambda s, g, pt: (layer, pt[s, g * pps + i], 0, 0))
    biasblk = pl.BlockSpec((None, n_pages + 1, ns, page), lambda s, g, pt: (s, 0, 0, 0))
    static = dict(pages_per_step=pps, n_pages=n_pages, page=page, ns=ns)

    bias = pl.pallas_call(
        functools.partial(_sselect_kernel, layer=layer, group=pps, n_pages=n_pages, page=page, ns=ns, topk=topk),
        grid_spec=pltpu.PrefetchScalarGridSpec(
            num_scalar_prefetch=1, grid=(bd,),
            in_specs=[pl.BlockSpec((ns, QI_DIM), lambda s, pt: (rb0 + s, C_QI // QI_DIM)),
                      pl.BlockSpec((ns, LANES), lambda s, pt: (rb0 + s, C_SM // LANES)),
                      pl.BlockSpec(memory_space=pl.ANY)],
            out_specs=pl.BlockSpec((None, n_pages + 1, ns, page), lambda s, pt: (s, 0, 0, 0)),
            scratch_shapes=[pltpu.VMEM((n_pages + 1, ns, page), jnp.int32)] * 2
                           + [pltpu.VMEM((2, n_pages, IDX_DIM, page), F32), pltpu.SemaphoreType.DMA((2,))]),
        out_shape=jax.ShapeDtypeStruct((bd, n_pages + 1, ns, page), F32),
        compiler_params=_params("arbitrary"),
        name="sample_select",
    )(page_table, proj, proj, cki_t)

    rows = ATT_HEADS * ns
    return pl.pallas_call(
        functools.partial(_sattn_kernel, **static),
        grid_spec=pltpu.PrefetchScalarGridSpec(
            num_scalar_prefetch=1, grid=(bd, ng),
            in_specs=[rowblk(ATT_DIM, C_Q), rowblk(KV_DIM, C_K), rowblk(KV_DIM, C_V), biasblk]
                     + [pageblk(KV_DIM, i) for i in range(pps)] * 2,
            out_specs=pl.BlockSpec((ns, ATT_DIM), lambda s, g, pt: (s, 0)),
            scratch_shapes=[pltpu.VMEM((rows, KV_DIM), MXU_DTYPE), pltpu.VMEM((rows, 1), F32),
                            pltpu.VMEM((rows, 1), F32), pltpu.VMEM((rows, KV_DIM), F32)]),
        out_shape=jax.ShapeDtypeStruct((bd * ns, ATT_DIM), F32),
        compiler_params=_params("parallel", "arbitrary"),
        name="sample_attention",
    )(page_table, proj, proj, proj, bias, *([ck_t] * pps), *([cv_t] * pps))


def _regroup_w_in(w):
    o_dt = D_INNER + D_INNER + BC_DIM
    o_q = o_dt + SSM_HEADS
    o_ki = o_q + ATT_DIM + 2 * KV_DIM + QI_DIM
    o_wi = o_ki + IDX_DIM
    o_g = o_wi + IDX_HEADS
    parts = [w[:, :o_dt], w[:, o_q:o_ki], w[:, o_g:o_g + 2 * D_MODEL], w[:, o_ki:o_wi], w[:, o_dt:o_q], w[:, o_wi:o_g]]
    used = sum(p.shape[1] for p in parts)
    parts.append(jnp.zeros((w.shape[0], PROJ_COLS - used), w.dtype))
    w = jnp.concatenate(parts, axis=1).astype(MXU_DTYPE)
    return w.reshape(w.shape[0], PROJ_COLS // PROJ_TN, PROJ_TN).transpose(1, 0, 2)


def _small_lanes(v):
    return jnp.zeros((1, LANES), F32).at[0, SM_DT:SM_DT + SSM_HEADS].set(v.astype(F32))


def _layer_consts(conv_w, conv_b, dt_bias, a_log, d_skip, ssm_norm_w):
    head_of_lane = jnp.arange(D_INNER) // SSM_HEAD_DIM
    expand = (jnp.arange(LANES)[:, None] == (SM_DT + head_of_lane)[None, :]).astype(jnp.bfloat16)
    return dict(
        cwx=conv_w[:, :D_INNER], cbx=conv_b[None, :D_INNER], cwb=conv_w[:, D_INNER:], cbb=conv_b[None, D_INNER:],
        dtb=_small_lanes(dt_bias), aneg=_small_lanes(-jnp.exp(a_log.astype(F32))),
        dskip=jnp.repeat(d_skip.astype(F32), SSM_HEAD_DIM)[None, :], normw=ssm_norm_w.astype(F32)[None, :],
        expand=expand)


def kernel(x_prompt, x_sample, cache_k, cache_v, cache_kidx, state_ssm, state_conv, page_table, meta_tokens, norm1_w, w_in, conv_w, conv_b, dt_bias, a_log, d_skip, ssm_norm_w, w_ssm_proj, w_attn_proj, w_out, norm2_w, w_up, w_down, final_norm_w):
    b, seq, d = x_prompt.shape
    bd, ns = x_sample.shape[:2]
    depth = w_in.shape[0]
    assert d == D_MODEL and ns % 8 == 0
    past = page_table.shape[1] * cache_kidx.shape[2]
    lv = N_META + seq
    lp = -(-lv // Q_BLOCK) * Q_BLOCK
    rp = b * lp
    rs = bd * ns
    r = -(-(rp + rs) // LANES) * LANES
    topk_p = min(TOPK_MAX, seq // 4)
    topk_s = min(TOPK_MAX, (past + ns) // 4)

    meta = meta_tokens.astype(F32)
    pad = jnp.zeros((lp - lv, d), F32)
    h = jnp.concatenate([piece for s in range(b) for piece in (meta, x_prompt[s], pad)]
                        + [x_sample.reshape(rs, d), jnp.zeros((r - rp - rs, d), F32)], axis=0)

    ck_t, cv_t, cki_t = _key_minor(cache_k), _key_minor(cache_v), _key_minor(cache_kidx)
    p_ssm = jnp.zeros((depth, b) + state_ssm.shape[2:], F32)
    s_ssm = jnp.zeros(state_ssm.shape, F32)
    zero_prefix = jnp.zeros((b, CONV_W - 1, D_INNER + BC_DIM), F32)
    outs = [[] for _ in range(10)]
    for l in range(depth):
        lw = _layer_consts(conv_w[l], conv_b[l], dt_bias[l], a_log[l], d_skip[l], ssm_norm_w[l])
        proj = _in_proj(h, norm1_w[l], _regroup_w_in(w_in[l]))

        yp, p_ssm = _ssd(proj, 0, b, SSD_CHUNK, lp // SSD_CHUNK, lv, l, None, p_ssm, zero_prefix, lw, MXU_DTYPE)
        ysm, s_ssm = _ssd(proj, rp, bd, ns, 1, ns, l, state_ssm, s_ssm, state_conv[l], lw, F32)
        ap = _prompt_attention(proj, b, lp, topk_p)
        asm = _sample_attention(proj, rp, page_table, ck_t, cv_t, cki_t, l, ns, topk_s)
        tail = r - rp - rs
        if tail:
            ysm = jnp.concatenate([ysm, jnp.zeros((tail, D_INNER), F32)], axis=0)
            asm = jnp.concatenate([asm, jnp.zeros((tail, ATT_DIM), F32)], axis=0)

        h = _mix(h, rp, yp, ysm, ap, asm, proj, w_ssm_proj[l].astype(MXU_DTYPE), w_attn_proj[l].astype(MXU_DTYPE),
                 w_out[l].astype(MXU_DTYPE), norm2_w[l], w_up[l].astype(MXU_DTYPE), w_down[l].astype(MXU_DTYPE),
                 final_norm_w if l == depth - 1 else None)

        def pcols(c0, w):
            return proj[:rp, c0:c0 + w].reshape(b, lp, w)[:, :lv]

        def scols(c0, w):
            return proj[rp:rp + rs, c0:c0 + w].reshape(bd, ns, w)

        xbc_w = D_INNER + BC_DIM
        p_conv = jnp.stack([proj[s * lp + lv - (CONV_W - 1):s * lp + lv, C_XS:C_XS + xbc_w] for s in range(b)])
        s_conv = jnp.concatenate([state_conv[l], scols(C_XS, xbc_w)], axis=1)[:, -(CONV_W - 1):]
        for acc, t in zip(outs, (
                pcols(C_K, KV_DIM).reshape(b, lv, ATT_KV_HEADS, HEAD_DIM),
                pcols(C_V, KV_DIM).reshape(b, lv, ATT_KV_HEADS, HEAD_DIM),
                pcols(C_SM + SM_KI, IDX_DIM),
                None,
                p_conv,
                scols(C_K, KV_DIM).reshape(bd, ns, ATT_KV_HEADS, HEAD_DIM),
                scols(C_V, KV_DIM).reshape(bd, ns, ATT_KV_HEADS, HEAD_DIM),
                scols(C_SM + SM_KI, IDX_DIM),
                None,
                s_conv)):
            acc.append(t)

    y = h
    y_prompt = jnp.stack([y[s * lp + N_META:s * lp + lv] for s in range(b)])
    y_sample = y[rp:rp + rs].reshape(bd, ns, d)
    stacked = [jnp.stack(o) if o[0] is not None else None for o in outs]
    stacked[3], stacked[8] = p_ssm, s_ssm
    return (y_prompt, y_sample) + tuple(stacked)
```
